```python
import math
import jax
import jax.numpy as jnp
from jax import lax
import numpy as np


D_MODEL = 2048
BATCH = 8
SEQ = 2048
DEPTH = 2

GRID_W = 64
CTX_LEN = 256
NORM_EPS = 1e-6
N_MOD = 6
Q_BLOCK = 128
HEAD_DIM = 128
ROPE_THETA = 10000.0
A_Q_HEADS = 8
A_KV_HEADS = 2
A_GROUP = A_Q_HEADS // A_KV_HEADS
B_HEADS = 4
ATTN_WIDTHS = (A_Q_HEADS * HEAD_DIM, A_KV_HEADS * HEAD_DIM, A_KV_HEADS * HEAD_DIM,
               2 * B_HEADS * HEAD_DIM, 2 * B_HEADS * HEAD_DIM, 2 * B_HEADS * HEAD_DIM)
ATTN_IN = sum(ATTN_WIDTHS)
ATTN_SPLITS = tuple(int(v) for v in np.cumsum(ATTN_WIDTHS)[:-1])
ATTN_OUT = A_Q_HEADS * HEAD_DIM + 2 * B_HEADS * HEAD_DIM
SSD_EXPAND = 2
SSD_INNER = SSD_EXPAND * D_MODEL
SSD_HEADDIM = 64
SSD_HEADS = SSD_INNER // SSD_HEADDIM
SSD_GROUPS = 8
SSD_HPG = SSD_HEADS // SSD_GROUPS
SSD_STATE = 128
SSD_CONV = 3
SSD_CHUNK = 128
SSD_CONV_DIM = SSD_INNER + 2 * SSD_GROUPS * SSD_STATE
SSD_IN = SSD_INNER + SSD_CONV_DIM + 2 * SSD_HEADS
N_EXPERTS = 64
EXPERT_FF = 512
TOP_K = 8
N_EXPERT_GROUPS = 8
TOPK_GROUPS = 4
ROUTED_SCALE = 2.5
EXPERT_BLOCK = 128
N_ATTN_LAYERS = (DEPTH + 1) // 2
N_SSD_LAYERS = DEPTH // 2

kernel_name = 'hybrid_diffusion_gqa_diffattn_ssd_moe'


def rms_norm(u, gain):
    uf = u.astype(jnp.float32)
    y = uf * lax.rsqrt(jnp.mean(uf * uf, axis=-1, keepdims=True) + NORM_EPS)
    return (y * gain.astype(jnp.float32)).astype(u.dtype)


def ada_mod(cond, w_mod, b_mod):
    m = jax.nn.silu(cond) @ w_mod + b_mod
    m = m.reshape(m.shape[:-1] + (1, N_MOD, cond.shape[-1]))
    return [m[..., k, :] for k in range(N_MOD)]


def modulate(u, shift, scale):
    return u * (1.0 + scale) + shift


def axial_rope_tables(n_tokens):
    rows = n_tokens // GRID_W
    row = jnp.repeat(jnp.arange(rows, dtype=jnp.float32), GRID_W)
    col = jnp.tile(jnp.arange(GRID_W, dtype=jnp.float32), rows)
    n_freq = HEAD_DIM // 4
    inv = ROPE_THETA ** (-jnp.arange(n_freq, dtype=jnp.float32) / n_freq)
    ang = jnp.concatenate([row[:, None] * inv, col[:, None] * inv], axis=-1)
    return jnp.cos(ang), jnp.sin(ang)


def apply_rope(u, cos, sin):
    shape = (cos.shape[0],) + (1,) * (u.ndim - 3) + (cos.shape[1],)
    c = cos.reshape(shape).astype(u.dtype)
    s = sin.reshape(shape).astype(u.dtype)
    up = u.reshape(u.shape[:-1] + (u.shape[-1] // 2, 2))
    u0, u1 = up[..., 0], up[..., 1]
    return jnp.stack([u0 * c - u1 * s, u0 * s + u1 * c], axis=-1).reshape(u.shape)


def attend_gqa(q, k, v):
    s = jnp.einsum('bqhgd,bkhd->bhgqk', q, k).astype(jnp.float32) * (HEAD_DIM ** -0.5)
    p = jax.nn.softmax(s, axis=-1).astype(v.dtype)
    return jnp.einsum('bhgqk,bkhd->bqhgd', p, v)


def attend_diff(q, k, v, lam):
    s = jnp.einsum('bqhmd,bkhmd->bhmqk', q, k).astype(jnp.float32) * (HEAD_DIM ** -0.5)
    p = jax.nn.softmax(s, axis=-1)
    a = (p[:, :, 0] - lam * p[:, :, 1]).astype(v.dtype)
    return jnp.einsum('bhqk,bkhe->bqhe', a, v)


def sweep_query_blocks(attend, q, k, v):
    b, s = q.shape[:2]
    nb = s // Q_BLOCK
    qb = jnp.moveaxis(q.reshape((b, nb, Q_BLOCK) + q.shape[2:]), 1, 0)
    out = lax.map(lambda qi: attend(qi, k, v), qb)
    return jnp.moveaxis(out, 0, 1).reshape((b, s) + out.shape[3:])


def attention_mixer(h_lat, h_ctx, w_in, w_out, q_gain, k_gain, lam_q1, lam_k1, lam_q2, lam_k2,
                    subln_gain, lambda_init, cos, sin, with_ctx_out):
    def project(h):
        b, t = h.shape[:2]
        qa, ka, va, qb, kb, vb = jnp.split(h @ w_in, ATTN_SPLITS, axis=-1)
        qa = rms_norm(qa.reshape(b, t, A_KV_HEADS, A_GROUP, HEAD_DIM), q_gain)
        ka = rms_norm(ka.reshape(b, t, A_KV_HEADS, HEAD_DIM), k_gain)
        va = va.reshape(b, t, A_KV_HEADS, HEAD_DIM)
        qb = qb.reshape(b, t, B_HEADS, 2, HEAD_DIM)
        kb = kb.reshape(b, t, B_HEADS, 2, HEAD_DIM)
        vb = vb.reshape(b, t, B_HEADS, 2 * HEAD_DIM)
        return qa, ka, va, qb, kb, vb

    qa_l, ka_l, va_l, qb_l, kb_l, vb_l = project(h_lat)
    qa_c, ka_c, va_c, qb_c, kb_c, vb_c = project(h_ctx)
    qa_l, ka_l, qb_l, kb_l = [apply_rope(u, cos, sin) for u in (qa_l, ka_l, qb_l, kb_l)]
    f32 = jnp.float32
    lam = (jnp.exp(jnp.sum(lam_q1.astype(f32) * lam_k1.astype(f32)))
           - jnp.exp(jnp.sum(lam_q2.astype(f32) * lam_k2.astype(f32))) + lambda_init)

    def diff(q, k, v):
        return attend_diff(q, k, v, lam)

    def cat(u_ctx, u_lat):
        return jnp.concatenate([u_ctx, u_lat], axis=1)

    ya_l = sweep_query_blocks(attend_gqa, qa_l, cat(ka_c, ka_l), cat(va_c, va_l))
    yb_l = sweep_query_blocks(diff, qb_l, cat(kb_c, kb_l), cat(vb_c, vb_l))

    def merge(ya, yb):
        b, t = ya.shape[:2]
        yb = rms_norm(yb, subln_gain) * (1.0 - lambda_init)
        return jnp.concatenate([ya.reshape(b, t, -1), yb.reshape(b, t, -1)], axis=-1) @ w_out

    y_lat = merge(ya_l, yb_l)
    y_ctx = merge(attend_gqa(qa_c, ka_c, va_c), diff(qb_c, kb_c, vb_c)) if with_ctx_out else None
    return y_lat, y_ctx


def depthwise_conv_centred(u, w, bias):
    ch = u.shape[-1]
    width = w.shape[0]
    pad = width // 2
    out = lax.conv_general_dilated(u, w.reshape(width, 1, ch).astype(u.dtype), window_strides=(1,),
                                   padding=[(pad, width - 1 - pad)],
                                   dimension_numbers=('NWC', 'WIO', 'NWC'), feature_group_count=ch)
    return out + bias


def segsum(a):
    t = a.shape[-1]
    xr = jnp.broadcast_to(a[..., :, None], a.shape + (t,))
    strict = jnp.tril(jnp.ones((t, t), dtype=bool), -1)
    cs = jnp.cumsum(jnp.where(strict, xr, 0.0), axis=-2)
    return jnp.where(jnp.tril(jnp.ones((t, t), dtype=bool)), cs, -jnp.inf)


def ssd_chunked(x, dt, a_coef, bm, cm, h0):
    b, t = x.shape[:2]
    nc = t // SSD_CHUNK
    g, hg, p = x.shape[2:]
    xd = (x * dt[..., None]).reshape(b, nc, SSD_CHUNK, g, hg, p)
    a = jnp.moveaxis((dt * a_coef).reshape(b, nc, SSD_CHUNK, g, hg), (1, 2), (3, 4))
    bc = bm.reshape(b, nc, SSD_CHUNK, g, -1)
    cc = cm.reshape(b, nc, SSD_CHUNK, g, -1)
    a_cum = jnp.cumsum(a, axis=-1)
    decay_in = jnp.exp(segsum(a))
    y_diag = jnp.einsum('bclgn,bcsgn,bghcls,bcsghp->bclghp', cc, bc, decay_in, xd)
    decay_to_end = jnp.exp(a_cum[..., -1:] - a_cum)
    states = jnp.einsum('bcsgn,bghcs,bcsghp->bcghpn', bc, decay_to_end, xd)
    states = jnp.concatenate([h0[:, None], states], axis=1)
    chunk_tot = jnp.pad(a_cum[..., -1], ((0, 0), (0, 0), (0, 0), (1, 0)))
    decay_chunk = jnp.exp(segsum(chunk_tot))
    new_states = jnp.einsum('bghzc,bcghpn->bzghpn', decay_chunk, states)
    prev, final = new_states[:, :-1], new_states[:, -1]
    y_off = jnp.einsum('bclgn,bcghpn,bghcl->bclghp', cc, prev, jnp.exp(a_cum))
    return (y_diag + y_off).reshape(b, t, g, hg, p), final


def flip_seq(u, flip):
    return jnp.flip(u, axis=1) if flip else u


def ssd_mixer(h_lat, h_ctx, w_in, conv_w, conv_b, dt_bias, a_log, d_skip, norm_gain, w_out, with_ctx_out):
    f32 = jnp.float32
    a_coef = -jnp.exp(a_log.astype(f32))

    def prep(h):
        b, t = h.shape[:2]
        z, xbc, dt = jnp.split(h @ w_in, [SSD_INNER, SSD_INNER + SSD_CONV_DIM], axis=-1)
        xbc = jax.nn.silu(depthwise_conv_centred(xbc, conv_w, conv_b))
        xs, bm, cm = jnp.split(xbc, [SSD_INNER, SSD_INNER + SSD_GROUPS * SSD_STATE], axis=-1)
        xs = xs.reshape(b, t, SSD_GROUPS, SSD_HPG, SSD_HEADDIM).astype(f32)
        bm = bm.reshape(b, t, SSD_GROUPS, SSD_STATE).astype(f32)
        cm = cm.reshape(b, t, SSD_GROUPS, SSD_STATE).astype(f32)
        dt = jax.nn.softplus(dt.astype(f32).reshape(b, t, 2, SSD_GROUPS, SSD_HPG)
                             + dt_bias.astype(f32).reshape(2, SSD_GROUPS, SSD_HPG))
        return z, xs, bm, cm, dt

    z_l, xs_l, b_l, c_l, dt_l = prep(h_lat)
    z_c, xs_c, b_c, c_c, dt_c = prep(h_ctx)
    h0 = jnp.zeros((h_lat.shape[0], SSD_GROUPS, SSD_HPG, SSD_HEADDIM, SSD_STATE), f32)
    y_lat_dirs, y_ctx_dirs = [], []
    for direction in range(2):
        fl = direction == 1
        a_d = a_coef[direction].reshape(SSD_GROUPS, SSD_HPG)
        skip = d_skip[direction].astype(f32).reshape(SSD_GROUPS, SSD_HPG, 1)
        y_c, h_c = ssd_chunked(flip_seq(xs_c, fl), flip_seq(dt_c[:, :, direction], fl), a_d,
                               flip_seq(b_c, fl), flip_seq(c_c, fl), h0)
        y_l, _ = ssd_chunked(flip_seq(xs_l, fl), flip_seq(dt_l[:, :, direction], fl), a_d,
                             flip_seq(b_l, fl), flip_seq(c_l, fl), h_c)
        y_lat_dirs.append(flip_seq(y_l, fl) + skip * xs_l)
        if with_ctx_out:
            y_ctx_dirs.append(flip_seq(y_c, fl) + skip * xs_c)

    def out(y_dirs, z):
        b, t = z.shape[:2]
        y = (y_dirs[0] + y_dirs[1]).reshape(b, t, SSD_INNER).astype(z.dtype)
        return rms_norm(y * jax.nn.silu(z), norm_gain) @ w_out

    y_lat = out(y_lat_dirs, z_l)
    y_ctx = out(y_ctx_dirs, z_c) if with_ctx_out else None
    return y_lat, y_ctx


def swiglu(t, wg, wu, wd):
    return (jax.nn.silu(t @ wg) * (t @ wu)) @ wd


def routed_experts(t, eidx, gates, w_gate, w_up, w_down):
    n_tok, k = eidx.shape
    n_assign = n_tok * k
    flat_e = eidx.reshape(-1)
    flat_tok = jnp.arange(n_assign, dtype=jnp.int32) // k
    order = jnp.argsort(flat_e)
    e_sorted = flat_e[order]
    counts = jnp.bincount(flat_e, length=N_EXPERTS)
    padded = (counts + EXPERT_BLOCK - 1) // EXPERT_BLOCK * EXPERT_BLOCK
    pad_end = jnp.cumsum(padded)
    pad_start = pad_end - padded
    start = jnp.cumsum(counts) - counts
    dest = pad_start[e_sorted] + (jnp.arange(n_assign, dtype=jnp.int32) - start[e_sorted])
    n_blocks = -(-n_assign // EXPERT_BLOCK) + N_EXPERTS
    n_rows = n_blocks * EXPERT_BLOCK
    row_tok = jnp.zeros((n_rows,), jnp.int32).at[dest].set(flat_tok[order])
    row_gate = jnp.zeros((n_rows,), gates.dtype).at[dest].set(gates.reshape(-1)[order])
    block_expert = jnp.minimum(jnp.searchsorted(pad_end, jnp.arange(n_blocks, dtype=jnp.int32) * EXPERT_BLOCK,
                                                side='right'), N_EXPERTS - 1)

    def one_block(args):
        tok, g, e = args
        return swiglu(t[tok], w_gate[e], w_up[e], w_down[e]) * g[:, None]

    y = lax.map(one_block, (row_tok.reshape(n_blocks, EXPERT_BLOCK),
                            row_gate.reshape(n_blocks, EXPERT_BLOCK), block_expert))
    return jax.ops.segment_sum(y.reshape(n_rows, -1), row_tok, num_segments=n_tok)


def moe_ffn(t, router_w, router_bias, w_gate, w_up, w_down, ws_gate, ws_up, ws_down):
    n_tok = t.shape[0]
    scores = jax.nn.sigmoid((t @ router_w).astype(jnp.float32))
    sel = scores + router_bias.astype(jnp.float32)
    per_group = N_EXPERTS // N_EXPERT_GROUPS
    group_score = lax.top_k(sel.reshape(n_tok, N_EXPERT_GROUPS, per_group), 2)[0].sum(-1)
    _, top_groups = lax.top_k(group_score, TOPK_GROUPS)
    group_ok = jnp.any(top_groups[:, :, None] == jnp.arange(N_EXPERT_GROUPS)[None, None, :], axis=1)
    sel = jnp.where(jnp.repeat(group_ok, per_group, axis=1), sel, -jnp.inf)
    _, eidx = lax.top_k(sel, TOP_K)
    w = jnp.take_along_axis(scores, eidx, axis=-1)
    w = w / jnp.sum(w, axis=-1, keepdims=True) * ROUTED_SCALE
    routed = routed_experts(t, eidx, w.astype(t.dtype), w_gate, w_up, w_down)
    return routed + swiglu(t, ws_gate, ws_up, ws_down)


def setup_inputs(seed: int = 0) -> dict:
    key = jax.random.key(seed)
    keys = jax.random.split(key, 40)
    counter = [0]

    def nk():
        k = keys[counter[0]]
        counter[0] += 1
        return k

    def nrm(shape, scale):
        return jax.random.normal(nk(), shape, jnp.float32) * scale

    def gain(shape):
        return 1.0 + nrm(shape, 0.02)

    D = D_MODEL
    NA, NS = N_ATTN_LAYERS, N_SSD_LAYERS
    dt0 = jnp.exp(jax.random.uniform(nk(), (NS, 2, SSD_HEADS), jnp.float32,
                                     minval=math.log(1e-3), maxval=math.log(1e-1)))
    dt_bias = dt0 + jnp.log(-jnp.expm1(-dt0))
    a_log = jnp.log(jax.random.uniform(nk(), (NS, 2, SSD_HEADS), jnp.float32, minval=1.0, maxval=16.0))
    return {
        'x': nrm((BATCH, SEQ, D), 1.0),
        'c': nrm((BATCH, D), 1.0),
        'ctx': nrm((BATCH, CTX_LEN, D), 1.0),
        'c_ctx': nrm((D,), 1.0),
        'mod_w': nrm((DEPTH, D, N_MOD * D), 0.5 * D ** -0.5),
        'mod_b': nrm((DEPTH, N_MOD * D), 0.02),
        'norm_mix': gain((DEPTH, D)),
        'norm_ffn': gain((DEPTH, D)),
        'norm_final': gain((D,)),
        'attn_w_in': nrm((NA, D, ATTN_IN), D ** -0.5),
        'attn_w_out': nrm((NA, ATTN_OUT, D), ATTN_OUT ** -0.5),
        'attn_q_gain': gain((NA, HEAD_DIM)),
        'attn_k_gain': gain((NA, HEAD_DIM)),
        'diff_lam_q1': nrm((NA, HEAD_DIM), 0.1),
        'diff_lam_k1': nrm((NA, HEAD_DIM), 0.1),
        'diff_lam_q2': nrm((NA, HEAD_DIM), 0.1),
        'diff_lam_k2': nrm((NA, HEAD_DIM), 0.1),
        'diff_subln': gain((NA, 2 * HEAD_DIM)),
        'ssd_w_in': nrm((NS, D, SSD_IN), D ** -0.5),
        'ssd_conv_w': nrm((NS, SSD_CONV, SSD_CONV_DIM), SSD_CONV ** -0.5),
        'ssd_conv_b': nrm((NS, SSD_CONV_DIM), 0.02),
        'ssd_dt_bias': dt_bias,
        'ssd_a_log': a_log,
        'ssd_d': gain((NS, 2, SSD_HEADS)),
        'ssd_norm': gain((NS, SSD_INNER)),
        'ssd_w_out': nrm((NS, SSD_INNER, D), SSD_INNER ** -0.5),
        'router_w': nrm((DEPTH, D, N_EXPERTS), D ** -0.5),
        'router_bias': nrm((DEPTH, N_EXPERTS), 0.01),
        'exp_w_gate': nrm((DEPTH, N_EXPERTS, D, EXPERT_FF), D ** -0.5),
        'exp_w_up': nrm((DEPTH, N_EXPERTS, D, EXPERT_FF), D ** -0.5),
        'exp_w_down': nrm((DEPTH, N_EXPERTS, EXPERT_FF, D), EXPERT_FF ** -0.5),
        'shared_w_gate': nrm((DEPTH, D, EXPERT_FF), D ** -0.5),
        'shared_w_up': nrm((DEPTH, D, EXPERT_FF), D ** -0.5),
        'shared_w_down': nrm((DEPTH, EXPERT_FF, D), EXPERT_FF ** -0.5),
    }


def reference(x, c, ctx, c_ctx, mod_w, mod_b, norm_mix, norm_ffn, norm_final,
              attn_w_in, attn_w_out, attn_q_gain, attn_k_gain,
              diff_lam_q1, diff_lam_k1, diff_lam_q2, diff_lam_k2, diff_subln,
              ssd_w_in, ssd_conv_w, ssd_conv_b, ssd_dt_bias, ssd_a_log, ssd_d, ssd_norm, ssd_w_out,
              router_w, router_bias, exp_w_gate, exp_w_up, exp_w_down,
              shared_w_gate, shared_w_up, shared_w_down):
    b, s, d = x.shape
    n_ctx = ctx.shape[1]
    cos, sin = axial_rope_tables(s)
    ctx_s = ctx
    for i in range(DEPTH):
        last = i == DEPTH - 1
        sh1, sc1, g1, sh2, sc2, g2 = ada_mod(c, mod_w[i], mod_b[i])
        csh1, csc1, cg1, csh2, csc2, cg2 = ada_mod(c_ctx, mod_w[i], mod_b[i])
        h_lat = modulate(rms_norm(x, norm_mix[i]), sh1, sc1)
        h_ctx = modulate(rms_norm(ctx_s, norm_mix[i]), csh1, csc1)
        j = i // 2
        if i % 2 == 0:
            lambda_init = 0.8 - 0.6 * math.exp(-0.3 * i)
            y_lat, y_ctx = attention_mixer(h_lat, h_ctx, attn_w_in[j], attn_w_out[j], attn_q_gain[j], attn_k_gain[j],
                                           diff_lam_q1[j], diff_lam_k1[j], diff_lam_q2[j], diff_lam_k2[j],
                                           diff_subln[j], lambda_init, cos, sin, not last)
        else:
            y_lat, y_ctx = ssd_mixer(h_lat, h_ctx, ssd_w_in[j], ssd_conv_w[j], ssd_conv_b[j], ssd_dt_bias[j],
                                     ssd_a_log[j], ssd_d[j], ssd_norm[j], ssd_w_out[j], not last)
        x = x + g1 * y_lat
        f_lat = modulate(rms_norm(x, norm_ffn[i]), sh2, sc2).reshape(-1, d)
        if last:
            f = moe_ffn(f_lat, router_w[i], router_bias[i], exp_w_gate[i], exp_w_up[i], exp_w_down[i],
                        shared_w_gate[i], shared_w_up[i], shared_w_down[i])
            x = x + g2 * f.reshape(b, s, d)
        else:
            ctx_s = ctx_s + cg1 * y_ctx
            f_ctx = modulate(rms_norm(ctx_s, norm_ffn[i]), csh2, csc2).reshape(-1, d)
            f = moe_ffn(jnp.concatenate([f_ctx, f_lat], axis=0), router_w[i], router_bias[i],
                        exp_w_gate[i], exp_w_up[i], exp_w_down[i],
                        shared_w_gate[i], shared_w_up[i], shared_w_down[i])
            ctx_s = ctx_s + cg2 * f[:b * n_ctx].reshape(b, n_ctx, d)
            x = x + g2 * f[b * n_ctx:].reshape(b, s, d)
    return rms_norm(x, norm_final)
```

```python
import functools
import math

import jax
import jax.numpy as jnp
from jax import lax
from jax.experimental import pallas as pl
from jax.experimental.pallas import tpu as pltpu

F32 = jnp.float32
BF16 = jnp.bfloat16

D_MODEL = 2048
DEPTH = 2
GRID_W = 64
NORM_EPS = 1e-6
N_MOD = 6
HEAD_DIM = 128
ROPE_THETA = 10000.0
A_Q_HEADS = 8
A_KV_HEADS = 2
A_GROUP = A_Q_HEADS // A_KV_HEADS
B_HEADS = 4
QA_W = A_Q_HEADS * HEAD_DIM
KA_W = A_KV_HEADS * HEAD_DIM
QB_W = 2 * B_HEADS * HEAD_DIM
ATTN_IN = QA_W + 2 * KA_W + 3 * QB_W
SSD_INNER = 2 * D_MODEL
SSD_HEADDIM = 64
SSD_HEADS = SSD_INNER // SSD_HEADDIM
SSD_GROUPS = 8
SSD_HPG = SSD_HEADS // SSD_GROUPS
SSD_STATE = 128
SSD_CONV = 3
SSD_CHUNK = 128
SSD_GROUP_W = SSD_HPG * SSD_HEADDIM
SSD_CONV_DIM = SSD_INNER + 2 * SSD_GROUPS * SSD_STATE
N_EXPERTS = 64
EXPERT_FF = 512
TOP_K = 8
N_EXPERT_GROUPS = 8
TOPK_GROUPS = 4
ROUTED_SCALE = 2.5
EXPERT_BLOCK = 128

VMEM_LIMIT_BYTES = 56 * 1024 * 1024


def _pick(n, prefs):
    for p in prefs:
        if n % p == 0:
            return p
    raise ValueError(f"no tile in {prefs} divides {n}")


def _params(sem):
    return pltpu.CompilerParams(dimension_semantics=sem, vmem_limit_bytes=VMEM_LIMIT_BYTES)


def _mm_kernel(x_ref, w_ref, o_ref):
    o_ref[...] = jnp.dot(x_ref[...], w_ref[...], preferred_element_type=F32).astype(o_ref.dtype)


def _matmul(x, w, out_dtype):
    m, k = x.shape
    n = w.shape[1]
    tm = _pick(m, (512, 384, 256, 128, 16, 8))
    tn = _pick(n, (512, 384, 256, 128))
    return pl.pallas_call(
        _mm_kernel,
        grid=(m // tm, n // tn),
        in_specs=[pl.BlockSpec((tm, k), lambda i, j: (i, 0)),
                  pl.BlockSpec((k, tn), lambda i, j: (0, j))],
        out_specs=pl.BlockSpec((tm, tn), lambda i, j: (i, j)),
        out_shape=jax.ShapeDtypeStruct((m, n), out_dtype),
        compiler_params=_params(("parallel", "arbitrary")),
        name="matmul",
    )(x, w)


def _softmax_rows(s):
    m = jnp.max(s, axis=-1, keepdims=True)
    p = jnp.exp(s - m)
    return p, jnp.sum(p, axis=-1, keepdims=True)


def _gqa_kernel(q_ref, k_ref, v_ref, o_ref):
    k = k_ref[0]
    v = v_ref[0]
    for g in range(A_GROUP):
        q = q_ref[0, :, g * HEAD_DIM:(g + 1) * HEAD_DIM]
        s = lax.dot_general(q, k, (((1,), (1,)), ((), ())), preferred_element_type=F32)
        p, l = _softmax_rows(s)
        o = jnp.dot(p.astype(BF16), v, preferred_element_type=F32) / l
        o_ref[0, :, g * HEAD_DIM:(g + 1) * HEAD_DIM] = o.astype(o_ref.dtype)


def _gqa_attention(q, k, v, q_row0, n_q, n_k):
    b = q.shape[0]
    tq = _pick(n_q, (256, 128))
    assert q_row0 % tq == 0
    q0 = q_row0 // tq
    gw = A_GROUP * HEAD_DIM
    return pl.pallas_call(
        _gqa_kernel,
        grid=(b, A_KV_HEADS, n_q // tq),
        in_specs=[pl.BlockSpec((1, tq, gw), lambda bi, h, i: (bi, q0 + i, h)),
                  pl.BlockSpec((1, n_k, HEAD_DIM), lambda bi, h, i: (bi, 0, h)),
                  pl.BlockSpec((1, n_k, HEAD_DIM), lambda bi, h, i: (bi, 0, h))],
        out_specs=pl.BlockSpec((1, tq, gw), lambda bi, h, i: (bi, i, h)),
        out_shape=jax.ShapeDtypeStruct((b, n_q, QA_W), BF16),
        compiler_params=_params(("parallel", "parallel", "arbitrary")),
        name="gqa_attention",
    )(q, k, v)


def _diff_kernel(lam_ref, gain_ref, q_ref, k_ref, v_ref, o_ref, *, out_scale):
    lam = lam_ref[0, 0]
    parts = []
    for m in range(2):
        q = q_ref[0, :, m * HEAD_DIM:(m + 1) * HEAD_DIM]
        k = k_ref[0, :, m * HEAD_DIM:(m + 1) * HEAD_DIM]
        s = lax.dot_general(q, k, (((1,), (1,)), ((), ())), preferred_element_type=F32)
        p, l = _softmax_rows(s)
        parts.append(p / l)
    a = parts[0] - lam * parts[1]
    y = jnp.dot(a.astype(BF16), v_ref[0], preferred_element_type=F32)
    y = y * lax.rsqrt(jnp.mean(y * y, axis=-1, keepdims=True) + NORM_EPS)
    o_ref[0] = (y * gain_ref[...] * out_scale).astype(o_ref.dtype)


def _diff_attention(q, k, v, lam, subln_gain, out_scale, q_row0, n_q, n_k):
    b = q.shape[0]
    tq = _pick(n_q, (256, 128))
    assert q_row0 % tq == 0
    q0 = q_row0 // tq
    hw = 2 * HEAD_DIM
    return pl.pallas_call(
        functools.partial(_diff_kernel, out_scale=out_scale),
        grid=(b, B_HEADS, n_q // tq),
        in_specs=[pl.BlockSpec(memory_space=pltpu.SMEM),
                  pl.BlockSpec((1, hw), lambda bi, h, i: (0, 0)),
                  pl.BlockSpec((1, tq, hw), lambda bi, h, i: (bi, q0 + i, h)),
                  pl.BlockSpec((1, n_k, hw), lambda bi, h, i: (bi, 0, h)),
                  pl.BlockSpec((1, n_k, hw), lambda bi, h, i: (bi, 0, h))],
        out_specs=pl.BlockSpec((1, tq, hw), lambda bi, h, i: (bi, i, h)),
        out_shape=jax.ShapeDtypeStruct((b, n_q, QB_W), BF16),
        compiler_params=_params(("parallel", "parallel", "arbitrary")),
        name="diff_attention",
    )(lam, subln_gain, q, k, v)


def _split3(a):
    a1 = a.astype(BF16)
    r1 = a - a1.astype(F32)
    a2 = r1.astype(BF16)
    a3 = (r1 - a2.astype(F32)).astype(BF16)
    return a1, a2, a3


def _dot_exact_rhs(a, rhs01):
    out = None
    for part in _split3(a):
        d = jnp.dot(part, rhs01, preferred_element_type=F32)
        out = d if out is None else out + d
    return out


def _dot_exact_lhs(lhs01, a):
    out = None
    for part in _split3(a):
        d = jnp.dot(lhs01, part, preferred_element_type=F32)
        out = d if out is None else out + d
    return out


def _ssd_kernel(acr_ref, acc_ref, skip_ref, x_ref, b_ref, c_ref, dtc_ref, dtr_ref, y_ref, state_ref):
    d = pl.program_id(1)
    step = pl.program_id(3)
    q = SSD_CHUNK

    @pl.when(step == 0)
    def _():
        state_ref[...] = jnp.zeros_like(state_ref)

    rows = lax.broadcasted_iota(jnp.int32, (q, q), 0)
    cols = lax.broadcasted_iota(jnp.int32, (q, q), 1)
    signed = (rows - cols) * (1 - 2 * d)
    keep = signed >= 0
    tri_lhs01 = jnp.where(keep, 1.0, 0.0).astype(BF16)
    tri_rhs01 = jnp.where(signed <= 0, 1.0, 0.0).astype(BF16)

    x = x_ref[0]
    bm = b_ref[0].astype(BF16)
    cm = c_ref[0].astype(BF16)
    dt_c = dtc_ref[0, 0, 0]
    dt_r = dtr_ref[0, 0, 0]
    a_c = dt_c * acr_ref[0, 0]
    a_r = dt_r * acc_ref[0, 0]
    cum_c = _dot_exact_lhs(tri_lhs01, a_c)
    cum_r = _dot_exact_rhs(a_r, tri_rhs01)
    tot_c = jnp.sum(a_c, axis=0, keepdims=True)
    to_end_c = jnp.exp(tot_c - cum_c)
    from_start_c = jnp.exp(cum_c)
    tot_e = jnp.exp(tot_c)

    g_mat = lax.dot_general(cm, bm, (((1,), (1,)), ((), ())), preferred_element_type=F32)
    state = state_ref[...]
    y_off = jnp.dot(cm, state.astype(BF16), preferred_element_type=F32)
    skip = skip_ref[0, 0]

    for h in range(SSD_HPG):
        sl = slice(h * SSD_HEADDIM, (h + 1) * SSD_HEADDIM)
        xh = x[:, sl]
        xd = xh * dt_c[:, h:h + 1]
        seg = cum_c[:, h:h + 1] - cum_r[h:h + 1, :]
        decay = jnp.where(keep, jnp.exp(jnp.where(keep, seg, 0.0)), 0.0)
        m_h = (g_mat * decay).astype(BF16)
        y_h = jnp.dot(m_h, xd.astype(BF16), preferred_element_type=F32)
        y_h = y_h + y_off[:, sl] * from_start_c[:, h:h + 1] + skip[:, h:h + 1] * xh
        y_ref[0, 0, :, sl] = y_h.astype(y_ref.dtype)
        xw = (xd * to_end_c[:, h:h + 1]).astype(BF16)
        s_h = lax.dot_general(bm, xw, (((0,), (0,)), ((), ())), preferred_element_type=F32)
        state_ref[:, sl] = state[:, sl] * tot_e[:, h:h + 1] + s_h


def _ssd_scan(xs, bm, cm, dt_c, dt_r, a_row, a_col, skip, n_ctx, out_dtype):
    b, t = xs.shape[:2]
    q = SSD_CHUNK
    ncc = n_ctx // q
    nch = t // q

    def chunk(d, s):
        back = jnp.where(s < ncc, ncc - 1 - s, nch - 1 - s + ncc)
        return jnp.where(d == 0, s, back)

    return pl.pallas_call(
        _ssd_kernel,
        grid=(b, 2, SSD_GROUPS, nch),
        in_specs=[pl.BlockSpec((1, 1, 1, SSD_HPG), lambda bi, d, g, s: (d, g, 0, 0)),
                  pl.BlockSpec((1, 1, SSD_HPG, 1), lambda bi, d, g, s: (d, g, 0, 0)),
                  pl.BlockSpec((1, 1, 1, SSD_HPG), lambda bi, d, g, s: (d, g, 0, 0)),
                  pl.BlockSpec((1, q, SSD_GROUP_W), lambda bi, d, g, s: (bi, chunk(d, s), g)),
                  pl.BlockSpec((1, q, SSD_STATE), lambda bi, d, g, s: (bi, chunk(d, s), g)),
                  pl.BlockSpec((1, q, SSD_STATE), lambda bi, d, g, s: (bi, chunk(d, s), g)),
                  pl.BlockSpec((1, 1, 1, q, SSD_HPG), lambda bi, d, g, s: (d, bi, g, chunk(d, s), 0)),
                  pl.BlockSpec((1, 1, 1, SSD_HPG, q), lambda bi, d, g, s: (d, bi, g, 0, chunk(d, s)))],
        out_specs=pl.BlockSpec((1, 1, q, SSD_GROUP_W), lambda bi, d, g, s: (d, bi, chunk(d, s), g)),
        out_shape=jax.ShapeDtypeStruct((2, b, t, SSD_INNER), out_dtype),
        scratch_shapes=[pltpu.VMEM((SSD_STATE, SSD_GROUP_W), F32)],
        compiler_params=_params(("parallel", "parallel", "parallel", "arbitrary")),
        name="ssd_scan",
    )(a_row, a_col, skip, xs, bm, cm, dt_c, dt_r)


def _expert_kernel(be_ref, x_ref, g_ref, wg_ref, wu_ref, wd_ref, o_ref):
    x = x_ref[...]
    hg = jnp.dot(x, wg_ref[0], preferred_element_type=F32)
    hu = jnp.dot(x, wu_ref[0], preferred_element_type=F32)
    h = (hg * jax.nn.sigmoid(hg) * hu).astype(BF16)
    y = jnp.dot(h, wd_ref[0], preferred_element_type=F32)
    o_ref[...] = (y * g_ref[...]).astype(o_ref.dtype)


def _expert_blocks(block_expert, x_rows, gate_rows, w_gate, w_up, w_down):
    n_rows, d = x_rows.shape
    tm = EXPERT_BLOCK
    ff = w_gate.shape[-1]
    grid_spec = pltpu.PrefetchScalarGridSpec(
        num_scalar_prefetch=1,
        grid=(n_rows // tm,),
        in_specs=[pl.BlockSpec((tm, d), lambda i, be: (i, 0)),
                  pl.BlockSpec((tm, 1), lambda i, be: (i, 0)),
                  pl.BlockSpec((1, d, ff), lambda i, be: (be[i], 0, 0)),
                  pl.BlockSpec((1, d, ff), lambda i, be: (be[i], 0, 0)),
                  pl.BlockSpec((1, ff, d), lambda i, be: (be[i], 0, 0))],
        out_specs=pl.BlockSpec((tm, d), lambda i, be: (i, 0)),
    )
    return pl.pallas_call(
        _expert_kernel,
        grid_spec=grid_spec,
        out_shape=jax.ShapeDtypeStruct((n_rows, d), F32),
        compiler_params=_params(("arbitrary",)),
        name="moe_experts",
    )(block_expert, x_rows, gate_rows, w_gate, w_up, w_down)


def _swiglu_kernel(x_ref, wg_ref, wu_ref, wd_ref, o_ref):
    x = x_ref[...]
    hg = jnp.dot(x, wg_ref[...], preferred_element_type=F32)
    hu = jnp.dot(x, wu_ref[...], preferred_element_type=F32)
    h = (hg * jax.nn.sigmoid(hg) * hu).astype(BF16)
    o_ref[...] = jnp.dot(h, wd_ref[...], preferred_element_type=F32).astype(o_ref.dtype)


def _shared_expert(x, wg, wu, wd):
    m, d = x.shape
    ff = wg.shape[1]
    tm = _pick(m, (512, 384, 256, 128))
    return pl.pallas_call(
        _swiglu_kernel,
        grid=(m // tm,),
        in_specs=[pl.BlockSpec((tm, d), lambda i: (i, 0)),
                  pl.BlockSpec((d, ff), lambda i: (0, 0)),
                  pl.BlockSpec((d, ff), lambda i: (0, 0)),
                  pl.BlockSpec((ff, d), lambda i: (0, 0))],
        out_specs=pl.BlockSpec((tm, d), lambda i: (i, 0)),
        out_shape=jax.ShapeDtypeStruct((m, d), F32),
        compiler_params=_params(("parallel",)),
        name="shared_expert",
    )(x, wg, wu, wd)


def _router_kernel(x_ref, w_ref, o_ref):
    o_ref[...] = jnp.dot(x_ref[...], w_ref[...], preferred_element_type=F32,
                         precision=lax.Precision.HIGHEST)


def _router_logits(x, w):
    m, d = x.shape
    n = w.shape[1]
    tm = _pick(m, (512, 384, 256, 128))
    return pl.pallas_call(
        _router_kernel,
        grid=(m // tm,),
        in_specs=[pl.BlockSpec((tm, d), lambda i: (i, 0)),
                  pl.BlockSpec((d, n), lambda i: (0, 0))],
        out_specs=pl.BlockSpec((tm, n), lambda i: (i, 0)),
        out_shape=jax.ShapeDtypeStruct((m, n), F32),
        compiler_params=_params(("parallel",)),
        name="router",
    )(x, w)


def _moe_ffn(t, router_w, router_bias, w_gate, w_up, w_down, ws_gate, ws_up, ws_down):
    n_tok = t.shape[0]
    scores = jax.nn.sigmoid(_router_logits(t, router_w))
    sel = scores + router_bias.astype(F32)
    per_group = N_EXPERTS // N_EXPERT_GROUPS
    group_score = lax.top_k(sel.reshape(n_tok, N_EXPERT_GROUPS, per_group), 2)[0].sum(-1)
    _, top_groups = lax.top_k(group_score, TOPK_GROUPS)
    group_ok = jnp.any(top_groups[:, :, None] == jnp.arange(N_EXPERT_GROUPS)[None, None, :], axis=1)
    sel = jnp.where(jnp.repeat(group_ok, per_group, axis=1), sel, -jnp.inf)
    _, eidx = lax.top_k(sel, TOP_K)
    w = jnp.take_along_axis(scores, eidx, axis=-1)
    gates = w / jnp.sum(w, axis=-1, keepdims=True) * ROUTED_SCALE

    n_assign = n_tok * TOP_K
    flat_e = eidx.reshape(-1)
    flat_tok = jnp.arange(n_assign, dtype=jnp.int32) // TOP_K
    order = jnp.argsort(flat_e)
    e_sorted = flat_e[order]
    counts = jnp.bincount(flat_e, length=N_EXPERTS)
    padded = (counts + EXPERT_BLOCK - 1) // EXPERT_BLOCK * EXPERT_BLOCK
    pad_end = jnp.cumsum(padded)
    pad_start = pad_end - padded
    start = jnp.cumsum(counts) - counts
    dest = pad_start[e_sorted] + (jnp.arange(n_assign, dtype=jnp.int32) - start[e_sorted])
    n_blocks = -(-n_assign // EXPERT_BLOCK) + N_EXPERTS
    n_rows = n_blocks * EXPERT_BLOCK
    row_tok = jnp.zeros((n_rows,), jnp.int32).at[dest].set(flat_tok[order])
    row_gate = jnp.zeros((n_rows,), F32).at[dest].set(gates.reshape(-1)[order])
    block_expert = jnp.minimum(jnp.searchsorted(pad_end, jnp.arange(n_blocks, dtype=jnp.int32) * EXPERT_BLOCK,
                                                side='right'), N_EXPERTS - 1).astype(jnp.int32)
    tb = t.astype(BF16)
    y = _expert_blocks(block_expert, tb[row_tok], row_gate[:, None],
                       w_gate.astype(BF16), w_up.astype(BF16), w_down.astype(BF16))
    routed = jax.ops.segment_sum(y, row_tok, num_segments=n_tok)
    return routed + _shared_expert(tb, ws_gate.astype(BF16), ws_up.astype(BF16), ws_down.astype(BF16))


def _rms(u, gain):
    return u * lax.rsqrt(jnp.mean(u * u, axis=-1, keepdims=True) + NORM_EPS) * gain


def _rope_tables(n_ctx, n_lat):
    rows = n_lat // GRID_W
    row = jnp.repeat(jnp.arange(rows, dtype=F32), GRID_W)
    col = jnp.tile(jnp.arange(GRID_W, dtype=F32), rows)
    n_freq = HEAD_DIM // 4
    inv = ROPE_THETA ** (-jnp.arange(n_freq, dtype=F32) / n_freq)
    ang = jnp.concatenate([row[:, None] * inv, col[:, None] * inv], axis=-1)
    cos = jnp.repeat(jnp.cos(ang), 2, axis=-1)
    sin = jnp.repeat(jnp.sin(ang), 2, axis=-1)
    sign = jnp.tile(jnp.array([-1.0, 1.0], F32), HEAD_DIM // 2)
    cos = jnp.concatenate([jnp.ones((n_ctx, HEAD_DIM), F32), cos], axis=0)
    sin = jnp.concatenate([jnp.zeros((n_ctx, HEAD_DIM), F32), sin * sign], axis=0)
    return cos, sin


def _rope(u, cos, sin):
    up = u.reshape(u.shape[:-1] + (HEAD_DIM // 2, 2))
    partner = jnp.flip(up, axis=-1).reshape(u.shape)
    return u * cos[None, :, None, :] + partner * sin[None, :, None, :]


def _ada_mod(cond, w, bias):
    m = jax.nn.silu(cond)
    rows = m.shape[0]
    pad = (-rows) % 16
    mp = jnp.pad(m, ((0, pad), (0, 0))).astype(BF16)
    out = _matmul(mp, w.astype(BF16), F32)[:rows] + bias
    return out.reshape(rows, N_MOD, -1)


def _attention_layer(h, n_ctx, w_in, w_out, q_gain, k_gain, lq1, lk1, lq2, lk2, subln, lambda_init):
    b, t, d = h.shape
    n_lat = t - n_ctx
    proj = _matmul(h.reshape(b * t, d).astype(BF16), w_in.astype(BF16), F32).reshape(b, t, ATTN_IN)
    o = 0
    qa = proj[..., o:o + QA_W]; o += QA_W
    ka = proj[..., o:o + KA_W]; o += KA_W
    va = proj[..., o:o + KA_W]; o += KA_W
    qb = proj[..., o:o + QB_W]; o += QB_W
    kb = proj[..., o:o + QB_W]; o += QB_W
    vb = proj[..., o:o + QB_W]
    cos, sin = _rope_tables(n_ctx, n_lat)
    scale = HEAD_DIM ** -0.5
    qa = _rope(_rms(qa.reshape(b, t, A_Q_HEADS, HEAD_DIM), q_gain), cos, sin) * scale
    ka = _rope(_rms(ka.reshape(b, t, A_KV_HEADS, HEAD_DIM), k_gain), cos, sin)
    qb = _rope(qb.reshape(b, t, 2 * B_HEADS, HEAD_DIM), cos, sin) * scale
    kb = _rope(kb.reshape(b, t, 2 * B_HEADS, HEAD_DIM), cos, sin)
    qa = qa.reshape(b, t, QA_W).astype(BF16)
    ka = ka.reshape(b, t, KA_W).astype(BF16)
    va = va.astype(BF16)
    qb = qb.reshape(b, t, QB_W).astype(BF16)
    kb = kb.reshape(b, t, QB_W).astype(BF16)
    vb = vb.astype(BF16)
    lam = (jnp.exp(jnp.sum(lq1 * lk1)) - jnp.exp(jnp.sum(lq2 * lk2)) + lambda_init).reshape(1, 1).astype(F32)
    gain = subln.reshape(1, 2 * HEAD_DIM)
    out_scale = 1.0 - lambda_init
    ya_l = _gqa_attention(qa, ka, va, n_ctx, n_lat, t)
    yb_l = _diff_attention(qb, kb, vb, lam, gain, out_scale, n_ctx, n_lat, t)
    ya_c = _gqa_attention(qa, ka, va, 0, n_ctx, n_ctx)
    yb_c = _diff_attention(qb, kb, vb, lam, gain, out_scale, 0, n_ctx, n_ctx)
    y = jnp.concatenate([jnp.concatenate([ya_c, yb_c], axis=-1),
                         jnp.concatenate([ya_l, yb_l], axis=-1)], axis=1)
    return _matmul(y.reshape(b * t, -1), w_out.astype(BF16), F32).reshape(b, t, d)


def _conv_silu(u, w, bias):
    up = jnp.pad(u, ((0, 0), (1, 1), (0, 0)))
    out = up[:, :-2] * w[0] + up[:, 1:-1] * w[1] + up[:, 2:] * w[2] + bias
    return jax.nn.silu(out)


def _ssd_layer(h, n_ctx, w_in, conv_w, conv_b, dt_bias, a_log, d_skip, norm_gain, w_out):
    b, t, d = h.shape
    proj = _matmul(h.reshape(b * t, d).astype(BF16), w_in.astype(BF16), F32).reshape(b, t, -1)
    z = proj[..., :SSD_INNER]
    xbc = proj[..., SSD_INNER:SSD_INNER + SSD_CONV_DIM]
    dt = proj[..., SSD_INNER + SSD_CONV_DIM:]
    xbc = jnp.concatenate([_conv_silu(xbc[:, :n_ctx], conv_w, conv_b),
                           _conv_silu(xbc[:, n_ctx:], conv_w, conv_b)], axis=1)
    xs = xbc[..., :SSD_INNER]
    bm = xbc[..., SSD_INNER:SSD_INNER + SSD_GROUPS * SSD_STATE]
    cm = xbc[..., SSD_INNER + SSD_GROUPS * SSD_STATE:]
    dt = jax.nn.softplus(dt.reshape(b, t, 2, SSD_GROUPS, SSD_HPG) + dt_bias.reshape(2, SSD_GROUPS, SSD_HPG))
    dt_c = jnp.transpose(dt, (2, 0, 3, 1, 4))
    dt_r = jnp.transpose(dt, (2, 0, 3, 4, 1))
    a_coef = -jnp.exp(a_log)
    a_row = a_coef.reshape(2, SSD_GROUPS, 1, SSD_HPG)
    a_col = a_coef.reshape(2, SSD_GROUPS, SSD_HPG, 1)
    skip = d_skip.reshape(2, SSD_GROUPS, 1, SSD_HPG)
    y = _ssd_scan(xs, bm, cm, dt_c, dt_r, a_row, a_col, skip, n_ctx, F32)
    y = (y[0] + y[1])[:, n_ctx:]
    zl = z[:, n_ctx:]
    u = _rms(y * jax.nn.silu(zl), norm_gain)
    n_lat = t - n_ctx
    return _matmul(u.reshape(b * n_lat, SSD_INNER).astype(BF16), w_out.astype(BF16), F32).reshape(b, n_lat, d)


def kernel(x, c, ctx, c_ctx, mod_w, mod_b, norm_mix, norm_ffn, norm_final, attn_w_in, attn_w_out, attn_q_gain,
           attn_k_gain, diff_lam_q1, diff_lam_k1, diff_lam_q2, diff_lam_k2, diff_subln, ssd_w_in, ssd_conv_w,
           ssd_conv_b, ssd_dt_bias, ssd_a_log, ssd_d, ssd_norm, ssd_w_out, router_w, router_bias, exp_w_gate,
           exp_w_up, exp_w_down, shared_w_gate, shared_w_up, shared_w_down):
    b, s, d = x.shape
    n_ctx = ctx.shape[1]
    t = n_ctx + s
    xa = jnp.concatenate([ctx, x], axis=1)
    cond = jnp.concatenate([c, c_ctx[None]], axis=0)
    for i in range(DEPTH):
        last = i == DEPTH - 1
        mod = _ada_mod(cond, mod_w[i], mod_b[i])
        mod_bt = jnp.concatenate([jnp.broadcast_to(mod[b:, None], (1, n_ctx, N_MOD, d)).repeat(b, axis=0),
                                  jnp.broadcast_to(mod[:b, None], (b, s, N_MOD, d))], axis=1)
        sh1, sc1, g1, sh2, sc2, g2 = [mod_bt[:, :, k] for k in range(N_MOD)]
        h = _rms(xa, norm_mix[i]) * (1.0 + sc1) + sh1
        j = i // 2
        if i % 2 == 0:
            lambda_init = 0.8 - 0.6 * math.exp(-0.3 * i)
            y = _attention_layer(h, n_ctx, attn_w_in[j], attn_w_out[j], attn_q_gain[j], attn_k_gain[j],
                                 diff_lam_q1[j], diff_lam_k1[j], diff_lam_q2[j], diff_lam_k2[j],
                                 diff_subln[j], lambda_init)
        else:
            y_lat = _ssd_layer(h, n_ctx, ssd_w_in[j], ssd_conv_w[j], ssd_conv_b[j], ssd_dt_bias[j],
                               ssd_a_log[j], ssd_d[j], ssd_norm[j], ssd_w_out[j])
            y = jnp.concatenate([jnp.zeros((b, n_ctx, d), F32), y_lat], axis=1)
        xa = xa + g1 * y
        f = _rms(xa, norm_ffn[i]) * (1.0 + sc2) + sh2
        if last:
            fl = f[:, n_ctx:].reshape(b * s, d)
            o = _moe_ffn(fl, router_w[i], router_bias[i], exp_w_gate[i], exp_w_up[i], exp_w_down[i],
                         shared_w_gate[i], shared_w_up[i], shared_w_down[i]).reshape(b, s, d)
            o = jnp.concatenate([jnp.zeros((b, n_ctx, d), F32), o], axis=1)
        else:
            o = _moe_ffn(f.reshape(b * t, d), router_w[i], router_bias[i], exp_w_gate[i], exp_w_up[i],
                         exp_w_down[i], shared_w_gate[i], shared_w_up[i], shared_w_down[i]).reshape(b, t, d)
        xa = xa + g2 * o
    return _rms(xa[:, n_ctx:], norm_final)
```

```python
import functools
import math

import jax
import jax.numpy as jnp
from jax import lax
from jax.experimental import pallas as pl
from jax.experimental.pallas import tpu as pltpu

F32 = jnp.float32
BF16 = jnp.bfloat16

D_MODEL = 2048
DEPTH = 2
GRID_W = 64
NORM_EPS = 1e-6
N_MOD = 6
HEAD_DIM = 128
ROPE_THETA = 10000.0
A_Q_HEADS = 8
A_KV_HEADS = 2
A_GROUP = A_Q_HEADS // A_KV_HEADS
B_HEADS = 4
QA_W = A_Q_HEADS * HEAD_DIM
KA_W = A_KV_HEADS * HEAD_DIM
QB_W = 2 * B_HEADS * HEAD_DIM
ATTN_IN = QA_W + 2 * KA_W + 3 * QB_W
SSD_INNER = 2 * D_MODEL
SSD_HEADDIM = 64
SSD_HEADS = SSD_INNER // SSD_HEADDIM
SSD_GROUPS = 8
SSD_HPG = SSD_HEADS // SSD_GROUPS
SSD_STATE = 128
SSD_CONV = 3
SSD_CHUNK = 128
SSD_GROUP_W = SSD_HPG * SSD_HEADDIM
SSD_CONV_DIM = SSD_INNER + 2 * SSD_GROUPS * SSD_STATE
N_EXPERTS = 64
EXPERT_FF = 512
TOP_K = 8
N_EXPERT_GROUPS = 8
TOPK_GROUPS = 4
ROUTED_SCALE = 2.5
EXPERT_TM = 256

VMEM_LIMIT_BYTES = 56 * 1024 * 1024


def _pick(n, prefs):
    for p in prefs:
        if n % p == 0:
            return p
    raise ValueError(f"no tile in {prefs} divides {n}")


def _params(sem):
    return pltpu.CompilerParams(dimension_semantics=sem, vmem_limit_bytes=VMEM_LIMIT_BYTES)


def _mm_kernel(x_ref, w_ref, o_ref):
    o_ref[...] = jnp.dot(x_ref[...], w_ref[...], preferred_element_type=F32).astype(o_ref.dtype)


def _matmul(x, w, out_dtype):
    m, k = x.shape
    n = w.shape[1]
    tm = _pick(m, (512, 384, 256, 128, 16, 8))
    tn = _pick(n, (512, 384, 256, 128))
    return pl.pallas_call(
        _mm_kernel,
        grid=(m // tm, n // tn),
        in_specs=[pl.BlockSpec((tm, k), lambda i, j: (i, 0)),
                  pl.BlockSpec((k, tn), lambda i, j: (0, j))],
        out_specs=pl.BlockSpec((tm, tn), lambda i, j: (i, j)),
        out_shape=jax.ShapeDtypeStruct((m, n), out_dtype),
        compiler_params=_params(("parallel", "arbitrary")),
        name="matmul",
    )(x, w)


def _softmax_rows(s):
    m = jnp.max(s, axis=-1, keepdims=True)
    p = jnp.exp(s - m)
    return p, jnp.sum(p, axis=-1, keepdims=True)


def _gqa_kernel(q_ref, k_ref, v_ref, o_ref):
    k = k_ref[0]
    v = v_ref[0]
    for g in range(A_GROUP):
        q = q_ref[0, :, g * HEAD_DIM:(g + 1) * HEAD_DIM]
        s = lax.dot_general(q, k, (((1,), (1,)), ((), ())), preferred_element_type=F32)
        p, l = _softmax_rows(s)
        o = jnp.dot(p.astype(BF16), v, preferred_element_type=F32) / l
        o_ref[0, :, g * HEAD_DIM:(g + 1) * HEAD_DIM] = o.astype(o_ref.dtype)


def _gqa_attention(q, k, v, q_row0, n_q, n_k):
    b = q.shape[0]
    tq = _pick(n_q, (256, 128))
    assert q_row0 % tq == 0
    q0 = q_row0 // tq
    gw = A_GROUP * HEAD_DIM
    return pl.pallas_call(
        _gqa_kernel,
        grid=(b, A_KV_HEADS, n_q // tq),
        in_specs=[pl.BlockSpec((1, tq, gw), lambda bi, h, i: (bi, q0 + i, h)),
                  pl.BlockSpec((1, n_k, HEAD_DIM), lambda bi, h, i: (bi, 0, h)),
                  pl.BlockSpec((1, n_k, HEAD_DIM), lambda bi, h, i: (bi, 0, h))],
        out_specs=pl.BlockSpec((1, tq, gw), lambda bi, h, i: (bi, i, h)),
        out_shape=jax.ShapeDtypeStruct((b, n_q, QA_W), BF16),
        compiler_params=_params(("parallel", "parallel", "arbitrary")),
        name="gqa_attention",
    )(q, k, v)


def _diff_kernel(lam_ref, gain_ref, q_ref, k_ref, v_ref, o_ref, *, out_scale):
    lam = lam_ref[0, 0]
    parts = []
    for m in range(2):
        q = q_ref[0, :, m * HEAD_DIM:(m + 1) * HEAD_DIM]
        k = k_ref[0, :, m * HEAD_DIM:(m + 1) * HEAD_DIM]
        s = lax.dot_general(q, k, (((1,), (1,)), ((), ())), preferred_element_type=F32)
        p, l = _softmax_rows(s)
        parts.append(p / l)
    a = parts[0] - lam * parts[1]
    y = jnp.dot(a.astype(BF16), v_ref[0], preferred_element_type=F32)
    y = y * lax.rsqrt(jnp.mean(y * y, axis=-1, keepdims=True) + NORM_EPS)
    o_ref[0] = (y * gain_ref[...] * out_scale).astype(o_ref.dtype)


def _diff_attention(q, k, v, lam, subln_gain, out_scale, q_row0, n_q, n_k):
    b = q.shape[0]
    tq = _pick(n_q, (256, 128))
    assert q_row0 % tq == 0
    q0 = q_row0 // tq
    hw = 2 * HEAD_DIM
    return pl.pallas_call(
        functools.partial(_diff_kernel, out_scale=out_scale),
        grid=(b, B_HEADS, n_q // tq),
        in_specs=[pl.BlockSpec(memory_space=pltpu.SMEM),
                  pl.BlockSpec((1, hw), lambda bi, h, i: (0, 0)),
                  pl.BlockSpec((1, tq, hw), lambda bi, h, i: (bi, q0 + i, h)),
                  pl.BlockSpec((1, n_k, hw), lambda bi, h, i: (bi, 0, h)),
                  pl.BlockSpec((1, n_k, hw), lambda bi, h, i: (bi, 0, h))],
        out_specs=pl.BlockSpec((1, tq, hw), lambda bi, h, i: (bi, i, h)),
        out_shape=jax.ShapeDtypeStruct((b, n_q, QB_W), BF16),
        compiler_params=_params(("parallel", "parallel", "arbitrary")),
        name="diff_attention",
    )(lam, subln_gain, q, k, v)


def _split3(a):
    a1 = a.astype(BF16)
    r1 = a - a1.astype(F32)
    a2 = r1.astype(BF16)
    a3 = (r1 - a2.astype(F32)).astype(BF16)
    return a1, a2, a3


def _dot_exact_rhs(a, rhs01):
    out = None
    for part in _split3(a):
        d = jnp.dot(part, rhs01, preferred_element_type=F32)
        out = d if out is None else out + d
    return out


def _dot_exact_lhs(lhs01, a):
    out = None
    for part in _split3(a):
        d = jnp.dot(lhs01, part, preferred_element_type=F32)
        out = d if out is None else out + d
    return out


def _ssd_kernel(acr_ref, acc_ref, skip_ref, x_ref, b_ref, c_ref, dtc_ref, dtr_ref, y_ref, state_ref):
    d = pl.program_id(1)
    step = pl.program_id(3)
    q = SSD_CHUNK

    @pl.when(step == 0)
    def _():
        state_ref[...] = jnp.zeros_like(state_ref)

    rows = lax.broadcasted_iota(jnp.int32, (q, q), 0)
    cols = lax.broadcasted_iota(jnp.int32, (q, q), 1)
    signed = (rows - cols) * (1 - 2 * d)
    keep = signed >= 0
    tri_lhs01 = jnp.where(keep, 1.0, 0.0).astype(BF16)
    tri_rhs01 = jnp.where(signed <= 0, 1.0, 0.0).astype(BF16)

    x = x_ref[0]
    bm = b_ref[0].astype(BF16)
    cm = c_ref[0].astype(BF16)
    dt_c = dtc_ref[0, 0, 0]
    dt_r = dtr_ref[0, 0, 0]
    a_c = dt_c * acr_ref[0, 0]
    a_r = dt_r * acc_ref[0, 0]
    cum_c = _dot_exact_lhs(tri_lhs01, a_c)
    cum_r = _dot_exact_rhs(a_r, tri_rhs01)
    tot_c = jnp.sum(a_c, axis=0, keepdims=True)
    to_end_c = jnp.exp(tot_c - cum_c)
    from_start_c = jnp.exp(cum_c)
    tot_e = jnp.exp(tot_c)

    g_mat = lax.dot_general(cm, bm, (((1,), (1,)), ((), ())), preferred_element_type=F32)
    state = state_ref[...]
    y_off = jnp.dot(cm, state.astype(BF16), preferred_element_type=F32)
    skip = skip_ref[0, 0]

    for h in range(SSD_HPG):
        sl = slice(h * SSD_HEADDIM, (h + 1) * SSD_HEADDIM)
        xh = x[:, sl]
        xd = xh * dt_c[:, h:h + 1]
        seg = cum_c[:, h:h + 1] - cum_r[h:h + 1, :]
        decay = jnp.where(keep, jnp.exp(jnp.where(keep, seg, 0.0)), 0.0)
        m_h = (g_mat * decay).astype(BF16)
        y_h = jnp.dot(m_h, xd.astype(BF16), preferred_element_type=F32)
        y_h = y_h + y_off[:, sl] * from_start_c[:, h:h + 1] + skip[:, h:h + 1] * xh
        y_ref[0, 0, :, sl] = y_h.astype(y_ref.dtype)
        xw = (xd * to_end_c[:, h:h + 1]).astype(BF16)
        s_h = lax.dot_general(bm, xw, (((0,), (0,)), ((), ())), preferred_element_type=F32)
        state_ref[:, sl] = state[:, sl] * tot_e[:, h:h + 1] + s_h


def _ssd_scan(xs, bm, cm, dt_c, dt_r, a_row, a_col, skip, n_ctx, out_dtype):
    b, t = xs.shape[:2]
    q = SSD_CHUNK
    ncc = n_ctx // q
    nch = t // q

    def chunk(d, s):
        back = jnp.where(s < ncc, ncc - 1 - s, nch - 1 - s + ncc)
        return jnp.where(d == 0, s, back)

    return pl.pallas_call(
        _ssd_kernel,
        grid=(b, 2, SSD_GROUPS, nch),
        in_specs=[pl.BlockSpec((1, 1, 1, SSD_HPG), lambda bi, d, g, s: (d, g, 0, 0)),
                  pl.BlockSpec((1, 1, SSD_HPG, 1), lambda bi, d, g, s: (d, g, 0, 0)),
                  pl.BlockSpec((1, 1, 1, SSD_HPG), lambda bi, d, g, s: (d, g, 0, 0)),
                  pl.BlockSpec((1, q, SSD_GROUP_W), lambda bi, d, g, s: (bi, chunk(d, s), g)),
                  pl.BlockSpec((1, q, SSD_STATE), lambda bi, d, g, s: (bi, chunk(d, s), g)),
                  pl.BlockSpec((1, q, SSD_STATE), lambda bi, d, g, s: (bi, chunk(d, s), g)),
                  pl.BlockSpec((1, 1, 1, q, SSD_HPG), lambda bi, d, g, s: (d, bi, g, chunk(d, s), 0)),
                  pl.BlockSpec((1, 1, 1, SSD_HPG, q), lambda bi, d, g, s: (d, bi, g, 0, chunk(d, s)))],
        out_specs=pl.BlockSpec((1, 1, q, SSD_GROUP_W), lambda bi, d, g, s: (d, bi, chunk(d, s), g)),
        out_shape=jax.ShapeDtypeStruct((2, b, t, SSD_INNER), out_dtype),
        scratch_shapes=[pltpu.VMEM((SSD_STATE, SSD_GROUP_W), F32)],
        compiler_params=_params(("parallel", "parallel", "parallel", "arbitrary")),
        name="ssd_scan",
    )(a_row, a_col, skip, xs, bm, cm, dt_c, dt_r)


def _route_kernel(f_ref, rwt_ref, bias_ref, idx_ref, gate_ref, rank_ref, cnt_ref, carry_ref):
    tn = f_ref.shape[0]
    ne, ng, pg = N_EXPERTS, N_EXPERT_GROUPS, N_EXPERTS // N_EXPERT_GROUPS
    neg = -jnp.inf

    @pl.when(pl.program_id(0) == 0)
    def _():
        carry_ref[...] = jnp.zeros_like(carry_ref)

    logits = lax.dot_general(rwt_ref[...], f_ref[...], (((1,), (1,)), ((), ())),
                             preferred_element_type=F32, precision=lax.Precision.HIGHEST)
    scores = jax.nn.sigmoid(logits)
    g3 = (scores + bias_ref[...]).reshape(ng, pg, tn)
    io3 = lax.broadcasted_iota(jnp.int32, (ng, pg, tn), 1)
    m1 = jnp.max(g3, axis=1, keepdims=True)
    i1 = jnp.min(jnp.where(g3 == m1, io3, pg), axis=1, keepdims=True)
    m2 = jnp.max(jnp.where(io3 == i1, neg, g3), axis=1, keepdims=True)
    work = (m1 + m2).reshape(ng, tn)
    iog = lax.broadcasted_iota(jnp.int32, (ng, tn), 0)
    ok = jnp.zeros((ng, tn), F32)
    for _ in range(TOPK_GROUPS):
        m = jnp.max(work, axis=0, keepdims=True)
        gi = jnp.min(jnp.where(work == m, iog, ng), axis=0, keepdims=True)
        hit = iog == gi
        ok = jnp.where(hit, 1.0, ok)
        work = jnp.where(hit, neg, work)
    sel = jnp.where(ok.reshape(ng, 1, tn) > 0.0, g3, neg).reshape(ne, tn)
    ioe = lax.broadcasted_iota(jnp.int32, (ne, tn), 0)
    onehot = jnp.zeros((ne, tn), F32)
    idxs, ws = [], []
    for _ in range(TOP_K):
        m = jnp.max(sel, axis=0, keepdims=True)
        ei = jnp.min(jnp.where(sel == m, ioe, ne), axis=0, keepdims=True)
        hit = ioe == ei
        idxs.append(ei)
        ws.append(jnp.sum(jnp.where(hit, scores, 0.0), axis=0, keepdims=True))
        sel = jnp.where(hit, neg, sel)
        onehot = jnp.where(hit, 1.0, onehot)
    w = jnp.concatenate(ws, axis=0)
    gate_ref[...] = w / jnp.sum(w, axis=0, keepdims=True) * ROUTED_SCALE
    idx_ref[...] = jnp.concatenate(idxs, axis=0)
    r = lax.broadcasted_iota(jnp.int32, (tn, tn), 0)
    c = lax.broadcasted_iota(jnp.int32, (tn, tn), 1)
    ahead = jnp.where(r < c, 1.0, 0.0).astype(BF16)
    cum = carry_ref[...] + jnp.dot(onehot.astype(BF16), ahead, preferred_element_type=F32)
    ranks = [jnp.sum(jnp.where(ioe == idxs[k], cum, 0.0), axis=0, keepdims=True) for k in range(TOP_K)]
    rank_ref[...] = jnp.concatenate(ranks, axis=0).astype(jnp.int32)
    total = carry_ref[...] + jnp.sum(onehot, axis=1, keepdims=True)
    carry_ref[...] = total
    cnt_ref[...] = total.astype(jnp.int32)


def _route(f, router_wt, router_bias):
    t, d = f.shape
    tn = _pick(t, (512, 256, 128))
    kt = pl.BlockSpec((TOP_K, tn), lambda i: (0, i))
    return pl.pallas_call(
        _route_kernel,
        grid=(t // tn,),
        in_specs=[pl.BlockSpec((tn, d), lambda i: (i, 0)),
                  pl.BlockSpec((N_EXPERTS, d), lambda i: (0, 0)),
                  pl.BlockSpec((N_EXPERTS, 1), lambda i: (0, 0))],
        out_specs=[kt, kt, kt, pl.BlockSpec((N_EXPERTS, 1), lambda i: (0, 0))],
        out_shape=[jax.ShapeDtypeStruct((TOP_K, t), jnp.int32), jax.ShapeDtypeStruct((TOP_K, t), F32),
                   jax.ShapeDtypeStruct((TOP_K, t), jnp.int32), jax.ShapeDtypeStruct((N_EXPERTS, 1), jnp.int32)],
        scratch_shapes=[pltpu.VMEM((N_EXPERTS, 1), F32)],
        compiler_params=_params(("arbitrary",)),
        name="moe_route",
    )(f, router_wt, router_bias)


def _scatter_kernel(dest_ref, f_ref, xs_ref, sem):
    ts = f_ref.shape[0]

    def row_copy(t, k):
        return pltpu.make_async_copy(f_ref.at[t], xs_ref.at[dest_ref[t * TOP_K + k]], sem)

    def issue(t, carry):
        for k in range(TOP_K):
            row_copy(t, k).start()
        return carry

    def drain(t, carry):
        for k in range(TOP_K):
            row_copy(t, k).wait()
        return carry

    lax.fori_loop(0, ts, issue, 0)
    lax.fori_loop(0, ts, drain, 0)


def _scatter_rows(dest_flat, f3, n_rows_alloc):
    t = f3.shape[0]
    ts = _pick(t, (256, 128))
    return pl.pallas_call(
        _scatter_kernel,
        grid=(t // ts,),
        in_specs=[pl.BlockSpec((ts * TOP_K,), lambda i: (i,), memory_space=pltpu.SMEM),
                  pl.BlockSpec((ts,) + f3.shape[1:], lambda i: (i, 0, 0))],
        out_specs=pl.BlockSpec(memory_space=pl.ANY),
        out_shape=jax.ShapeDtypeStruct((n_rows_alloc,) + f3.shape[1:], f3.dtype),
        scratch_shapes=[pltpu.SemaphoreType.DMA(())],
        compiler_params=_params(("arbitrary",)),
        name="moe_scatter",
    )(dest_flat, f3)


def _zero_rows_kernel(rows_ref, xs_in_ref, xs_ref, zero_ref, sem):
    del xs_in_ref
    zero_ref[...] = jnp.zeros_like(zero_ref)
    n = rows_ref.shape[0]

    def row_copy(i):
        return pltpu.make_async_copy(zero_ref, xs_ref.at[rows_ref[i]], sem)

    def issue(i, carry):
        row_copy(i).start()
        return carry

    def drain(i, carry):
        row_copy(i).wait()
        return carry

    lax.fori_loop(0, n, issue, 0)
    lax.fori_loop(0, n, drain, 0)


def _zero_rows(rows, xs):
    n = rows.shape[0]
    step = _pick(n, (512, 256, 128))
    return pl.pallas_call(
        _zero_rows_kernel,
        grid=(n // step,),
        in_specs=[pl.BlockSpec((step,), lambda i: (i,), memory_space=pltpu.SMEM),
                  pl.BlockSpec(memory_space=pl.ANY)],
        out_specs=pl.BlockSpec(memory_space=pl.ANY),
        out_shape=jax.ShapeDtypeStruct(xs.shape, xs.dtype),
        scratch_shapes=[pltpu.VMEM(xs.shape[1:], xs.dtype), pltpu.SemaphoreType.DMA(())],
        input_output_aliases={1: 0},
        compiler_params=_params(("arbitrary",)),
        name="moe_zero_pad_rows",
    )(rows, xs)


def _expert_kernel(be_ref, blk_ref, nused_ref, x_ref, wg_ref, wu_ref, wd_ref, o_ref, wg_s, wu_s, wd_s):
    i = pl.program_id(0)
    n_slab = x_ref.shape[1]

    @pl.when(jnp.logical_or(i == 0, be_ref[i] != be_ref[jnp.maximum(i - 1, 0)]))
    def _():
        wg_s[...] = wg_ref[0].astype(BF16)
        wu_s[...] = wu_ref[0].astype(BF16)
        wd_s[...] = wd_ref[0].astype(BF16)

    @pl.when(i < nused_ref[0])
    def _():
        x = jnp.concatenate([x_ref[:, j, :] for j in range(n_slab)], axis=1).astype(BF16)
        hg = jnp.dot(x, wg_s[...], preferred_element_type=F32)
        hu = jnp.dot(x, wu_s[...], preferred_element_type=F32)
        h = (hg * jax.nn.sigmoid(hg) * hu).astype(BF16)
        y = jnp.dot(h, wd_s[...], preferred_element_type=F32)
        for j in range(n_slab):
            o_ref[:, j, :] = y[:, j * 128:(j + 1) * 128]


def _expert_blocks(block_expert, block_index, n_used, xs, n_blocks, tm, w_gate, w_up, w_down):
    d, ff = w_gate.shape[1:]
    slab = xs.shape[1:]
    grid_spec = pltpu.PrefetchScalarGridSpec(
        num_scalar_prefetch=3,
        grid=(n_blocks,),
        in_specs=[pl.BlockSpec((tm,) + slab, lambda i, be, blk, nu: (blk[i], 0, 0)),
                  pl.BlockSpec((1, d, ff), lambda i, be, blk, nu: (be[i], 0, 0)),
                  pl.BlockSpec((1, d, ff), lambda i, be, blk, nu: (be[i], 0, 0)),
                  pl.BlockSpec((1, ff, d), lambda i, be, blk, nu: (be[i], 0, 0))],
        out_specs=pl.BlockSpec((tm,) + slab, lambda i, be, blk, nu: (blk[i], 0, 0)),
        scratch_shapes=[pltpu.VMEM((d, ff), BF16), pltpu.VMEM((d, ff), BF16), pltpu.VMEM((ff, d), BF16)],
    )
    return pl.pallas_call(
        _expert_kernel,
        grid_spec=grid_spec,
        out_shape=jax.ShapeDtypeStruct((n_blocks * tm,) + slab, F32),
        compiler_params=_params(("arbitrary",)),
        name="moe_experts",
    )(block_expert, block_index, n_used, xs, w_gate, w_up, w_down)


def _combine_kernel(dest_ref, gate_ref, sh_ref, ys_ref, o_ref, buf, sem):
    tn = gate_ref.shape[0]
    n_slab = buf.shape[2]

    def row_copy(t, k):
        return pltpu.make_async_copy(ys_ref.at[dest_ref[t * TOP_K + k]], buf.at[k, t], sem)

    def issue(t, carry):
        for k in range(TOP_K):
            row_copy(t, k).start()
        return carry

    def drain(t, carry):
        for k in range(TOP_K):
            row_copy(t, k).wait()
        return carry

    lax.fori_loop(0, tn, issue, 0)
    lax.fori_loop(0, tn, drain, 0)
    gb = [jnp.broadcast_to(gate_ref[:, k:k + 1], (tn, 128)) for k in range(TOP_K)]
    for j in range(n_slab):
        acc = sh_ref[:, j * 128:(j + 1) * 128]
        for k in range(TOP_K):
            acc = acc + gb[k] * buf[k, :, j, :]
        o_ref[:, j * 128:(j + 1) * 128] = acc


def _combine(dest_flat, gates, shared, ys):
    t, d = shared.shape
    tn = 128
    slab = ys.shape[1:]
    return pl.pallas_call(
        _combine_kernel,
        grid=(t // tn,),
        in_specs=[pl.BlockSpec((tn * TOP_K,), lambda i: (i,), memory_space=pltpu.SMEM),
                  pl.BlockSpec((tn, TOP_K), lambda i: (i, 0)),
                  pl.BlockSpec((tn, d), lambda i: (i, 0)),
                  pl.BlockSpec(memory_space=pl.ANY)],
        out_specs=pl.BlockSpec((tn, d), lambda i: (i, 0)),
        out_shape=jax.ShapeDtypeStruct((t, d), F32),
        scratch_shapes=[pltpu.VMEM((TOP_K, tn) + slab, F32), pltpu.SemaphoreType.DMA(())],
        compiler_params=_params(("arbitrary",)),
        name="moe_combine",
    )(dest_flat, gates, shared, ys)


def _swiglu_kernel(x_ref, wg_ref, wu_ref, wd_ref, o_ref):
    x = x_ref[...]
    hg = jnp.dot(x, wg_ref[...], preferred_element_type=F32)
    hu = jnp.dot(x, wu_ref[...], preferred_element_type=F32)
    h = (hg * jax.nn.sigmoid(hg) * hu).astype(BF16)
    o_ref[...] = jnp.dot(h, wd_ref[...], preferred_element_type=F32).astype(o_ref.dtype)


def _shared_expert(x, wg, wu, wd):
    m, d = x.shape
    ff = wg.shape[1]
    tm = _pick(m, (512, 384, 256, 128))
    return pl.pallas_call(
        _swiglu_kernel,
        grid=(m // tm,),
        in_specs=[pl.BlockSpec((tm, d), lambda i: (i, 0)),
                  pl.BlockSpec((d, ff), lambda i: (0, 0)),
                  pl.BlockSpec((d, ff), lambda i: (0, 0)),
                  pl.BlockSpec((ff, d), lambda i: (0, 0))],
        out_specs=pl.BlockSpec((tm, d), lambda i: (i, 0)),
        out_shape=jax.ShapeDtypeStruct((m, d), F32),
        compiler_params=_params(("parallel",)),
        name="shared_expert",
    )(x, wg, wu, wd)


def _moe_ffn(f, router_w, router_bias, w_gate, w_up, w_down, ws_gate, ws_up, ws_down):
    t, d = f.shape
    tm = EXPERT_TM
    idx, gate, rank, cnt = _route(f, router_w.T, router_bias.astype(F32)[:, None])
    counts = cnt[:, 0]
    padded = (counts + tm - 1) // tm * tm
    pad_end = jnp.cumsum(padded)
    pad_start = pad_end - padded
    n_used = pad_end[-1] // tm
    n_blocks = (t * TOP_K) // tm + N_EXPERTS
    n_rows = n_blocks * tm
    dest = (pad_start[idx] + rank).T.reshape(-1).astype(jnp.int32)
    block_index = jnp.minimum(jnp.arange(n_blocks, dtype=jnp.int32), n_used - 1).astype(jnp.int32)
    block_expert = jnp.minimum(jnp.searchsorted(pad_end, block_index * tm, side='right'),
                               N_EXPERTS - 1).astype(jnp.int32)
    j = jnp.arange(tm, dtype=jnp.int32)[None, :]
    e = jnp.arange(N_EXPERTS, dtype=jnp.int32)[:, None]
    pad_rows = jnp.where(counts[:, None] + j < padded[:, None], pad_start[:, None] + counts[:, None] + j,
                         n_rows + e * tm + j).reshape(-1).astype(jnp.int32)
    f3 = f.reshape(t, d // 128, 128)
    xs = _scatter_rows(dest, f3, n_rows + N_EXPERTS * tm)
    xs = _zero_rows(pad_rows, xs)
    ys = _expert_blocks(block_expert, block_index, n_used.reshape(1).astype(jnp.int32), xs, n_blocks, tm,
                        w_gate, w_up, w_down)
    shared = _shared_expert(f.astype(BF16), ws_gate.astype(BF16), ws_up.astype(BF16), ws_down.astype(BF16))
    return _combine(dest, gate.T, shared, ys)


def _rms(u, gain):
    return u * lax.rsqrt(jnp.mean(u * u, axis=-1, keepdims=True) + NORM_EPS) * gain


def _rope_tables(n_ctx, n_lat):
    rows = n_lat // GRID_W
    row = jnp.repeat(jnp.arange(rows, dtype=F32), GRID_W)
    col = jnp.tile(jnp.arange(GRID_W, dtype=F32), rows)
    n_freq = HEAD_DIM // 4
    inv = ROPE_THETA ** (-jnp.arange(n_freq, dtype=F32) / n_freq)
    ang = jnp.concatenate([row[:, None] * inv, col[:, None] * inv], axis=-1)
    cos = jnp.repeat(jnp.cos(ang), 2, axis=-1)
    sin = jnp.repeat(jnp.sin(ang), 2, axis=-1)
    sign = jnp.tile(jnp.array([-1.0, 1.0], F32), HEAD_DIM // 2)
    cos = jnp.concatenate([jnp.ones((n_ctx, HEAD_DIM), F32), cos], axis=0)
    sin = jnp.concatenate([jnp.zeros((n_ctx, HEAD_DIM), F32), sin * sign], axis=0)
    return cos, sin


def _rope(u, cos, sin):
    up = u.reshape(u.shape[:-1] + (HEAD_DIM // 2, 2))
    partner = jnp.flip(up, axis=-1).reshape(u.shape)
    return u * cos[None, :, None, :] + partner * sin[None, :, None, :]


def _ada_mod(cond, w, bias):
    m = jax.nn.silu(cond)
    rows = m.shape[0]
    pad = (-rows) % 16
    mp = jnp.pad(m, ((0, pad), (0, 0))).astype(BF16)
    out = _matmul(mp, w.astype(BF16), F32)[:rows] + bias
    return out.reshape(rows, N_MOD, -1)


def _attention_layer(h, n_ctx, w_in, w_out, q_gain, k_gain, lq1, lk1, lq2, lk2, subln, lambda_init):
    b, t, d = h.shape
    n_lat = t - n_ctx
    proj = _matmul(h.reshape(b * t, d).astype(BF16), w_in.astype(BF16), F32).reshape(b, t, ATTN_IN)
    o = 0
    qa = proj[..., o:o + QA_W]; o += QA_W
    ka = proj[..., o:o + KA_W]; o += KA_W
    va = proj[..., o:o + KA_W]; o += KA_W
    qb = proj[..., o:o + QB_W]; o += QB_W
    kb = proj[..., o:o + QB_W]; o += QB_W
    vb = proj[..., o:o + QB_W]
    cos, sin = _rope_tables(n_ctx, n_lat)
    scale = HEAD_DIM ** -0.5
    qa = _rope(_rms(qa.reshape(b, t, A_Q_HEADS, HEAD_DIM), q_gain), cos, sin) * scale
    ka = _rope(_rms(ka.reshape(b, t, A_KV_HEADS, HEAD_DIM), k_gain), cos, sin)
    qb = _rope(qb.reshape(b, t, 2 * B_HEADS, HEAD_DIM), cos, sin) * scale
    kb = _rope(kb.reshape(b, t, 2 * B_HEADS, HEAD_DIM), cos, sin)
    qa = qa.reshape(b, t, QA_W).astype(BF16)
    ka = ka.reshape(b, t, KA_W).astype(BF16)
    va = va.astype(BF16)
    qb = qb.reshape(b, t, QB_W).astype(BF16)
    kb = kb.reshape(b, t, QB_W).astype(BF16)
    vb = vb.astype(BF16)
    lam = (jnp.exp(jnp.sum(lq1 * lk1)) - jnp.exp(jnp.sum(lq2 * lk2)) + lambda_init).reshape(1, 1).astype(F32)
    gain = subln.reshape(1, 2 * HEAD_DIM)
    out_scale = 1.0 - lambda_init
    ya_l = _gqa_attention(qa, ka, va, n_ctx, n_lat, t)
    yb_l = _diff_attention(qb, kb, vb, lam, gain, out_scale, n_ctx, n_lat, t)
    ya_c = _gqa_attention(qa, ka, va, 0, n_ctx, n_ctx)
    yb_c = _diff_attention(qb, kb, vb, lam, gain, out_scale, 0, n_ctx, n_ctx)
    y = jnp.concatenate([jnp.concatenate([ya_c, yb_c], axis=-1),
                         jnp.concatenate([ya_l, yb_l], axis=-1)], axis=1)
    return _matmul(y.reshape(b * t, -1), w_out.astype(BF16), F32).reshape(b, t, d)


def _conv_silu(u, w, bias):
    up = jnp.pad(u, ((0, 0), (1, 1), (0, 0)))
    out = up[:, :-2] * w[0] + up[:, 1:-1] * w[1] + up[:, 2:] * w[2] + bias
    return jax.nn.silu(out)


def _ssd_layer(h, n_ctx, w_in, conv_w, conv_b, dt_bias, a_log, d_skip, norm_gain, w_out):
    b, t, d = h.shape
    proj = _matmul(h.reshape(b * t, d).astype(BF16), w_in.astype(BF16), F32).reshape(b, t, -1)
    z = proj[..., :SSD_INNER]
    xbc = proj[..., SSD_INNER:SSD_INNER + SSD_CONV_DIM]
    dt = proj[..., SSD_INNER + SSD_CONV_DIM:]
    xbc = jnp.concatenate([_conv_silu(xbc[:, :n_ctx], conv_w, conv_b),
                           _conv_silu(xbc[:, n_ctx:], conv_w, conv_b)], axis=1)
    xs = xbc[..., :SSD_INNER]
    bm = xbc[..., SSD_INNER:SSD_INNER + SSD_GROUPS * SSD_STATE]
    cm = xbc[..., SSD_INNER + SSD_GROUPS * SSD_STATE:]
    dt = jax.nn.softplus(dt.reshape(b, t, 2, SSD_GROUPS, SSD_HPG) + dt_bias.reshape(2, SSD_GROUPS, SSD_HPG))
    dt_c = jnp.transpose(dt, (2, 0, 3, 1, 4))
    dt_r = jnp.transpose(dt, (2, 0, 3, 4, 1))
    a_coef = -jnp.exp(a_log)
    a_row = a_coef.reshape(2, SSD_GROUPS, 1, SSD_HPG)
    a_col = a_coef.reshape(2, SSD_GROUPS, SSD_HPG, 1)
    skip = d_skip.reshape(2, SSD_GROUPS, 1, SSD_HPG)
    y = _ssd_scan(xs, bm, cm, dt_c, dt_r, a_row, a_col, skip, n_ctx, F32)
    y = (y[0] + y[1])[:, n_ctx:]
    zl = z[:, n_ctx:]
    u = _rms(y * jax.nn.silu(zl), norm_gain)
    n_lat = t - n_ctx
    return _matmul(u.reshape(b * n_lat, SSD_INNER).astype(BF16), w_out.astype(BF16), F32).reshape(b, n_lat, d)


def kernel(x, c, ctx, c_ctx, mod_w, mod_b, norm_mix, norm_ffn, norm_final, attn_w_in, attn_w_out, attn_q_gain,
           attn_k_gain, diff_lam_q1, diff_lam_k1, diff_lam_q2, diff_lam_k2, diff_subln, ssd_w_in, ssd_conv_w,
           ssd_conv_b, ssd_dt_bias, ssd_a_log, ssd_d, ssd_norm, ssd_w_out, router_w, router_bias, exp_w_gate,
           exp_w_up, exp_w_down, shared_w_gate, shared_w_up, shared_w_down):
    b, s, d = x.shape
    n_ctx = ctx.shape[1]
    t = n_ctx + s
    xa = jnp.concatenate([ctx, x], axis=1)
    cond = jnp.concatenate([c, c_ctx[None]], axis=0)
    for i in range(DEPTH):
        last = i == DEPTH - 1
        mod = _ada_mod(cond, mod_w[i], mod_b[i])
        mod_bt = jnp.concatenate([jnp.broadcast_to(mod[b:, None], (1, n_ctx, N_MOD, d)).repeat(b, axis=0),
                                  jnp.broadcast_to(mod[:b, None], (b, s, N_MOD, d))], axis=1)
        sh1, sc1, g1, sh2, sc2, g2 = [mod_bt[:, :, k] for k in range(N_MOD)]
        h = _rms(xa, norm_mix[i]) * (1.0 + sc1) + sh1
        j = i // 2
        if i % 2 == 0:
            lambda_init = 0.8 - 0.6 * math.exp(-0.3 * i)
            y = _attention_layer(h, n_ctx, attn_w_in[j], attn_w_out[j], attn_q_gain[j], attn_k_gain[j],
                                 diff_lam_q1[j], diff_lam_k1[j], diff_lam_q2[j], diff_lam_k2[j],
                                 diff_subln[j], lambda_init)
        else:
            y_lat = _ssd_layer(h, n_ctx, ssd_w_in[j], ssd_conv_w[j], ssd_conv_b[j], ssd_dt_bias[j],
                               ssd_a_log[j], ssd_d[j], ssd_norm[j], ssd_w_out[j])
            y = jnp.concatenate([jnp.zeros((b, n_ctx, d), F32), y_lat], axis=1)
        xa = xa + g1 * y
        f = _rms(xa, norm_ffn[i]) * (1.0 + sc2) + sh2
        if last:
            fl = f[:, n_ctx:].reshape(b * s, d)
            o = _moe_ffn(fl, router_w[i], router_bias[i], exp_w_gate[i], exp_w_up[i], exp_w_down[i],
                         shared_w_gate[i], shared_w_up[i], shared_w_down[i]).reshape(b, s, d)
            o = jnp.concatenate([jnp.zeros((b, n_ctx, d), F32), o], axis=1)
        else:
            o = _moe_ffn(f.reshape(b * t, d), router_w[i], router_bias[i], exp_w_gate[i], exp_w_up[i],
                         exp_w_down[i], shared_w_gate[i], shared_w_up[i], shared_w_down[i]).reshape(b, t, d)
        xa = xa + g2 * o
    return _rms(xa[:, n_ctx:], norm_final)
```

```python
import functools
import math

import jax
import jax.numpy as jnp
from jax import lax
from jax.experimental import pallas as pl
from jax.experimental.pallas import tpu as pltpu

F32 = jnp.float32
BF16 = jnp.bfloat16

D_MODEL = 2048
DEPTH = 2
GRID_W = 64
NORM_EPS = 1e-6
N_MOD = 6
HEAD_DIM = 128
ROPE_THETA = 10000.0
A_Q_HEADS = 8
A_KV_HEADS = 2
A_GROUP = A_Q_HEADS // A_KV_HEADS
B_HEADS = 4
QA_W = A_Q_HEADS * HEAD_DIM
KA_W = A_KV_HEADS * HEAD_DIM
QB_W = 2 * B_HEADS * HEAD_DIM
ATTN_IN = QA_W + 2 * KA_W + 3 * QB_W
SSD_INNER = 2 * D_MODEL
SSD_HEADDIM = 64
SSD_HEADS = SSD_INNER // SSD_HEADDIM
SSD_GROUPS = 8
SSD_HPG = SSD_HEADS // SSD_GROUPS
SSD_STATE = 128
SSD_CONV = 3
SSD_CHUNK = 128
SSD_GROUP_W = SSD_HPG * SSD_HEADDIM
SSD_CONV_DIM = SSD_INNER + 2 * SSD_GROUPS * SSD_STATE
N_EXPERTS = 64
EXPERT_FF = 512
TOP_K = 8
N_EXPERT_GROUPS = 8
TOPK_GROUPS = 4
ROUTED_SCALE = 2.5
EXPERT_TM = 256

VMEM_LIMIT_BYTES = 56 * 1024 * 1024


def _pick(n, prefs):
    for p in prefs:
        if n % p == 0:
            return p
    raise ValueError(f"no tile in {prefs} divides {n}")


def _params(sem):
    return pltpu.CompilerParams(dimension_semantics=sem, vmem_limit_bytes=VMEM_LIMIT_BYTES)


def _mm_kernel(x_ref, w_ref, o_ref):
    o_ref[...] = jnp.dot(x_ref[...], w_ref[...], preferred_element_type=F32).astype(o_ref.dtype)


def _matmul(x, w, out_dtype):
    m, k = x.shape
    n = w.shape[1]
    tm = _pick(m, (512, 384, 256, 128, 16, 8))
    tn = _pick(n, (512, 384, 256, 128))
    return pl.pallas_call(
        _mm_kernel,
        grid=(m // tm, n // tn),
        in_specs=[pl.BlockSpec((tm, k), lambda i, j: (i, 0)),
                  pl.BlockSpec((k, tn), lambda i, j: (0, j))],
        out_specs=pl.BlockSpec((tm, tn), lambda i, j: (i, j)),
        out_shape=jax.ShapeDtypeStruct((m, n), out_dtype),
        compiler_params=_params(("parallel", "arbitrary")),
        name="matmul",
    )(x, w)


def _softmax_rows(s):
    m = jnp.max(s, axis=-1, keepdims=True)
    p = jnp.exp(s - m)
    return p, jnp.sum(p, axis=-1, keepdims=True)


def _gqa_kernel(q_ref, k_ref, v_ref, o_ref):
    k = k_ref[0]
    v = v_ref[0]
    for g in range(A_GROUP):
        q = q_ref[0, :, g * HEAD_DIM:(g + 1) * HEAD_DIM]
        s = lax.dot_general(q, k, (((1,), (1,)), ((), ())), preferred_element_type=F32)
        p, l = _softmax_rows(s)
        o = jnp.dot(p.astype(BF16), v, preferred_element_type=F32) / l
        o_ref[0, :, g * HEAD_DIM:(g + 1) * HEAD_DIM] = o.astype(o_ref.dtype)


def _gqa_attention(q, k, v, q_row0, n_q, n_k):
    b = q.shape[0]
    tq = _pick(n_q, (256, 128))
    assert q_row0 % tq == 0
    q0 = q_row0 // tq
    gw = A_GROUP * HEAD_DIM
    return pl.pallas_call(
        _gqa_kernel,
        grid=(b, A_KV_HEADS, n_q // tq),
        in_specs=[pl.BlockSpec((1, tq, gw), lambda bi, h, i: (bi, q0 + i, h)),
                  pl.BlockSpec((1, n_k, HEAD_DIM), lambda bi, h, i: (bi, 0, h)),
                  pl.BlockSpec((1, n_k, HEAD_DIM), lambda bi, h, i: (bi, 0, h))],
        out_specs=pl.BlockSpec((1, tq, gw), lambda bi, h, i: (bi, i, h)),
        out_shape=jax.ShapeDtypeStruct((b, n_q, QA_W), BF16),
        compiler_params=_params(("parallel", "parallel", "arbitrary")),
        name="gqa_attention",
    )(q, k, v)


def _diff_kernel(lam_ref, gain_ref, q_ref, k_ref, v_ref, o_ref, *, out_scale):
    lam = lam_ref[0, 0]
    parts = []
    for m in range(2):
        q = q_ref[0, :, m * HEAD_DIM:(m + 1) * HEAD_DIM]
        k = k_ref[0, :, m * HEAD_DIM:(m + 1) * HEAD_DIM]
        s = lax.dot_general(q, k, (((1,), (1,)), ((), ())), preferred_element_type=F32)
        p, l = _softmax_rows(s)
        parts.append(p / l)
    a = parts[0] - lam * parts[1]
    y = jnp.dot(a.astype(BF16), v_ref[0], preferred_element_type=F32)
    y = y * lax.rsqrt(jnp.mean(y * y, axis=-1, keepdims=True) + NORM_EPS)
    o_ref[0] = (y * gain_ref[...] * out_scale).astype(o_ref.dtype)


def _diff_attention(q, k, v, lam, subln_gain, out_scale, q_row0, n_q, n_k):
    b = q.shape[0]
    tq = _pick(n_q, (256, 128))
    assert q_row0 % tq == 0
    q0 = q_row0 // tq
    hw = 2 * HEAD_DIM
    return pl.pallas_call(
        functools.partial(_diff_kernel, out_scale=out_scale),
        grid=(b, B_HEADS, n_q // tq),
        in_specs=[pl.BlockSpec(memory_space=pltpu.SMEM),
                  pl.BlockSpec((1, hw), lambda bi, h, i: (0, 0)),
                  pl.BlockSpec((1, tq, hw), lambda bi, h, i: (bi, q0 + i, h)),
                  pl.BlockSpec((1, n_k, hw), lambda bi, h, i: (bi, 0, h)),
                  pl.BlockSpec((1, n_k, hw), lambda bi, h, i: (bi, 0, h))],
        out_specs=pl.BlockSpec((1, tq, hw), lambda bi, h, i: (bi, i, h)),
        out_shape=jax.ShapeDtypeStruct((b, n_q, QB_W), BF16),
        compiler_params=_params(("parallel", "parallel", "arbitrary")),
        name="diff_attention",
    )(lam, subln_gain, q, k, v)


def _split3(a):
    a1 = a.astype(BF16)
    r1 = a - a1.astype(F32)
    a2 = r1.astype(BF16)
    a3 = (r1 - a2.astype(F32)).astype(BF16)
    return a1, a2, a3


def _dot_exact_rhs(a, rhs01):
    out = None
    for part in _split3(a):
        d = jnp.dot(part, rhs01, preferred_element_type=F32)
        out = d if out is None else out + d
    return out


def _dot_exact_lhs(lhs01, a):
    out = None
    for part in _split3(a):
        d = jnp.dot(lhs01, part, preferred_element_type=F32)
        out = d if out is None else out + d
    return out


def _ssd_kernel(acr_ref, acc_ref, skip_ref, x_ref, b_ref, c_ref, dtc_ref, dtr_ref, y_ref, state_ref):
    d = pl.program_id(1)
    q = SSD_CHUNK

    @pl.when(pl.program_id(3) == 0)
    def _():
        state_ref[...] = jnp.zeros_like(state_ref)

    rows = lax.broadcasted_iota(jnp.int32, (q, q), 0)
    cols = lax.broadcasted_iota(jnp.int32, (q, q), 1)
    signed = (rows - cols) * (1 - 2 * d)
    keep_sl = signed <= 0
    tri_ks01 = jnp.where(keep_sl, 1.0, 0.0).astype(BF16)
    tri_sk01 = jnp.where(signed >= 0, 1.0, 0.0).astype(BF16)

    x = x_ref[0]
    bm = b_ref[0].astype(BF16)
    cm = c_ref[0].astype(BF16)
    dt_r = dtr_ref[0, 0, 0]
    a_r = dt_r * acc_ref[0, 0]
    a_c = dtc_ref[0, 0, 0] * acr_ref[0, 0]
    cum_r = _dot_exact_rhs(a_r, tri_ks01)
    cum_c = _dot_exact_lhs(tri_sk01, a_c)
    tot = jnp.sum(a_r, axis=1, keepdims=True)
    to_end_r = jnp.exp(tot - cum_r)
    from_start_r = jnp.exp(cum_r)
    tot_e = jnp.exp(tot)
    skip = skip_ref[0, 0]

    g_sl = lax.dot_general(bm, cm, (((1,), (1,)), ((), ())), preferred_element_type=F32)
    state = state_ref[...]
    y_off = lax.dot_general(state.astype(BF16), cm, (((1,), (1,)), ((), ())), preferred_element_type=F32)

    xw_parts = []
    for h in range(SSD_HPG):
        sl = slice(h * SSD_HEADDIM, (h + 1) * SSD_HEADDIM)
        xh = x[sl, :]
        xd = xh * dt_r[h:h + 1, :]
        seg = cum_r[h:h + 1, :] - cum_c[:, h:h + 1]
        decay = jnp.where(keep_sl, jnp.exp(jnp.where(keep_sl, seg, 0.0)), 0.0)
        m_h = (g_sl * decay).astype(BF16)
        y_h = jnp.dot(xd.astype(BF16), m_h, preferred_element_type=F32)
        y_h = y_h + y_off[sl, :] * from_start_r[h:h + 1, :] + skip[h:h + 1, :] * xh
        y_ref[0, 0, sl, :] = y_h.astype(y_ref.dtype)
        xw_parts.append((xd * to_end_r[h:h + 1, :]).astype(BF16))
    s_new = jnp.dot(jnp.concatenate(xw_parts, axis=0), bm, preferred_element_type=F32)
    for h in range(SSD_HPG):
        sl = slice(h * SSD_HEADDIM, (h + 1) * SSD_HEADDIM)
        state_ref[sl, :] = state[sl, :] * tot_e[h:h + 1, :] + s_new[sl, :]


def _ssd_scan(xs_t, bm, cm, dt_c, dt_r, a_row, a_col, skip, n_ctx, out_dtype):
    b, _, t = xs_t.shape
    q = SSD_CHUNK
    ncc = n_ctx // q
    nch = t // q

    def chunk(d, s):
        back = jnp.where(s < ncc, ncc - 1 - s, nch - 1 - s + ncc)
        return jnp.where(d == 0, s, back)

    return pl.pallas_call(
        _ssd_kernel,
        grid=(b, 2, SSD_GROUPS, nch),
        in_specs=[pl.BlockSpec((1, 1, 1, SSD_HPG), lambda bi, d, g, s: (d, g, 0, 0)),
                  pl.BlockSpec((1, 1, SSD_HPG, 1), lambda bi, d, g, s: (d, g, 0, 0)),
                  pl.BlockSpec((1, 1, SSD_HPG, 1), lambda bi, d, g, s: (d, g, 0, 0)),
                  pl.BlockSpec((1, SSD_GROUP_W, q), lambda bi, d, g, s: (bi, g, chunk(d, s))),
                  pl.BlockSpec((1, q, SSD_STATE), lambda bi, d, g, s: (bi, chunk(d, s), g)),
                  pl.BlockSpec((1, q, SSD_STATE), lambda bi, d, g, s: (bi, chunk(d, s), g)),
                  pl.BlockSpec((1, 1, 1, q, SSD_HPG), lambda bi, d, g, s: (d, bi, g, chunk(d, s), 0)),
                  pl.BlockSpec((1, 1, 1, SSD_HPG, q), lambda bi, d, g, s: (d, bi, g, 0, chunk(d, s)))],
        out_specs=pl.BlockSpec((1, 1, SSD_GROUP_W, q), lambda bi, d, g, s: (d, bi, g, chunk(d, s))),
        out_shape=jax.ShapeDtypeStruct((2, b, SSD_INNER, t), out_dtype),
        scratch_shapes=[pltpu.VMEM((SSD_GROUP_W, SSD_STATE), F32)],
        compiler_params=_params(("parallel", "parallel", "parallel", "arbitrary")),
        name="ssd_scan",
    )(a_row, a_col, skip, xs_t, bm, cm, dt_c, dt_r)


def _route_kernel(f_ref, rwt_ref, bias_ref, idx_ref, gate_ref, rank_ref, cnt_ref, carry_ref):
    tn = f_ref.shape[0]
    ne, ng, pg = N_EXPERTS, N_EXPERT_GROUPS, N_EXPERTS // N_EXPERT_GROUPS
    neg = -jnp.inf

    @pl.when(pl.program_id(0) == 0)
    def _():
        carry_ref[...] = jnp.zeros_like(carry_ref)

    logits = lax.dot_general(rwt_ref[...], f_ref[...], (((1,), (1,)), ((), ())),
                             preferred_element_type=F32, precision=lax.Precision.HIGHEST)
    scores = jax.nn.sigmoid(logits)
    g3 = (scores + bias_ref[...]).reshape(ng, pg, tn)
    io3 = lax.broadcasted_iota(jnp.int32, (ng, pg, tn), 1)
    m1 = jnp.max(g3, axis=1, keepdims=True)
    i1 = jnp.min(jnp.where(g3 == m1, io3, pg), axis=1, keepdims=True)
    m2 = jnp.max(jnp.where(io3 == i1, neg, g3), axis=1, keepdims=True)
    work = (m1 + m2).reshape(ng, tn)
    iog = lax.broadcasted_iota(jnp.int32, (ng, tn), 0)
    ok = jnp.zeros((ng, tn), F32)
    for _ in range(TOPK_GROUPS):
        m = jnp.max(work, axis=0, keepdims=True)
        gi = jnp.min(jnp.where(work == m, iog, ng), axis=0, keepdims=True)
        hit = iog == gi
        ok = jnp.where(hit, 1.0, ok)
        work = jnp.where(hit, neg, work)
    sel = jnp.where(ok.reshape(ng, 1, tn) > 0.0, g3, neg).reshape(ne, tn)
    ioe = lax.broadcasted_iota(jnp.int32, (ne, tn), 0)
    onehot = jnp.zeros((ne, tn), F32)
    idxs, ws = [], []
    for _ in range(TOP_K):
        m = jnp.max(sel, axis=0, keepdims=True)
        ei = jnp.min(jnp.where(sel == m, ioe, ne), axis=0, keepdims=True)
        hit = ioe == ei
        idxs.append(ei)
        ws.append(jnp.sum(jnp.where(hit, scores, 0.0), axis=0, keepdims=True))
        sel = jnp.where(hit, neg, sel)
        onehot = jnp.where(hit, 1.0, onehot)
    w = jnp.concatenate(ws, axis=0)
    gate_ref[...] = w / jnp.sum(w, axis=0, keepdims=True) * ROUTED_SCALE
    idx_ref[...] = jnp.concatenate(idxs, axis=0)
    r = lax.broadcasted_iota(jnp.int32, (tn, tn), 0)
    c = lax.broadcasted_iota(jnp.int32, (tn, tn), 1)
    ahead = jnp.where(r < c, 1.0, 0.0).astype(BF16)
    cum = carry_ref[...] + jnp.dot(onehot.astype(BF16), ahead, preferred_element_type=F32)
    ranks = [jnp.sum(jnp.where(ioe == idxs[k], cum, 0.0), axis=0, keepdims=True) for k in range(TOP_K)]
    rank_ref[...] = jnp.concatenate(ranks, axis=0).astype(jnp.int32)
    total = carry_ref[...] + jnp.sum(onehot, axis=1, keepdims=True)
    carry_ref[...] = total
    cnt_ref[...] = total.astype(jnp.int32)


def _route(f, router_wt, router_bias):
    t, d = f.shape
    tn = _pick(t, (512, 256, 128))
    kt = pl.BlockSpec((TOP_K, tn), lambda i: (0, i))
    return pl.pallas_call(
        _route_kernel,
        grid=(t // tn,),
        in_specs=[pl.BlockSpec((tn, d), lambda i: (i, 0)),
                  pl.BlockSpec((N_EXPERTS, d), lambda i: (0, 0)),
                  pl.BlockSpec((N_EXPERTS, 1), lambda i: (0, 0))],
        out_specs=[kt, kt, kt, pl.BlockSpec((N_EXPERTS, 1), lambda i: (0, 0))],
        out_shape=[jax.ShapeDtypeStruct((TOP_K, t), jnp.int32), jax.ShapeDtypeStruct((TOP_K, t), F32),
                   jax.ShapeDtypeStruct((TOP_K, t), jnp.int32), jax.ShapeDtypeStruct((N_EXPERTS, 1), jnp.int32)],
        scratch_shapes=[pltpu.VMEM((N_EXPERTS, 1), F32)],
        compiler_params=_params(("arbitrary",)),
        name="moe_route",
    )(f, router_wt, router_bias)


def _scatter_kernel(dest_ref, f_ref, xs_ref, sem):
    ts = f_ref.shape[0]

    def row_copy(t, k):
        return pltpu.make_async_copy(f_ref.at[pl.ds(t, 1)], xs_ref.at[pl.ds(dest_ref[t * TOP_K + k], 1)], sem)

    def issue(t, carry):
        for k in range(TOP_K):
            row_copy(t, k).start()
        return carry

    def drain(t, carry):
        for k in range(TOP_K):
            row_copy(t, k).wait()
        return carry

    lax.fori_loop(0, ts, issue, 0)
    lax.fori_loop(0, ts, drain, 0)


def _scatter_rows(dest_flat, f, n_rows_alloc):
    t, d = f.shape
    ts = _pick(t, (256, 128))
    return pl.pallas_call(
        _scatter_kernel,
        grid=(t // ts,),
        in_specs=[pl.BlockSpec((ts * TOP_K,), lambda i: (i,), memory_space=pltpu.SMEM),
                  pl.BlockSpec((ts, d), lambda i: (i, 0))],
        out_specs=pl.BlockSpec(memory_space=pl.ANY),
        out_shape=jax.ShapeDtypeStruct((n_rows_alloc, d), f.dtype),
        scratch_shapes=[pltpu.SemaphoreType.DMA(())],
        compiler_params=_params(("arbitrary",)),
        name="moe_scatter",
    )(dest_flat, f)


def _zero_rows_kernel(rows_ref, xs_in_ref, xs_ref, zero_ref, sem):
    del xs_in_ref
    zero_ref[...] = jnp.zeros_like(zero_ref)
    n = rows_ref.shape[0]

    def row_copy(i):
        return pltpu.make_async_copy(zero_ref, xs_ref.at[pl.ds(rows_ref[i], 1)], sem)

    def issue(i, carry):
        row_copy(i).start()
        return carry

    def drain(i, carry):
        row_copy(i).wait()
        return carry

    lax.fori_loop(0, n, issue, 0)
    lax.fori_loop(0, n, drain, 0)


def _zero_rows(rows, xs):
    n = rows.shape[0]
    step = _pick(n, (512, 256, 128))
    return pl.pallas_call(
        _zero_rows_kernel,
        grid=(n // step,),
        in_specs=[pl.BlockSpec((step,), lambda i: (i,), memory_space=pltpu.SMEM),
                  pl.BlockSpec(memory_space=pl.ANY)],
        out_specs=pl.BlockSpec(memory_space=pl.ANY),
        out_shape=jax.ShapeDtypeStruct(xs.shape, xs.dtype),
        scratch_shapes=[pltpu.VMEM((1,) + xs.shape[1:], xs.dtype), pltpu.SemaphoreType.DMA(())],
        input_output_aliases={1: 0},
        compiler_params=_params(("arbitrary",)),
        name="moe_zero_pad_rows",
    )(rows, xs)


def _expert_kernel(be_ref, blk_ref, nused_ref, x_ref, wg_ref, wu_ref, wd_ref, o_ref, wg_s, wu_s, wd_s):
    i = pl.program_id(0)

    @pl.when(jnp.logical_or(i == 0, be_ref[i] != be_ref[jnp.maximum(i - 1, 0)]))
    def _():
        wg_s[...] = wg_ref[0].astype(BF16)
        wu_s[...] = wu_ref[0].astype(BF16)
        wd_s[...] = wd_ref[0].astype(BF16)

    @pl.when(i < nused_ref[0])
    def _():
        x = x_ref[...].astype(BF16)
        hg = jnp.dot(x, wg_s[...], preferred_element_type=F32)
        hu = jnp.dot(x, wu_s[...], preferred_element_type=F32)
        h = (hg * jax.nn.sigmoid(hg) * hu).astype(BF16)
        o_ref[...] = jnp.dot(h, wd_s[...], preferred_element_type=F32)


def _expert_blocks(block_expert, block_index, n_used, xs, n_blocks, tm, w_gate, w_up, w_down):
    d, ff = w_gate.shape[1:]
    grid_spec = pltpu.PrefetchScalarGridSpec(
        num_scalar_prefetch=3,
        grid=(n_blocks,),
        in_specs=[pl.BlockSpec((tm, d), lambda i, be, blk, nu: (blk[i], 0)),
                  pl.BlockSpec((1, d, ff), lambda i, be, blk, nu: (be[i], 0, 0)),
                  pl.BlockSpec((1, d, ff), lambda i, be, blk, nu: (be[i], 0, 0)),
                  pl.BlockSpec((1, ff, d), lambda i, be, blk, nu: (be[i], 0, 0))],
        out_specs=pl.BlockSpec((tm, d), lambda i, be, blk, nu: (blk[i], 0)),
        scratch_shapes=[pltpu.VMEM((d, ff), BF16), pltpu.VMEM((d, ff), BF16), pltpu.VMEM((ff, d), BF16)],
    )
    return pl.pallas_call(
        _expert_kernel,
        grid_spec=grid_spec,
        out_shape=jax.ShapeDtypeStruct((n_blocks * tm, d), F32),
        compiler_params=_params(("arbitrary",)),
        name="moe_experts",
    )(block_expert, block_index, n_used, xs, w_gate, w_up, w_down)


def _combine_kernel(dest_ref, gate_ref, sh_ref, ys_ref, o_ref, buf, sem):
    tn = gate_ref.shape[0]

    def row_copy(t, k):
        return pltpu.make_async_copy(ys_ref.at[pl.ds(dest_ref[t * TOP_K + k], 1)], buf.at[k, pl.ds(t, 1)], sem)

    def issue(t, carry):
        for k in range(TOP_K):
            row_copy(t, k).start()
        return carry

    def drain(t, carry):
        for k in range(TOP_K):
            row_copy(t, k).wait()
        return carry

    lax.fori_loop(0, tn, issue, 0)
    lax.fori_loop(0, tn, drain, 0)
    acc = sh_ref[...]
    for k in range(TOP_K):
        acc = acc + gate_ref[:, k:k + 1] * buf[k]
    o_ref[...] = acc


def _combine(dest_flat, gates, shared, ys):
    t, d = shared.shape
    tn = 128
    return pl.pallas_call(
        _combine_kernel,
        grid=(t // tn,),
        in_specs=[pl.BlockSpec((tn * TOP_K,), lambda i: (i,), memory_space=pltpu.SMEM),
                  pl.BlockSpec((tn, TOP_K), lambda i: (i, 0)),
                  pl.BlockSpec((tn, d), lambda i: (i, 0)),
                  pl.BlockSpec(memory_space=pl.ANY)],
        out_specs=pl.BlockSpec((tn, d), lambda i: (i, 0)),
        out_shape=jax.ShapeDtypeStruct((t, d), F32),
        scratch_shapes=[pltpu.VMEM((TOP_K, tn, d), F32), pltpu.SemaphoreType.DMA(())],
        compiler_params=_params(("arbitrary",)),
        name="moe_combine",
    )(dest_flat, gates, shared, ys)


def _swiglu_kernel(x_ref, wg_ref, wu_ref, wd_ref, o_ref):
    x = x_ref[...]
    hg = jnp.dot(x, wg_ref[...], preferred_element_type=F32)
    hu = jnp.dot(x, wu_ref[...], preferred_element_type=F32)
    h = (hg * jax.nn.sigmoid(hg) * hu).astype(BF16)
    o_ref[...] = jnp.dot(h, wd_ref[...], preferred_element_type=F32).astype(o_ref.dtype)


def _shared_expert(x, wg, wu, wd):
    m, d = x.shape
    ff = wg.shape[1]
    tm = _pick(m, (512, 384, 256, 128))
    return pl.pallas_call(
        _swiglu_kernel,
        grid=(m // tm,),
        in_specs=[pl.BlockSpec((tm, d), lambda i: (i, 0)),
                  pl.BlockSpec((d, ff), lambda i: (0, 0)),
                  pl.BlockSpec((d, ff), lambda i: (0, 0)),
                  pl.BlockSpec((ff, d), lambda i: (0, 0))],
        out_specs=pl.BlockSpec((tm, d), lambda i: (i, 0)),
        out_shape=jax.ShapeDtypeStruct((m, d), F32),
        compiler_params=_params(("parallel",)),
        name="shared_expert",
    )(x, wg, wu, wd)


def _moe_ffn(f, router_w, router_bias, w_gate, w_up, w_down, ws_gate, ws_up, ws_down):
    t, d = f.shape
    tm = EXPERT_TM
    idx, gate, rank, cnt = _route(f, router_w.T, router_bias.astype(F32)[:, None])
    counts = cnt[:, 0]
    padded = (counts + tm - 1) // tm * tm
    pad_end = jnp.cumsum(padded)
    pad_start = pad_end - padded
    n_used = pad_end[-1] // tm
    n_blocks = (t * TOP_K) // tm + N_EXPERTS
    n_rows = n_blocks * tm
    experts = jnp.arange(N_EXPERTS, dtype=jnp.int32)
    start_of = jnp.sum(jnp.where(idx[..., None] == experts, pad_start, 0), axis=-1)
    dest = (start_of + rank).T.reshape(-1).astype(jnp.int32)
    block_index = jnp.minimum(jnp.arange(n_blocks, dtype=jnp.int32), n_used - 1).astype(jnp.int32)
    block_expert = jnp.minimum(jnp.sum(pad_end[None, :] <= (block_index * tm)[:, None], axis=1),
                               N_EXPERTS - 1).astype(jnp.int32)
    j = jnp.arange(tm, dtype=jnp.int32)[None, :]
    e = jnp.arange(N_EXPERTS, dtype=jnp.int32)[:, None]
    pad_rows = jnp.where(counts[:, None] + j < padded[:, None], pad_start[:, None] + counts[:, None] + j,
                         n_rows + e * tm + j).reshape(-1).astype(jnp.int32)
    xs = _scatter_rows(dest, f, n_rows + N_EXPERTS * tm)
    xs = _zero_rows(pad_rows, xs)
    ys = _expert_blocks(block_expert, block_index, n_used.reshape(1).astype(jnp.int32), xs, n_blocks, tm,
                        w_gate, w_up, w_down)
    shared = _shared_expert(f.astype(BF16), ws_gate.astype(BF16), ws_up.astype(BF16), ws_down.astype(BF16))
    return _combine(dest, gate.T, shared, ys)


def _rms(u, gain):
    return u * lax.rsqrt(jnp.mean(u * u, axis=-1, keepdims=True) + NORM_EPS) * gain


def _rope_tables(n_ctx, n_lat):
    rows = n_lat // GRID_W
    row = jnp.repeat(jnp.arange(rows, dtype=F32), GRID_W)
    col = jnp.tile(jnp.arange(GRID_W, dtype=F32), rows)
    n_freq = HEAD_DIM // 4
    inv = ROPE_THETA ** (-jnp.arange(n_freq, dtype=F32) / n_freq)
    ang = jnp.concatenate([row[:, None] * inv, col[:, None] * inv], axis=-1)
    cos = jnp.repeat(jnp.cos(ang), 2, axis=-1)
    sin = jnp.repeat(jnp.sin(ang), 2, axis=-1)
    sign = jnp.tile(jnp.array([-1.0, 1.0], F32), HEAD_DIM // 2)
    cos = jnp.concatenate([jnp.ones((n_ctx, HEAD_DIM), F32), cos], axis=0)
    sin = jnp.concatenate([jnp.zeros((n_ctx, HEAD_DIM), F32), sin * sign], axis=0)
    return cos, sin


def _rope(u, cos, sin):
    up = u.reshape(u.shape[:-1] + (HEAD_DIM // 2, 2))
    partner = jnp.flip(up, axis=-1).reshape(u.shape)
    return u * cos[None, :, None, :] + partner * sin[None, :, None, :]


def _ada_mod(cond, w, bias):
    m = jax.nn.silu(cond)
    rows = m.shape[0]
    pad = (-rows) % 16
    mp = jnp.pad(m, ((0, pad), (0, 0))).astype(BF16)
    out = _matmul(mp, w.astype(BF16), F32)[:rows] + bias
    return out.reshape(rows, N_MOD, -1)


def _attention_layer(h, n_ctx, w_in, w_out, q_gain, k_gain, lq1, lk1, lq2, lk2, subln, lambda_init):
    b, t, d = h.shape
    n_lat = t - n_ctx
    proj = _matmul(h.reshape(b * t, d).astype(BF16), w_in.astype(BF16), F32).reshape(b, t, ATTN_IN)
    o = 0
    qa = proj[..., o:o + QA_W]; o += QA_W
    ka = proj[..., o:o + KA_W]; o += KA_W
    va = proj[..., o:o + KA_W]; o += KA_W
    qb = proj[..., o:o + QB_W]; o += QB_W
    kb = proj[..., o:o + QB_W]; o += QB_W
    vb = proj[..., o:o + QB_W]
    cos, sin = _rope_tables(n_ctx, n_lat)
    scale = HEAD_DIM ** -0.5
    qa = _rope(_rms(qa.reshape(b, t, A_Q_HEADS, HEAD_DIM), q_gain), cos, sin) * scale
    ka = _rope(_rms(ka.reshape(b, t, A_KV_HEADS, HEAD_DIM), k_gain), cos, sin)
    qb = _rope(qb.reshape(b, t, 2 * B_HEADS, HEAD_DIM), cos, sin) * scale
    kb = _rope(kb.reshape(b, t, 2 * B_HEADS, HEAD_DIM), cos, sin)
    qa = qa.reshape(b, t, QA_W).astype(BF16)
    ka = ka.reshape(b, t, KA_W).astype(BF16)
    va = va.astype(BF16)
    qb = qb.reshape(b, t, QB_W).astype(BF16)
    kb = kb.reshape(b, t, QB_W).astype(BF16)
    vb = vb.astype(BF16)
    lam = (jnp.exp(jnp.sum(lq1 * lk1)) - jnp.exp(jnp.sum(lq2 * lk2)) + lambda_init).reshape(1, 1).astype(F32)
    gain = subln.reshape(1, 2 * HEAD_DIM)
    out_scale = 1.0 - lambda_init
    ya_l = _gqa_attention(qa, ka, va, n_ctx, n_lat, t)
    yb_l = _diff_attention(qb, kb, vb, lam, gain, out_scale, n_ctx, n_lat, t)
    ya_c = _gqa_attention(qa, ka, va, 0, n_ctx, n_ctx)
    yb_c = _diff_attention(qb, kb, vb, lam, gain, out_scale, 0, n_ctx, n_ctx)
    y = jnp.concatenate([jnp.concatenate([ya_c, yb_c], axis=-1),
                         jnp.concatenate([ya_l, yb_l], axis=-1)], axis=1)
    return _matmul(y.reshape(b * t, -1), w_out.astype(BF16), F32).reshape(b, t, d)


def _conv_silu(u, w, bias):
    up = jnp.pad(u, ((0, 0), (1, 1), (0, 0)))
    out = up[:, :-2] * w[0] + up[:, 1:-1] * w[1] + up[:, 2:] * w[2] + bias
    return jax.nn.silu(out)


def _ssd_layer(h, n_ctx, w_in, conv_w, conv_b, dt_bias, a_log, d_skip, norm_gain, w_out):
    b, t, d = h.shape
    proj = _matmul(h.reshape(b * t, d).astype(BF16), w_in.astype(BF16), F32).reshape(b, t, -1)
    z = proj[..., :SSD_INNER]
    xbc = proj[..., SSD_INNER:SSD_INNER + SSD_CONV_DIM]
    dt = proj[..., SSD_INNER + SSD_CONV_DIM:]
    xbc = jnp.concatenate([_conv_silu(xbc[:, :n_ctx], conv_w, conv_b),
                           _conv_silu(xbc[:, n_ctx:], conv_w, conv_b)], axis=1)
    xs = xbc[..., :SSD_INNER]
    bm = xbc[..., SSD_INNER:SSD_INNER + SSD_GROUPS * SSD_STATE]
    cm = xbc[..., SSD_INNER + SSD_GROUPS * SSD_STATE:]
    dt = jax.nn.softplus(dt.reshape(b, t, 2, SSD_GROUPS, SSD_HPG) + dt_bias.reshape(2, SSD_GROUPS, SSD_HPG))
    dt_c = jnp.transpose(dt, (2, 0, 3, 1, 4))
    dt_r = jnp.transpose(dt, (2, 0, 3, 4, 1))
    a_coef = -jnp.exp(a_log)
    a_row = a_coef.reshape(2, SSD_GROUPS, 1, SSD_HPG)
    a_col = a_coef.reshape(2, SSD_GROUPS, SSD_HPG, 1)
    skip = d_skip.reshape(2, SSD_GROUPS, SSD_HPG, 1)
    y = _ssd_scan(jnp.transpose(xs, (0, 2, 1)), bm, cm, dt_c, dt_r, a_row, a_col, skip, n_ctx, F32)
    y = jnp.transpose(y[0] + y[1], (0, 2, 1))[:, n_ctx:]
    zl = z[:, n_ctx:]
    u = _rms(y * jax.nn.silu(zl), norm_gain)
    n_lat = t - n_ctx
    return _matmul(u.reshape(b * n_lat, SSD_INNER).astype(BF16), w_out.astype(BF16), F32).reshape(b, n_lat, d)


def kernel(x, c, ctx, c_ctx, mod_w, mod_b, norm_mix, norm_ffn, norm_final, attn_w_in, attn_w_out, attn_q_gain,
           attn_k_gain, diff_lam_q1, diff_lam_k1, diff_lam_q2, diff_lam_k2, diff_subln, ssd_w_in, ssd_conv_w,
           ssd_conv_b, ssd_dt_bias, ssd_a_log, ssd_d, ssd_norm, ssd_w_out, router_w, router_bias, exp_w_gate,
           exp_w_up, exp_w_down, shared_w_gate, shared_w_up, shared_w_down):
    b, s, d = x.shape
    n_ctx = ctx.shape[1]
    t = n_ctx + s
    xa = jnp.concatenate([ctx, x], axis=1)
    cond = jnp.concatenate([c, c_ctx[None]], axis=0)
    for i in range(DEPTH):
        last = i == DEPTH - 1
        mod = _ada_mod(cond, mod_w[i], mod_b[i])
        mod_bt = jnp.concatenate([jnp.broadcast_to(mod[b:, None], (1, n_ctx, N_MOD, d)).repeat(b, axis=0),
                                  jnp.broadcast_to(mod[:b, None], (b, s, N_MOD, d))], axis=1)
        sh1, sc1, g1, sh2, sc2, g2 = [mod_bt[:, :, k] for k in range(N_MOD)]
        h = _rms(xa, norm_mix[i]) * (1.0 + sc1) + sh1
        j = i // 2
        if i % 2 == 0:
            lambda_init = 0.8 - 0.6 * math.exp(-0.3 * i)
            y = _attention_layer(h, n_ctx, attn_w_in[j], attn_w_out[j], attn_q_gain[j], attn_k_gain[j],
                                 diff_lam_q1[j], diff_lam_k1[j], diff_lam_q2[j], diff_lam_k2[j],
                                 diff_subln[j], lambda_init)
        else:
            y_lat = _ssd_layer(h, n_ctx, ssd_w_in[j], ssd_conv_w[j], ssd_conv_b[j], ssd_dt_bias[j],
                               ssd_a_log[j], ssd_d[j], ssd_norm[j], ssd_w_out[j])
            y = jnp.concatenate([jnp.zeros((b, n_ctx, d), F32), y_lat], axis=1)
        xa = xa + g1 * y
        f = _rms(xa, norm_ffn[i]) * (1.0 + sc2) + sh2
        if last:
            fl = f[:, n_ctx:].reshape(b * s, d)
            o = _moe_ffn(fl, router_w[i], router_bias[i], exp_w_gate[i], exp_w_up[i], exp_w_down[i],
                         shared_w_gate[i], shared_w_up[i], shared_w_down[i]).reshape(b, s, d)
            o = jnp.concatenate([jnp.zeros((b, n_ctx, d), F32), o], axis=1)
        else:
            o = _moe_ffn(f.reshape(b * t, d), router_w[i], router_bias[i], exp_w_gate[i], exp_w_up[i],
                         exp_w_down[i], shared_w_gate[i], shared_w_up[i], shared_w_down[i]).reshape(b, t, d)
        xa = xa + g2 * o
    return _rms(xa[:, n_ctx:], norm_final)
```

```python
import functools
import math

import jax
import jax.numpy as jnp
from jax import lax
from jax.experimental import pallas as pl
from jax.experimental.pallas import tpu as pltpu

F32 = jnp.float32
BF16 = jnp.bfloat16

D_MODEL = 2048
DEPTH = 2
GRID_W = 64
NORM_EPS = 1e-6
N_MOD = 6
HEAD_DIM = 128
ROPE_THETA = 10000.0
A_Q_HEADS = 8
A_KV_HEADS = 2
A_GROUP = A_Q_HEADS // A_KV_HEADS
B_HEADS = 4
QA_W = A_Q_HEADS * HEAD_DIM
KA_W = A_KV_HEADS * HEAD_DIM
QB_W = 2 * B_HEADS * HEAD_DIM
ATTN_IN = QA_W + 2 * KA_W + 3 * QB_W
SSD_INNER = 2 * D_MODEL
SSD_HEADDIM = 64
SSD_HEADS = SSD_INNER // SSD_HEADDIM
SSD_GROUPS = 8
SSD_HPG = SSD_HEADS // SSD_GROUPS
SSD_STATE = 128
SSD_CONV = 3
SSD_CHUNK = 128
SSD_GROUP_W = SSD_HPG * SSD_HEADDIM
SSD_CONV_DIM = SSD_INNER + 2 * SSD_GROUPS * SSD_STATE
N_EXPERTS = 64
EXPERT_FF = 512
TOP_K = 8
N_EXPERT_GROUPS = 8
TOPK_GROUPS = 4
ROUTED_SCALE = 2.5
EXPERT_TM = 256

VMEM_LIMIT_BYTES = 56 * 1024 * 1024


def _pick(n, prefs):
    for p in prefs:
        if n % p == 0:
            return p
    raise ValueError(f"no tile in {prefs} divides {n}")


def _params(sem):
    return pltpu.CompilerParams(dimension_semantics=sem, vmem_limit_bytes=VMEM_LIMIT_BYTES)


def _mm_kernel(x_ref, w_ref, o_ref):
    o_ref[...] = jnp.dot(x_ref[...], w_ref[...], preferred_element_type=F32).astype(o_ref.dtype)


def _matmul(x, w, out_dtype):
    m, k = x.shape
    n = w.shape[1]
    tm = _pick(m, (1024, 512, 384, 256, 128, 16, 8))
    tn = _pick(n, (1024, 512, 384, 256, 128))
    return pl.pallas_call(
        _mm_kernel,
        grid=(m // tm, n // tn),
        in_specs=[pl.BlockSpec((tm, k), lambda i, j: (i, 0)),
                  pl.BlockSpec((k, tn), lambda i, j: (0, j))],
        out_specs=pl.BlockSpec((tm, tn), lambda i, j: (i, j)),
        out_shape=jax.ShapeDtypeStruct((m, n), out_dtype),
        compiler_params=_params(("parallel", "arbitrary")),
        name="matmul",
    )(x, w)


def _segment_rows(mod_ref, k, is_ctx):
    return jnp.where(is_ctx, mod_ref[0, 0, k:k + 1, :], mod_ref[0, 1, k:k + 1, :])


def _is_ctx_rows(tm, tiles_per_batch, n_ctx):
    row = (pl.program_id(0) % tiles_per_batch) * tm + lax.broadcasted_iota(jnp.int32, (tm, 1), 0)
    return row < n_ctx


def _norm_mod(x, gain, shift, scale):
    xn = x * lax.rsqrt(jnp.mean(x * x, axis=-1, keepdims=True) + NORM_EPS) * gain
    return xn * (1.0 + scale) + shift


def _proj_kernel(x_ref, gain_ref, mod_ref, w_ref, *rest, tiles_per_batch, n_ctx, heads):
    if heads:
        hp_ref, cm1_ref, sn_ref, o_ref, h_s = rest
    else:
        o_ref, h_s = rest
    tm = x_ref.shape[0]
    is_ctx = _is_ctx_rows(tm, tiles_per_batch, n_ctx)

    @pl.when(pl.program_id(1) == 0)
    def _():
        h = _norm_mod(x_ref[...], gain_ref[...], _segment_rows(mod_ref, 0, is_ctx), _segment_rows(mod_ref, 1, is_ctx))
        h_s[...] = h.astype(BF16)

    acc = jnp.dot(h_s[...], w_ref[...], preferred_element_type=F32)
    if not heads:
        o_ref[...] = acc.astype(o_ref.dtype)
        return
    even = (lax.broadcasted_iota(jnp.int32, (tm, HEAD_DIM), 1) & 1) == 0
    cm1 = cm1_ref[...]
    sn = sn_ref[...]
    for hh in range(heads):
        u = acc[:, hh * HEAD_DIM:(hh + 1) * HEAD_DIM]
        gainvec = hp_ref[hh, 0:1, :]
        norm_on = hp_ref[hh, 1:2, :]
        rope_on = hp_ref[hh, 2:3, :]
        r = lax.rsqrt(jnp.mean(u * u, axis=-1, keepdims=True) + NORM_EPS)
        v = u * jnp.where(norm_on > 0.0, r, 1.0) * gainvec
        partner = jnp.where(even, pltpu.roll(v, HEAD_DIM - 1, axis=1), pltpu.roll(v, 1, axis=1))
        out = v * (1.0 + rope_on * cm1) + partner * (rope_on * sn)
        o_ref[:, hh * HEAD_DIM:(hh + 1) * HEAD_DIM] = out.astype(o_ref.dtype)


def _fused_proj(xa, gain, modsel, w, n_ctx, out_dtype, head_params=None, rope=None):
    b, t, d = xa.shape
    n = w.shape[1]
    tm = _pick(t, (768, 512, 384, 256))
    tpb = t // tm
    heads = 0 if head_params is None else 4
    tn = heads * HEAD_DIM if heads else _pick(n, (1152, 1024, 768, 512, 384, 256, 128))
    in_specs = [pl.BlockSpec((tm, d), lambda i, j: (i, 0)),
                pl.BlockSpec((1, d), lambda i, j: (0, 0)),
                pl.BlockSpec((1, 2, N_MOD, d), lambda i, j: (i // tpb, 0, 0, 0)),
                pl.BlockSpec((d, tn), lambda i, j: (0, j))]
    args = [xa.reshape(b * t, d), gain.reshape(1, d), modsel, w]
    if heads:
        in_specs += [pl.BlockSpec((heads, 3, HEAD_DIM), lambda i, j: (j, 0, 0)),
                     pl.BlockSpec((tm, HEAD_DIM), lambda i, j: (i % tpb, 0)),
                     pl.BlockSpec((tm, HEAD_DIM), lambda i, j: (i % tpb, 0))]
        args += [head_params, rope[0], rope[1]]
    return pl.pallas_call(
        functools.partial(_proj_kernel, tiles_per_batch=tpb, n_ctx=n_ctx, heads=heads),
        grid=(b * tpb, n // tn),
        in_specs=in_specs,
        out_specs=pl.BlockSpec((tm, tn), lambda i, j: (i, j)),
        out_shape=jax.ShapeDtypeStruct((b * t, n), out_dtype),
        scratch_shapes=[pltpu.VMEM((tm, d), BF16)],
        compiler_params=_params(("parallel", "arbitrary")),
        name="norm_mod_proj",
    )(*args)


def _out_proj_kernel(y_ref, w_ref, xa_ref, mod_ref, gain_ref, xo_ref, f_ref, *, tiles_per_batch, n_ctx):
    tm = y_ref.shape[1]
    is_ctx = _is_ctx_rows(tm, tiles_per_batch, n_ctx)
    acc = jnp.dot(y_ref[0], w_ref[...], preferred_element_type=F32)
    x_new = xa_ref[0] + _segment_rows(mod_ref, 2, is_ctx) * acc
    xo_ref[0] = x_new
    f_ref[0] = _norm_mod(x_new, gain_ref[...], _segment_rows(mod_ref, 3, is_ctx), _segment_rows(mod_ref, 4, is_ctx))


def _fused_out_proj(y, w, xa, xa_row0, modsel, gain, n_ctx):
    b, tq, k = y.shape
    d = w.shape[1]
    tm = _pick(tq, (384, 256)) if n_ctx else _pick(tq, (256, 128))
    assert xa_row0 % tm == 0
    r0 = xa_row0 // tm
    tpb = tq // tm
    tile = pl.BlockSpec((1, tm, d), lambda i: (i // tpb, i % tpb, 0))
    return pl.pallas_call(
        functools.partial(_out_proj_kernel, tiles_per_batch=tpb, n_ctx=n_ctx),
        grid=(b * tpb,),
        in_specs=[pl.BlockSpec((1, tm, k), lambda i: (i // tpb, i % tpb, 0)),
                  pl.BlockSpec((k, d), lambda i: (0, 0)),
                  pl.BlockSpec((1, tm, d), lambda i: (i // tpb, r0 + i % tpb, 0)),
                  pl.BlockSpec((1, 2, N_MOD, d), lambda i: (i // tpb, 0, 0, 0)),
                  pl.BlockSpec((1, d), lambda i: (0, 0))],
        out_specs=[tile, tile],
        out_shape=[jax.ShapeDtypeStruct((b, tq, d), F32), jax.ShapeDtypeStruct((b, tq, d), F32)],
        compiler_params=_params(("parallel",)),
        name="out_proj_residual_norm",
    )(y, w, xa, modsel, gain.reshape(1, d))


def _softmax_rows(s):
    m = jnp.max(s, axis=-1, keepdims=True)
    p = jnp.exp(s - m)
    return p, jnp.sum(p, axis=-1, keepdims=True)


def _gqa_heads(q_ref, k, v, o_ref):
    for g in range(A_GROUP):
        q = q_ref[0, :, g * HEAD_DIM:(g + 1) * HEAD_DIM]
        s = lax.dot_general(q, k, (((1,), (1,)), ((), ())), preferred_element_type=F32)
        p, l = _softmax_rows(s)
        o = jnp.dot(p.astype(BF16), v, preferred_element_type=F32) / l
        o_ref[0, :, g * HEAD_DIM:(g + 1) * HEAD_DIM] = o.astype(o_ref.dtype)


def _gqa_kernel(q_ref, k_ref, v_ref, o_ref, *, n_ctx):
    @pl.when(pl.program_id(2) == 0)
    def _():
        _gqa_heads(q_ref, k_ref[0, :n_ctx], v_ref[0, :n_ctx], o_ref)

    @pl.when(pl.program_id(2) > 0)
    def _():
        _gqa_heads(q_ref, k_ref[0], v_ref[0], o_ref)


def _gqa_attention(qkv, n_ctx):
    b, t, _ = qkv.shape
    tq = n_ctx
    gw = A_GROUP * HEAD_DIM
    k0 = QA_W // HEAD_DIM
    v0 = (QA_W + KA_W) // HEAD_DIM
    return pl.pallas_call(
        functools.partial(_gqa_kernel, n_ctx=n_ctx),
        grid=(b, A_KV_HEADS, t // tq),
        in_specs=[pl.BlockSpec((1, tq, gw), lambda bi, h, i: (bi, i, h)),
                  pl.BlockSpec((1, t, HEAD_DIM), lambda bi, h, i: (bi, 0, k0 + h)),
                  pl.BlockSpec((1, t, HEAD_DIM), lambda bi, h, i: (bi, 0, v0 + h))],
        out_specs=pl.BlockSpec((1, tq, gw), lambda bi, h, i: (bi, i, h)),
        out_shape=jax.ShapeDtypeStruct((b, t, QA_W + QB_W), BF16),
        compiler_params=_params(("parallel", "parallel", "arbitrary")),
        name="gqa_attention",
    )(qkv, qkv, qkv)


def _diff_head(lam, gain_ref, q_ref, k, v, o_ref, out_scale):
    parts = []
    for m in range(2):
        q = q_ref[0, :, m * HEAD_DIM:(m + 1) * HEAD_DIM]
        s = lax.dot_general(q, k[:, m * HEAD_DIM:(m + 1) * HEAD_DIM], (((1,), (1,)), ((), ())),
                            preferred_element_type=F32)
        p, l = _softmax_rows(s)
        parts.append(p / l)
    a = parts[0] - lam * parts[1]
    y = jnp.dot(a.astype(BF16), v, preferred_element_type=F32)
    y = y * lax.rsqrt(jnp.mean(y * y, axis=-1, keepdims=True) + NORM_EPS)
    o_ref[0] = (y * gain_ref[...] * out_scale).astype(o_ref.dtype)


def _diff_kernel(lam_ref, gain_ref, q_ref, k_ref, v_ref, y_in_ref, o_ref, *, out_scale, n_ctx):
    del y_in_ref
    lam = lam_ref[0, 0]

    @pl.when(pl.program_id(2) == 0)
    def _():
        _diff_head(lam, gain_ref, q_ref, k_ref[0, :n_ctx], v_ref[0, :n_ctx], o_ref, out_scale)

    @pl.when(pl.program_id(2) > 0)
    def _():
        _diff_head(lam, gain_ref, q_ref, k_ref[0], v_ref[0], o_ref, out_scale)


def _diff_attention(qkv, y, lam, subln_gain, out_scale, n_ctx):
    b, t, _ = qkv.shape
    tq = n_ctx
    hw = 2 * HEAD_DIM
    q0 = (QA_W + 2 * KA_W) // hw
    k0 = q0 + B_HEADS
    v0 = k0 + B_HEADS
    o0 = QA_W // hw
    return pl.pallas_call(
        functools.partial(_diff_kernel, out_scale=out_scale, n_ctx=n_ctx),
        grid=(b, B_HEADS, t // tq),
        in_specs=[pl.BlockSpec(memory_space=pltpu.SMEM),
                  pl.BlockSpec((1, hw), lambda bi, h, i: (0, 0)),
                  pl.BlockSpec((1, tq, hw), lambda bi, h, i: (bi, i, q0 + h)),
                  pl.BlockSpec((1, t, hw), lambda bi, h, i: (bi, 0, k0 + h)),
                  pl.BlockSpec((1, t, hw), lambda bi, h, i: (bi, 0, v0 + h)),
                  pl.BlockSpec(memory_space=pl.ANY)],
        out_specs=pl.BlockSpec((1, tq, hw), lambda bi, h, i: (bi, i, o0 + h)),
        out_shape=jax.ShapeDtypeStruct(y.shape, y.dtype),
        input_output_aliases={5: 0},
        compiler_params=_params(("parallel", "parallel", "arbitrary")),
        name="diff_attention",
    )(lam, subln_gain, qkv, qkv, qkv, y)


def _split3(a):
    a1 = a.astype(BF16)
    r1 = a - a1.astype(F32)
    a2 = r1.astype(BF16)
    a3 = (r1 - a2.astype(F32)).astype(BF16)
    return a1, a2, a3


def _dot_exact_rhs(a, rhs01):
    out = None
    for part in _split3(a):
        d = jnp.dot(part, rhs01, preferred_element_type=F32)
        out = d if out is None else out + d
    return out


def _dot_exact_lhs(lhs01, a):
    out = None
    for part in _split3(a):
        d = jnp.dot(lhs01, part, preferred_element_type=F32)
        out = d if out is None else out + d
    return out


def _ssd_kernel(acr_ref, acc_ref, skip_ref, x_ref, b_ref, c_ref, dtc_ref, dtr_ref, y_ref, state_ref):
    d = pl.program_id(1)
    q = SSD_CHUNK

    @pl.when(pl.program_id(3) == 0)
    def _():
        state_ref[...] = jnp.zeros_like(state_ref)

    rows = lax.broadcasted_iota(jnp.int32, (q, q), 0)
    cols = lax.broadcasted_iota(jnp.int32, (q, q), 1)
    signed = (rows - cols) * (1 - 2 * d)
    keep_sl = signed <= 0
    tri_ks01 = jnp.where(keep_sl, 1.0, 0.0).astype(BF16)
    tri_sk01 = jnp.where(signed >= 0, 1.0, 0.0).astype(BF16)

    x = x_ref[0]
    bm = b_ref[0].astype(BF16)
    cm = c_ref[0].astype(BF16)
    dt_r = dtr_ref[0, 0, 0]
    a_r = dt_r * acc_ref[0, 0]
    a_c = dtc_ref[0, 0, 0] * acr_ref[0, 0]
    cum_r = _dot_exact_rhs(a_r, tri_ks01)
    cum_c = _dot_exact_lhs(tri_sk01, a_c)
    tot = jnp.sum(a_r, axis=1, keepdims=True)
    to_end_r = jnp.exp(tot - cum_r)
    from_start_r = jnp.exp(cum_r)
    tot_e = jnp.exp(tot)
    skip = skip_ref[0, 0]

    g_sl = lax.dot_general(bm, cm, (((1,), (1,)), ((), ())), preferred_element_type=F32)
    state = state_ref[...]
    y_off = lax.dot_general(state.astype(BF16), cm, (((1,), (1,)), ((), ())), preferred_element_type=F32)

    xw_parts = []
    for h in range(SSD_HPG):
        sl = slice(h * SSD_HEADDIM, (h + 1) * SSD_HEADDIM)
        xh = x[sl, :]
        xd = xh * dt_r[h:h + 1, :]
        seg = cum_r[h:h + 1, :] - cum_c[:, h:h + 1]
        decay = jnp.where(keep_sl, jnp.exp(jnp.where(keep_sl, seg, 0.0)), 0.0)
        m_h = (g_sl * decay).astype(BF16)
        y_h = jnp.dot(xd.astype(BF16), m_h, preferred_element_type=F32)
        y_h = y_h + y_off[sl, :] * from_start_r[h:h + 1, :] + skip[h:h + 1, :] * xh
        y_ref[0, 0, sl, :] = y_h.astype(y_ref.dtype)
        xw_parts.append((xd * to_end_r[h:h + 1, :]).astype(BF16))
    s_new = jnp.dot(jnp.concatenate(xw_parts, axis=0), bm, preferred_element_type=F32)
    for h in range(SSD_HPG):
        sl = slice(h * SSD_HEADDIM, (h + 1) * SSD_HEADDIM)
        state_ref[sl, :] = state[sl, :] * tot_e[h:h + 1, :] + s_new[sl, :]


def _ssd_scan(xs_t, bm, cm, dt_c, dt_r, a_row, a_col, skip, n_ctx, out_dtype):
    b, _, t = xs_t.shape
    q = SSD_CHUNK
    ncc = n_ctx // q
    nch = t // q

    def chunk(d, s):
        back = jnp.where(s < ncc, ncc - 1 - s, nch - 1 - s + ncc)
        return jnp.where(d == 0, s, back)

    return pl.pallas_call(
        _ssd_kernel,
        grid=(b, 2, SSD_GROUPS, nch),
        in_specs=[pl.BlockSpec((1, 1, 1, SSD_HPG), lambda bi, d, g, s: (d, g, 0, 0)),
                  pl.BlockSpec((1, 1, SSD_HPG, 1), lambda bi, d, g, s: (d, g, 0, 0)),
                  pl.BlockSpec((1, 1, SSD_HPG, 1), lambda bi, d, g, s: (d, g, 0, 0)),
                  pl.BlockSpec((1, SSD_GROUP_W, q), lambda bi, d, g, s: (bi, g, chunk(d, s))),
                  pl.BlockSpec((1, q, SSD_STATE), lambda bi, d, g, s: (bi, chunk(d, s), g)),
                  pl.BlockSpec((1, q, SSD_STATE), lambda bi, d, g, s: (bi, chunk(d, s), g)),
                  pl.BlockSpec((1, 1, 1, q, SSD_HPG), lambda bi, d, g, s: (d, bi, g, chunk(d, s), 0)),
                  pl.BlockSpec((1, 1, 1, SSD_HPG, q), lambda bi, d, g, s: (d, bi, g, 0, chunk(d, s)))],
        out_specs=pl.BlockSpec((1, 1, SSD_GROUP_W, q), lambda bi, d, g, s: (d, bi, g, chunk(d, s))),
        out_shape=jax.ShapeDtypeStruct((2, b, SSD_INNER, t), out_dtype),
        scratch_shapes=[pltpu.VMEM((SSD_GROUP_W, SSD_STATE), F32)],
        compiler_params=_params(("parallel", "parallel", "parallel", "arbitrary")),
        name="ssd_scan",
    )(a_row, a_col, skip, xs_t, bm, cm, dt_c, dt_r)


def _route_kernel(f_ref, rwt_ref, bias_ref, idx_ref, gate_ref, rank_ref, cnt_ref, carry_ref):
    tn = f_ref.shape[0]
    ne, ng, pg = N_EXPERTS, N_EXPERT_GROUPS, N_EXPERTS // N_EXPERT_GROUPS
    neg = -jnp.inf

    @pl.when(pl.program_id(0) == 0)
    def _():
        carry_ref[...] = jnp.zeros_like(carry_ref)

    logits = lax.dot_general(rwt_ref[...], f_ref[...], (((1,), (1,)), ((), ())),
                             preferred_element_type=F32, precision=lax.Precision.HIGHEST)
    scores = jax.nn.sigmoid(logits)
    g3 = (scores + bias_ref[...]).reshape(ng, pg, tn)
    io3 = lax.broadcasted_iota(jnp.int32, (ng, pg, tn), 1)
    m1 = jnp.max(g3, axis=1, keepdims=True)
    i1 = jnp.min(jnp.where(g3 == m1, io3, pg), axis=1, keepdims=True)
    m2 = jnp.max(jnp.where(io3 == i1, neg, g3), axis=1, keepdims=True)
    work = (m1 + m2).reshape(ng, tn)
    iog = lax.broadcasted_iota(jnp.int32, (ng, tn), 0)
    ok = jnp.zeros((ng, tn), F32)
    for _ in range(TOPK_GROUPS):
        m = jnp.max(work, axis=0, keepdims=True)
        gi = jnp.min(jnp.where(work == m, iog, ng), axis=0, keepdims=True)
        hit = iog == gi
        ok = jnp.where(hit, 1.0, ok)
        work = jnp.where(hit, neg, work)
    sel = jnp.where(ok.reshape(ng, 1, tn) > 0.0, g3, neg).reshape(ne, tn)
    ioe = lax.broadcasted_iota(jnp.int32, (ne, tn), 0)
    onehot = jnp.zeros((ne, tn), F32)
    idxs, ws = [], []
    for _ in range(TOP_K):
        m = jnp.max(sel, axis=0, keepdims=True)
        ei = jnp.min(jnp.where(sel == m, ioe, ne), axis=0, keepdims=True)
        hit = ioe == ei
        idxs.append(ei)
        ws.append(jnp.sum(jnp.where(hit, scores, 0.0), axis=0, keepdims=True))
        sel = jnp.where(hit, neg, sel)
        onehot = jnp.where(hit, 1.0, onehot)
    w = jnp.concatenate(ws, axis=0)
    gate_ref[...] = w / jnp.sum(w, axis=0, keepdims=True) * ROUTED_SCALE
    idx_ref[...] = jnp.concatenate(idxs, axis=0)
    r = lax.broadcasted_iota(jnp.int32, (tn, tn), 0)
    c = lax.broadcasted_iota(jnp.int32, (tn, tn), 1)
    ahead = jnp.where(r < c, 1.0, 0.0).astype(BF16)
    cum = carry_ref[...] + jnp.dot(onehot.astype(BF16), ahead, preferred_element_type=F32)
    ranks = [jnp.sum(jnp.where(ioe == idxs[k], cum, 0.0), axis=0, keepdims=True) for k in range(TOP_K)]
    rank_ref[...] = jnp.concatenate(ranks, axis=0).astype(jnp.int32)
    total = carry_ref[...] + jnp.sum(onehot, axis=1, keepdims=True)
    carry_ref[...] = total
    cnt_ref[...] = total.astype(jnp.int32)


def _route(f, router_wt, router_bias):
    t, d = f.shape
    tn = _pick(t, (512, 256, 128))
    kt = pl.BlockSpec((TOP_K, tn), lambda i: (0, i))
    return pl.pallas_call(
        _route_kernel,
        grid=(t // tn,),
        in_specs=[pl.BlockSpec((tn, d), lambda i: (i, 0)),
                  pl.BlockSpec((N_EXPERTS, d), lambda i: (0, 0)),
                  pl.BlockSpec((N_EXPERTS, 1), lambda i: (0, 0))],
        out_specs=[kt, kt, kt, pl.BlockSpec((N_EXPERTS, 1), lambda i: (0, 0))],
        out_shape=[jax.ShapeDtypeStruct((TOP_K, t), jnp.int32), jax.ShapeDtypeStruct((TOP_K, t), F32),
                   jax.ShapeDtypeStruct((TOP_K, t), jnp.int32), jax.ShapeDtypeStruct((N_EXPERTS, 1), jnp.int32)],
        scratch_shapes=[pltpu.VMEM((N_EXPERTS, 1), F32)],
        compiler_params=_params(("arbitrary",)),
        name="moe_route",
    )(f, router_wt, router_bias)


def _scatter_kernel(dest_ref, f_ref, xs_ref, sem):
    ts = f_ref.shape[0]

    def row_copy(t, k):
        return pltpu.make_async_copy(f_ref.at[pl.ds(t, 1)], xs_ref.at[pl.ds(dest_ref[t * TOP_K + k], 1)], sem)

    def issue(t, carry):
        for k in range(TOP_K):
            row_copy(t, k).start()
        return carry

    def drain(t, carry):
        for k in range(TOP_K):
            row_copy(t, k).wait()
        return carry

    lax.fori_loop(0, ts, issue, 0)
    lax.fori_loop(0, ts, drain, 0)


def _scatter_rows(dest_flat, f, n_rows_alloc):
    t, d = f.shape
    ts = _pick(t, (256, 128))
    return pl.pallas_call(
        _scatter_kernel,
        grid=(t // ts,),
        in_specs=[pl.BlockSpec((ts * TOP_K,), lambda i: (i,), memory_space=pltpu.SMEM),
                  pl.BlockSpec((ts, d), lambda i: (i, 0))],
        out_specs=pl.BlockSpec(memory_space=pl.ANY),
        out_shape=jax.ShapeDtypeStruct((n_rows_alloc, d), f.dtype),
        scratch_shapes=[pltpu.SemaphoreType.DMA(())],
        compiler_params=_params(("arbitrary",)),
        name="moe_scatter",
    )(dest_flat, f)


def _zero_rows_kernel(rows_ref, xs_in_ref, xs_ref, zero_ref, sem):
    del xs_in_ref
    zero_ref[...] = jnp.zeros_like(zero_ref)
    n = rows_ref.shape[0]

    def row_copy(i):
        return pltpu.make_async_copy(zero_ref, xs_ref.at[pl.ds(rows_ref[i], 1)], sem)

    def issue(i, carry):
        row_copy(i).start()
        return carry

    def drain(i, carry):
        row_copy(i).wait()
        return carry

    lax.fori_loop(0, n, issue, 0)
    lax.fori_loop(0, n, drain, 0)


def _zero_rows(rows, xs):
    n = rows.shape[0]
    step = _pick(n, (512, 256, 128))
    return pl.pallas_call(
        _zero_rows_kernel,
        grid=(n // step,),
        in_specs=[pl.BlockSpec((step,), lambda i: (i,), memory_space=pltpu.SMEM),
                  pl.BlockSpec(memory_space=pl.ANY)],
        out_specs=pl.BlockSpec(memory_space=pl.ANY),
        out_shape=jax.ShapeDtypeStruct(xs.shape, xs.dtype),
        scratch_shapes=[pltpu.VMEM((1,) + xs.shape[1:], xs.dtype), pltpu.SemaphoreType.DMA(())],
        input_output_aliases={1: 0},
        compiler_params=_params(("arbitrary",)),
        name="moe_zero_pad_rows",
    )(rows, xs)


def _expert_kernel(be_ref, blk_ref, nused_ref, x_ref, wg_ref, wu_ref, wd_ref, o_ref, wg_s, wu_s, wd_s):
    i = pl.program_id(0)

    @pl.when(jnp.logical_or(i == 0, be_ref[i] != be_ref[jnp.maximum(i - 1, 0)]))
    def _():
        wg_s[...] = wg_ref[0].astype(BF16)
        wu_s[...] = wu_ref[0].astype(BF16)
        wd_s[...] = wd_ref[0].astype(BF16)

    @pl.when(i < nused_ref[0])
    def _():
        x = x_ref[...].astype(BF16)
        hg = jnp.dot(x, wg_s[...], preferred_element_type=F32)
        hu = jnp.dot(x, wu_s[...], preferred_element_type=F32)
        h = (hg * jax.nn.sigmoid(hg) * hu).astype(BF16)
        o_ref[...] = jnp.dot(h, wd_s[...], preferred_element_type=F32)


def _expert_blocks(block_expert, block_index, n_used, xs, n_blocks, tm, w_gate, w_up, w_down):
    d, ff = w_gate.shape[1:]
    grid_spec = pltpu.PrefetchScalarGridSpec(
        num_scalar_prefetch=3,
        grid=(n_blocks,),
        in_specs=[pl.BlockSpec((tm, d), lambda i, be, blk, nu: (blk[i], 0)),
                  pl.BlockSpec((1, d, ff), lambda i, be, blk, nu: (be[i], 0, 0)),
                  pl.BlockSpec((1, d, ff), lambda i, be, blk, nu: (be[i], 0, 0)),
                  pl.BlockSpec((1, ff, d), lambda i, be, blk, nu: (be[i], 0, 0))],
        out_specs=pl.BlockSpec((tm, d), lambda i, be, blk, nu: (blk[i], 0)),
        scratch_shapes=[pltpu.VMEM((d, ff), BF16), pltpu.VMEM((d, ff), BF16), pltpu.VMEM((ff, d), BF16)],
    )
    return pl.pallas_call(
        _expert_kernel,
        grid_spec=grid_spec,
        out_shape=jax.ShapeDtypeStruct((n_blocks * tm, d), F32),
        compiler_params=_params(("arbitrary",)),
        name="moe_experts",
    )(block_expert, block_index, n_used, xs, w_gate, w_up, w_down)


def _combine_kernel(dest_ref, gate_ref, sh_ref, xa_ref, mod_ref, ys_ref, o_ref, buf, sem, *, tiles_per_batch, n_ctx):
    tn = gate_ref.shape[0]

    def row_copy(t, k):
        return pltpu.make_async_copy(ys_ref.at[pl.ds(dest_ref[t * TOP_K + k], 1)], buf.at[k, pl.ds(t, 1)], sem)

    def issue(t, carry):
        for k in range(TOP_K):
            row_copy(t, k).start()
        return carry

    def drain(t, carry):
        for k in range(TOP_K):
            row_copy(t, k).wait()
        return carry

    lax.fori_loop(0, tn, issue, 0)
    lax.fori_loop(0, tn, drain, 0)
    acc = sh_ref[...]
    for k in range(TOP_K):
        acc = acc + gate_ref[:, k:k + 1] * buf[k]
    is_ctx = _is_ctx_rows(tn, tiles_per_batch, n_ctx)
    o_ref[...] = xa_ref[...] + _segment_rows(mod_ref, 5, is_ctx) * acc


def _combine(dest_flat, gates, shared, ys, xa, modsel, rows_per_batch, n_ctx):
    t, d = shared.shape
    tn = 128
    tpb = rows_per_batch // tn
    return pl.pallas_call(
        functools.partial(_combine_kernel, tiles_per_batch=tpb, n_ctx=n_ctx),
        grid=(t // tn,),
        in_specs=[pl.BlockSpec((tn * TOP_K,), lambda i: (i,), memory_space=pltpu.SMEM),
                  pl.BlockSpec((tn, TOP_K), lambda i: (i, 0)),
                  pl.BlockSpec((tn, d), lambda i: (i, 0)),
                  pl.BlockSpec((tn, d), lambda i: (i, 0)),
                  pl.BlockSpec((1, 2, N_MOD, d), lambda i: (i // tpb, 0, 0, 0)),
                  pl.BlockSpec(memory_space=pl.ANY)],
        out_specs=pl.BlockSpec((tn, d), lambda i: (i, 0)),
        out_shape=jax.ShapeDtypeStruct((t, d), F32),
        scratch_shapes=[pltpu.VMEM((TOP_K, tn, d), F32), pltpu.SemaphoreType.DMA(())],
        compiler_params=_params(("arbitrary",)),
        name="moe_combine",
    )(dest_flat, gates, shared, xa, modsel, ys)


def _swiglu_kernel(x_ref, wg_ref, wu_ref, wd_ref, o_ref):
    x = x_ref[...].astype(BF16)
    hg = jnp.dot(x, wg_ref[...], preferred_element_type=F32)
    hu = jnp.dot(x, wu_ref[...], preferred_element_type=F32)
    h = (hg * jax.nn.sigmoid(hg) * hu).astype(BF16)
    o_ref[...] = jnp.dot(h, wd_ref[...], preferred_element_type=F32).astype(o_ref.dtype)


def _shared_expert(x, wg, wu, wd):
    m, d = x.shape
    ff = wg.shape[1]
    tm = _pick(m, (512, 384, 256, 128))
    return pl.pallas_call(
        _swiglu_kernel,
        grid=(m // tm,),
        in_specs=[pl.BlockSpec((tm, d), lambda i: (i, 0)),
                  pl.BlockSpec((d, ff), lambda i: (0, 0)),
                  pl.BlockSpec((d, ff), lambda i: (0, 0)),
                  pl.BlockSpec((ff, d), lambda i: (0, 0))],
        out_specs=pl.BlockSpec((tm, d), lambda i: (i, 0)),
        out_shape=jax.ShapeDtypeStruct((m, d), F32),
        compiler_params=_params(("parallel",)),
        name="shared_expert",
    )(x, wg, wu, wd)


def _moe_ffn(f, xa, modsel, rows_per_batch, n_ctx, router_w, router_bias, w_gate, w_up, w_down,
             ws_gate, ws_up, ws_down):
    t, d = f.shape
    tm = EXPERT_TM
    idx, gate, rank, cnt = _route(f, router_w.T, router_bias.astype(F32)[:, None])
    counts = cnt[:, 0]
    padded = (counts + tm - 1) // tm * tm
    pad_end = jnp.cumsum(padded)
    pad_start = pad_end - padded
    n_used = pad_end[-1] // tm
    n_blocks = (t * TOP_K) // tm + N_EXPERTS
    n_rows = n_blocks * tm
    experts = jnp.arange(N_EXPERTS, dtype=jnp.int32)
    start_of = jnp.sum(jnp.where(idx[..., None] == experts, pad_start, 0), axis=-1)
    dest = (start_of + rank).T.reshape(-1).astype(jnp.int32)
    block_index = jnp.minimum(jnp.arange(n_blocks, dtype=jnp.int32), n_used - 1).astype(jnp.int32)
    block_expert = jnp.minimum(jnp.sum(pad_end[None, :] <= (block_index * tm)[:, None], axis=1),
                               N_EXPERTS - 1).astype(jnp.int32)
    j = jnp.arange(tm, dtype=jnp.int32)[None, :]
    e = jnp.arange(N_EXPERTS, dtype=jnp.int32)[:, None]
    pad_rows = jnp.where(counts[:, None] + j < padded[:, None], pad_start[:, None] + counts[:, None] + j,
                         n_rows + e * tm + j).reshape(-1).astype(jnp.int32)
    xs = _scatter_rows(dest, f, n_rows + N_EXPERTS * tm)
    xs = _zero_rows(pad_rows, xs)
    ys = _expert_blocks(block_expert, block_index, n_used.reshape(1).astype(jnp.int32), xs, n_blocks, tm,
                        w_gate, w_up, w_down)
    shared = _shared_expert(f, ws_gate.astype(BF16), ws_up.astype(BF16), ws_down.astype(BF16))
    return _combine(dest, gate.T, shared, ys, xa, modsel, rows_per_batch, n_ctx)


def _rms(u, gain):
    return u * lax.rsqrt(jnp.mean(u * u, axis=-1, keepdims=True) + NORM_EPS) * gain


def _rope_tables(n_ctx, n_lat):
    rows = n_lat // GRID_W
    row = jnp.repeat(jnp.arange(rows, dtype=F32), GRID_W)
    col = jnp.tile(jnp.arange(GRID_W, dtype=F32), rows)
    n_freq = HEAD_DIM // 4
    inv = ROPE_THETA ** (-jnp.arange(n_freq, dtype=F32) / n_freq)
    ang = jnp.concatenate([row[:, None] * inv, col[:, None] * inv], axis=-1)
    cos = jnp.repeat(jnp.cos(ang), 2, axis=-1)
    sin = jnp.repeat(jnp.sin(ang), 2, axis=-1)
    sign = jnp.tile(jnp.array([-1.0, 1.0], F32), HEAD_DIM // 2)
    cos = jnp.concatenate([jnp.ones((n_ctx, HEAD_DIM), F32), cos], axis=0)
    sin = jnp.concatenate([jnp.zeros((n_ctx, HEAD_DIM), F32), sin * sign], axis=0)
    return cos, sin


def _rope(u, cos, sin):
    up = u.reshape(u.shape[:-1] + (HEAD_DIM // 2, 2))
    partner = jnp.flip(up, axis=-1).reshape(u.shape)
    return u * cos[None, :, None, :] + partner * sin[None, :, None, :]


def _ada_mod(cond, w, bias):
    m = jax.nn.silu(cond)
    rows = m.shape[0]
    pad = (-rows) % 16
    mp = jnp.pad(m, ((0, pad), (0, 0))).astype(BF16)
    out = _matmul(mp, w.astype(BF16), F32)[:rows] + bias
    return out.reshape(rows, N_MOD, -1)


def _head_params(q_gain, k_gain):
    scale = HEAD_DIM ** -0.5
    ones = jnp.ones((HEAD_DIM,), F32)
    zeros = jnp.zeros((HEAD_DIM,), F32)

    def rows(gainvec, norm_on, rope_on, count):
        one = jnp.stack([gainvec, ones if norm_on else zeros, ones if rope_on else zeros])
        return jnp.broadcast_to(one, (count, 3, HEAD_DIM))

    return jnp.concatenate([rows(q_gain * scale, True, True, A_Q_HEADS), rows(k_gain, True, True, A_KV_HEADS),
                            rows(ones, False, False, A_KV_HEADS), rows(ones * scale, False, True, 2 * B_HEADS),
                            rows(ones, False, True, 2 * B_HEADS), rows(ones, False, False, 2 * B_HEADS)], axis=0)


def _attention_layer(xa, modsel, gain, n_ctx, w_in, q_gain, k_gain, lq1, lk1, lq2, lk2, subln, lambda_init):
    b, t, d = xa.shape
    cos, sin = _rope_tables(n_ctx, t - n_ctx)
    qkv = _fused_proj(xa, gain, modsel, w_in.astype(BF16), n_ctx, BF16,
                      head_params=_head_params(q_gain, k_gain), rope=(cos - 1.0, sin)).reshape(b, t, ATTN_IN)
    lam = (jnp.exp(jnp.sum(lq1 * lk1)) - jnp.exp(jnp.sum(lq2 * lk2)) + lambda_init).reshape(1, 1).astype(F32)
    y = _gqa_attention(qkv, n_ctx)
    return _diff_attention(qkv, y, lam, subln.reshape(1, 2 * HEAD_DIM), 1.0 - lambda_init, n_ctx)


def _conv_silu(u, w, bias):
    up = jnp.pad(u, ((0, 0), (1, 1), (0, 0)))
    out = up[:, :-2] * w[0] + up[:, 1:-1] * w[1] + up[:, 2:] * w[2] + bias
    return jax.nn.silu(out)


def _ssd_layer(xa, modsel, gain, n_ctx, w_in, conv_w, conv_b, dt_bias, a_log, d_skip, norm_gain):
    b, t, d = xa.shape
    proj = _fused_proj(xa, gain, modsel, w_in.astype(BF16), n_ctx, F32).reshape(b, t, -1)
    z = proj[..., :SSD_INNER]
    xbc = proj[..., SSD_INNER:SSD_INNER + SSD_CONV_DIM]
    dt = proj[..., SSD_INNER + SSD_CONV_DIM:]
    xbc = jnp.concatenate([_conv_silu(xbc[:, :n_ctx], conv_w, conv_b),
                           _conv_silu(xbc[:, n_ctx:], conv_w, conv_b)], axis=1)
    xs = xbc[..., :SSD_INNER]
    bm = xbc[..., SSD_INNER:SSD_INNER + SSD_GROUPS * SSD_STATE]
    cm = xbc[..., SSD_INNER + SSD_GROUPS * SSD_STATE:]
    dt = jax.nn.softplus(dt.reshape(b, t, 2, SSD_GROUPS, SSD_HPG) + dt_bias.reshape(2, SSD_GROUPS, SSD_HPG))
    dt_c = jnp.transpose(dt, (2, 0, 3, 1, 4))
    dt_r = jnp.transpose(dt, (2, 0, 3, 4, 1))
    a_coef = -jnp.exp(a_log)
    a_row = a_coef.reshape(2, SSD_GROUPS, 1, SSD_HPG)
    a_col = a_coef.reshape(2, SSD_GROUPS, SSD_HPG, 1)
    skip = d_skip.reshape(2, SSD_GROUPS, SSD_HPG, 1)
    y = _ssd_scan(jnp.transpose(xs, (0, 2, 1)), bm, cm, dt_c, dt_r, a_row, a_col, skip, n_ctx, F32)
    y = jnp.transpose(y[0] + y[1], (0, 2, 1))[:, n_ctx:]
    zl = z[:, n_ctx:]
    return _rms(y * jax.nn.silu(zl), norm_gain).astype(BF16)


def kernel(x, c, ctx, c_ctx, mod_w, mod_b, norm_mix, norm_ffn, norm_final, attn_w_in, attn_w_out, attn_q_gain,
           attn_k_gain, diff_lam_q1, diff_lam_k1, diff_lam_q2, diff_lam_k2, diff_subln, ssd_w_in, ssd_conv_w,
           ssd_conv_b, ssd_dt_bias, ssd_a_log, ssd_d, ssd_norm, ssd_w_out, router_w, router_bias, exp_w_gate,
           exp_w_up, exp_w_down, shared_w_gate, shared_w_up, shared_w_down):
    b, s, d = x.shape
    n_ctx = ctx.shape[1]
    t = n_ctx + s
    xa = jnp.concatenate([ctx, x], axis=1)
    cond = jnp.concatenate([c, c_ctx[None]], axis=0)
    out = None
    for i in range(DEPTH):
        last = i == DEPTH - 1
        mod = _ada_mod(cond, mod_w[i], mod_b[i])
        modsel = jnp.stack([jnp.broadcast_to(mod[b], (b, N_MOD, d)), mod[:b]], axis=1)
        j = i // 2
        moe_w = (router_w[i], router_bias[i], exp_w_gate[i], exp_w_up[i], exp_w_down[i],
                 shared_w_gate[i], shared_w_up[i], shared_w_down[i])
        if i % 2 == 0:
            lambda_init = 0.8 - 0.6 * math.exp(-0.3 * i)
            y = _attention_layer(xa, modsel, norm_mix[i], n_ctx, attn_w_in[j], attn_q_gain[j], attn_k_gain[j],
                                 diff_lam_q1[j], diff_lam_k1[j], diff_lam_q2[j], diff_lam_k2[j],
                                 diff_subln[j], lambda_init)
            w_out = attn_w_out[j]
        else:
            y = _ssd_layer(xa, modsel, norm_mix[i], n_ctx, ssd_w_in[j], ssd_conv_w[j], ssd_conv_b[j],
                           ssd_dt_bias[j], ssd_a_log[j], ssd_d[j], ssd_norm[j])
            w_out = ssd_w_out[j]
        if last:
            x_new, f = _fused_out_proj(y if y.shape[1] == s else y[:, n_ctx:], w_out.astype(BF16), xa, n_ctx,
                                       modsel, norm_ffn[i], 0)
            out = _moe_ffn(f.reshape(b * s, d), x_new.reshape(b * s, d), modsel, s, 0, *moe_w).reshape(b, s, d)
        else:
            x_new, f = _fused_out_proj(y, w_out.astype(BF16), xa, 0, modsel, norm_ffn[i], n_ctx)
            xa = _moe_ffn(f.reshape(b * t, d), x_new.reshape(b * t, d), modsel, t, n_ctx, *moe_w).reshape(b, t, d)
    return _rms(out, norm_final)
```

```python
import functools
import math

import jax
import jax.numpy as jnp
from jax import lax
from jax.experimental import pallas as pl
from jax.experimental.pallas import tpu as pltpu

F32 = jnp.float32
BF16 = jnp.bfloat16

D_MODEL = 2048
DEPTH = 2
GRID_W = 64
NORM_EPS = 1e-6
N_MOD = 6
HEAD_DIM = 128
ROPE_THETA = 10000.0
A_Q_HEADS = 8
A_KV_HEADS = 2
A_GROUP = A_Q_HEADS // A_KV_HEADS
B_HEADS = 4
QA_W = A_Q_HEADS * HEAD_DIM
KA_W = A_KV_HEADS * HEAD_DIM
QB_W = 2 * B_HEADS * HEAD_DIM
ATTN_IN = QA_W + 2 * KA_W + 3 * QB_W
SSD_INNER = 2 * D_MODEL
SSD_HEADDIM = 64
SSD_HEADS = SSD_INNER // SSD_HEADDIM
SSD_GROUPS = 8
SSD_HPG = SSD_HEADS // SSD_GROUPS
SSD_STATE = 128
SSD_CONV = 3
SSD_CHUNK = 128
SSD_GROUP_W = SSD_HPG * SSD_HEADDIM
SSD_CONV_DIM = SSD_INNER + 2 * SSD_GROUPS * SSD_STATE
N_EXPERTS = 64
EXPERT_FF = 512
TOP_K = 8
N_EXPERT_GROUPS = 8
TOPK_GROUPS = 4
ROUTED_SCALE = 2.5
EXPERT_TM = 512

VMEM_LIMIT_BYTES = 56 * 1024 * 1024


def _pick(n, prefs):
    for p in prefs:
        if n % p == 0:
            return p
    raise ValueError(f"no tile in {prefs} divides {n}")


def _params(sem):
    return pltpu.CompilerParams(dimension_semantics=sem, vmem_limit_bytes=VMEM_LIMIT_BYTES)


def _mm_kernel(x_ref, w_ref, o_ref):
    o_ref[...] = jnp.dot(x_ref[...], w_ref[...], preferred_element_type=F32).astype(o_ref.dtype)


def _matmul(x, w, out_dtype):
    m, k = x.shape
    n = w.shape[1]
    tm = _pick(m, (1024, 512, 384, 256, 128, 16, 8))
    tn = _pick(n, (1024, 512, 384, 256, 128))
    return pl.pallas_call(
        _mm_kernel,
        grid=(m // tm, n // tn),
        in_specs=[pl.BlockSpec((tm, k), lambda i, j: (i, 0)),
                  pl.BlockSpec((k, tn), lambda i, j: (0, j))],
        out_specs=pl.BlockSpec((tm, tn), lambda i, j: (i, j)),
        out_shape=jax.ShapeDtypeStruct((m, n), out_dtype),
        compiler_params=_params(("parallel", "arbitrary")),
        name="matmul",
    )(x, w)


def _segment_rows(mod_ref, k, is_ctx):
    return jnp.where(is_ctx, mod_ref[0, 0, k:k + 1, :], mod_ref[0, 1, k:k + 1, :])


def _is_ctx_rows(tm, tiles_per_batch, n_ctx):
    row = (pl.program_id(0) % tiles_per_batch) * tm + lax.broadcasted_iota(jnp.int32, (tm, 1), 0)
    return row < n_ctx


def _norm_mod(x, gain, shift, scale):
    xn = x * lax.rsqrt(jnp.mean(x * x, axis=-1, keepdims=True) + NORM_EPS) * gain
    return xn * (1.0 + scale) + shift


def _proj_kernel(x_ref, gain_ref, mod_ref, w_ref, *rest, tiles_per_batch, n_ctx, heads):
    if heads:
        hp_ref, cm1_ref, sn_ref, o_ref, h_s = rest
    else:
        o_ref, h_s = rest
    tm = x_ref.shape[0]
    is_ctx = _is_ctx_rows(tm, tiles_per_batch, n_ctx)

    @pl.when(pl.program_id(1) == 0)
    def _():
        h = _norm_mod(x_ref[...], gain_ref[...], _segment_rows(mod_ref, 0, is_ctx), _segment_rows(mod_ref, 1, is_ctx))
        h_s[...] = h.astype(BF16)

    acc = jnp.dot(h_s[...], w_ref[...], preferred_element_type=F32)
    if not heads:
        o_ref[...] = acc.astype(o_ref.dtype)
        return
    even = (lax.broadcasted_iota(jnp.int32, (tm, HEAD_DIM), 1) & 1) == 0
    cm1 = cm1_ref[...]
    sn = sn_ref[...]
    for hh in range(heads):
        u = acc[:, hh * HEAD_DIM:(hh + 1) * HEAD_DIM]
        gainvec = hp_ref[hh, 0:1, :]
        norm_on = hp_ref[hh, 1:2, :]
        rope_on = hp_ref[hh, 2:3, :]
        r = lax.rsqrt(jnp.mean(u * u, axis=-1, keepdims=True) + NORM_EPS)
        v = u * jnp.where(norm_on > 0.0, r, 1.0) * gainvec
        partner = jnp.where(even, pltpu.roll(v, HEAD_DIM - 1, axis=1), pltpu.roll(v, 1, axis=1))
        out = v * (1.0 + rope_on * cm1) + partner * (rope_on * sn)
        o_ref[:, hh * HEAD_DIM:(hh + 1) * HEAD_DIM] = out.astype(o_ref.dtype)


def _fused_proj(xa, gain, modsel, w, n_ctx, out_dtype, head_params=None, rope=None):
    b, t, d = xa.shape
    n = w.shape[1]
    tm = _pick(t, (768, 512, 384, 256))
    tpb = t // tm
    heads = 0 if head_params is None else 4
    tn = heads * HEAD_DIM if heads else _pick(n, (1152, 1024, 768, 512, 384, 256, 128))
    in_specs = [pl.BlockSpec((tm, d), lambda i, j: (i, 0)),
                pl.BlockSpec((1, d), lambda i, j: (0, 0)),
                pl.BlockSpec((1, 2, N_MOD, d), lambda i, j: (i // tpb, 0, 0, 0)),
                pl.BlockSpec((d, tn), lambda i, j: (0, j))]
    args = [xa.reshape(b * t, d), gain.reshape(1, d), modsel, w]
    if heads:
        in_specs += [pl.BlockSpec((heads, 3, HEAD_DIM), lambda i, j: (j, 0, 0)),
                     pl.BlockSpec((tm, HEAD_DIM), lambda i, j: (i % tpb, 0)),
                     pl.BlockSpec((tm, HEAD_DIM), lambda i, j: (i % tpb, 0))]
        args += [head_params, rope[0], rope[1]]
    return pl.pallas_call(
        functools.partial(_proj_kernel, tiles_per_batch=tpb, n_ctx=n_ctx, heads=heads),
        grid=(b * tpb, n // tn),
        in_specs=in_specs,
        out_specs=pl.BlockSpec((tm, tn), lambda i, j: (i, j)),
        out_shape=jax.ShapeDtypeStruct((b * t, n), out_dtype),
        scratch_shapes=[pltpu.VMEM((tm, d), BF16)],
        compiler_params=_params(("parallel", "arbitrary")),
        name="norm_mod_proj",
    )(*args)


def _out_proj_kernel(y_ref, w_ref, xa_ref, mod_ref, gain_ref, xo_ref, f_ref, *, tiles_per_batch, n_ctx):
    tm = y_ref.shape[1]
    is_ctx = _is_ctx_rows(tm, tiles_per_batch, n_ctx)
    acc = jnp.dot(y_ref[0], w_ref[...], preferred_element_type=F32)
    x_new = xa_ref[0] + _segment_rows(mod_ref, 2, is_ctx) * acc
    xo_ref[0] = x_new
    f_ref[0] = _norm_mod(x_new, gain_ref[...], _segment_rows(mod_ref, 3, is_ctx), _segment_rows(mod_ref, 4, is_ctx))


def _fused_out_proj(y, w, xa, xa_row0, modsel, gain, n_ctx):
    b, tq, k = y.shape
    d = w.shape[1]
    tm = _pick(tq, (384, 256)) if n_ctx else _pick(tq, (256, 128))
    assert xa_row0 % tm == 0
    r0 = xa_row0 // tm
    tpb = tq // tm
    tile = pl.BlockSpec((1, tm, d), lambda i: (i // tpb, i % tpb, 0))
    return pl.pallas_call(
        functools.partial(_out_proj_kernel, tiles_per_batch=tpb, n_ctx=n_ctx),
        grid=(b * tpb,),
        in_specs=[pl.BlockSpec((1, tm, k), lambda i: (i // tpb, i % tpb, 0)),
                  pl.BlockSpec((k, d), lambda i: (0, 0)),
                  pl.BlockSpec((1, tm, d), lambda i: (i // tpb, r0 + i % tpb, 0)),
                  pl.BlockSpec((1, 2, N_MOD, d), lambda i: (i // tpb, 0, 0, 0)),
                  pl.BlockSpec((1, d), lambda i: (0, 0))],
        out_specs=[tile, tile],
        out_shape=[jax.ShapeDtypeStruct((b, tq, d), F32), jax.ShapeDtypeStruct((b, tq, d), F32)],
        compiler_params=_params(("parallel",)),
        name="out_proj_residual_norm",
    )(y, w, xa, modsel, gain.reshape(1, d))


def _softmax_rows(s):
    m = jnp.max(s, axis=-1, keepdims=True)
    p = jnp.exp(s - m)
    return p, jnp.sum(p, axis=-1, keepdims=True)


def _gqa_heads(q_ref, k, v, o_ref):
    for g in range(A_GROUP):
        q = q_ref[0, :, g * HEAD_DIM:(g + 1) * HEAD_DIM]
        s = lax.dot_general(q, k, (((1,), (1,)), ((), ())), preferred_element_type=F32)
        p, l = _softmax_rows(s)
        o = jnp.dot(p.astype(BF16), v, preferred_element_type=F32) / l
        o_ref[0, :, g * HEAD_DIM:(g + 1) * HEAD_DIM] = o.astype(o_ref.dtype)


def _gqa_kernel(q_ref, k_ref, v_ref, o_ref, *, n_ctx):
    @pl.when(pl.program_id(2) == 0)
    def _():
        _gqa_heads(q_ref, k_ref[0, :n_ctx], v_ref[0, :n_ctx], o_ref)

    @pl.when(pl.program_id(2) > 0)
    def _():
        _gqa_heads(q_ref, k_ref[0], v_ref[0], o_ref)


def _gqa_attention(qkv, n_ctx):
    b, t, _ = qkv.shape
    tq = n_ctx
    gw = A_GROUP * HEAD_DIM
    k0 = QA_W // HEAD_DIM
    v0 = (QA_W + KA_W) // HEAD_DIM
    return pl.pallas_call(
        functools.partial(_gqa_kernel, n_ctx=n_ctx),
        grid=(b, A_KV_HEADS, t // tq),
        in_specs=[pl.BlockSpec((1, tq, gw), lambda bi, h, i: (bi, i, h)),
                  pl.BlockSpec((1, t, HEAD_DIM), lambda bi, h, i: (bi, 0, k0 + h)),
                  pl.BlockSpec((1, t, HEAD_DIM), lambda bi, h, i: (bi, 0, v0 + h))],
        out_specs=pl.BlockSpec((1, tq, gw), lambda bi, h, i: (bi, i, h)),
        out_shape=jax.ShapeDtypeStruct((b, t, QA_W + QB_W), BF16),
        compiler_params=_params(("parallel", "parallel", "arbitrary")),
        name="gqa_attention",
    )(qkv, qkv, qkv)


def _diff_head(lam, gain_ref, q_ref, k, v, o_ref, out_scale):
    parts = []
    for m in range(2):
        q = q_ref[0, :, m * HEAD_DIM:(m + 1) * HEAD_DIM]
        s = lax.dot_general(q, k[:, m * HEAD_DIM:(m + 1) * HEAD_DIM], (((1,), (1,)), ((), ())),
                            preferred_element_type=F32)
        p, l = _softmax_rows(s)
        parts.append(p / l)
    a = parts[0] - lam * parts[1]
    y = jnp.dot(a.astype(BF16), v, preferred_element_type=F32)
    y = y * lax.rsqrt(jnp.mean(y * y, axis=-1, keepdims=True) + NORM_EPS)
    o_ref[0] = (y * gain_ref[...] * out_scale).astype(o_ref.dtype)


def _diff_kernel(lam_ref, gain_ref, q_ref, k_ref, v_ref, y_in_ref, o_ref, *, out_scale, n_ctx):
    del y_in_ref
    lam = lam_ref[0, 0]

    @pl.when(pl.program_id(2) == 0)
    def _():
        _diff_head(lam, gain_ref, q_ref, k_ref[0, :n_ctx], v_ref[0, :n_ctx], o_ref, out_scale)

    @pl.when(pl.program_id(2) > 0)
    def _():
        _diff_head(lam, gain_ref, q_ref, k_ref[0], v_ref[0], o_ref, out_scale)


def _diff_attention(qkv, y, lam, subln_gain, out_scale, n_ctx):
    b, t, _ = qkv.shape
    tq = n_ctx
    hw = 2 * HEAD_DIM
    q0 = (QA_W + 2 * KA_W) // hw
    k0 = q0 + B_HEADS
    v0 = k0 + B_HEADS
    o0 = QA_W // hw
    return pl.pallas_call(
        functools.partial(_diff_kernel, out_scale=out_scale, n_ctx=n_ctx),
        grid=(b, B_HEADS, t // tq),
        in_specs=[pl.BlockSpec(memory_space=pltpu.SMEM),
                  pl.BlockSpec((1, hw), lambda bi, h, i: (0, 0)),
                  pl.BlockSpec((1, tq, hw), lambda bi, h, i: (bi, i, q0 + h)),
                  pl.BlockSpec((1, t, hw), lambda bi, h, i: (bi, 0, k0 + h)),
                  pl.BlockSpec((1, t, hw), lambda bi, h, i: (bi, 0, v0 + h)),
                  pl.BlockSpec(memory_space=pl.ANY)],
        out_specs=pl.BlockSpec((1, tq, hw), lambda bi, h, i: (bi, i, o0 + h)),
        out_shape=jax.ShapeDtypeStruct(y.shape, y.dtype),
        input_output_aliases={5: 0},
        compiler_params=_params(("parallel", "parallel", "arbitrary")),
        name="diff_attention",
    )(lam, subln_gain, qkv, qkv, qkv, y)


def _split3(a):
    a1 = a.astype(BF16)
    r1 = a - a1.astype(F32)
    a2 = r1.astype(BF16)
    a3 = (r1 - a2.astype(F32)).astype(BF16)
    return a1, a2, a3


def _dot_exact_rhs(a, rhs01):
    out = None
    for part in _split3(a):
        d = jnp.dot(part, rhs01, preferred_element_type=F32)
        out = d if out is None else out + d
    return out


def _dot_exact_lhs(lhs01, a):
    out = None
    for part in _split3(a):
        d = jnp.dot(lhs01, part, preferred_element_type=F32)
        out = d if out is None else out + d
    return out


def _ssd_kernel(acr_ref, acc_ref, skip_ref, x_ref, b_ref, c_ref, dtc_ref, dtr_ref, y_ref, state_ref):
    d = pl.program_id(1)
    q = SSD_CHUNK

    @pl.when(pl.program_id(3) == 0)
    def _():
        state_ref[...] = jnp.zeros_like(state_ref)

    rows = lax.broadcasted_iota(jnp.int32, (q, q), 0)
    cols = lax.broadcasted_iota(jnp.int32, (q, q), 1)
    signed = (rows - cols) * (1 - 2 * d)
    keep_sl = signed <= 0
    tri_ks01 = jnp.where(keep_sl, 1.0, 0.0).astype(BF16)
    tri_sk01 = jnp.where(signed >= 0, 1.0, 0.0).astype(BF16)

    x = jnp.transpose(x_ref[0])
    bm = b_ref[0].astype(BF16)
    cm = c_ref[0].astype(BF16)
    dt_r = dtr_ref[0, 0, 0]
    a_r = dt_r * acc_ref[0, 0]
    a_c = dtc_ref[0, 0, 0] * acr_ref[0, 0]
    cum_r = _dot_exact_rhs(a_r, tri_ks01)
    cum_c = _dot_exact_lhs(tri_sk01, a_c)
    tot = jnp.sum(a_r, axis=1, keepdims=True)
    to_end_r = jnp.exp(tot - cum_r)
    from_start_r = jnp.exp(cum_r)
    tot_e = jnp.exp(tot)
    skip = skip_ref[0, 0]

    g_sl = lax.dot_general(bm, cm, (((1,), (1,)), ((), ())), preferred_element_type=F32)
    state = state_ref[...]
    y_off = lax.dot_general(state.astype(BF16), cm, (((1,), (1,)), ((), ())), preferred_element_type=F32)

    xw_parts, y_parts = [], []
    for h in range(SSD_HPG):
        sl = slice(h * SSD_HEADDIM, (h + 1) * SSD_HEADDIM)
        xh = x[sl, :]
        xd = xh * dt_r[h:h + 1, :]
        seg = cum_r[h:h + 1, :] - cum_c[:, h:h + 1]
        decay = jnp.where(keep_sl, jnp.exp(jnp.where(keep_sl, seg, 0.0)), 0.0)
        m_h = (g_sl * decay).astype(BF16)
        y_h = jnp.dot(xd.astype(BF16), m_h, preferred_element_type=F32)
        y_parts.append(y_h + y_off[sl, :] * from_start_r[h:h + 1, :] + skip[h:h + 1, :] * xh)
        xw_parts.append((xd * to_end_r[h:h + 1, :]).astype(BF16))
    y_ref[0, 0] = jnp.transpose(jnp.concatenate(y_parts, axis=0)).astype(y_ref.dtype)
    s_new = jnp.dot(jnp.concatenate(xw_parts, axis=0), bm, preferred_element_type=F32)
    for h in range(SSD_HPG):
        sl = slice(h * SSD_HEADDIM, (h + 1) * SSD_HEADDIM)
        state_ref[sl, :] = state[sl, :] * tot_e[h:h + 1, :] + s_new[sl, :]


def _ssd_scan(xbc, dt_c, dt_r, a_row, a_col, skip, n_ctx, out_dtype):
    b, t, _ = xbc.shape
    b0 = SSD_INNER // SSD_STATE
    c0 = b0 + SSD_GROUPS
    q = SSD_CHUNK
    ncc = n_ctx // q
    nch = t // q

    def chunk(d, s):
        back = jnp.where(s < ncc, ncc - 1 - s, nch - 1 - s + ncc)
        return jnp.where(d == 0, s, back)

    return pl.pallas_call(
        _ssd_kernel,
        grid=(b, 2, SSD_GROUPS, nch),
        in_specs=[pl.BlockSpec((1, 1, 1, SSD_HPG), lambda bi, d, g, s: (d, g, 0, 0)),
                  pl.BlockSpec((1, 1, SSD_HPG, 1), lambda bi, d, g, s: (d, g, 0, 0)),
                  pl.BlockSpec((1, 1, SSD_HPG, 1), lambda bi, d, g, s: (d, g, 0, 0)),
                  pl.BlockSpec((1, q, SSD_GROUP_W), lambda bi, d, g, s: (bi, chunk(d, s), g)),
                  pl.BlockSpec((1, q, SSD_STATE), lambda bi, d, g, s: (bi, chunk(d, s), b0 + g)),
                  pl.BlockSpec((1, q, SSD_STATE), lambda bi, d, g, s: (bi, chunk(d, s), c0 + g)),
                  pl.BlockSpec((1, 1, 1, q, SSD_HPG), lambda bi, d, g, s: (d, bi, g, chunk(d, s), 0)),
                  pl.BlockSpec((1, 1, 1, SSD_HPG, q), lambda bi, d, g, s: (d, bi, g, 0, chunk(d, s)))],
        out_specs=pl.BlockSpec((1, 1, q, SSD_GROUP_W), lambda bi, d, g, s: (d, bi, chunk(d, s), g)),
        out_shape=jax.ShapeDtypeStruct((2, b, t, SSD_INNER), out_dtype),
        scratch_shapes=[pltpu.VMEM((SSD_GROUP_W, SSD_STATE), F32)],
        compiler_params=_params(("parallel", "parallel", "parallel", "arbitrary")),
        name="ssd_scan",
    )(a_row, a_col, skip, xbc, xbc, xbc, dt_c, dt_r)


def _conv_silu_kernel(x_ref, w_ref, b_ref, o_ref, *, n_ctx):
    x = x_ref[0]
    t = x.shape[0]
    row = lax.broadcasted_iota(jnp.int32, (t, 1), 0)
    prev = jnp.where((row == 0) | (row == n_ctx), 0.0, pltpu.roll(x, 1, axis=0))
    nxt = jnp.where((row == n_ctx - 1) | (row == t - 1), 0.0, pltpu.roll(x, t - 1, axis=0))
    u = prev * w_ref[0:1, :] + x * w_ref[1:2, :] + nxt * w_ref[2:3, :] + b_ref[...]
    o_ref[0] = u * jax.nn.sigmoid(u)


def _conv_silu(proj, col0, conv_w, conv_b, n_ctx):
    b, t, _ = proj.shape
    tc = 512
    c0 = col0 // tc
    return pl.pallas_call(
        functools.partial(_conv_silu_kernel, n_ctx=n_ctx),
        grid=(b, SSD_CONV_DIM // tc),
        in_specs=[pl.BlockSpec((1, t, tc), lambda bi, j: (bi, 0, c0 + j)),
                  pl.BlockSpec((SSD_CONV, tc), lambda bi, j: (0, j)),
                  pl.BlockSpec((1, tc), lambda bi, j: (0, j))],
        out_specs=pl.BlockSpec((1, t, tc), lambda bi, j: (bi, 0, j)),
        out_shape=jax.ShapeDtypeStruct((b, t, SSD_CONV_DIM), F32),
        compiler_params=_params(("parallel", "parallel")),
        name="ssd_conv_silu",
    )(proj, conv_w, conv_b.reshape(1, -1))


def _gate_norm_kernel(y_ref, z_ref, gain_ref, o_ref):
    z = z_ref[0]
    g = (y_ref[0, 0] + y_ref[1, 0]) * (z * jax.nn.sigmoid(z))
    o_ref[0] = (g * lax.rsqrt(jnp.mean(g * g, axis=-1, keepdims=True) + NORM_EPS) * gain_ref[...]).astype(o_ref.dtype)


def _gate_norm(y, proj, gain, n_ctx):
    _, b, t, c = y.shape
    tt = 256
    assert n_ctx % tt == 0
    r0 = n_ctx // tt
    s = t - n_ctx
    return pl.pallas_call(
        _gate_norm_kernel,
        grid=(b, s // tt),
        in_specs=[pl.BlockSpec((2, 1, tt, c), lambda bi, i: (0, bi, r0 + i, 0)),
                  pl.BlockSpec((1, tt, c), lambda bi, i: (bi, r0 + i, 0)),
                  pl.BlockSpec((1, c), lambda bi, i: (0, 0))],
        out_specs=pl.BlockSpec((1, tt, c), lambda bi, i: (bi, i, 0)),
        out_shape=jax.ShapeDtypeStruct((b, s, c), BF16),
        compiler_params=_params(("parallel", "parallel")),
        name="ssd_gate_norm",
    )(y, proj, gain.reshape(1, c))


def _route_kernel(f_ref, rwt_ref, bias_ref, idx_ref, gate_ref, rank_ref, cnt_ref, carry_ref):
    tn = f_ref.shape[0]
    ne, ng, pg = N_EXPERTS, N_EXPERT_GROUPS, N_EXPERTS // N_EXPERT_GROUPS
    neg = -jnp.inf

    @pl.when(pl.program_id(0) == 0)
    def _():
        carry_ref[...] = jnp.zeros_like(carry_ref)

    logits = lax.dot_general(rwt_ref[...], f_ref[...], (((1,), (1,)), ((), ())),
                             preferred_element_type=F32, precision=lax.Precision.HIGHEST)
    scores = jax.nn.sigmoid(logits)
    g3 = (scores + bias_ref[...]).reshape(ng, pg, tn)
    io3 = lax.broadcasted_iota(jnp.int32, (ng, pg, tn), 1)
    m1 = jnp.max(g3, axis=1, keepdims=True)
    i1 = jnp.min(jnp.where(g3 == m1, io3, pg), axis=1, keepdims=True)
    m2 = jnp.max(jnp.where(io3 == i1, neg, g3), axis=1, keepdims=True)
    work = (m1 + m2).reshape(ng, tn)
    iog = lax.broadcasted_iota(jnp.int32, (ng, tn), 0)
    ok = jnp.zeros((ng, tn), F32)
    for _ in range(TOPK_GROUPS):
        m = jnp.max(work, axis=0, keepdims=True)
        gi = jnp.min(jnp.where(work == m, iog, ng), axis=0, keepdims=True)
        hit = iog == gi
        ok = jnp.where(hit, 1.0, ok)
        work = jnp.where(hit, neg, work)
    sel = jnp.where(ok.reshape(ng, 1, tn) > 0.0, g3, neg).reshape(ne, tn)
    ioe = lax.broadcasted_iota(jnp.int32, (ne, tn), 0)
    onehot = jnp.zeros((ne, tn), F32)
    idxs, ws = [], []
    for _ in range(TOP_K):
        m = jnp.max(sel, axis=0, keepdims=True)
        ei = jnp.min(jnp.where(sel == m, ioe, ne), axis=0, keepdims=True)
        hit = ioe == ei
        idxs.append(ei)
        ws.append(jnp.sum(jnp.where(hit, scores, 0.0), axis=0, keepdims=True))
        sel = jnp.where(hit, neg, sel)
        onehot = jnp.where(hit, 1.0, onehot)
    w = jnp.concatenate(ws, axis=0)
    gate_ref[...] = w / jnp.sum(w, axis=0, keepdims=True) * ROUTED_SCALE
    idx_ref[...] = jnp.concatenate(idxs, axis=0)
    r = lax.broadcasted_iota(jnp.int32, (tn, tn), 0)
    c = lax.broadcasted_iota(jnp.int32, (tn, tn), 1)
    ahead = jnp.where(r < c, 1.0, 0.0).astype(BF16)
    cum = carry_ref[...] + jnp.dot(onehot.astype(BF16), ahead, preferred_element_type=F32)
    ranks = [jnp.sum(jnp.where(ioe == idxs[k], cum, 0.0), axis=0, keepdims=True) for k in range(TOP_K)]
    rank_ref[...] = jnp.concatenate(ranks, axis=0).astype(jnp.int32)
    total = carry_ref[...] + jnp.sum(onehot, axis=1, keepdims=True)
    carry_ref[...] = total
    cnt_ref[...] = total.astype(jnp.int32)


def _route(f, router_wt, router_bias):
    t, d = f.shape
    tn = _pick(t, (512, 256, 128))
    kt = pl.BlockSpec((TOP_K, tn), lambda i: (0, i))
    return pl.pallas_call(
        _route_kernel,
        grid=(t // tn,),
        in_specs=[pl.BlockSpec((tn, d), lambda i: (i, 0)),
                  pl.BlockSpec((N_EXPERTS, d), lambda i: (0, 0)),
                  pl.BlockSpec((N_EXPERTS, 1), lambda i: (0, 0))],
        out_specs=[kt, kt, kt, pl.BlockSpec((N_EXPERTS, 1), lambda i: (0, 0))],
        out_shape=[jax.ShapeDtypeStruct((TOP_K, t), jnp.int32), jax.ShapeDtypeStruct((TOP_K, t), F32),
                   jax.ShapeDtypeStruct((TOP_K, t), jnp.int32), jax.ShapeDtypeStruct((N_EXPERTS, 1), jnp.int32)],
        scratch_shapes=[pltpu.VMEM((N_EXPERTS, 1), F32)],
        compiler_params=_params(("arbitrary",)),
        name="moe_route",
    )(f, router_wt, router_bias)


def _scatter_kernel(dest_ref, f_ref, xs_in_ref, xs_ref, sem):
    del xs_in_ref
    ts = f_ref.shape[0]

    def row_copy(t, k):
        return pltpu.make_async_copy(f_ref.at[pl.ds(t, 1)], xs_ref.at[pl.ds(dest_ref[t * TOP_K + k], 1)], sem)

    def issue(t, carry):
        for k in range(TOP_K):
            row_copy(t, k).start()
        return carry

    def drain(t, carry):
        for k in range(TOP_K):
            row_copy(t, k).wait()
        return carry

    lax.fori_loop(0, ts, issue, 0)
    lax.fori_loop(0, ts, drain, 0)


def _scatter_rows(dest_flat, f, xs):
    t, d = f.shape
    ts = _pick(t, (256, 128))
    return pl.pallas_call(
        _scatter_kernel,
        grid=(t // ts,),
        in_specs=[pl.BlockSpec((ts * TOP_K,), lambda i: (i,), memory_space=pltpu.SMEM),
                  pl.BlockSpec((ts, d), lambda i: (i, 0)),
                  pl.BlockSpec(memory_space=pl.ANY)],
        out_specs=pl.BlockSpec(memory_space=pl.ANY),
        out_shape=jax.ShapeDtypeStruct(xs.shape, xs.dtype),
        scratch_shapes=[pltpu.SemaphoreType.DMA(())],
        input_output_aliases={2: 0},
        compiler_params=_params(("arbitrary",)),
        name="moe_scatter",
    )(dest_flat, f, xs)


def _zero_blocks_kernel(rows_ref, xs_ref, zero_ref, sem):
    tm = zero_ref.shape[0]
    zero_ref[...] = jnp.zeros_like(zero_ref)

    def block_copy(e):
        return pltpu.make_async_copy(zero_ref, xs_ref.at[pl.ds(pl.multiple_of(rows_ref[e], tm), tm)], sem)

    for e in range(N_EXPERTS):
        block_copy(e).start()
    for e in range(N_EXPERTS):
        block_copy(e).wait()


def _zero_blocks(block_rows, n_rows_alloc, tm, d):
    grid_spec = pltpu.PrefetchScalarGridSpec(
        num_scalar_prefetch=1, grid=(1,), in_specs=[],
        out_specs=pl.BlockSpec(memory_space=pl.ANY),
        scratch_shapes=[pltpu.VMEM((tm, d), F32), pltpu.SemaphoreType.DMA(())])
    return pl.pallas_call(
        _zero_blocks_kernel,
        grid_spec=grid_spec,
        out_shape=jax.ShapeDtypeStruct((n_rows_alloc, d), F32),
        compiler_params=_params(("arbitrary",)),
        name="moe_zero_tail_blocks",
    )(block_rows)


def _expert_kernel(be_ref, blk_ref, nused_ref, x_ref, wg_ref, wu_ref, wd_ref, o_ref, wg_s, wu_s, wd_s):
    i = pl.program_id(0)

    @pl.when(jnp.logical_or(i == 0, be_ref[i] != be_ref[jnp.maximum(i - 1, 0)]))
    def _():
        wg_s[...] = wg_ref[0].astype(BF16)
        wu_s[...] = wu_ref[0].astype(BF16)
        wd_s[...] = wd_ref[0].astype(BF16)

    @pl.when(i < nused_ref[0])
    def _():
        x = x_ref[...].astype(BF16)
        hg = jnp.dot(x, wg_s[...], preferred_element_type=F32)
        hu = jnp.dot(x, wu_s[...], preferred_element_type=F32)
        h = (hg * jax.nn.sigmoid(hg) * hu).astype(BF16)
        o_ref[...] = jnp.dot(h, wd_s[...], preferred_element_type=F32)


def _expert_blocks(block_expert, block_index, n_used, xs, n_blocks, tm, w_gate, w_up, w_down):
    d, ff = w_gate.shape[1:]
    grid_spec = pltpu.PrefetchScalarGridSpec(
        num_scalar_prefetch=3,
        grid=(n_blocks,),
        in_specs=[pl.BlockSpec((tm, d), lambda i, be, blk, nu: (blk[i], 0)),
                  pl.BlockSpec((1, d, ff), lambda i, be, blk, nu: (be[i], 0, 0)),
                  pl.BlockSpec((1, d, ff), lambda i, be, blk, nu: (be[i], 0, 0)),
                  pl.BlockSpec((1, ff, d), lambda i, be, blk, nu: (be[i], 0, 0))],
        out_specs=pl.BlockSpec((tm, d), lambda i, be, blk, nu: (blk[i], 0)),
        scratch_shapes=[pltpu.VMEM((d, ff), BF16), pltpu.VMEM((d, ff), BF16), pltpu.VMEM((ff, d), BF16)],
    )
    return pl.pallas_call(
        _expert_kernel,
        grid_spec=grid_spec,
        out_shape=jax.ShapeDtypeStruct((n_blocks * tm, d), F32),
        compiler_params=_params(("arbitrary",)),
        name="moe_experts",
    )(block_expert, block_index, n_used, xs, w_gate, w_up, w_down)


def _combine_kernel(dest_ref, gate_ref, sh_ref, xa_ref, mod_ref, ys_ref, o_ref, buf, sem, *, tiles_per_batch, n_ctx):
    tn = gate_ref.shape[0]

    def row_copy(t, k):
        return pltpu.make_async_copy(ys_ref.at[pl.ds(dest_ref[t * TOP_K + k], 1)], buf.at[k, pl.ds(t, 1)], sem)

    def issue(t, carry):
        for k in range(TOP_K):
            row_copy(t, k).start()
        return carry

    def drain(t, carry):
        for k in range(TOP_K):
            row_copy(t, k).wait()
        return carry

    lax.fori_loop(0, tn, issue, 0)
    lax.fori_loop(0, tn, drain, 0)
    acc = sh_ref[...]
    for k in range(TOP_K):
        acc = acc + gate_ref[:, k:k + 1] * buf[k]
    is_ctx = _is_ctx_rows(tn, tiles_per_batch, n_ctx)
    o_ref[...] = xa_ref[...] + _segment_rows(mod_ref, 5, is_ctx) * acc


def _combine(dest_flat, gates, shared, ys, xa, modsel, rows_per_batch, n_ctx):
    t, d = shared.shape
    tn = 128
    tpb = rows_per_batch // tn
    return pl.pallas_call(
        functools.partial(_combine_kernel, tiles_per_batch=tpb, n_ctx=n_ctx),
        grid=(t // tn,),
        in_specs=[pl.BlockSpec((tn * TOP_K,), lambda i: (i,), memory_space=pltpu.SMEM),
                  pl.BlockSpec((tn, TOP_K), lambda i: (i, 0)),
                  pl.BlockSpec((tn, d), lambda i: (i, 0)),
                  pl.BlockSpec((tn, d), lambda i: (i, 0)),
                  pl.BlockSpec((1, 2, N_MOD, d), lambda i: (i // tpb, 0, 0, 0)),
                  pl.BlockSpec(memory_space=pl.ANY)],
        out_specs=pl.BlockSpec((tn, d), lambda i: (i, 0)),
        out_shape=jax.ShapeDtypeStruct((t, d), F32),
        scratch_shapes=[pltpu.VMEM((TOP_K, tn, d), F32), pltpu.SemaphoreType.DMA(())],
        compiler_params=_params(("arbitrary",)),
        name="moe_combine",
    )(dest_flat, gates, shared, xa, modsel, ys)


def _swiglu_kernel(x_ref, wg_ref, wu_ref, wd_ref, o_ref):
    x = x_ref[...].astype(BF16)
    hg = jnp.dot(x, wg_ref[...], preferred_element_type=F32)
    hu = jnp.dot(x, wu_ref[...], preferred_element_type=F32)
    h = (hg * jax.nn.sigmoid(hg) * hu).astype(BF16)
    o_ref[...] = jnp.dot(h, wd_ref[...], preferred_element_type=F32).astype(o_ref.dtype)


def _shared_expert(x, wg, wu, wd):
    m, d = x.shape
    ff = wg.shape[1]
    tm = _pick(m, (512, 384, 256, 128))
    return pl.pallas_call(
        _swiglu_kernel,
        grid=(m // tm,),
        in_specs=[pl.BlockSpec((tm, d), lambda i: (i, 0)),
                  pl.BlockSpec((d, ff), lambda i: (0, 0)),
                  pl.BlockSpec((d, ff), lambda i: (0, 0)),
                  pl.BlockSpec((ff, d), lambda i: (0, 0))],
        out_specs=pl.BlockSpec((tm, d), lambda i: (i, 0)),
        out_shape=jax.ShapeDtypeStruct((m, d), F32),
        compiler_params=_params(("parallel",)),
        name="shared_expert",
    )(x, wg, wu, wd)


def _moe_ffn(f, xa, modsel, rows_per_batch, n_ctx, router_w, router_bias, w_gate, w_up, w_down,
             ws_gate, ws_up, ws_down):
    t, d = f.shape
    tm = EXPERT_TM
    idx, gate, rank, cnt = _route(f, router_w.T, router_bias.astype(F32)[:, None])
    counts = cnt[:, 0]
    padded = (counts + tm - 1) // tm * tm
    pad_end = jnp.cumsum(padded)
    pad_start = pad_end - padded
    n_used = pad_end[-1] // tm
    n_blocks = (t * TOP_K) // tm + N_EXPERTS
    n_rows = n_blocks * tm
    experts = jnp.arange(N_EXPERTS, dtype=jnp.int32)
    start_of = jnp.sum(jnp.where(idx[..., None] == experts, pad_start, 0), axis=-1)
    dest = (start_of + rank).T.reshape(-1).astype(jnp.int32)
    block_index = jnp.minimum(jnp.arange(n_blocks, dtype=jnp.int32), n_used - 1).astype(jnp.int32)
    block_expert = jnp.minimum(jnp.sum(pad_end[None, :] <= (block_index * tm)[:, None], axis=1),
                               N_EXPERTS - 1).astype(jnp.int32)
    tail_rows = jnp.where(padded > counts, pad_end - tm, n_rows + experts * tm).astype(jnp.int32)
    xs = _zero_blocks(tail_rows, n_rows + N_EXPERTS * tm, tm, d)
    xs = _scatter_rows(dest, f, xs)
    ys = _expert_blocks(block_expert, block_index, n_used.reshape(1).astype(jnp.int32), xs, n_blocks, tm,
                        w_gate, w_up, w_down)
    shared = _shared_expert(f, ws_gate.astype(BF16), ws_up.astype(BF16), ws_down.astype(BF16))
    return _combine(dest, gate.T, shared, ys, xa, modsel, rows_per_batch, n_ctx)


def _rms(u, gain):
    return u * lax.rsqrt(jnp.mean(u * u, axis=-1, keepdims=True) + NORM_EPS) * gain


def _rope_tables(n_ctx, n_lat):
    rows = n_lat // GRID_W
    row = jnp.repeat(jnp.arange(rows, dtype=F32), GRID_W)
    col = jnp.tile(jnp.arange(GRID_W, dtype=F32), rows)
    n_freq = HEAD_DIM // 4
    inv = ROPE_THETA ** (-jnp.arange(n_freq, dtype=F32) / n_freq)
    ang = jnp.concatenate([row[:, None] * inv, col[:, None] * inv], axis=-1)
    cos = jnp.repeat(jnp.cos(ang), 2, axis=-1)
    sin = jnp.repeat(jnp.sin(ang), 2, axis=-1)
    sign = jnp.tile(jnp.array([-1.0, 1.0], F32), HEAD_DIM // 2)
    cos = jnp.concatenate([jnp.ones((n_ctx, HEAD_DIM), F32), cos], axis=0)
    sin = jnp.concatenate([jnp.zeros((n_ctx, HEAD_DIM), F32), sin * sign], axis=0)
    return cos, sin


def _ada_mod(cond, w, bias):
    m = jax.nn.silu(cond)
    rows = m.shape[0]
    pad = (-rows) % 16
    mp = jnp.pad(m, ((0, pad), (0, 0))).astype(BF16)
    out = _matmul(mp, w.astype(BF16), F32)[:rows] + bias
    return out.reshape(rows, N_MOD, -1)


def _head_params(q_gain, k_gain):
    scale = HEAD_DIM ** -0.5
    ones = jnp.ones((HEAD_DIM,), F32)
    zeros = jnp.zeros((HEAD_DIM,), F32)

    def rows(gainvec, norm_on, rope_on, count):
        one = jnp.stack([gainvec, ones if norm_on else zeros, ones if rope_on else zeros])
        return jnp.broadcast_to(one, (count, 3, HEAD_DIM))

    return jnp.concatenate([rows(q_gain * scale, True, True, A_Q_HEADS), rows(k_gain, True, True, A_KV_HEADS),
                            rows(ones, False, False, A_KV_HEADS), rows(ones * scale, False, True, 2 * B_HEADS),
                            rows(ones, False, True, 2 * B_HEADS), rows(ones, False, False, 2 * B_HEADS)], axis=0)


def _attention_layer(xa, modsel, gain, n_ctx, w_in, q_gain, k_gain, lq1, lk1, lq2, lk2, subln, lambda_init):
    b, t, d = xa.shape
    cos, sin = _rope_tables(n_ctx, t - n_ctx)
    qkv = _fused_proj(xa, gain, modsel, w_in.astype(BF16), n_ctx, BF16,
                      head_params=_head_params(q_gain, k_gain), rope=(cos - 1.0, sin)).reshape(b, t, ATTN_IN)
    lam = (jnp.exp(jnp.sum(lq1 * lk1)) - jnp.exp(jnp.sum(lq2 * lk2)) + lambda_init).reshape(1, 1).astype(F32)
    y = _gqa_attention(qkv, n_ctx)
    return _diff_attention(qkv, y, lam, subln.reshape(1, 2 * HEAD_DIM), 1.0 - lambda_init, n_ctx)


def _ssd_layer(xa, modsel, gain, n_ctx, w_in, conv_w, conv_b, dt_bias, a_log, d_skip, norm_gain):
    b, t, d = xa.shape
    proj = _fused_proj(xa, gain, modsel, w_in.astype(BF16), n_ctx, F32).reshape(b, t, -1)
    xbc = _conv_silu(proj, SSD_INNER, conv_w, conv_b, n_ctx)
    dt = proj[..., SSD_INNER + SSD_CONV_DIM:]
    dt = jax.nn.softplus(dt.reshape(b, t, 2, SSD_GROUPS, SSD_HPG) + dt_bias.reshape(2, SSD_GROUPS, SSD_HPG))
    dt_c = jnp.transpose(dt, (2, 0, 3, 1, 4))
    dt_r = jnp.transpose(dt, (2, 0, 3, 4, 1))
    a_coef = -jnp.exp(a_log)
    a_row = a_coef.reshape(2, SSD_GROUPS, 1, SSD_HPG)
    a_col = a_coef.reshape(2, SSD_GROUPS, SSD_HPG, 1)
    skip = d_skip.reshape(2, SSD_GROUPS, SSD_HPG, 1)
    y = _ssd_scan(xbc, dt_c, dt_r, a_row, a_col, skip, n_ctx, F32)
    return _gate_norm(y, proj, norm_gain, n_ctx)


def kernel(x, c, ctx, c_ctx, mod_w, mod_b, norm_mix, norm_ffn, norm_final, attn_w_in, attn_w_out, attn_q_gain,
           attn_k_gain, diff_lam_q1, diff_lam_k1, diff_lam_q2, diff_lam_k2, diff_subln, ssd_w_in, ssd_conv_w,
           ssd_conv_b, ssd_dt_bias, ssd_a_log, ssd_d, ssd_norm, ssd_w_out, router_w, router_bias, exp_w_gate,
           exp_w_up, exp_w_down, shared_w_gate, shared_w_up, shared_w_down):
    b, s, d = x.shape
    n_ctx = ctx.shape[1]
    t = n_ctx + s
    xa = jnp.concatenate([ctx, x], axis=1)
    cond = jnp.concatenate([c, c_ctx[None]], axis=0)
    out = None
    for i in range(DEPTH):
        last = i == DEPTH - 1
        mod = _ada_mod(cond, mod_w[i], mod_b[i])
        modsel = jnp.stack([jnp.broadcast_to(mod[b], (b, N_MOD, d)), mod[:b]], axis=1)
        j = i // 2
        moe_w = (router_w[i], router_bias[i], exp_w_gate[i], exp_w_up[i], exp_w_down[i],
                 shared_w_gate[i], shared_w_up[i], shared_w_down[i])
        if i % 2 == 0:
            lambda_init = 0.8 - 0.6 * math.exp(-0.3 * i)
            y = _attention_layer(xa, modsel, norm_mix[i], n_ctx, attn_w_in[j], attn_q_gain[j], attn_k_gain[j],
                                 diff_lam_q1[j], diff_lam_k1[j], diff_lam_q2[j], diff_lam_k2[j],
                                 diff_subln[j], lambda_init)
            w_out = attn_w_out[j]
        else:
            y = _ssd_layer(xa, modsel, norm_mix[i], n_ctx, ssd_w_in[j], ssd_conv_w[j], ssd_conv_b[j],
                           ssd_dt_bias[j], ssd_a_log[j], ssd_d[j], ssd_norm[j])
            w_out = ssd_w_out[j]
        if last:
            x_new, f = _fused_out_proj(y if y.shape[1] == s else y[:, n_ctx:], w_out.astype(BF16), xa, n_ctx,
                                       modsel, norm_ffn[i], 0)
            out = _moe_ffn(f.reshape(b * s, d), x_new.reshape(b * s, d), modsel, s, 0, *moe_w).reshape(b, s, d)
        else:
            x_new, f = _fused_out_proj(y, w_out.astype(BF16), xa, 0, modsel, norm_ffn[i], n_ctx)
            xa = _moe_ffn(f.reshape(b * t, d), x_new.reshape(b * t, d), modsel, t, n_ctx, *moe_w).reshape(b, t, d)
    return _rms(out, norm_final)
```

```python
import functools
import math

import jax
import jax.numpy as jnp
from jax import lax
from jax.experimental import pallas as pl
from jax.experimental.pallas import tpu as pltpu

F32 = jnp.float32
BF16 = jnp.bfloat16

D_MODEL = 2048
DEPTH = 2
GRID_W = 64
NORM_EPS = 1e-6
N_MOD = 6
HEAD_DIM = 128
ROPE_THETA = 10000.0
A_Q_HEADS = 8
A_KV_HEADS = 2
A_GROUP = A_Q_HEADS // A_KV_HEADS
B_HEADS = 4
QA_W = A_Q_HEADS * HEAD_DIM
KA_W = A_KV_HEADS * HEAD_DIM
QB_W = 2 * B_HEADS * HEAD_DIM
ATTN_IN = QA_W + 2 * KA_W + 3 * QB_W
SSD_INNER = 2 * D_MODEL
SSD_HEADDIM = 64
SSD_HEADS = SSD_INNER // SSD_HEADDIM
SSD_GROUPS = 8
SSD_HPG = SSD_HEADS // SSD_GROUPS
SSD_STATE = 128
SSD_CONV = 3
SSD_CHUNK = 128
SSD_GROUP_W = SSD_HPG * SSD_HEADDIM
SSD_CONV_DIM = SSD_INNER + 2 * SSD_GROUPS * SSD_STATE
N_EXPERTS = 64
EXPERT_FF = 512
TOP_K = 8
N_EXPERT_GROUPS = 8
TOPK_GROUPS = 4
ROUTED_SCALE = 2.5
EXPERT_TM = 512

VMEM_LIMIT_BYTES = 56 * 1024 * 1024


def _pick(n, prefs):
    for p in prefs:
        if n % p == 0:
            return p
    raise ValueError(f"no tile in {prefs} divides {n}")


def _params(sem):
    return pltpu.CompilerParams(dimension_semantics=sem, vmem_limit_bytes=VMEM_LIMIT_BYTES)


def _mm_kernel(x_ref, w_ref, o_ref):
    o_ref[...] = jnp.dot(x_ref[...], w_ref[...], preferred_element_type=F32).astype(o_ref.dtype)


def _matmul(x, w, out_dtype):
    m, k = x.shape
    n = w.shape[1]
    tm = _pick(m, (1024, 512, 384, 256, 128, 16, 8))
    tn = _pick(n, (1024, 512, 384, 256, 128))
    return pl.pallas_call(
        _mm_kernel,
        grid=(m // tm, n // tn),
        in_specs=[pl.BlockSpec((tm, k), lambda i, j: (i, 0)),
                  pl.BlockSpec((k, tn), lambda i, j: (0, j))],
        out_specs=pl.BlockSpec((tm, tn), lambda i, j: (i, j)),
        out_shape=jax.ShapeDtypeStruct((m, n), out_dtype),
        compiler_params=_params(("parallel", "arbitrary")),
        name="matmul",
    )(x, w)


def _segment_rows(mod_ref, k, is_ctx):
    return jnp.where(is_ctx, mod_ref[0, 0, k:k + 1, :], mod_ref[0, 1, k:k + 1, :])


def _is_ctx_rows(tm, tiles_per_batch, n_ctx):
    row = (pl.program_id(0) % tiles_per_batch) * tm + lax.broadcasted_iota(jnp.int32, (tm, 1), 0)
    return row < n_ctx


def _norm_mod(x, gain, shift, scale):
    xn = x * lax.rsqrt(jnp.mean(x * x, axis=-1, keepdims=True) + NORM_EPS) * gain
    return xn * (1.0 + scale) + shift


_Q_SCALE = HEAD_DIM ** -0.5
_ATTN_HEADS = ([(0, True, 1.0)] * A_Q_HEADS + [(1, True, 1.0)] * A_KV_HEADS + [(None, False, 1.0)] * A_KV_HEADS
               + [(None, True, _Q_SCALE)] * (2 * B_HEADS) + [(None, True, 1.0)] * (2 * B_HEADS)
               + [(None, False, 1.0)] * (2 * B_HEADS))
_PROJ_HEADS_PER_TILE = 4


def _head_epilogue(u, cfg, gains_ref, cm1, sn, even):
    gain_row, rope, scale = cfg
    if gain_row is not None:
        u = u * lax.rsqrt(jnp.mean(u * u, axis=-1, keepdims=True) + NORM_EPS) * gains_ref[gain_row:gain_row + 1, :]
    if scale != 1.0:
        u = u * scale
    if rope:
        partner = jnp.where(even, pltpu.roll(u, HEAD_DIM - 1, axis=1), pltpu.roll(u, 1, axis=1))
        u = u * (1.0 + cm1) + partner * sn
    return u


def _proj_kernel(x_ref, gain_ref, mod_ref, w_ref, *rest, tiles_per_batch, n_ctx, heads):
    if heads:
        gains_ref, cm1_ref, sn_ref, o_ref, h_s = rest
    else:
        o_ref, h_s = rest
    tm = x_ref.shape[0]
    is_ctx = _is_ctx_rows(tm, tiles_per_batch, n_ctx)

    @pl.when(pl.program_id(1) == 0)
    def _():
        h = _norm_mod(x_ref[...], gain_ref[...], _segment_rows(mod_ref, 0, is_ctx), _segment_rows(mod_ref, 1, is_ctx))
        h_s[...] = h.astype(BF16)

    if not heads:
        o_ref[...] = jnp.dot(h_s[...], w_ref[...], preferred_element_type=F32).astype(o_ref.dtype)
        return
    even = (lax.broadcasted_iota(jnp.int32, (tm, HEAD_DIM), 1) & 1) == 0
    hw = 2 * HEAD_DIM
    for tile in range(len(_ATTN_HEADS) // heads):
        @pl.when(pl.program_id(1) == tile)
        def _(tile=tile):
            cm1 = cm1_ref[...]
            sn = sn_ref[...]
            for half in range(heads // 2):
                acc = jnp.dot(h_s[...], w_ref[:, half * hw:(half + 1) * hw], preferred_element_type=F32)
                for hh in range(2):
                    col = half * hw + hh * HEAD_DIM
                    cfg = _ATTN_HEADS[tile * heads + half * 2 + hh]
                    out = _head_epilogue(acc[:, hh * HEAD_DIM:(hh + 1) * HEAD_DIM], cfg, gains_ref, cm1, sn, even)
                    o_ref[:, col:col + HEAD_DIM] = out.astype(o_ref.dtype)


def _fused_proj(xa, gain, modsel, w, n_ctx, out_dtype, head_gains=None, rope=None):
    b, t, d = xa.shape
    n = w.shape[1]
    tm = _pick(t, (768, 512, 384, 256))
    tpb = t // tm
    heads = 0 if head_gains is None else _PROJ_HEADS_PER_TILE
    tn = heads * HEAD_DIM if heads else _pick(n, (1152, 1024, 768, 512, 384, 256, 128))
    in_specs = [pl.BlockSpec((tm, d), lambda i, j: (i, 0)),
                pl.BlockSpec((1, d), lambda i, j: (0, 0)),
                pl.BlockSpec((1, 2, N_MOD, d), lambda i, j: (i // tpb, 0, 0, 0)),
                pl.BlockSpec((d, tn), lambda i, j: (0, j))]
    args = [xa.reshape(b * t, d), gain.reshape(1, d), modsel, w]
    if heads:
        in_specs += [pl.BlockSpec((2, HEAD_DIM), lambda i, j: (0, 0)),
                     pl.BlockSpec((tm, HEAD_DIM), lambda i, j: (i % tpb, 0)),
                     pl.BlockSpec((tm, HEAD_DIM), lambda i, j: (i % tpb, 0))]
        args += [head_gains, rope[0], rope[1]]
    return pl.pallas_call(
        functools.partial(_proj_kernel, tiles_per_batch=tpb, n_ctx=n_ctx, heads=heads),
        grid=(b * tpb, n // tn),
        in_specs=in_specs,
        out_specs=pl.BlockSpec((tm, tn), lambda i, j: (i, j)),
        out_shape=jax.ShapeDtypeStruct((b * t, n), out_dtype),
        scratch_shapes=[pltpu.VMEM((tm, d), BF16)],
        compiler_params=_params(("parallel", "arbitrary")),
        name="norm_mod_proj",
    )(*args)


def _out_proj_kernel(y_ref, w_ref, xa_ref, mod_ref, gain_ref, xo_ref, f_ref, *, tiles_per_batch, n_ctx):
    tm = y_ref.shape[1]
    is_ctx = _is_ctx_rows(tm, tiles_per_batch, n_ctx)
    acc = jnp.dot(y_ref[0], w_ref[...], preferred_element_type=F32)
    x_new = xa_ref[0] + _segment_rows(mod_ref, 2, is_ctx) * acc
    xo_ref[0] = x_new
    f_ref[0] = _norm_mod(x_new, gain_ref[...], _segment_rows(mod_ref, 3, is_ctx), _segment_rows(mod_ref, 4, is_ctx))


def _fused_out_proj(y, w, xa, xa_row0, modsel, gain, n_ctx):
    b, tq, k = y.shape
    d = w.shape[1]
    tm = _pick(tq, (384, 256)) if n_ctx else _pick(tq, (256, 128))
    assert xa_row0 % tm == 0
    r0 = xa_row0 // tm
    tpb = tq // tm
    tile = pl.BlockSpec((1, tm, d), lambda i: (i // tpb, i % tpb, 0))
    return pl.pallas_call(
        functools.partial(_out_proj_kernel, tiles_per_batch=tpb, n_ctx=n_ctx),
        grid=(b * tpb,),
        in_specs=[pl.BlockSpec((1, tm, k), lambda i: (i // tpb, i % tpb, 0)),
                  pl.BlockSpec((k, d), lambda i: (0, 0)),
                  pl.BlockSpec((1, tm, d), lambda i: (i // tpb, r0 + i % tpb, 0)),
                  pl.BlockSpec((1, 2, N_MOD, d), lambda i: (i // tpb, 0, 0, 0)),
                  pl.BlockSpec((1, d), lambda i: (0, 0))],
        out_specs=[tile, tile],
        out_shape=[jax.ShapeDtypeStruct((b, tq, d), F32), jax.ShapeDtypeStruct((b, tq, d), F32)],
        compiler_params=_params(("parallel",)),
        name="out_proj_residual_norm",
    )(y, w, xa, modsel, gain.reshape(1, d))


def _softmax_rows(s):
    m = jnp.max(s, axis=-1, keepdims=True)
    p = jnp.exp(s - m)
    return p, jnp.sum(p, axis=-1, keepdims=True)


def _gqa_heads(q_ref, k, v, o_ref):
    for g in range(A_GROUP):
        q = q_ref[0, :, g * HEAD_DIM:(g + 1) * HEAD_DIM]
        s = lax.dot_general(q, k, (((1,), (1,)), ((), ())), preferred_element_type=F32)
        p, l = _softmax_rows(s)
        o = jnp.dot(p.astype(BF16), v, preferred_element_type=F32) / l
        o_ref[0, :, g * HEAD_DIM:(g + 1) * HEAD_DIM] = o.astype(o_ref.dtype)


def _gqa_kernel(q_ref, k_ref, v_ref, o_ref, *, n_ctx):
    @pl.when(pl.program_id(2) == 0)
    def _():
        _gqa_heads(q_ref, k_ref[0, :n_ctx], v_ref[0, :n_ctx], o_ref)

    @pl.when(pl.program_id(2) > 0)
    def _():
        _gqa_heads(q_ref, k_ref[0], v_ref[0], o_ref)


def _gqa_attention(qkv, n_ctx):
    b, t, _ = qkv.shape
    tq = n_ctx
    gw = A_GROUP * HEAD_DIM
    k0 = QA_W // HEAD_DIM
    v0 = (QA_W + KA_W) // HEAD_DIM
    return pl.pallas_call(
        functools.partial(_gqa_kernel, n_ctx=n_ctx),
        grid=(b, A_KV_HEADS, t // tq),
        in_specs=[pl.BlockSpec((1, tq, gw), lambda bi, h, i: (bi, i, h)),
                  pl.BlockSpec((1, t, HEAD_DIM), lambda bi, h, i: (bi, 0, k0 + h)),
                  pl.BlockSpec((1, t, HEAD_DIM), lambda bi, h, i: (bi, 0, v0 + h))],
        out_specs=pl.BlockSpec((1, tq, gw), lambda bi, h, i: (bi, i, h)),
        out_shape=jax.ShapeDtypeStruct((b, t, QA_W + QB_W), BF16),
        compiler_params=_params(("parallel", "parallel", "arbitrary")),
        name="gqa_attention",
    )(qkv, qkv, qkv)


def _diff_head(lam, gain_ref, q_ref, k, v, o_ref, out_scale):
    parts = []
    for m in range(2):
        q = q_ref[0, :, m * HEAD_DIM:(m + 1) * HEAD_DIM]
        s = lax.dot_general(q, k[:, m * HEAD_DIM:(m + 1) * HEAD_DIM], (((1,), (1,)), ((), ())),
                            preferred_element_type=F32)
        p, l = _softmax_rows(s)
        parts.append(p / l)
    a = parts[0] - lam * parts[1]
    y = jnp.dot(a.astype(BF16), v, preferred_element_type=F32)
    y = y * lax.rsqrt(jnp.mean(y * y, axis=-1, keepdims=True) + NORM_EPS)
    o_ref[0] = (y * gain_ref[...] * out_scale).astype(o_ref.dtype)


def _diff_kernel(lam_ref, gain_ref, q_ref, k_ref, v_ref, y_in_ref, o_ref, *, out_scale, n_ctx):
    del y_in_ref
    lam = lam_ref[0, 0]

    @pl.when(pl.program_id(2) == 0)
    def _():
        _diff_head(lam, gain_ref, q_ref, k_ref[0, :n_ctx], v_ref[0, :n_ctx], o_ref, out_scale)

    @pl.when(pl.program_id(2) > 0)
    def _():
        _diff_head(lam, gain_ref, q_ref, k_ref[0], v_ref[0], o_ref, out_scale)


def _diff_attention(qkv, y, lam, subln_gain, out_scale, n_ctx):
    b, t, _ = qkv.shape
    tq = n_ctx
    hw = 2 * HEAD_DIM
    q0 = (QA_W + 2 * KA_W) // hw
    k0 = q0 + B_HEADS
    v0 = k0 + B_HEADS
    o0 = QA_W // hw
    return pl.pallas_call(
        functools.partial(_diff_kernel, out_scale=out_scale, n_ctx=n_ctx),
        grid=(b, B_HEADS, t // tq),
        in_specs=[pl.BlockSpec(memory_space=pltpu.SMEM),
                  pl.BlockSpec((1, hw), lambda bi, h, i: (0, 0)),
                  pl.BlockSpec((1, tq, hw), lambda bi, h, i: (bi, i, q0 + h)),
                  pl.BlockSpec((1, t, hw), lambda bi, h, i: (bi, 0, k0 + h)),
                  pl.BlockSpec((1, t, hw), lambda bi, h, i: (bi, 0, v0 + h)),
                  pl.BlockSpec(memory_space=pl.ANY)],
        out_specs=pl.BlockSpec((1, tq, hw), lambda bi, h, i: (bi, i, o0 + h)),
        out_shape=jax.ShapeDtypeStruct(y.shape, y.dtype),
        input_output_aliases={5: 0},
        compiler_params=_params(("parallel", "parallel", "arbitrary")),
        name="diff_attention",
    )(lam, subln_gain, qkv, qkv, qkv, y)


def _split3(a):
    a1 = a.astype(BF16)
    r1 = a - a1.astype(F32)
    a2 = r1.astype(BF16)
    a3 = (r1 - a2.astype(F32)).astype(BF16)
    return a1, a2, a3


def _dot_exact_rhs(a, rhs01):
    out = None
    for part in _split3(a):
        d = jnp.dot(part, rhs01, preferred_element_type=F32)
        out = d if out is None else out + d
    return out


def _dot_exact_lhs(lhs01, a):
    out = None
    for part in _split3(a):
        d = jnp.dot(lhs01, part, preferred_element_type=F32)
        out = d if out is None else out + d
    return out


def _ssd_kernel(acr_ref, acc_ref, skip_ref, x_ref, b_ref, c_ref, dtc_ref, dtr_ref, y_ref, state_ref):
    d = pl.program_id(1)
    q = SSD_CHUNK

    @pl.when(pl.program_id(3) == 0)
    def _():
        state_ref[...] = jnp.zeros_like(state_ref)

    rows = lax.broadcasted_iota(jnp.int32, (q, q), 0)
    cols = lax.broadcasted_iota(jnp.int32, (q, q), 1)
    signed = (rows - cols) * (1 - 2 * d)
    keep_sl = signed <= 0
    tri_ks01 = jnp.where(keep_sl, 1.0, 0.0).astype(BF16)
    tri_sk01 = jnp.where(signed >= 0, 1.0, 0.0).astype(BF16)

    x = jnp.transpose(x_ref[0])
    bm = b_ref[0].astype(BF16)
    cm = c_ref[0].astype(BF16)
    dt_r = dtr_ref[0, 0, 0]
    a_r = dt_r * acc_ref[0, 0]
    a_c = dtc_ref[0, 0, 0] * acr_ref[0, 0]
    cum_r = _dot_exact_rhs(a_r, tri_ks01)
    cum_c = _dot_exact_lhs(tri_sk01, a_c)
    tot = jnp.sum(a_r, axis=1, keepdims=True)
    to_end_r = jnp.exp(tot - cum_r)
    from_start_r = jnp.exp(cum_r)
    tot_e = jnp.exp(tot)
    skip = skip_ref[0, 0]

    g_sl = lax.dot_general(bm, cm, (((1,), (1,)), ((), ())), preferred_element_type=F32)
    state = state_ref[...]
    y_off = lax.dot_general(state.astype(BF16), cm, (((1,), (1,)), ((), ())), preferred_element_type=F32)

    xw_parts, y_parts = [], []
    for h in range(SSD_HPG):
        sl = slice(h * SSD_HEADDIM, (h + 1) * SSD_HEADDIM)
        xh = x[sl, :]
        xd = xh * dt_r[h:h + 1, :]
        seg = cum_r[h:h + 1, :] - cum_c[:, h:h + 1]
        decay = jnp.where(keep_sl, jnp.exp(jnp.where(keep_sl, seg, 0.0)), 0.0)
        m_h = (g_sl * decay).astype(BF16)
        y_h = jnp.dot(xd.astype(BF16), m_h, preferred_element_type=F32)
        y_parts.append(y_h + y_off[sl, :] * from_start_r[h:h + 1, :] + skip[h:h + 1, :] * xh)
        xw_parts.append((xd * to_end_r[h:h + 1, :]).astype(BF16))
    y_ref[0, 0] = jnp.transpose(jnp.concatenate(y_parts, axis=0)).astype(y_ref.dtype)
    s_new = jnp.dot(jnp.concatenate(xw_parts, axis=0), bm, preferred_element_type=F32)
    for h in range(SSD_HPG):
        sl = slice(h * SSD_HEADDIM, (h + 1) * SSD_HEADDIM)
        state_ref[sl, :] = state[sl, :] * tot_e[h:h + 1, :] + s_new[sl, :]


def _ssd_scan(xbc, dt_c, dt_r, a_row, a_col, skip, n_ctx, out_dtype):
    b, t, _ = xbc.shape
    b0 = SSD_INNER // SSD_STATE
    c0 = b0 + SSD_GROUPS
    q = SSD_CHUNK
    ncc = n_ctx // q
    nch = t // q

    def chunk(d, s):
        back = jnp.where(s < ncc, ncc - 1 - s, nch - 1 - s + ncc)
        return jnp.where(d == 0, s, back)

    return pl.pallas_call(
        _ssd_kernel,
        grid=(b, 2, SSD_GROUPS, nch),
        in_specs=[pl.BlockSpec((1, 1, 1, SSD_HPG), lambda bi, d, g, s: (d, g, 0, 0)),
                  pl.BlockSpec((1, 1, SSD_HPG, 1), lambda bi, d, g, s: (d, g, 0, 0)),
                  pl.BlockSpec((1, 1, SSD_HPG, 1), lambda bi, d, g, s: (d, g, 0, 0)),
                  pl.BlockSpec((1, q, SSD_GROUP_W), lambda bi, d, g, s: (bi, chunk(d, s), g)),
                  pl.BlockSpec((1, q, SSD_STATE), lambda bi, d, g, s: (bi, chunk(d, s), b0 + g)),
                  pl.BlockSpec((1, q, SSD_STATE), lambda bi, d, g, s: (bi, chunk(d, s), c0 + g)),
                  pl.BlockSpec((1, 1, 1, q, SSD_HPG), lambda bi, d, g, s: (d, bi, g, chunk(d, s), 0)),
                  pl.BlockSpec((1, 1, 1, SSD_HPG, q), lambda bi, d, g, s: (d, bi, g, 0, chunk(d, s)))],
        out_specs=pl.BlockSpec((1, 1, q, SSD_GROUP_W), lambda bi, d, g, s: (d, bi, chunk(d, s), g)),
        out_shape=jax.ShapeDtypeStruct((2, b, t, SSD_INNER), out_dtype),
        scratch_shapes=[pltpu.VMEM((SSD_GROUP_W, SSD_STATE), F32)],
        compiler_params=_params(("parallel", "parallel", "parallel", "arbitrary")),
        name="ssd_scan",
    )(a_row, a_col, skip, xbc, xbc, xbc, dt_c, dt_r)


def _conv_silu_kernel(x_ref, w_ref, b_ref, o_ref, *, n_ctx):
    x = x_ref[0]
    t = x.shape[0]
    row = lax.broadcasted_iota(jnp.int32, (t, 1), 0)
    prev = jnp.where((row == 0) | (row == n_ctx), 0.0, pltpu.roll(x, 1, axis=0))
    nxt = jnp.where((row == n_ctx - 1) | (row == t - 1), 0.0, pltpu.roll(x, t - 1, axis=0))
    u = prev * w_ref[0:1, :] + x * w_ref[1:2, :] + nxt * w_ref[2:3, :] + b_ref[...]
    o_ref[0] = u * jax.nn.sigmoid(u)


def _conv_silu(proj, col0, conv_w, conv_b, n_ctx):
    b, t, _ = proj.shape
    tc = 512
    c0 = col0 // tc
    return pl.pallas_call(
        functools.partial(_conv_silu_kernel, n_ctx=n_ctx),
        grid=(b, SSD_CONV_DIM // tc),
        in_specs=[pl.BlockSpec((1, t, tc), lambda bi, j: (bi, 0, c0 + j)),
                  pl.BlockSpec((SSD_CONV, tc), lambda bi, j: (0, j)),
                  pl.BlockSpec((1, tc), lambda bi, j: (0, j))],
        out_specs=pl.BlockSpec((1, t, tc), lambda bi, j: (bi, 0, j)),
        out_shape=jax.ShapeDtypeStruct((b, t, SSD_CONV_DIM), F32),
        compiler_params=_params(("parallel", "parallel")),
        name="ssd_conv_silu",
    )(proj, conv_w, conv_b.reshape(1, -1))


def _gate_norm_kernel(y_ref, z_ref, gain_ref, o_ref):
    z = z_ref[0]
    g = (y_ref[0, 0] + y_ref[1, 0]) * (z * jax.nn.sigmoid(z))
    o_ref[0] = (g * lax.rsqrt(jnp.mean(g * g, axis=-1, keepdims=True) + NORM_EPS) * gain_ref[...]).astype(o_ref.dtype)


def _gate_norm(y, proj, gain, n_ctx):
    _, b, t, c = y.shape
    tt = 256
    assert n_ctx % tt == 0
    r0 = n_ctx // tt
    s = t - n_ctx
    return pl.pallas_call(
        _gate_norm_kernel,
        grid=(b, s // tt),
        in_specs=[pl.BlockSpec((2, 1, tt, c), lambda bi, i: (0, bi, r0 + i, 0)),
                  pl.BlockSpec((1, tt, c), lambda bi, i: (bi, r0 + i, 0)),
                  pl.BlockSpec((1, c), lambda bi, i: (0, 0))],
        out_specs=pl.BlockSpec((1, tt, c), lambda bi, i: (bi, i, 0)),
        out_shape=jax.ShapeDtypeStruct((b, s, c), BF16),
        compiler_params=_params(("parallel", "parallel")),
        name="ssd_gate_norm",
    )(y, proj, gain.reshape(1, c))


def _route_kernel(f_ref, rwt_ref, bias_ref, idx_ref, gate_ref, rank_ref, cnt_ref, carry_ref):
    tn = f_ref.shape[0]
    ne, ng, pg = N_EXPERTS, N_EXPERT_GROUPS, N_EXPERTS // N_EXPERT_GROUPS
    neg = -jnp.inf

    @pl.when(pl.program_id(0) == 0)
    def _():
        carry_ref[...] = jnp.zeros_like(carry_ref)

    logits = lax.dot_general(rwt_ref[...], f_ref[...], (((1,), (1,)), ((), ())),
                             preferred_element_type=F32, precision=lax.Precision.HIGHEST)
    scores = jax.nn.sigmoid(logits)
    g3 = (scores + bias_ref[...]).reshape(ng, pg, tn)
    io3 = lax.broadcasted_iota(jnp.int32, (ng, pg, tn), 1)
    m1 = jnp.max(g3, axis=1, keepdims=True)
    i1 = jnp.min(jnp.where(g3 == m1, io3, pg), axis=1, keepdims=True)
    m2 = jnp.max(jnp.where(io3 == i1, neg, g3), axis=1, keepdims=True)
    work = (m1 + m2).reshape(ng, tn)
    iog = lax.broadcasted_iota(jnp.int32, (ng, tn), 0)
    ok = jnp.zeros((ng, tn), F32)
    for _ in range(TOPK_GROUPS):
        m = jnp.max(work, axis=0, keepdims=True)
        gi = jnp.min(jnp.where(work == m, iog, ng), axis=0, keepdims=True)
        hit = iog == gi
        ok = jnp.where(hit, 1.0, ok)
        work = jnp.where(hit, neg, work)
    sel = jnp.where(ok.reshape(ng, 1, tn) > 0.0, g3, neg).reshape(ne, tn)
    ioe = lax.broadcasted_iota(jnp.int32, (ne, tn), 0)
    onehot = jnp.zeros((ne, tn), F32)
    idxs, ws = [], []
    for _ in range(TOP_K):
        m = jnp.max(sel, axis=0, keepdims=True)
        ei = jnp.min(jnp.where(sel == m, ioe, ne), axis=0, keepdims=True)
        hit = ioe == ei
        idxs.append(ei)
        ws.append(jnp.sum(jnp.where(hit, scores, 0.0), axis=0, keepdims=True))
        sel = jnp.where(hit, neg, sel)
        onehot = jnp.where(hit, 1.0, onehot)
    w = jnp.concatenate(ws, axis=0)
    gate_ref[...] = w / jnp.sum(w, axis=0, keepdims=True) * ROUTED_SCALE
    idx_ref[...] = jnp.concatenate(idxs, axis=0)
    r = lax.broadcasted_iota(jnp.int32, (tn, tn), 0)
    c = lax.broadcasted_iota(jnp.int32, (tn, tn), 1)
    ahead = jnp.where(r < c, 1.0, 0.0).astype(BF16)
    cum = carry_ref[...] + jnp.dot(onehot.astype(BF16), ahead, preferred_element_type=F32)
    ranks = [jnp.sum(jnp.where(ioe == idxs[k], cum, 0.0), axis=0, keepdims=True) for k in range(TOP_K)]
    rank_ref[...] = jnp.concatenate(ranks, axis=0).astype(jnp.int32)
    total = carry_ref[...] + jnp.sum(onehot, axis=1, keepdims=True)
    carry_ref[...] = total
    cnt_ref[...] = total.astype(jnp.int32)


def _route(f, router_wt, router_bias):
    t, d = f.shape
    tn = _pick(t, (512, 256, 128))
    kt = pl.BlockSpec((TOP_K, tn), lambda i: (0, i))
    return pl.pallas_call(
        _route_kernel,
        grid=(t // tn,),
        in_specs=[pl.BlockSpec((tn, d), lambda i: (i, 0)),
                  pl.BlockSpec((N_EXPERTS, d), lambda i: (0, 0)),
                  pl.BlockSpec((N_EXPERTS, 1), lambda i: (0, 0))],
        out_specs=[kt, kt, kt, pl.BlockSpec((N_EXPERTS, 1), lambda i: (0, 0))],
        out_shape=[jax.ShapeDtypeStruct((TOP_K, t), jnp.int32), jax.ShapeDtypeStruct((TOP_K, t), F32),
                   jax.ShapeDtypeStruct((TOP_K, t), jnp.int32), jax.ShapeDtypeStruct((N_EXPERTS, 1), jnp.int32)],
        scratch_shapes=[pltpu.VMEM((N_EXPERTS, 1), F32)],
        compiler_params=_params(("arbitrary",)),
        name="moe_route",
    )(f, router_wt, router_bias)


def _scatter_kernel(dest_ref, f_ref, xs_in_ref, xs_ref, sem):
    del xs_in_ref
    ts = f_ref.shape[0]

    def row_copy(t, k):
        return pltpu.make_async_copy(f_ref.at[pl.ds(t, 1)], xs_ref.at[pl.ds(dest_ref[t * TOP_K + k], 1)], sem)

    def issue(t, carry):
        for k in range(TOP_K):
            row_copy(t, k).start()
        return carry

    def drain(t, carry):
        for k in range(TOP_K):
            row_copy(t, k).wait()
        return carry

    lax.fori_loop(0, ts, issue, 0)
    lax.fori_loop(0, ts, drain, 0)


def _scatter_rows(dest_flat, f, xs):
    t, d = f.shape
    ts = _pick(t, (256, 128))
    return pl.pallas_call(
        _scatter_kernel,
        grid=(t // ts,),
        in_specs=[pl.BlockSpec((ts * TOP_K,), lambda i: (i,), memory_space=pltpu.SMEM),
                  pl.BlockSpec((ts, d), lambda i: (i, 0)),
                  pl.BlockSpec(memory_space=pl.ANY)],
        out_specs=pl.BlockSpec(memory_space=pl.ANY),
        out_shape=jax.ShapeDtypeStruct(xs.shape, xs.dtype),
        scratch_shapes=[pltpu.SemaphoreType.DMA(())],
        input_output_aliases={2: 0},
        compiler_params=_params(("arbitrary",)),
        name="moe_scatter",
    )(dest_flat, f, xs)


def _zero_blocks_kernel(rows_ref, xs_ref, zero_ref, sem):
    tm = zero_ref.shape[0]
    zero_ref[...] = jnp.zeros_like(zero_ref)

    def block_copy(e):
        return pltpu.make_async_copy(zero_ref, xs_ref.at[pl.ds(pl.multiple_of(rows_ref[e], tm), tm)], sem)

    for e in range(N_EXPERTS):
        block_copy(e).start()
    for e in range(N_EXPERTS):
        block_copy(e).wait()


def _zero_blocks(block_rows, n_rows_alloc, tm, d):
    grid_spec = pltpu.PrefetchScalarGridSpec(
        num_scalar_prefetch=1, grid=(1,), in_specs=[],
        out_specs=pl.BlockSpec(memory_space=pl.ANY),
        scratch_shapes=[pltpu.VMEM((tm, d), F32), pltpu.SemaphoreType.DMA(())])
    return pl.pallas_call(
        _zero_blocks_kernel,
        grid_spec=grid_spec,
        out_shape=jax.ShapeDtypeStruct((n_rows_alloc, d), F32),
        compiler_params=_params(("arbitrary",)),
        name="moe_zero_tail_blocks",
    )(block_rows)


def _expert_kernel(be_ref, blk_ref, nused_ref, x_ref, wg_ref, wu_ref, wd_ref, o_ref, wg_s, wu_s, wd_s):
    i = pl.program_id(0)

    @pl.when(jnp.logical_or(i == 0, be_ref[i] != be_ref[jnp.maximum(i - 1, 0)]))
    def _():
        wg_s[...] = wg_ref[0, 0].astype(BF16)
        wu_s[...] = wu_ref[0, 0].astype(BF16)
        wd_s[...] = wd_ref[0, 0].astype(BF16)

    @pl.when(i < nused_ref[0])
    def _():
        x = x_ref[...].astype(BF16)
        hg = jnp.dot(x, wg_s[...], preferred_element_type=F32)
        hu = jnp.dot(x, wu_s[...], preferred_element_type=F32)
        h = (hg * jax.nn.sigmoid(hg) * hu).astype(BF16)
        o_ref[...] = jnp.dot(h, wd_s[...], preferred_element_type=F32)


def _expert_blocks(block_expert, block_index, n_used, xs, n_blocks, tm, layer, w_gate, w_up, w_down):
    d, ff = w_gate.shape[2:]
    grid_spec = pltpu.PrefetchScalarGridSpec(
        num_scalar_prefetch=3,
        grid=(n_blocks,),
        in_specs=[pl.BlockSpec((tm, d), lambda i, be, blk, nu: (blk[i], 0)),
                  pl.BlockSpec((1, 1, d, ff), lambda i, be, blk, nu: (layer, be[i], 0, 0)),
                  pl.BlockSpec((1, 1, d, ff), lambda i, be, blk, nu: (layer, be[i], 0, 0)),
                  pl.BlockSpec((1, 1, ff, d), lambda i, be, blk, nu: (layer, be[i], 0, 0))],
        out_specs=pl.BlockSpec((tm, d), lambda i, be, blk, nu: (blk[i], 0)),
        scratch_shapes=[pltpu.VMEM((d, ff), BF16), pltpu.VMEM((d, ff), BF16), pltpu.VMEM((ff, d), BF16)],
    )
    return pl.pallas_call(
        _expert_kernel,
        grid_spec=grid_spec,
        out_shape=jax.ShapeDtypeStruct((n_blocks * tm, d), F32),
        compiler_params=_params(("arbitrary",)),
        name="moe_experts",
    )(block_expert, block_index, n_used, xs, w_gate, w_up, w_down)


def _combine_kernel(dest_ref, gate_ref, sh_ref, xa_ref, mod_ref, ys_ref, o_ref, buf, sem, *, tiles_per_batch, n_ctx):
    tn = gate_ref.shape[0]

    def row_copy(t, k):
        return pltpu.make_async_copy(ys_ref.at[pl.ds(dest_ref[t * TOP_K + k], 1)], buf.at[k, pl.ds(t, 1)], sem)

    def issue(t, carry):
        for k in range(TOP_K):
            row_copy(t, k).start()
        return carry

    def drain(t, carry):
        for k in range(TOP_K):
            row_copy(t, k).wait()
        return carry

    lax.fori_loop(0, tn, issue, 0)
    lax.fori_loop(0, tn, drain, 0)
    acc = sh_ref[...]
    for k in range(TOP_K):
        acc = acc + gate_ref[:, k:k + 1] * buf[k]
    is_ctx = _is_ctx_rows(tn, tiles_per_batch, n_ctx)
    o_ref[...] = xa_ref[...] + _segment_rows(mod_ref, 5, is_ctx) * acc


def _combine(dest_flat, gates, shared, ys, xa, modsel, rows_per_batch, n_ctx):
    t, d = shared.shape
    tn = 128
    tpb = rows_per_batch // tn
    return pl.pallas_call(
        functools.partial(_combine_kernel, tiles_per_batch=tpb, n_ctx=n_ctx),
        grid=(t // tn,),
        in_specs=[pl.BlockSpec((tn * TOP_K,), lambda i: (i,), memory_space=pltpu.SMEM),
                  pl.BlockSpec((tn, TOP_K), lambda i: (i, 0)),
                  pl.BlockSpec((tn, d), lambda i: (i, 0)),
                  pl.BlockSpec((tn, d), lambda i: (i, 0)),
                  pl.BlockSpec((1, 2, N_MOD, d), lambda i: (i // tpb, 0, 0, 0)),
                  pl.BlockSpec(memory_space=pl.ANY)],
        out_specs=pl.BlockSpec((tn, d), lambda i: (i, 0)),
        out_shape=jax.ShapeDtypeStruct((t, d), F32),
        scratch_shapes=[pltpu.VMEM((TOP_K, tn, d), F32), pltpu.SemaphoreType.DMA(())],
        compiler_params=_params(("arbitrary",)),
        name="moe_combine",
    )(dest_flat, gates, shared, xa, modsel, ys)


def _swiglu_kernel(x_ref, wg_ref, wu_ref, wd_ref, o_ref):
    x = x_ref[...].astype(BF16)
    hg = jnp.dot(x, wg_ref[...], preferred_element_type=F32)
    hu = jnp.dot(x, wu_ref[...], preferred_element_type=F32)
    h = (hg * jax.nn.sigmoid(hg) * hu).astype(BF16)
    o_ref[...] = jnp.dot(h, wd_ref[...], preferred_element_type=F32).astype(o_ref.dtype)


def _shared_expert(x, wg, wu, wd):
    m, d = x.shape
    ff = wg.shape[1]
    tm = _pick(m, (512, 384, 256, 128))
    return pl.pallas_call(
        _swiglu_kernel,
        grid=(m // tm,),
        in_specs=[pl.BlockSpec((tm, d), lambda i: (i, 0)),
                  pl.BlockSpec((d, ff), lambda i: (0, 0)),
                  pl.BlockSpec((d, ff), lambda i: (0, 0)),
                  pl.BlockSpec((ff, d), lambda i: (0, 0))],
        out_specs=pl.BlockSpec((tm, d), lambda i: (i, 0)),
        out_shape=jax.ShapeDtypeStruct((m, d), F32),
        compiler_params=_params(("parallel",)),
        name="shared_expert",
    )(x, wg, wu, wd)


def _moe_ffn(f, xa, modsel, rows_per_batch, n_ctx, layer, router_w, router_bias, w_gate, w_up, w_down,
             ws_gate, ws_up, ws_down):
    t, d = f.shape
    tm = EXPERT_TM
    idx, gate, rank, cnt = _route(f, router_w.T, router_bias.astype(F32)[:, None])
    counts = cnt[:, 0]
    padded = (counts + tm - 1) // tm * tm
    pad_end = jnp.cumsum(padded)
    pad_start = pad_end - padded
    n_used = pad_end[-1] // tm
    n_blocks = (t * TOP_K) // tm + N_EXPERTS
    n_rows = n_blocks * tm
    experts = jnp.arange(N_EXPERTS, dtype=jnp.int32)
    start_of = jnp.sum(jnp.where(idx[..., None] == experts, pad_start, 0), axis=-1)
    dest = (start_of + rank).T.reshape(-1).astype(jnp.int32)
    block_index = jnp.minimum(jnp.arange(n_blocks, dtype=jnp.int32), n_used - 1).astype(jnp.int32)
    block_expert = jnp.minimum(jnp.sum(pad_end[None, :] <= (block_index * tm)[:, None], axis=1),
                               N_EXPERTS - 1).astype(jnp.int32)
    tail_rows = jnp.where(padded > counts, pad_end - tm, n_rows + experts * tm).astype(jnp.int32)
    xs = _zero_blocks(tail_rows, n_rows + N_EXPERTS * tm, tm, d)
    xs = _scatter_rows(dest, f, xs)
    ys = _expert_blocks(block_expert, block_index, n_used.reshape(1).astype(jnp.int32), xs, n_blocks, tm, layer,
                        w_gate, w_up, w_down)
    shared = _shared_expert(f, ws_gate.astype(BF16), ws_up.astype(BF16), ws_down.astype(BF16))
    return _combine(dest, gate.T, shared, ys, xa, modsel, rows_per_batch, n_ctx)


def _rms(u, gain):
    return u * lax.rsqrt(jnp.mean(u * u, axis=-1, keepdims=True) + NORM_EPS) * gain


def _rope_tables(n_ctx, n_lat):
    rows = n_lat // GRID_W
    row = jnp.repeat(jnp.arange(rows, dtype=F32), GRID_W)
    col = jnp.tile(jnp.arange(GRID_W, dtype=F32), rows)
    n_freq = HEAD_DIM // 4
    inv = ROPE_THETA ** (-jnp.arange(n_freq, dtype=F32) / n_freq)
    ang = jnp.concatenate([row[:, None] * inv, col[:, None] * inv], axis=-1)
    cos = jnp.repeat(jnp.cos(ang), 2, axis=-1)
    sin = jnp.repeat(jnp.sin(ang), 2, axis=-1)
    sign = jnp.tile(jnp.array([-1.0, 1.0], F32), HEAD_DIM // 2)
    cos = jnp.concatenate([jnp.ones((n_ctx, HEAD_DIM), F32), cos], axis=0)
    sin = jnp.concatenate([jnp.zeros((n_ctx, HEAD_DIM), F32), sin * sign], axis=0)
    return cos, sin


def _ada_mod(cond, w, bias):
    m = jax.nn.silu(cond)
    rows = m.shape[0]
    pad = (-rows) % 16
    mp = jnp.pad(m, ((0, pad), (0, 0))).astype(BF16)
    out = _matmul(mp, w.astype(BF16), F32)[:rows] + bias
    return out.reshape(rows, N_MOD, -1)


def _attention_layer(xa, modsel, gain, n_ctx, w_in, q_gain, k_gain, lq1, lk1, lq2, lk2, subln, lambda_init):
    b, t, d = xa.shape
    cos, sin = _rope_tables(n_ctx, t - n_ctx)
    head_gains = jnp.stack([q_gain * _Q_SCALE, k_gain]).astype(F32)
    qkv = _fused_proj(xa, gain, modsel, w_in.astype(BF16), n_ctx, BF16,
                      head_gains=head_gains, rope=(cos - 1.0, sin)).reshape(b, t, ATTN_IN)
    lam = (jnp.exp(jnp.sum(lq1 * lk1)) - jnp.exp(jnp.sum(lq2 * lk2)) + lambda_init).reshape(1, 1).astype(F32)
    y = _gqa_attention(qkv, n_ctx)
    return _diff_attention(qkv, y, lam, subln.reshape(1, 2 * HEAD_DIM), 1.0 - lambda_init, n_ctx)


def _ssd_layer(xa, modsel, gain, n_ctx, w_in, conv_w, conv_b, dt_bias, a_log, d_skip, norm_gain):
    b, t, d = xa.shape
    proj = _fused_proj(xa, gain, modsel, w_in.astype(BF16), n_ctx, F32).reshape(b, t, -1)
    xbc = _conv_silu(proj, SSD_INNER, conv_w, conv_b, n_ctx)
    dt = proj[..., SSD_INNER + SSD_CONV_DIM:]
    dt = jax.nn.softplus(dt.reshape(b, t, 2, SSD_GROUPS, SSD_HPG) + dt_bias.reshape(2, SSD_GROUPS, SSD_HPG))
    dt_c = jnp.transpose(dt, (2, 0, 3, 1, 4))
    dt_r = jnp.transpose(dt, (2, 0, 3, 4, 1))
    a_coef = -jnp.exp(a_log)
    a_row = a_coef.reshape(2, SSD_GROUPS, 1, SSD_HPG)
    a_col = a_coef.reshape(2, SSD_GROUPS, SSD_HPG, 1)
    skip = d_skip.reshape(2, SSD_GROUPS, SSD_HPG, 1)
    y = _ssd_scan(xbc, dt_c, dt_r, a_row, a_col, skip, n_ctx, F32)
    return _gate_norm(y, proj, norm_gain, n_ctx)


def kernel(x, c, ctx, c_ctx, mod_w, mod_b, norm_mix, norm_ffn, norm_final, attn_w_in, attn_w_out, attn_q_gain,
           attn_k_gain, diff_lam_q1, diff_lam_k1, diff_lam_q2, diff_lam_k2, diff_subln, ssd_w_in, ssd_conv_w,
           ssd_conv_b, ssd_dt_bias, ssd_a_log, ssd_d, ssd_norm, ssd_w_out, router_w, router_bias, exp_w_gate,
           exp_w_up, exp_w_down, shared_w_gate, shared_w_up, shared_w_down):
    b, s, d = x.shape
    n_ctx = ctx.shape[1]
    t = n_ctx + s
    xa = jnp.concatenate([ctx, x], axis=1)
    cond = jnp.concatenate([c, c_ctx[None]], axis=0)
    out = None
    for i in range(DEPTH):
        last = i == DEPTH - 1
        mod = _ada_mod(cond, mod_w[i], mod_b[i])
        modsel = jnp.stack([jnp.broadcast_to(mod[b], (b, N_MOD, d)), mod[:b]], axis=1)
        j = i // 2
        moe_w = (i, router_w[i], router_bias[i], exp_w_gate, exp_w_up, exp_w_down,
                 shared_w_gate[i], shared_w_up[i], shared_w_down[i])
        if i % 2 == 0:
            lambda_init = 0.8 - 0.6 * math.exp(-0.3 * i)
            y = _attention_layer(xa, modsel, norm_mix[i], n_ctx, attn_w_in[j], attn_q_gain[j], attn_k_gain[j],
                                 diff_lam_q1[j], diff_lam_k1[j], diff_lam_q2[j], diff_lam_k2[j],
                                 diff_subln[j], lambda_init)
            w_out = attn_w_out[j]
        else:
            y = _ssd_layer(xa, modsel, norm_mix[i], n_ctx, ssd_w_in[j], ssd_conv_w[j], ssd_conv_b[j],
                           ssd_dt_bias[j], ssd_a_log[j], ssd_d[j], ssd_norm[j])
            w_out = ssd_w_out[j]
        if last:
            x_new, f = _fused_out_proj(y if y.shape[1] == s else y[:, n_ctx:], w_out.astype(BF16), xa, n_ctx,
                                       modsel, norm_ffn[i], 0)
            out = _moe_ffn(f.reshape(b * s, d), x_new.reshape(b * s, d), modsel, s, 0, *moe_w).reshape(b, s, d)
        else:
            x_new, f = _fused_out_proj(y, w_out.astype(BF16), xa, 0, modsel, norm_ffn[i], n_ctx)
            xa = _moe_ffn(f.reshape(b * t, d), x_new.reshape(b * t, d), modsel, t, n_ctx, *moe_w).reshape(b, t, d)
    return _rms(out, norm_final)
```

```python
import functools
import math

import jax
import jax.numpy as jnp
from jax import lax
from jax.experimental import pallas as pl
from jax.experimental.pallas import tpu as pltpu

F32 = jnp.float32
BF16 = jnp.bfloat16

D_MODEL = 2048
DEPTH = 2
GRID_W = 64
NORM_EPS = 1e-6
N_MOD = 6
HEAD_DIM = 128
ROPE_THETA = 10000.0
A_Q_HEADS = 8
A_KV_HEADS = 2
A_GROUP = A_Q_HEADS // A_KV_HEADS
B_HEADS = 4
QA_W = A_Q_HEADS * HEAD_DIM
KA_W = A_KV_HEADS * HEAD_DIM
QB_W = 2 * B_HEADS * HEAD_DIM
ATTN_IN = QA_W + 2 * KA_W + 3 * QB_W
SSD_INNER = 2 * D_MODEL
SSD_HEADDIM = 64
SSD_HEADS = SSD_INNER // SSD_HEADDIM
SSD_GROUPS = 8
SSD_HPG = SSD_HEADS // SSD_GROUPS
SSD_STATE = 128
SSD_CONV = 3
SSD_CHUNK = 128
SSD_GROUP_W = SSD_HPG * SSD_HEADDIM
SSD_CONV_DIM = SSD_INNER + 2 * SSD_GROUPS * SSD_STATE
N_EXPERTS = 64
EXPERT_FF = 512
TOP_K = 8
N_EXPERT_GROUPS = 8
TOPK_GROUPS = 4
ROUTED_SCALE = 2.5
EXPERT_TM = 512

VMEM_LIMIT_BYTES = 56 * 1024 * 1024


def _pick(n, prefs):
    for p in prefs:
        if n % p == 0:
            return p
    raise ValueError(f"no tile in {prefs} divides {n}")


def _params(sem):
    return pltpu.CompilerParams(dimension_semantics=sem, vmem_limit_bytes=VMEM_LIMIT_BYTES)


def _mm_kernel(x_ref, w_ref, o_ref):
    o_ref[...] = jnp.dot(x_ref[...], w_ref[...], preferred_element_type=F32).astype(o_ref.dtype)


def _matmul(x, w, out_dtype):
    m, k = x.shape
    n = w.shape[1]
    tm = _pick(m, (1024, 512, 384, 256, 128, 16, 8))
    tn = _pick(n, (1024, 512, 384, 256, 128))
    return pl.pallas_call(
        _mm_kernel,
        grid=(m // tm, n // tn),
        in_specs=[pl.BlockSpec((tm, k), lambda i, j: (i, 0)),
                  pl.BlockSpec((k, tn), lambda i, j: (0, j))],
        out_specs=pl.BlockSpec((tm, tn), lambda i, j: (i, j)),
        out_shape=jax.ShapeDtypeStruct((m, n), out_dtype),
        compiler_params=_params(("parallel", "arbitrary")),
        name="matmul",
    )(x, w)


def _segment_rows(mod_ref, k, is_ctx):
    return jnp.where(is_ctx, mod_ref[0, 0, k:k + 1, :], mod_ref[0, 1, k:k + 1, :])


def _is_ctx_rows(tm, tiles_per_batch, n_ctx):
    row = (pl.program_id(0) % tiles_per_batch) * tm + lax.broadcasted_iota(jnp.int32, (tm, 1), 0)
    return row < n_ctx


def _norm_mod(x, gain, shift, scale):
    xn = x * lax.rsqrt(jnp.mean(x * x, axis=-1, keepdims=True) + NORM_EPS) * gain
    return xn * (1.0 + scale) + shift


_Q_SCALE = HEAD_DIM ** -0.5
_ATTN_HEADS = ([(0, True, 1.0)] * A_Q_HEADS + [(1, True, 1.0)] * A_KV_HEADS + [(None, False, 1.0)] * A_KV_HEADS
               + [(None, True, _Q_SCALE)] * (2 * B_HEADS) + [(None, True, 1.0)] * (2 * B_HEADS)
               + [(None, False, 1.0)] * (2 * B_HEADS))
_PROJ_HEADS_PER_TILE = 4


def _head_epilogue(u, cfg, gains_ref, cm1, sn, even):
    gain_row, rope, scale = cfg
    if gain_row is not None:
        u = u * lax.rsqrt(jnp.mean(u * u, axis=-1, keepdims=True) + NORM_EPS) * gains_ref[gain_row:gain_row + 1, :]
    if scale != 1.0:
        u = u * scale
    if rope:
        partner = jnp.where(even, pltpu.roll(u, HEAD_DIM - 1, axis=1), pltpu.roll(u, 1, axis=1))
        u = u * (1.0 + cm1) + partner * sn
    return u


def _proj_kernel(x_ref, gain_ref, mod_ref, w_ref, *rest, tiles_per_batch, n_ctx, heads):
    if heads:
        gains_ref, cm1_ref, sn_ref, o_ref, h_s = rest
    else:
        o_ref, h_s = rest
    tm = x_ref.shape[0]
    is_ctx = _is_ctx_rows(tm, tiles_per_batch, n_ctx)

    @pl.when(pl.program_id(1) == 0)
    def _():
        h = _norm_mod(x_ref[...], gain_ref[...], _segment_rows(mod_ref, 0, is_ctx), _segment_rows(mod_ref, 1, is_ctx))
        h_s[...] = h.astype(BF16)

    if not heads:
        o_ref[...] = jnp.dot(h_s[...], w_ref[...], preferred_element_type=F32).astype(o_ref.dtype)
        return
    even = (lax.broadcasted_iota(jnp.int32, (tm, HEAD_DIM), 1) & 1) == 0
    hw = 2 * HEAD_DIM
    for tile in range(len(_ATTN_HEADS) // heads):
        @pl.when(pl.program_id(1) == tile)
        def _(tile=tile):
            cm1 = cm1_ref[...]
            sn = sn_ref[...]
            for half in range(heads // 2):
                acc = jnp.dot(h_s[...], w_ref[:, half * hw:(half + 1) * hw], preferred_element_type=F32)
                for hh in range(2):
                    col = half * hw + hh * HEAD_DIM
                    cfg = _ATTN_HEADS[tile * heads + half * 2 + hh]
                    out = _head_epilogue(acc[:, hh * HEAD_DIM:(hh + 1) * HEAD_DIM], cfg, gains_ref, cm1, sn, even)
                    o_ref[:, col:col + HEAD_DIM] = out.astype(o_ref.dtype)


def _fused_proj(xa, gain, modsel, w, n_ctx, out_dtype, head_gains=None, rope=None):
    b, t, d = xa.shape
    n = w.shape[1]
    tm = _pick(t, (768, 512, 384, 256))
    tpb = t // tm
    heads = 0 if head_gains is None else _PROJ_HEADS_PER_TILE
    tn = heads * HEAD_DIM if heads else _pick(n, (1152, 1024, 768, 512, 384, 256, 128))
    in_specs = [pl.BlockSpec((tm, d), lambda i, j: (i, 0)),
                pl.BlockSpec((1, d), lambda i, j: (0, 0)),
                pl.BlockSpec((1, 2, N_MOD, d), lambda i, j: (i // tpb, 0, 0, 0)),
                pl.BlockSpec((d, tn), lambda i, j: (0, j))]
    args = [xa.reshape(b * t, d), gain.reshape(1, d), modsel, w]
    if heads:
        in_specs += [pl.BlockSpec((2, HEAD_DIM), lambda i, j: (0, 0)),
                     pl.BlockSpec((tm, HEAD_DIM), lambda i, j: (i % tpb, 0)),
                     pl.BlockSpec((tm, HEAD_DIM), lambda i, j: (i % tpb, 0))]
        args += [head_gains, rope[0], rope[1]]
    return pl.pallas_call(
        functools.partial(_proj_kernel, tiles_per_batch=tpb, n_ctx=n_ctx, heads=heads),
        grid=(b * tpb, n // tn),
        in_specs=in_specs,
        out_specs=pl.BlockSpec((tm, tn), lambda i, j: (i, j)),
        out_shape=jax.ShapeDtypeStruct((b * t, n), out_dtype),
        scratch_shapes=[pltpu.VMEM((tm, d), BF16)],
        compiler_params=_params(("parallel", "arbitrary")),
        name="norm_mod_proj",
    )(*args)


def _out_proj_kernel(y_ref, w_ref, xa_ref, mod_ref, gain_ref, xo_ref, f_ref, *, tiles_per_batch, n_ctx):
    tm = y_ref.shape[1]
    is_ctx = _is_ctx_rows(tm, tiles_per_batch, n_ctx)
    acc = jnp.dot(y_ref[0], w_ref[...], preferred_element_type=F32)
    x_new = xa_ref[0] + _segment_rows(mod_ref, 2, is_ctx) * acc
    xo_ref[0] = x_new
    f_ref[0] = _norm_mod(x_new, gain_ref[...], _segment_rows(mod_ref, 3, is_ctx), _segment_rows(mod_ref, 4, is_ctx))


def _fused_out_proj(y, w, xa, xa_row0, modsel, gain, n_ctx):
    b, tq, k = y.shape
    d = w.shape[1]
    tm = _pick(tq, (384, 256)) if n_ctx else _pick(tq, (256, 128))
    assert xa_row0 % tm == 0
    r0 = xa_row0 // tm
    tpb = tq // tm
    tile = pl.BlockSpec((1, tm, d), lambda i: (i // tpb, i % tpb, 0))
    return pl.pallas_call(
        functools.partial(_out_proj_kernel, tiles_per_batch=tpb, n_ctx=n_ctx),
        grid=(b * tpb,),
        in_specs=[pl.BlockSpec((1, tm, k), lambda i: (i // tpb, i % tpb, 0)),
                  pl.BlockSpec((k, d), lambda i: (0, 0)),
                  pl.BlockSpec((1, tm, d), lambda i: (i // tpb, r0 + i % tpb, 0)),
                  pl.BlockSpec((1, 2, N_MOD, d), lambda i: (i // tpb, 0, 0, 0)),
                  pl.BlockSpec((1, d), lambda i: (0, 0))],
        out_specs=[tile, tile],
        out_shape=[jax.ShapeDtypeStruct((b, tq, d), F32), jax.ShapeDtypeStruct((b, tq, d), F32)],
        compiler_params=_params(("parallel",)),
        name="out_proj_residual_norm",
    )(y, w, xa, modsel, gain.reshape(1, d))


def _softmax_rows(s):
    m = jnp.max(s, axis=-1, keepdims=True)
    p = jnp.exp(s - m)
    return p, jnp.sum(p, axis=-1, keepdims=True)


def _gqa_heads(q_ref, k, v, o_ref):
    for g in range(A_GROUP):
        q = q_ref[0, :, g * HEAD_DIM:(g + 1) * HEAD_DIM]
        s = lax.dot_general(q, k, (((1,), (1,)), ((), ())), preferred_element_type=F32)
        p, l = _softmax_rows(s)
        o = jnp.dot(p.astype(BF16), v, preferred_element_type=F32) / l
        o_ref[0, :, g * HEAD_DIM:(g + 1) * HEAD_DIM] = o.astype(o_ref.dtype)


def _gqa_kernel(q_ref, k_ref, v_ref, o_ref, *, n_ctx):
    @pl.when(pl.program_id(2) == 0)
    def _():
        _gqa_heads(q_ref, k_ref[0, :n_ctx], v_ref[0, :n_ctx], o_ref)

    @pl.when(pl.program_id(2) > 0)
    def _():
        _gqa_heads(q_ref, k_ref[0], v_ref[0], o_ref)


def _gqa_attention(qkv, n_ctx):
    b, t, _ = qkv.shape
    tq = n_ctx
    gw = A_GROUP * HEAD_DIM
    k0 = QA_W // HEAD_DIM
    v0 = (QA_W + KA_W) // HEAD_DIM
    return pl.pallas_call(
        functools.partial(_gqa_kernel, n_ctx=n_ctx),
        grid=(b, A_KV_HEADS, t // tq),
        in_specs=[pl.BlockSpec((1, tq, gw), lambda bi, h, i: (bi, i, h)),
                  pl.BlockSpec((1, t, HEAD_DIM), lambda bi, h, i: (bi, 0, k0 + h)),
                  pl.BlockSpec((1, t, HEAD_DIM), lambda bi, h, i: (bi, 0, v0 + h))],
        out_specs=pl.BlockSpec((1, tq, gw), lambda bi, h, i: (bi, i, h)),
        out_shape=jax.ShapeDtypeStruct((b, t, QA_W + QB_W), BF16),
        compiler_params=_params(("parallel", "parallel", "arbitrary")),
        name="gqa_attention",
    )(qkv, qkv, qkv)


def _diff_head(lam, gain_ref, q_ref, k, v, o_ref, out_scale):
    parts = []
    for m in range(2):
        q = q_ref[0, :, m * HEAD_DIM:(m + 1) * HEAD_DIM]
        s = lax.dot_general(q, k[:, m * HEAD_DIM:(m + 1) * HEAD_DIM], (((1,), (1,)), ((), ())),
                            preferred_element_type=F32)
        p, l = _softmax_rows(s)
        parts.append(p / l)
    a = parts[0] - lam * parts[1]
    y = jnp.dot(a.astype(BF16), v, preferred_element_type=F32)
    y = y * lax.rsqrt(jnp.mean(y * y, axis=-1, keepdims=True) + NORM_EPS)
    o_ref[0] = (y * gain_ref[...] * out_scale).astype(o_ref.dtype)


def _diff_kernel(lam_ref, gain_ref, q_ref, k_ref, v_ref, y_in_ref, o_ref, *, out_scale, n_ctx):
    del y_in_ref
    lam = lam_ref[0, 0]

    @pl.when(pl.program_id(2) == 0)
    def _():
        _diff_head(lam, gain_ref, q_ref, k_ref[0, :n_ctx], v_ref[0, :n_ctx], o_ref, out_scale)

    @pl.when(pl.program_id(2) > 0)
    def _():
        _diff_head(lam, gain_ref, q_ref, k_ref[0], v_ref[0], o_ref, out_scale)


def _diff_attention(qkv, y, lam, subln_gain, out_scale, n_ctx):
    b, t, _ = qkv.shape
    tq = n_ctx
    hw = 2 * HEAD_DIM
    q0 = (QA_W + 2 * KA_W) // hw
    k0 = q0 + B_HEADS
    v0 = k0 + B_HEADS
    o0 = QA_W // hw
    return pl.pallas_call(
        functools.partial(_diff_kernel, out_scale=out_scale, n_ctx=n_ctx),
        grid=(b, B_HEADS, t // tq),
        in_specs=[pl.BlockSpec(memory_space=pltpu.SMEM),
                  pl.BlockSpec((1, hw), lambda bi, h, i: (0, 0)),
                  pl.BlockSpec((1, tq, hw), lambda bi, h, i: (bi, i, q0 + h)),
                  pl.BlockSpec((1, t, hw), lambda bi, h, i: (bi, 0, k0 + h)),
                  pl.BlockSpec((1, t, hw), lambda bi, h, i: (bi, 0, v0 + h)),
                  pl.BlockSpec(memory_space=pl.ANY)],
        out_specs=pl.BlockSpec((1, tq, hw), lambda bi, h, i: (bi, i, o0 + h)),
        out_shape=jax.ShapeDtypeStruct(y.shape, y.dtype),
        input_output_aliases={5: 0},
        compiler_params=_params(("parallel", "parallel", "arbitrary")),
        name="diff_attention",
    )(lam, subln_gain, qkv, qkv, qkv, y)


def _split3(a):
    a1 = a.astype(BF16)
    r1 = a - a1.astype(F32)
    a2 = r1.astype(BF16)
    a3 = (r1 - a2.astype(F32)).astype(BF16)
    return a1, a2, a3


def _dot_exact_rhs(a, rhs01):
    out = None
    for part in _split3(a):
        d = jnp.dot(part, rhs01, preferred_element_type=F32)
        out = d if out is None else out + d
    return out


def _dot_exact_lhs(lhs01, a):
    out = None
    for part in _split3(a):
        d = jnp.dot(lhs01, part, preferred_element_type=F32)
        out = d if out is None else out + d
    return out


SSD_GROUPS_PER_STEP = 8


def _ssd_group(d_sign, acr, acc, skip, x_tok, b_tok, c_tok, dt_c, dt_r, state):
    q = SSD_CHUNK
    rows = lax.broadcasted_iota(jnp.int32, (q, q), 0)
    cols = lax.broadcasted_iota(jnp.int32, (q, q), 1)
    signed = (rows - cols) * d_sign
    keep_sl = signed <= 0
    tri_ks01 = jnp.where(keep_sl, 1.0, 0.0).astype(BF16)
    tri_sk01 = jnp.where(signed >= 0, 1.0, 0.0).astype(BF16)

    x = jnp.transpose(x_tok)
    bm = b_tok.astype(BF16)
    cm = c_tok.astype(BF16)
    a_r = dt_r * acc
    a_c = dt_c * acr
    cum_r = _dot_exact_rhs(a_r, tri_ks01)
    cum_c = _dot_exact_lhs(tri_sk01, a_c)
    tot = jnp.sum(a_r, axis=1, keepdims=True)
    to_end_r = jnp.exp(tot - cum_r)
    from_start_r = jnp.exp(cum_r)
    tot_e = jnp.exp(tot)

    g_sl = lax.dot_general(bm, cm, (((1,), (1,)), ((), ())), preferred_element_type=F32)
    y_off = lax.dot_general(state.astype(BF16), cm, (((1,), (1,)), ((), ())), preferred_element_type=F32)

    xw_parts, y_parts = [], []
    for h in range(SSD_HPG):
        sl = slice(h * SSD_HEADDIM, (h + 1) * SSD_HEADDIM)
        xh = x[sl, :]
        xd = xh * dt_r[h:h + 1, :]
        seg = cum_r[h:h + 1, :] - cum_c[:, h:h + 1]
        decay = jnp.where(keep_sl, jnp.exp(jnp.where(keep_sl, seg, 0.0)), 0.0)
        m_h = (g_sl * decay).astype(BF16)
        y_h = jnp.dot(xd.astype(BF16), m_h, preferred_element_type=F32)
        y_parts.append(y_h + y_off[sl, :] * from_start_r[h:h + 1, :] + skip[h:h + 1, :] * xh)
        xw_parts.append((xd * to_end_r[h:h + 1, :]).astype(BF16))
    s_new = jnp.dot(jnp.concatenate(xw_parts, axis=0), bm, preferred_element_type=F32)
    decay_rows = jnp.concatenate([jnp.broadcast_to(tot_e[h:h + 1, :], (SSD_HEADDIM, 1)) for h in range(SSD_HPG)], axis=0)
    return jnp.transpose(jnp.concatenate(y_parts, axis=0)), state * decay_rows + s_new


def _ssd_kernel(acr_ref, acc_ref, skip_ref, x_ref, b_ref, c_ref, dtc_ref, dtr_ref, y_ref, state_ref):
    @pl.when(pl.program_id(3) == 0)
    def _():
        state_ref[...] = jnp.zeros_like(state_ref)

    d_sign = 1 - 2 * pl.program_id(1)
    for g in range(SSD_GROUPS_PER_STEP):
        xs = slice(g * SSD_GROUP_W, (g + 1) * SSD_GROUP_W)
        ns = slice(g * SSD_STATE, (g + 1) * SSD_STATE)
        y, state = _ssd_group(d_sign, acr_ref[0, g], acc_ref[0, g], skip_ref[0, g], x_ref[0, :, xs], b_ref[0, :, ns],
                              c_ref[0, :, ns], dtc_ref[0, 0, g], dtr_ref[0, 0, g], state_ref[g])
        y_ref[0, 0, :, xs] = y.astype(y_ref.dtype)
        state_ref[g] = state


def _ssd_scan(xbc, dt_c, dt_r, a_row, a_col, skip, n_ctx, out_dtype):
    b, t, _ = xbc.shape
    gs = SSD_GROUPS_PER_STEP
    b0 = SSD_INNER // (gs * SSD_STATE)
    c0 = b0 + SSD_GROUPS // gs
    q = SSD_CHUNK
    ncc = n_ctx // q
    nch = t // q

    def chunk(d, s):
        back = jnp.where(s < ncc, ncc - 1 - s, nch - 1 - s + ncc)
        return jnp.where(d == 0, s, back)

    return pl.pallas_call(
        _ssd_kernel,
        grid=(b, 2, SSD_GROUPS // gs, nch),
        in_specs=[pl.BlockSpec((1, gs, 1, SSD_HPG), lambda bi, d, g, s: (d, g, 0, 0)),
                  pl.BlockSpec((1, gs, SSD_HPG, 1), lambda bi, d, g, s: (d, g, 0, 0)),
                  pl.BlockSpec((1, gs, SSD_HPG, 1), lambda bi, d, g, s: (d, g, 0, 0)),
                  pl.BlockSpec((1, q, gs * SSD_GROUP_W), lambda bi, d, g, s: (bi, chunk(d, s), g)),
                  pl.BlockSpec((1, q, gs * SSD_STATE), lambda bi, d, g, s: (bi, chunk(d, s), b0 + g)),
                  pl.BlockSpec((1, q, gs * SSD_STATE), lambda bi, d, g, s: (bi, chunk(d, s), c0 + g)),
                  pl.BlockSpec((1, 1, gs, q, SSD_HPG), lambda bi, d, g, s: (d, bi, g, chunk(d, s), 0)),
                  pl.BlockSpec((1, 1, gs, SSD_HPG, q), lambda bi, d, g, s: (d, bi, g, 0, chunk(d, s)))],
        out_specs=pl.BlockSpec((1, 1, q, gs * SSD_GROUP_W), lambda bi, d, g, s: (d, bi, chunk(d, s), g)),
        out_shape=jax.ShapeDtypeStruct((2, b, t, SSD_INNER), out_dtype),
        scratch_shapes=[pltpu.VMEM((gs, SSD_GROUP_W, SSD_STATE), F32)],
        compiler_params=_params(("parallel", "parallel", "parallel", "arbitrary")),
        name="ssd_scan",
    )(a_row, a_col, skip, xbc, xbc, xbc, dt_c, dt_r)


def _conv_silu_kernel(x_ref, w_ref, b_ref, o_ref, *, n_ctx):
    x = x_ref[0]
    t = x.shape[0]
    row = lax.broadcasted_iota(jnp.int32, (t, 1), 0)
    prev = jnp.where((row == 0) | (row == n_ctx), 0.0, pltpu.roll(x, 1, axis=0))
    nxt = jnp.where((row == n_ctx - 1) | (row == t - 1), 0.0, pltpu.roll(x, t - 1, axis=0))
    u = prev * w_ref[0:1, :] + x * w_ref[1:2, :] + nxt * w_ref[2:3, :] + b_ref[...]
    o_ref[0] = u * jax.nn.sigmoid(u)


def _conv_silu(proj, col0, conv_w, conv_b, n_ctx):
    b, t, _ = proj.shape
    tc = 512
    c0 = col0 // tc
    return pl.pallas_call(
        functools.partial(_conv_silu_kernel, n_ctx=n_ctx),
        grid=(b, SSD_CONV_DIM // tc),
        in_specs=[pl.BlockSpec((1, t, tc), lambda bi, j: (bi, 0, c0 + j)),
                  pl.BlockSpec((SSD_CONV, tc), lambda bi, j: (0, j)),
                  pl.BlockSpec((1, tc), lambda bi, j: (0, j))],
        out_specs=pl.BlockSpec((1, t, tc), lambda bi, j: (bi, 0, j)),
        out_shape=jax.ShapeDtypeStruct((b, t, SSD_CONV_DIM), F32),
        compiler_params=_params(("parallel", "parallel")),
        name="ssd_conv_silu",
    )(proj, conv_w, conv_b.reshape(1, -1))


def _gate_norm_kernel(y_ref, z_ref, gain_ref, o_ref):
    z = z_ref[0]
    g = (y_ref[0, 0] + y_ref[1, 0]) * (z * jax.nn.sigmoid(z))
    o_ref[0] = (g * lax.rsqrt(jnp.mean(g * g, axis=-1, keepdims=True) + NORM_EPS) * gain_ref[...]).astype(o_ref.dtype)


def _gate_norm(y, proj, gain, n_ctx):
    _, b, t, c = y.shape
    tt = 256
    assert n_ctx % tt == 0
    r0 = n_ctx // tt
    s = t - n_ctx
    return pl.pallas_call(
        _gate_norm_kernel,
        grid=(b, s // tt),
        in_specs=[pl.BlockSpec((2, 1, tt, c), lambda bi, i: (0, bi, r0 + i, 0)),
                  pl.BlockSpec((1, tt, c), lambda bi, i: (bi, r0 + i, 0)),
                  pl.BlockSpec((1, c), lambda bi, i: (0, 0))],
        out_specs=pl.BlockSpec((1, tt, c), lambda bi, i: (bi, i, 0)),
        out_shape=jax.ShapeDtypeStruct((b, s, c), BF16),
        compiler_params=_params(("parallel", "parallel")),
        name="ssd_gate_norm",
    )(y, proj, gain.reshape(1, c))


def _route_kernel(f_ref, rwt_ref, bias_ref, idx_ref, gate_ref, rank_ref, cnt_ref, carry_ref):
    tn = f_ref.shape[0]
    ne, ng, pg = N_EXPERTS, N_EXPERT_GROUPS, N_EXPERTS // N_EXPERT_GROUPS
    neg = -jnp.inf

    @pl.when(pl.program_id(0) == 0)
    def _():
        carry_ref[...] = jnp.zeros_like(carry_ref)

    logits = lax.dot_general(rwt_ref[...], f_ref[...], (((1,), (1,)), ((), ())),
                             preferred_element_type=F32, precision=lax.Precision.HIGHEST)
    scores = jax.nn.sigmoid(logits)
    g3 = (scores + bias_ref[...]).reshape(ng, pg, tn)
    io3 = lax.broadcasted_iota(jnp.int32, (ng, pg, tn), 1)
    m1 = jnp.max(g3, axis=1, keepdims=True)
    i1 = jnp.min(jnp.where(g3 == m1, io3, pg), axis=1, keepdims=True)
    m2 = jnp.max(jnp.where(io3 == i1, neg, g3), axis=1, keepdims=True)
    work = (m1 + m2).reshape(ng, tn)
    iog = lax.broadcasted_iota(jnp.int32, (ng, tn), 0)
    ok = jnp.zeros((ng, tn), F32)
    for _ in range(TOPK_GROUPS):
        m = jnp.max(work, axis=0, keepdims=True)
        gi = jnp.min(jnp.where(work == m, iog, ng), axis=0, keepdims=True)
        hit = iog == gi
        ok = jnp.where(hit, 1.0, ok)
        work = jnp.where(hit, neg, work)
    sel = jnp.where(ok.reshape(ng, 1, tn) > 0.0, g3, neg).reshape(ne, tn)
    ioe = lax.broadcasted_iota(jnp.int32, (ne, tn), 0)
    onehot = jnp.zeros((ne, tn), F32)
    idxs, ws = [], []
    for _ in range(TOP_K):
        m = jnp.max(sel, axis=0, keepdims=True)
        ei = jnp.min(jnp.where(sel == m, ioe, ne), axis=0, keepdims=True)
        hit = ioe == ei
        idxs.append(ei)
        ws.append(jnp.sum(jnp.where(hit, scores, 0.0), axis=0, keepdims=True))
        sel = jnp.where(hit, neg, sel)
        onehot = jnp.where(hit, 1.0, onehot)
    w = jnp.concatenate(ws, axis=0)
    gate_ref[...] = w / jnp.sum(w, axis=0, keepdims=True) * ROUTED_SCALE
    idx_ref[...] = jnp.concatenate(idxs, axis=0)
    r = lax.broadcasted_iota(jnp.int32, (tn, tn), 0)
    c = lax.broadcasted_iota(jnp.int32, (tn, tn), 1)
    ahead = jnp.where(r < c, 1.0, 0.0).astype(BF16)
    cum = carry_ref[...] + jnp.dot(onehot.astype(BF16), ahead, preferred_element_type=F32)
    ranks = [jnp.sum(jnp.where(ioe == idxs[k], cum, 0.0), axis=0, keepdims=True) for k in range(TOP_K)]
    rank_ref[...] = jnp.concatenate(ranks, axis=0).astype(jnp.int32)
    total = carry_ref[...] + jnp.sum(onehot, axis=1, keepdims=True)
    carry_ref[...] = total
    cnt_ref[...] = total.astype(jnp.int32)


def _route(f, router_wt, router_bias):
    t, d = f.shape
    tn = _pick(t, (512, 256, 128))
    kt = pl.BlockSpec((TOP_K, tn), lambda i: (0, i))
    return pl.pallas_call(
        _route_kernel,
        grid=(t // tn,),
        in_specs=[pl.BlockSpec((tn, d), lambda i: (i, 0)),
                  pl.BlockSpec((N_EXPERTS, d), lambda i: (0, 0)),
                  pl.BlockSpec((N_EXPERTS, 1), lambda i: (0, 0))],
        out_specs=[kt, kt, kt, pl.BlockSpec((N_EXPERTS, 1), lambda i: (0, 0))],
        out_shape=[jax.ShapeDtypeStruct((TOP_K, t), jnp.int32), jax.ShapeDtypeStruct((TOP_K, t), F32),
                   jax.ShapeDtypeStruct((TOP_K, t), jnp.int32), jax.ShapeDtypeStruct((N_EXPERTS, 1), jnp.int32)],
        scratch_shapes=[pltpu.VMEM((N_EXPERTS, 1), F32)],
        compiler_params=_params(("arbitrary",)),
        name="moe_route",
    )(f, router_wt, router_bias)


def _scatter_kernel(dest_ref, f_ref, xs_in_ref, xs_ref, sem):
    del xs_in_ref
    ts = f_ref.shape[0]

    def row_copy(t, k):
        return pltpu.make_async_copy(f_ref.at[pl.ds(t, 1)], xs_ref.at[pl.ds(dest_ref[t * TOP_K + k], 1)], sem)

    def issue(t, carry):
        for k in range(TOP_K):
            row_copy(t, k).start()
        return carry

    def drain(t, carry):
        for k in range(TOP_K):
            row_copy(t, k).wait()
        return carry

    lax.fori_loop(0, ts, issue, 0)
    lax.fori_loop(0, ts, drain, 0)


def _scatter_rows(dest_flat, f, xs):
    t, d = f.shape
    ts = _pick(t, (512, 256, 128))
    return pl.pallas_call(
        _scatter_kernel,
        grid=(t // ts,),
        in_specs=[pl.BlockSpec((ts * TOP_K,), lambda i: (i,), memory_space=pltpu.SMEM),
                  pl.BlockSpec((ts, d), lambda i: (i, 0)),
                  pl.BlockSpec(memory_space=pl.ANY)],
        out_specs=pl.BlockSpec(memory_space=pl.ANY),
        out_shape=jax.ShapeDtypeStruct(xs.shape, xs.dtype),
        scratch_shapes=[pltpu.SemaphoreType.DMA(())],
        input_output_aliases={2: 0},
        compiler_params=_params(("arbitrary",)),
        name="moe_scatter",
    )(dest_flat, f, xs)


def _zero_blocks_kernel(rows_ref, xs_ref, zero_ref, sem):
    tm = zero_ref.shape[0]
    zero_ref[...] = jnp.zeros_like(zero_ref)

    def block_copy(e):
        return pltpu.make_async_copy(zero_ref, xs_ref.at[pl.ds(pl.multiple_of(rows_ref[e], tm), tm)], sem)

    for e in range(N_EXPERTS):
        block_copy(e).start()
    for e in range(N_EXPERTS):
        block_copy(e).wait()


def _zero_blocks(block_rows, n_rows_alloc, tm, d):
    grid_spec = pltpu.PrefetchScalarGridSpec(
        num_scalar_prefetch=1, grid=(1,), in_specs=[],
        out_specs=pl.BlockSpec(memory_space=pl.ANY),
        scratch_shapes=[pltpu.VMEM((tm, d), F32), pltpu.SemaphoreType.DMA(())])
    return pl.pallas_call(
        _zero_blocks_kernel,
        grid_spec=grid_spec,
        out_shape=jax.ShapeDtypeStruct((n_rows_alloc, d), F32),
        compiler_params=_params(("arbitrary",)),
        name="moe_zero_tail_blocks",
    )(block_rows)


def _expert_kernel(be_ref, blk_ref, nused_ref, x_ref, wg_ref, wu_ref, wd_ref, o_ref, wg_s, wu_s, wd_s):
    i = pl.program_id(0)

    @pl.when(jnp.logical_or(i == 0, be_ref[i] != be_ref[jnp.maximum(i - 1, 0)]))
    def _():
        wg_s[...] = wg_ref[0, 0].astype(BF16)
        wu_s[...] = wu_ref[0, 0].astype(BF16)
        wd_s[...] = wd_ref[0, 0].astype(BF16)

    @pl.when(i < nused_ref[0])
    def _():
        x = x_ref[...].astype(BF16)
        hg = jnp.dot(x, wg_s[...], preferred_element_type=F32)
        hu = jnp.dot(x, wu_s[...], preferred_element_type=F32)
        h = (hg * jax.nn.sigmoid(hg) * hu).astype(BF16)
        o_ref[...] = jnp.dot(h, wd_s[...], preferred_element_type=F32)


def _expert_blocks(block_expert, block_index, n_used, xs, n_blocks, tm, layer, w_gate, w_up, w_down):
    d, ff = w_gate.shape[2:]
    grid_spec = pltpu.PrefetchScalarGridSpec(
        num_scalar_prefetch=3,
        grid=(n_blocks,),
        in_specs=[pl.BlockSpec((tm, d), lambda i, be, blk, nu: (blk[i], 0)),
                  pl.BlockSpec((1, 1, d, ff), lambda i, be, blk, nu: (layer, be[i], 0, 0)),
                  pl.BlockSpec((1, 1, d, ff), lambda i, be, blk, nu: (layer, be[i], 0, 0)),
                  pl.BlockSpec((1, 1, ff, d), lambda i, be, blk, nu: (layer, be[i], 0, 0))],
        out_specs=pl.BlockSpec((tm, d), lambda i, be, blk, nu: (blk[i], 0)),
        scratch_shapes=[pltpu.VMEM((d, ff), BF16), pltpu.VMEM((d, ff), BF16), pltpu.VMEM((ff, d), BF16)],
    )
    return pl.pallas_call(
        _expert_kernel,
        grid_spec=grid_spec,
        out_shape=jax.ShapeDtypeStruct((n_blocks * tm, d), F32),
        compiler_params=_params(("arbitrary",)),
        name="moe_experts",
    )(block_expert, block_index, n_used, xs, w_gate, w_up, w_down)


def _combine_kernel(dest_ref, gate_ref, sh_ref, xa_ref, mod_ref, ys_ref, o_ref, buf, sem, *, tiles_per_batch, n_ctx):
    tn = gate_ref.shape[0]

    def row_copy(t, k):
        return pltpu.make_async_copy(ys_ref.at[pl.ds(dest_ref[t * TOP_K + k], 1)], buf.at[k, pl.ds(t, 1)], sem)

    def issue(t, carry):
        for k in range(TOP_K):
            row_copy(t, k).start()
        return carry

    def drain(t, carry):
        for k in range(TOP_K):
            row_copy(t, k).wait()
        return carry

    lax.fori_loop(0, tn, issue, 0)
    lax.fori_loop(0, tn, drain, 0)
    acc = sh_ref[...]
    for k in range(TOP_K):
        acc = acc + gate_ref[:, k:k + 1] * buf[k]
    is_ctx = _is_ctx_rows(tn, tiles_per_batch, n_ctx)
    o_ref[...] = xa_ref[...] + _segment_rows(mod_ref, 5, is_ctx) * acc


def _combine(dest_flat, gates, shared, ys, xa, modsel, rows_per_batch, n_ctx):
    t, d = shared.shape
    tn = 256
    tpb = rows_per_batch // tn
    return pl.pallas_call(
        functools.partial(_combine_kernel, tiles_per_batch=tpb, n_ctx=n_ctx),
        grid=(t // tn,),
        in_specs=[pl.BlockSpec((tn * TOP_K,), lambda i: (i,), memory_space=pltpu.SMEM),
                  pl.BlockSpec((tn, TOP_K), lambda i: (i, 0)),
                  pl.BlockSpec((tn, d), lambda i: (i, 0)),
                  pl.BlockSpec((tn, d), lambda i: (i, 0)),
                  pl.BlockSpec((1, 2, N_MOD, d), lambda i: (i // tpb, 0, 0, 0)),
                  pl.BlockSpec(memory_space=pl.ANY)],
        out_specs=pl.BlockSpec((tn, d), lambda i: (i, 0)),
        out_shape=jax.ShapeDtypeStruct((t, d), F32),
        scratch_shapes=[pltpu.VMEM((TOP_K, tn, d), F32), pltpu.SemaphoreType.DMA(())],
        compiler_params=_params(("arbitrary",)),
        name="moe_combine",
    )(dest_flat, gates, shared, xa, modsel, ys)


def _swiglu_kernel(x_ref, wg_ref, wu_ref, wd_ref, o_ref):
    x = x_ref[...].astype(BF16)
    hg = jnp.dot(x, wg_ref[...], preferred_element_type=F32)
    hu = jnp.dot(x, wu_ref[...], preferred_element_type=F32)
    h = (hg * jax.nn.sigmoid(hg) * hu).astype(BF16)
    o_ref[...] = jnp.dot(h, wd_ref[...], preferred_element_type=F32).astype(o_ref.dtype)


def _shared_expert(x, wg, wu, wd):
    m, d = x.shape
    ff = wg.shape[1]
    tm = _pick(m, (512, 384, 256, 128))
    return pl.pallas_call(
        _swiglu_kernel,
        grid=(m // tm,),
        in_specs=[pl.BlockSpec((tm, d), lambda i: (i, 0)),
                  pl.BlockSpec((d, ff), lambda i: (0, 0)),
                  pl.BlockSpec((d, ff), lambda i: (0, 0)),
                  pl.BlockSpec((ff, d), lambda i: (0, 0))],
        out_specs=pl.BlockSpec((tm, d), lambda i: (i, 0)),
        out_shape=jax.ShapeDtypeStruct((m, d), F32),
        compiler_params=_params(("parallel",)),
        name="shared_expert",
    )(x, wg, wu, wd)


def _moe_ffn(f, xa, modsel, rows_per_batch, n_ctx, layer, router_w, router_bias, w_gate, w_up, w_down,
             ws_gate, ws_up, ws_down):
    t, d = f.shape
    tm = EXPERT_TM
    idx, gate, rank, cnt = _route(f, router_w.T, router_bias.astype(F32)[:, None])
    counts = cnt[:, 0]
    padded = (counts + tm - 1) // tm * tm
    pad_end = jnp.cumsum(padded)
    pad_start = pad_end - padded
    n_used = pad_end[-1] // tm
    n_blocks = (t * TOP_K) // tm + N_EXPERTS
    n_rows = n_blocks * tm
    experts = jnp.arange(N_EXPERTS, dtype=jnp.int32)
    start_of = jnp.sum(jnp.where(idx[..., None] == experts, pad_start, 0), axis=-1)
    dest = (start_of + rank).T.reshape(-1).astype(jnp.int32)
    block_index = jnp.minimum(jnp.arange(n_blocks, dtype=jnp.int32), n_used - 1).astype(jnp.int32)
    block_expert = jnp.minimum(jnp.sum(pad_end[None, :] <= (block_index * tm)[:, None], axis=1),
                               N_EXPERTS - 1).astype(jnp.int32)
    tail_rows = jnp.where(padded > counts, pad_end - tm, n_rows + experts * tm).astype(jnp.int32)
    xs = _zero_blocks(tail_rows, n_rows + N_EXPERTS * tm, tm, d)
    xs = _scatter_rows(dest, f, xs)
    ys = _expert_blocks(block_expert, block_index, n_used.reshape(1).astype(jnp.int32), xs, n_blocks, tm, layer,
                        w_gate, w_up, w_down)
    shared = _shared_expert(f, ws_gate.astype(BF16), ws_up.astype(BF16), ws_down.astype(BF16))
    return _combine(dest, gate.T, shared, ys, xa, modsel, rows_per_batch, n_ctx)


def _rms(u, gain):
    return u * lax.rsqrt(jnp.mean(u * u, axis=-1, keepdims=True) + NORM_EPS) * gain


def _rope_tables(n_ctx, n_lat):
    rows = n_lat // GRID_W
    row = jnp.repeat(jnp.arange(rows, dtype=F32), GRID_W)
    col = jnp.tile(jnp.arange(GRID_W, dtype=F32), rows)
    n_freq = HEAD_DIM // 4
    inv = ROPE_THETA ** (-jnp.arange(n_freq, dtype=F32) / n_freq)
    ang = jnp.concatenate([row[:, None] * inv, col[:, None] * inv], axis=-1)
    cos = jnp.repeat(jnp.cos(ang), 2, axis=-1)
    sin = jnp.repeat(jnp.sin(ang), 2, axis=-1)
    sign = jnp.tile(jnp.array([-1.0, 1.0], F32), HEAD_DIM // 2)
    cos = jnp.concatenate([jnp.ones((n_ctx, HEAD_DIM), F32), cos], axis=0)
    sin = jnp.concatenate([jnp.zeros((n_ctx, HEAD_DIM), F32), sin * sign], axis=0)
    return cos, sin


def _ada_mod(cond, w, bias):
    m = jax.nn.silu(cond)
    rows = m.shape[0]
    pad = (-rows) % 16
    mp = jnp.pad(m, ((0, pad), (0, 0))).astype(BF16)
    out = _matmul(mp, w.astype(BF16), F32)[:rows] + bias
    return out.reshape(rows, N_MOD, -1)


def _attention_layer(xa, modsel, gain, n_ctx, w_in, q_gain, k_gain, lq1, lk1, lq2, lk2, subln, lambda_init):
    b, t, d = xa.shape
    cos, sin = _rope_tables(n_ctx, t - n_ctx)
    head_gains = jnp.stack([q_gain * _Q_SCALE, k_gain]).astype(F32)
    qkv = _fused_proj(xa, gain, modsel, w_in.astype(BF16), n_ctx, BF16,
                      head_gains=head_gains, rope=(cos - 1.0, sin)).reshape(b, t, ATTN_IN)
    lam = (jnp.exp(jnp.sum(lq1 * lk1)) - jnp.exp(jnp.sum(lq2 * lk2)) + lambda_init).reshape(1, 1).astype(F32)
    y = _gqa_attention(qkv, n_ctx)
    return _diff_attention(qkv, y, lam, subln.reshape(1, 2 * HEAD_DIM), 1.0 - lambda_init, n_ctx)


def _ssd_layer(xa, modsel, gain, n_ctx, w_in, conv_w, conv_b, dt_bias, a_log, d_skip, norm_gain):
    b, t, d = xa.shape
    proj = _fused_proj(xa, gain, modsel, w_in.astype(BF16), n_ctx, F32).reshape(b, t, -1)
    xbc = _conv_silu(proj, SSD_INNER, conv_w, conv_b, n_ctx)
    dt = proj[..., SSD_INNER + SSD_CONV_DIM:]
    dt = jax.nn.softplus(dt.reshape(b, t, 2, SSD_GROUPS, SSD_HPG) + dt_bias.reshape(2, SSD_GROUPS, SSD_HPG))
    dt_c = jnp.transpose(dt, (2, 0, 3, 1, 4))
    dt_r = jnp.transpose(dt, (2, 0, 3, 4, 1))
    a_coef = -jnp.exp(a_log)
    a_row = a_coef.reshape(2, SSD_GROUPS, 1, SSD_HPG)
    a_col = a_coef.reshape(2, SSD_GROUPS, SSD_HPG, 1)
    skip = d_skip.reshape(2, SSD_GROUPS, SSD_HPG, 1)
    y = _ssd_scan(xbc, dt_c, dt_r, a_row, a_col, skip, n_ctx, F32)
    return _gate_norm(y, proj, norm_gain, n_ctx)


def kernel(x, c, ctx, c_ctx, mod_w, mod_b, norm_mix, norm_ffn, norm_final, attn_w_in, attn_w_out, attn_q_gain,
           attn_k_gain, diff_lam_q1, diff_lam_k1, diff_lam_q2, diff_lam_k2, diff_subln, ssd_w_in, ssd_conv_w,
           ssd_conv_b, ssd_dt_bias, ssd_a_log, ssd_d, ssd_norm, ssd_w_out, router_w, router_bias, exp_w_gate,
           exp_w_up, exp_w_down, shared_w_gate, shared_w_up, shared_w_down):
    b, s, d = x.shape
    n_ctx = ctx.shape[1]
    t = n_ctx + s
    xa = jnp.concatenate([ctx, x], axis=1)
    cond = jnp.concatenate([c, c_ctx[None]], axis=0)
    out = None
    for i in range(DEPTH):
        last = i == DEPTH - 1
        mod = _ada_mod(cond, mod_w[i], mod_b[i])
        modsel = jnp.stack([jnp.broadcast_to(mod[b], (b, N_MOD, d)), mod[:b]], axis=1)
        j = i // 2
        moe_w = (i, router_w[i], router_bias[i], exp_w_gate, exp_w_up, exp_w_down,
                 shared_w_gate[i], shared_w_up[i], shared_w_down[i])
        if i % 2 == 0:
            lambda_init = 0.8 - 0.6 * math.exp(-0.3 * i)
            y = _attention_layer(xa, modsel, norm_mix[i], n_ctx, attn_w_in[j], attn_q_gain[j], attn_k_gain[j],
                                 diff_lam_q1[j], diff_lam_k1[j], diff_lam_q2[j], diff_lam_k2[j],
                                 diff_subln[j], lambda_init)
            w_out = attn_w_out[j]
        else:
            y = _ssd_layer(xa, modsel, norm_mix[i], n_ctx, ssd_w_in[j], ssd_conv_w[j], ssd_conv_b[j],
                           ssd_dt_bias[j], ssd_a_log[j], ssd_d[j], ssd_norm[j])
            w_out = ssd_w_out[j]
        if last:
            x_new, f = _fused_out_proj(y if y.shape[1] == s else y[:, n_ctx:], w_out.astype(BF16), xa, n_ctx,
                                       modsel, norm_ffn[i], 0)
            out = _moe_ffn(f.reshape(b * s, d), x_new.reshape(b * s, d), modsel, s, 0, *moe_w).reshape(b, s, d)
        else:
            x_new, f = _fused_out_proj(y, w_out.astype(BF16), xa, 0, modsel, norm_ffn[i], n_ctx)
            xa = _moe_ffn(f.reshape(b * t, d), x_new.reshape(b * t, d), modsel, t, n_ctx, *moe_w).reshape(b, t, d)
    return _rms(out, norm_final)
```

```python
import functools
import math

import jax
import jax.numpy as jnp
from jax import lax
from jax.experimental import pallas as pl
from jax.experimental.pallas import tpu as pltpu

F32 = jnp.float32
BF16 = jnp.bfloat16

D_MODEL = 2048
DEPTH = 2
GRID_W = 64
NORM_EPS = 1e-6
N_MOD = 6
HEAD_DIM = 128
ROPE_THETA = 10000.0
A_Q_HEADS = 8
A_KV_HEADS = 2
A_GROUP = A_Q_HEADS // A_KV_HEADS
B_HEADS = 4
QA_W = A_Q_HEADS * HEAD_DIM
KA_W = A_KV_HEADS * HEAD_DIM
QB_W = 2 * B_HEADS * HEAD_DIM
ATTN_IN = QA_W + 2 * KA_W + 3 * QB_W
SSD_INNER = 2 * D_MODEL
SSD_HEADDIM = 64
SSD_HEADS = SSD_INNER // SSD_HEADDIM
SSD_GROUPS = 8
SSD_HPG = SSD_HEADS // SSD_GROUPS
SSD_STATE = 128
SSD_CONV = 3
SSD_CHUNK = 128
SSD_GROUP_W = SSD_HPG * SSD_HEADDIM
SSD_CONV_DIM = SSD_INNER + 2 * SSD_GROUPS * SSD_STATE
N_EXPERTS = 64
EXPERT_FF = 512
TOP_K = 8
N_EXPERT_GROUPS = 8
TOPK_GROUPS = 4
ROUTED_SCALE = 2.5
EXPERT_TM = 512

VMEM_LIMIT_BYTES = 56 * 1024 * 1024


def _pick(n, prefs):
    for p in prefs:
        if n % p == 0:
            return p
    raise ValueError(f"no tile in {prefs} divides {n}")


def _params(sem):
    return pltpu.CompilerParams(dimension_semantics=sem, vmem_limit_bytes=VMEM_LIMIT_BYTES)


def _mm_kernel(x_ref, w_ref, o_ref):
    o_ref[...] = jnp.dot(x_ref[...], w_ref[...], preferred_element_type=F32).astype(o_ref.dtype)


def _matmul(x, w, out_dtype):
    m, k = x.shape
    n = w.shape[1]
    tm = _pick(m, (1024, 512, 384, 256, 128, 16, 8))
    tn = _pick(n, (1024, 512, 384, 256, 128))
    return pl.pallas_call(
        _mm_kernel,
        grid=(m // tm, n // tn),
        in_specs=[pl.BlockSpec((tm, k), lambda i, j: (i, 0)),
                  pl.BlockSpec((k, tn), lambda i, j: (0, j))],
        out_specs=pl.BlockSpec((tm, tn), lambda i, j: (i, j)),
        out_shape=jax.ShapeDtypeStruct((m, n), out_dtype),
        compiler_params=_params(("parallel", "arbitrary")),
        name="matmul",
    )(x, w)


def _pack_bf16_pairs(v):
    half = v.shape[1] // 2
    lo = lax.bitcast_convert_type(v[:, :half].astype(BF16).astype(F32), jnp.uint32) >> 16
    hi = lax.bitcast_convert_type(v[:, half:].astype(BF16).astype(F32), jnp.uint32) & jnp.uint32(0xFFFF0000)
    return hi | lo


def _unpack_bf16_pairs(w):
    lo = lax.bitcast_convert_type(w << 16, F32)
    hi = lax.bitcast_convert_type(w & jnp.uint32(0xFFFF0000), F32)
    return lo, hi


def _segment_rows(mod_ref, k, is_ctx):
    return jnp.where(is_ctx, mod_ref[0, 0, k:k + 1, :], mod_ref[0, 1, k:k + 1, :])


def _is_ctx_rows(tm, tiles_per_batch, n_ctx):
    row = (pl.program_id(0) % tiles_per_batch) * tm + lax.broadcasted_iota(jnp.int32, (tm, 1), 0)
    return row < n_ctx


def _norm_mod(x, gain, shift, scale):
    xn = x * lax.rsqrt(jnp.mean(x * x, axis=-1, keepdims=True) + NORM_EPS) * gain
    return xn * (1.0 + scale) + shift


_Q_SCALE = HEAD_DIM ** -0.5
_ATTN_HEADS = ([(0, True, 1.0)] * A_Q_HEADS + [(1, True, 1.0)] * A_KV_HEADS + [(None, False, 1.0)] * A_KV_HEADS
               + [(None, True, _Q_SCALE)] * (2 * B_HEADS) + [(None, True, 1.0)] * (2 * B_HEADS)
               + [(None, False, 1.0)] * (2 * B_HEADS))
_PROJ_HEADS_PER_TILE = 4


def _head_epilogue(u, cfg, gains_ref, cm1, sn, even):
    gain_row, rope, scale = cfg
    if gain_row is not None:
        u = u * lax.rsqrt(jnp.mean(u * u, axis=-1, keepdims=True) + NORM_EPS) * gains_ref[gain_row:gain_row + 1, :]
    if scale != 1.0:
        u = u * scale
    if rope:
        partner = jnp.where(even, pltpu.roll(u, HEAD_DIM - 1, axis=1), pltpu.roll(u, 1, axis=1))
        u = u * (1.0 + cm1) + partner * sn
    return u


def _proj_kernel(x_ref, gain_ref, mod_ref, w_ref, *rest, tiles_per_batch, n_ctx, heads):
    if heads:
        gains_ref, cm1_ref, sn_ref, o_ref, h_s = rest
    else:
        o_ref, h_s = rest
    tm = x_ref.shape[0]
    is_ctx = _is_ctx_rows(tm, tiles_per_batch, n_ctx)

    @pl.when(pl.program_id(1) == 0)
    def _():
        h = _norm_mod(x_ref[...], gain_ref[...], _segment_rows(mod_ref, 0, is_ctx), _segment_rows(mod_ref, 1, is_ctx))
        h_s[...] = h.astype(BF16)

    if not heads:
        o_ref[...] = jnp.dot(h_s[...], w_ref[...], preferred_element_type=F32).astype(o_ref.dtype)
        return
    even = (lax.broadcasted_iota(jnp.int32, (tm, HEAD_DIM), 1) & 1) == 0
    hw = 2 * HEAD_DIM
    for tile in range(len(_ATTN_HEADS) // heads):
        @pl.when(pl.program_id(1) == tile)
        def _(tile=tile):
            cm1 = cm1_ref[...]
            sn = sn_ref[...]
            for half in range(heads // 2):
                acc = jnp.dot(h_s[...], w_ref[:, half * hw:(half + 1) * hw], preferred_element_type=F32)
                for hh in range(2):
                    col = half * hw + hh * HEAD_DIM
                    cfg = _ATTN_HEADS[tile * heads + half * 2 + hh]
                    out = _head_epilogue(acc[:, hh * HEAD_DIM:(hh + 1) * HEAD_DIM], cfg, gains_ref, cm1, sn, even)
                    o_ref[:, col:col + HEAD_DIM] = out.astype(o_ref.dtype)


def _fused_proj(xa, gain, modsel, w, n_ctx, out_dtype, head_gains=None, rope=None):
    b, t, d = xa.shape
    n = w.shape[1]
    tm = _pick(t, (768, 512, 384, 256))
    tpb = t // tm
    heads = 0 if head_gains is None else _PROJ_HEADS_PER_TILE
    tn = heads * HEAD_DIM if heads else _pick(n, (1152, 1024, 768, 512, 384, 256, 128))
    in_specs = [pl.BlockSpec((tm, d), lambda i, j: (i, 0)),
                pl.BlockSpec((1, d), lambda i, j: (0, 0)),
                pl.BlockSpec((1, 2, N_MOD, d), lambda i, j: (i // tpb, 0, 0, 0)),
                pl.BlockSpec((d, tn), lambda i, j: (0, j))]
    args = [xa.reshape(b * t, d), gain.reshape(1, d), modsel, w]
    if heads:
        in_specs += [pl.BlockSpec((2, HEAD_DIM), lambda i, j: (0, 0)),
                     pl.BlockSpec((tm, HEAD_DIM), lambda i, j: (i % tpb, 0)),
                     pl.BlockSpec((tm, HEAD_DIM), lambda i, j: (i % tpb, 0))]
        args += [head_gains, rope[0], rope[1]]
    return pl.pallas_call(
        functools.partial(_proj_kernel, tiles_per_batch=tpb, n_ctx=n_ctx, heads=heads),
        grid=(b * tpb, n // tn),
        in_specs=in_specs,
        out_specs=pl.BlockSpec((tm, tn), lambda i, j: (i, j)),
        out_shape=jax.ShapeDtypeStruct((b * t, n), out_dtype),
        scratch_shapes=[pltpu.VMEM((tm, d), BF16)],
        compiler_params=_params(("parallel", "arbitrary")),
        name="norm_mod_proj",
    )(*args)


def _out_proj_kernel(y_ref, w_ref, xa_ref, mod_ref, gain_ref, xo_ref, f_ref, fp_ref, *, tiles_per_batch, n_ctx):
    tm = y_ref.shape[1]
    is_ctx = _is_ctx_rows(tm, tiles_per_batch, n_ctx)
    acc = jnp.dot(y_ref[0], w_ref[...], preferred_element_type=F32)
    x_new = xa_ref[0] + _segment_rows(mod_ref, 2, is_ctx) * acc
    xo_ref[0] = x_new
    f = _norm_mod(x_new, gain_ref[...], _segment_rows(mod_ref, 3, is_ctx), _segment_rows(mod_ref, 4, is_ctx))
    f_ref[0] = f
    fp_ref[0] = _pack_bf16_pairs(f)


def _fused_out_proj(y, w, xa, xa_row0, modsel, gain, n_ctx):
    b, tq, k = y.shape
    d = w.shape[1]
    tm = _pick(tq, (384, 256)) if n_ctx else _pick(tq, (256, 128))
    assert xa_row0 % tm == 0
    r0 = xa_row0 // tm
    tpb = tq // tm
    tile = pl.BlockSpec((1, tm, d), lambda i: (i // tpb, i % tpb, 0))
    return pl.pallas_call(
        functools.partial(_out_proj_kernel, tiles_per_batch=tpb, n_ctx=n_ctx),
        grid=(b * tpb,),
        in_specs=[pl.BlockSpec((1, tm, k), lambda i: (i // tpb, i % tpb, 0)),
                  pl.BlockSpec((k, d), lambda i: (0, 0)),
                  pl.BlockSpec((1, tm, d), lambda i: (i // tpb, r0 + i % tpb, 0)),
                  pl.BlockSpec((1, 2, N_MOD, d), lambda i: (i // tpb, 0, 0, 0)),
                  pl.BlockSpec((1, d), lambda i: (0, 0))],
        out_specs=[tile, tile, pl.BlockSpec((1, tm, d // 2), lambda i: (i // tpb, i % tpb, 0))],
        out_shape=[jax.ShapeDtypeStruct((b, tq, d), F32), jax.ShapeDtypeStruct((b, tq, d), F32),
                   jax.ShapeDtypeStruct((b, tq, d // 2), jnp.uint32)],
        compiler_params=_params(("parallel",)),
        name="out_proj_residual_norm",
    )(y, w, xa, modsel, gain.reshape(1, d))


def _softmax_rows(s):
    m = jnp.max(s, axis=-1, keepdims=True)
    p = jnp.exp(s - m)
    return p, jnp.sum(p, axis=-1, keepdims=True)


def _gqa_heads(q_ref, k, v, o_ref):
    for g in range(A_GROUP):
        q = q_ref[0, :, g * HEAD_DIM:(g + 1) * HEAD_DIM]
        s = lax.dot_general(q, k, (((1,), (1,)), ((), ())), preferred_element_type=F32)
        p, l = _softmax_rows(s)
        o = jnp.dot(p.astype(BF16), v, preferred_element_type=F32) / l
        o_ref[0, :, g * HEAD_DIM:(g + 1) * HEAD_DIM] = o.astype(o_ref.dtype)


def _gqa_kernel(q_ref, k_ref, v_ref, o_ref, *, n_ctx):
    @pl.when(pl.program_id(2) == 0)
    def _():
        _gqa_heads(q_ref, k_ref[0, :n_ctx], v_ref[0, :n_ctx], o_ref)

    @pl.when(pl.program_id(2) > 0)
    def _():
        _gqa_heads(q_ref, k_ref[0], v_ref[0], o_ref)


def _gqa_attention(qkv, n_ctx):
    b, t, _ = qkv.shape
    tq = n_ctx
    gw = A_GROUP * HEAD_DIM
    k0 = QA_W // HEAD_DIM
    v0 = (QA_W + KA_W) // HEAD_DIM
    return pl.pallas_call(
        functools.partial(_gqa_kernel, n_ctx=n_ctx),
        grid=(b, A_KV_HEADS, t // tq),
        in_specs=[pl.BlockSpec((1, tq, gw), lambda bi, h, i: (bi, i, h)),
                  pl.BlockSpec((1, t, HEAD_DIM), lambda bi, h, i: (bi, 0, k0 + h)),
                  pl.BlockSpec((1, t, HEAD_DIM), lambda bi, h, i: (bi, 0, v0 + h))],
        out_specs=pl.BlockSpec((1, tq, gw), lambda bi, h, i: (bi, i, h)),
        out_shape=jax.ShapeDtypeStruct((b, t, QA_W + QB_W), BF16),
        compiler_params=_params(("parallel", "parallel", "arbitrary")),
        name="gqa_attention",
    )(qkv, qkv, qkv)


def _diff_head(lam, gain_ref, q_ref, k, v, o_ref, out_scale):
    parts = []
    for m in range(2):
        q = q_ref[0, :, m * HEAD_DIM:(m + 1) * HEAD_DIM]
        s = lax.dot_general(q, k[:, m * HEAD_DIM:(m + 1) * HEAD_DIM], (((1,), (1,)), ((), ())),
                            preferred_element_type=F32)
        p, l = _softmax_rows(s)
        parts.append(p / l)
    a = parts[0] - lam * parts[1]
    y = jnp.dot(a.astype(BF16), v, preferred_element_type=F32)
    y = y * lax.rsqrt(jnp.mean(y * y, axis=-1, keepdims=True) + NORM_EPS)
    o_ref[0] = (y * gain_ref[...] * out_scale).astype(o_ref.dtype)


def _diff_kernel(lam_ref, gain_ref, q_ref, k_ref, v_ref, y_in_ref, o_ref, *, out_scale, n_ctx):
    del y_in_ref
    lam = lam_ref[0, 0]

    @pl.when(pl.program_id(2) == 0)
    def _():
        _diff_head(lam, gain_ref, q_ref, k_ref[0, :n_ctx], v_ref[0, :n_ctx], o_ref, out_scale)

    @pl.when(pl.program_id(2) > 0)
    def _():
        _diff_head(lam, gain_ref, q_ref, k_ref[0], v_ref[0], o_ref, out_scale)


def _diff_attention(qkv, y, lam, subln_gain, out_scale, n_ctx):
    b, t, _ = qkv.shape
    tq = n_ctx
    hw = 2 * HEAD_DIM
    q0 = (QA_W + 2 * KA_W) // hw
    k0 = q0 + B_HEADS
    v0 = k0 + B_HEADS
    o0 = QA_W // hw
    return pl.pallas_call(
        functools.partial(_diff_kernel, out_scale=out_scale, n_ctx=n_ctx),
        grid=(b, B_HEADS, t // tq),
        in_specs=[pl.BlockSpec(memory_space=pltpu.SMEM),
                  pl.BlockSpec((1, hw), lambda bi, h, i: (0, 0)),
                  pl.BlockSpec((1, tq, hw), lambda bi, h, i: (bi, i, q0 + h)),
                  pl.BlockSpec((1, t, hw), lambda bi, h, i: (bi, 0, k0 + h)),
                  pl.BlockSpec((1, t, hw), lambda bi, h, i: (bi, 0, v0 + h)),
                  pl.BlockSpec(memory_space=pl.ANY)],
        out_specs=pl.BlockSpec((1, tq, hw), lambda bi, h, i: (bi, i, o0 + h)),
        out_shape=jax.ShapeDtypeStruct(y.shape, y.dtype),
        input_output_aliases={5: 0},
        compiler_params=_params(("parallel", "parallel", "arbitrary")),
        name="diff_attention",
    )(lam, subln_gain, qkv, qkv, qkv, y)


def _split3(a):
    a1 = a.astype(BF16)
    r1 = a - a1.astype(F32)
    a2 = r1.astype(BF16)
    a3 = (r1 - a2.astype(F32)).astype(BF16)
    return a1, a2, a3


def _dot_exact_rhs(a, rhs01):
    out = None
    for part in _split3(a):
        d = jnp.dot(part, rhs01, preferred_element_type=F32)
        out = d if out is None else out + d
    return out


def _dot_exact_lhs(lhs01, a):
    out = None
    for part in _split3(a):
        d = jnp.dot(lhs01, part, preferred_element_type=F32)
        out = d if out is None else out + d
    return out


SSD_GROUPS_PER_STEP = 8


def _ssd_group(d_sign, acr, acc, skip, x_tok, b_tok, c_tok, dt_c, dt_r, state):
    q = SSD_CHUNK
    rows = lax.broadcasted_iota(jnp.int32, (q, q), 0)
    cols = lax.broadcasted_iota(jnp.int32, (q, q), 1)
    signed = (rows - cols) * d_sign
    keep_sl = signed <= 0
    tri_ks01 = jnp.where(keep_sl, 1.0, 0.0).astype(BF16)
    tri_sk01 = jnp.where(signed >= 0, 1.0, 0.0).astype(BF16)

    x = jnp.transpose(x_tok)
    bm = b_tok.astype(BF16)
    cm = c_tok.astype(BF16)
    a_r = dt_r * acc
    a_c = dt_c * acr
    cum_r = _dot_exact_rhs(a_r, tri_ks01)
    cum_c = _dot_exact_lhs(tri_sk01, a_c)
    tot = jnp.sum(a_r, axis=1, keepdims=True)
    to_end_r = jnp.exp(tot - cum_r)
    from_start_r = jnp.exp(cum_r)
    tot_e = jnp.exp(tot)

    g_sl = lax.dot_general(bm, cm, (((1,), (1,)), ((), ())), preferred_element_type=F32)
    y_off = lax.dot_general(state.astype(BF16), cm, (((1,), (1,)), ((), ())), preferred_element_type=F32)

    xw_parts, y_parts = [], []
    for h in range(SSD_HPG):
        sl = slice(h * SSD_HEADDIM, (h + 1) * SSD_HEADDIM)
        xh = x[sl, :]
        xd = xh * dt_r[h:h + 1, :]
        seg = cum_r[h:h + 1, :] - cum_c[:, h:h + 1]
        decay = jnp.where(keep_sl, jnp.exp(jnp.where(keep_sl, seg, 0.0)), 0.0)
        m_h = (g_sl * decay).astype(BF16)
        y_h = jnp.dot(xd.astype(BF16), m_h, preferred_element_type=F32)
        y_parts.append(y_h + y_off[sl, :] * from_start_r[h:h + 1, :] + skip[h:h + 1, :] * xh)
        xw_parts.append((xd * to_end_r[h:h + 1, :]).astype(BF16))
    s_new = jnp.dot(jnp.concatenate(xw_parts, axis=0), bm, preferred_element_type=F32)
    decay_rows = jnp.concatenate([jnp.broadcast_to(tot_e[h:h + 1, :], (SSD_HEADDIM, 1)) for h in range(SSD_HPG)], axis=0)
    return jnp.transpose(jnp.concatenate(y_parts, axis=0)), state * decay_rows + s_new


def _ssd_kernel(acr_ref, acc_ref, skip_ref, x_ref, b_ref, c_ref, dtc_ref, dtr_ref, y_ref, state_ref):
    @pl.when(pl.program_id(3) == 0)
    def _():
        state_ref[...] = jnp.zeros_like(state_ref)

    d_sign = 1 - 2 * pl.program_id(1)
    for g in range(SSD_GROUPS_PER_STEP):
        xs = slice(g * SSD_GROUP_W, (g + 1) * SSD_GROUP_W)
        ns = slice(g * SSD_STATE, (g + 1) * SSD_STATE)
        y, state = _ssd_group(d_sign, acr_ref[0, g], acc_ref[0, g], skip_ref[0, g], x_ref[0, :, xs], b_ref[0, :, ns],
                              c_ref[0, :, ns], dtc_ref[0, 0, g], dtr_ref[0, 0, g], state_ref[g])
        y_ref[0, 0, :, xs] = y.astype(y_ref.dtype)
        state_ref[g] = state


def _ssd_scan(xbc, dt_c, dt_r, a_row, a_col, skip, n_ctx, out_dtype):
    b, t, _ = xbc.shape
    gs = SSD_GROUPS_PER_STEP
    b0 = SSD_INNER // (gs * SSD_STATE)
    c0 = b0 + SSD_GROUPS // gs
    q = SSD_CHUNK
    ncc = n_ctx // q
    nch = t // q

    def chunk(d, s):
        back = jnp.where(s < ncc, ncc - 1 - s, nch - 1 - s + ncc)
        return jnp.where(d == 0, s, back)

    return pl.pallas_call(
        _ssd_kernel,
        grid=(b, 2, SSD_GROUPS // gs, nch),
        in_specs=[pl.BlockSpec((1, gs, 1, SSD_HPG), lambda bi, d, g, s: (d, g, 0, 0)),
                  pl.BlockSpec((1, gs, SSD_HPG, 1), lambda bi, d, g, s: (d, g, 0, 0)),
                  pl.BlockSpec((1, gs, SSD_HPG, 1), lambda bi, d, g, s: (d, g, 0, 0)),
                  pl.BlockSpec((1, q, gs * SSD_GROUP_W), lambda bi, d, g, s: (bi, chunk(d, s), g)),
                  pl.BlockSpec((1, q, gs * SSD_STATE), lambda bi, d, g, s: (bi, chunk(d, s), b0 + g)),
                  pl.BlockSpec((1, q, gs * SSD_STATE), lambda bi, d, g, s: (bi, chunk(d, s), c0 + g)),
                  pl.BlockSpec((1, 1, gs, q, SSD_HPG), lambda bi, d, g, s: (d, bi, g, chunk(d, s), 0)),
                  pl.BlockSpec((1, 1, gs, SSD_HPG, q), lambda bi, d, g, s: (d, bi, g, 0, chunk(d, s)))],
        out_specs=pl.BlockSpec((1, 1, q, gs * SSD_GROUP_W), lambda bi, d, g, s: (d, bi, chunk(d, s), g)),
        out_shape=jax.ShapeDtypeStruct((2, b, t, SSD_INNER), out_dtype),
        scratch_shapes=[pltpu.VMEM((gs, SSD_GROUP_W, SSD_STATE), F32)],
        compiler_params=_params(("parallel", "parallel", "parallel", "arbitrary")),
        name="ssd_scan",
    )(a_row, a_col, skip, xbc, xbc, xbc, dt_c, dt_r)


def _conv_silu_kernel(x_ref, w_ref, b_ref, o_ref, *, n_ctx):
    x = x_ref[0]
    t = x.shape[0]
    row = lax.broadcasted_iota(jnp.int32, (t, 1), 0)
    prev = jnp.where((row == 0) | (row == n_ctx), 0.0, pltpu.roll(x, 1, axis=0))
    nxt = jnp.where((row == n_ctx - 1) | (row == t - 1), 0.0, pltpu.roll(x, t - 1, axis=0))
    u = prev * w_ref[0:1, :] + x * w_ref[1:2, :] + nxt * w_ref[2:3, :] + b_ref[...]
    o_ref[0] = u * jax.nn.sigmoid(u)


def _conv_silu(proj, col0, conv_w, conv_b, n_ctx):
    b, t, _ = proj.shape
    tc = 512
    c0 = col0 // tc
    return pl.pallas_call(
        functools.partial(_conv_silu_kernel, n_ctx=n_ctx),
        grid=(b, SSD_CONV_DIM // tc),
        in_specs=[pl.BlockSpec((1, t, tc), lambda bi, j: (bi, 0, c0 + j)),
                  pl.BlockSpec((SSD_CONV, tc), lambda bi, j: (0, j)),
                  pl.BlockSpec((1, tc), lambda bi, j: (0, j))],
        out_specs=pl.BlockSpec((1, t, tc), lambda bi, j: (bi, 0, j)),
        out_shape=jax.ShapeDtypeStruct((b, t, SSD_CONV_DIM), F32),
        compiler_params=_params(("parallel", "parallel")),
        name="ssd_conv_silu",
    )(proj, conv_w, conv_b.reshape(1, -1))


def _gate_norm_kernel(y_ref, z_ref, gain_ref, o_ref):
    z = z_ref[0]
    g = (y_ref[0, 0] + y_ref[1, 0]) * (z * jax.nn.sigmoid(z))
    o_ref[0] = (g * lax.rsqrt(jnp.mean(g * g, axis=-1, keepdims=True) + NORM_EPS) * gain_ref[...]).astype(o_ref.dtype)


def _gate_norm(y, proj, gain, n_ctx):
    _, b, t, c = y.shape
    tt = 256
    assert n_ctx % tt == 0
    r0 = n_ctx // tt
    s = t - n_ctx
    return pl.pallas_call(
        _gate_norm_kernel,
        grid=(b, s // tt),
        in_specs=[pl.BlockSpec((2, 1, tt, c), lambda bi, i: (0, bi, r0 + i, 0)),
                  pl.BlockSpec((1, tt, c), lambda bi, i: (bi, r0 + i, 0)),
                  pl.BlockSpec((1, c), lambda bi, i: (0, 0))],
        out_specs=pl.BlockSpec((1, tt, c), lambda bi, i: (bi, i, 0)),
        out_shape=jax.ShapeDtypeStruct((b, s, c), BF16),
        compiler_params=_params(("parallel", "parallel")),
        name="ssd_gate_norm",
    )(y, proj, gain.reshape(1, c))


def _route_kernel(f_ref, rwt_ref, bias_ref, idx_ref, gate_ref, rank_ref, cnt_ref, carry_ref):
    tn = f_ref.shape[0]
    ne, ng, pg = N_EXPERTS, N_EXPERT_GROUPS, N_EXPERTS // N_EXPERT_GROUPS
    neg = -jnp.inf

    @pl.when(pl.program_id(0) == 0)
    def _():
        carry_ref[...] = jnp.zeros_like(carry_ref)

    logits = lax.dot_general(rwt_ref[...], f_ref[...], (((1,), (1,)), ((), ())),
                             preferred_element_type=F32, precision=lax.Precision.HIGHEST)
    scores = jax.nn.sigmoid(logits)
    g3 = (scores + bias_ref[...]).reshape(ng, pg, tn)
    io3 = lax.broadcasted_iota(jnp.int32, (ng, pg, tn), 1)
    m1 = jnp.max(g3, axis=1, keepdims=True)
    i1 = jnp.min(jnp.where(g3 == m1, io3, pg), axis=1, keepdims=True)
    m2 = jnp.max(jnp.where(io3 == i1, neg, g3), axis=1, keepdims=True)
    work = (m1 + m2).reshape(ng, tn)
    iog = lax.broadcasted_iota(jnp.int32, (ng, tn), 0)
    ok = jnp.zeros((ng, tn), F32)
    for _ in range(TOPK_GROUPS):
        m = jnp.max(work, axis=0, keepdims=True)
        gi = jnp.min(jnp.where(work == m, iog, ng), axis=0, keepdims=True)
        hit = iog == gi
        ok = jnp.where(hit, 1.0, ok)
        work = jnp.where(hit, neg, work)
    sel = jnp.where(ok.reshape(ng, 1, tn) > 0.0, g3, neg).reshape(ne, tn)
    ioe = lax.broadcasted_iota(jnp.int32, (ne, tn), 0)
    onehot = jnp.zeros((ne, tn), F32)
    idxs, ws = [], []
    for _ in range(TOP_K):
        m = jnp.max(sel, axis=0, keepdims=True)
        ei = jnp.min(jnp.where(sel == m, ioe, ne), axis=0, keepdims=True)
        hit = ioe == ei
        idxs.append(ei)
        ws.append(jnp.sum(jnp.where(hit, scores, 0.0), axis=0, keepdims=True))
        sel = jnp.where(hit, neg, sel)
        onehot = jnp.where(hit, 1.0, onehot)
    w = jnp.concatenate(ws, axis=0)
    gate_ref[...] = w / jnp.sum(w, axis=0, keepdims=True) * ROUTED_SCALE
    idx_ref[...] = jnp.concatenate(idxs, axis=0)
    r = lax.broadcasted_iota(jnp.int32, (tn, tn), 0)
    c = lax.broadcasted_iota(jnp.int32, (tn, tn), 1)
    ahead = jnp.where(r < c, 1.0, 0.0).astype(BF16)
    cum = carry_ref[...] + jnp.dot(onehot.astype(BF16), ahead, preferred_element_type=F32)
    ranks = [jnp.sum(jnp.where(ioe == idxs[k], cum, 0.0), axis=0, keepdims=True) for k in range(TOP_K)]
    rank_ref[...] = jnp.concatenate(ranks, axis=0).astype(jnp.int32)
    total = carry_ref[...] + jnp.sum(onehot, axis=1, keepdims=True)
    carry_ref[...] = total
    cnt_ref[...] = total.astype(jnp.int32)


def _route(f, router_wt, router_bias):
    t, d = f.shape
    tn = _pick(t, (512, 256, 128))
    kt = pl.BlockSpec((TOP_K, tn), lambda i: (0, i))
    return pl.pallas_call(
        _route_kernel,
        grid=(t // tn,),
        in_specs=[pl.BlockSpec((tn, d), lambda i: (i, 0)),
                  pl.BlockSpec((N_EXPERTS, d), lambda i: (0, 0)),
                  pl.BlockSpec((N_EXPERTS, 1), lambda i: (0, 0))],
        out_specs=[kt, kt, kt, pl.BlockSpec((N_EXPERTS, 1), lambda i: (0, 0))],
        out_shape=[jax.ShapeDtypeStruct((TOP_K, t), jnp.int32), jax.ShapeDtypeStruct((TOP_K, t), F32),
                   jax.ShapeDtypeStruct((TOP_K, t), jnp.int32), jax.ShapeDtypeStruct((N_EXPERTS, 1), jnp.int32)],
        scratch_shapes=[pltpu.VMEM((N_EXPERTS, 1), F32)],
        compiler_params=_params(("arbitrary",)),
        name="moe_route",
    )(f, router_wt, router_bias)


def _scatter_kernel(dest_ref, f_ref, xs_in_ref, xs_ref, sem):
    del xs_in_ref
    ts = f_ref.shape[0]

    def row_copy(t, k):
        return pltpu.make_async_copy(f_ref.at[pl.ds(t, 1)], xs_ref.at[pl.ds(dest_ref[t * TOP_K + k], 1)], sem)

    def issue(t, carry):
        for k in range(TOP_K):
            row_copy(t, k).start()
        return carry

    def drain(t, carry):
        for k in range(TOP_K):
            row_copy(t, k).wait()
        return carry

    lax.fori_loop(0, ts, issue, 0)
    lax.fori_loop(0, ts, drain, 0)


def _scatter_rows(dest_flat, f, xs):
    t, d = f.shape
    ts = _pick(t, (512, 256, 128))
    return pl.pallas_call(
        _scatter_kernel,
        grid=(t // ts,),
        in_specs=[pl.BlockSpec((ts * TOP_K,), lambda i: (i,), memory_space=pltpu.SMEM),
                  pl.BlockSpec((ts, d), lambda i: (i, 0)),
                  pl.BlockSpec(memory_space=pl.ANY)],
        out_specs=pl.BlockSpec(memory_space=pl.ANY),
        out_shape=jax.ShapeDtypeStruct(xs.shape, xs.dtype),
        scratch_shapes=[pltpu.SemaphoreType.DMA(())],
        input_output_aliases={2: 0},
        compiler_params=_params(("arbitrary",)),
        name="moe_scatter",
    )(dest_flat, f, xs)


def _zero_blocks_kernel(rows_ref, xs_ref, zero_ref, sem):
    tm = zero_ref.shape[0]
    zero_ref[...] = jnp.zeros_like(zero_ref)

    def block_copy(e):
        return pltpu.make_async_copy(zero_ref, xs_ref.at[pl.ds(pl.multiple_of(rows_ref[e], tm), tm)], sem)

    for e in range(N_EXPERTS):
        block_copy(e).start()
    for e in range(N_EXPERTS):
        block_copy(e).wait()


def _zero_blocks(block_rows, n_rows_alloc, tm, d):
    grid_spec = pltpu.PrefetchScalarGridSpec(
        num_scalar_prefetch=1, grid=(1,), in_specs=[],
        out_specs=pl.BlockSpec(memory_space=pl.ANY),
        scratch_shapes=[pltpu.VMEM((tm, d), jnp.uint32), pltpu.SemaphoreType.DMA(())])
    return pl.pallas_call(
        _zero_blocks_kernel,
        grid_spec=grid_spec,
        out_shape=jax.ShapeDtypeStruct((n_rows_alloc, d), jnp.uint32),
        compiler_params=_params(("arbitrary",)),
        name="moe_zero_tail_blocks",
    )(block_rows)


def _expert_kernel(be_ref, blk_ref, nused_ref, x_ref, wg_ref, wu_ref, wd_ref, o_ref, wg_s, wu_s, wd_s):
    i = pl.program_id(0)

    @pl.when(jnp.logical_or(i == 0, be_ref[i] != be_ref[jnp.maximum(i - 1, 0)]))
    def _():
        wg_s[...] = wg_ref[0, 0].astype(BF16)
        wu_s[...] = wu_ref[0, 0].astype(BF16)
        wd_s[...] = wd_ref[0, 0].astype(BF16)

    @pl.when(i < nused_ref[0])
    def _():
        lo, hi = _unpack_bf16_pairs(x_ref[...])
        x = jnp.concatenate([lo.astype(BF16), hi.astype(BF16)], axis=1)
        hg = jnp.dot(x, wg_s[...], preferred_element_type=F32)
        hu = jnp.dot(x, wu_s[...], preferred_element_type=F32)
        h = (hg * jax.nn.sigmoid(hg) * hu).astype(BF16)
        o_ref[...] = _pack_bf16_pairs(jnp.dot(h, wd_s[...], preferred_element_type=F32))


def _expert_blocks(block_expert, block_index, n_used, xs, n_blocks, tm, layer, w_gate, w_up, w_down):
    d, ff = w_gate.shape[2:]
    grid_spec = pltpu.PrefetchScalarGridSpec(
        num_scalar_prefetch=3,
        grid=(n_blocks,),
        in_specs=[pl.BlockSpec((tm, d // 2), lambda i, be, blk, nu: (blk[i], 0)),
                  pl.BlockSpec((1, 1, d, ff), lambda i, be, blk, nu: (layer, be[i], 0, 0)),
                  pl.BlockSpec((1, 1, d, ff), lambda i, be, blk, nu: (layer, be[i], 0, 0)),
                  pl.BlockSpec((1, 1, ff, d), lambda i, be, blk, nu: (layer, be[i], 0, 0))],
        out_specs=pl.BlockSpec((tm, d // 2), lambda i, be, blk, nu: (blk[i], 0)),
        scratch_shapes=[pltpu.VMEM((d, ff), BF16), pltpu.VMEM((d, ff), BF16), pltpu.VMEM((ff, d), BF16)],
    )
    return pl.pallas_call(
        _expert_kernel,
        grid_spec=grid_spec,
        out_shape=jax.ShapeDtypeStruct((n_blocks * tm, d // 2), jnp.uint32),
        compiler_params=_params(("arbitrary",)),
        name="moe_experts",
    )(block_expert, block_index, n_used, xs, w_gate, w_up, w_down)


def _combine_kernel(dest_ref, gate_ref, sh_ref, xa_ref, mod_ref, ys_ref, o_ref, buf, sem, *, tiles_per_batch, n_ctx):
    tn = gate_ref.shape[0]

    def row_copy(t, k):
        return pltpu.make_async_copy(ys_ref.at[pl.ds(dest_ref[t * TOP_K + k], 1)], buf.at[k, pl.ds(t, 1)], sem)

    def issue(t, carry):
        for k in range(TOP_K):
            row_copy(t, k).start()
        return carry

    def drain(t, carry):
        for k in range(TOP_K):
            row_copy(t, k).wait()
        return carry

    lax.fori_loop(0, tn, issue, 0)
    lax.fori_loop(0, tn, drain, 0)
    half = buf.shape[2]
    acc_lo = sh_ref[:, :half]
    acc_hi = sh_ref[:, half:]
    for k in range(TOP_K):
        lo, hi = _unpack_bf16_pairs(buf[k])
        acc_lo = acc_lo + gate_ref[:, k:k + 1] * lo
        acc_hi = acc_hi + gate_ref[:, k:k + 1] * hi
    acc = jnp.concatenate([acc_lo, acc_hi], axis=1)
    is_ctx = _is_ctx_rows(tn, tiles_per_batch, n_ctx)
    o_ref[...] = xa_ref[...] + _segment_rows(mod_ref, 5, is_ctx) * acc


def _combine(dest_flat, gates, shared, ys, xa, modsel, rows_per_batch, n_ctx):
    t, d = shared.shape
    tn = 256
    tpb = rows_per_batch // tn
    return pl.pallas_call(
        functools.partial(_combine_kernel, tiles_per_batch=tpb, n_ctx=n_ctx),
        grid=(t // tn,),
        in_specs=[pl.BlockSpec((tn * TOP_K,), lambda i: (i,), memory_space=pltpu.SMEM),
                  pl.BlockSpec((tn, TOP_K), lambda i: (i, 0)),
                  pl.BlockSpec((tn, d), lambda i: (i, 0)),
                  pl.BlockSpec((tn, d), lambda i: (i, 0)),
                  pl.BlockSpec((1, 2, N_MOD, d), lambda i: (i // tpb, 0, 0, 0)),
                  pl.BlockSpec(memory_space=pl.ANY)],
        out_specs=pl.BlockSpec((tn, d), lambda i: (i, 0)),
        out_shape=jax.ShapeDtypeStruct((t, d), F32),
        scratch_shapes=[pltpu.VMEM((TOP_K, tn, d // 2), jnp.uint32), pltpu.SemaphoreType.DMA(())],
        compiler_params=_params(("arbitrary",)),
        name="moe_combine",
    )(dest_flat, gates, shared, xa, modsel, ys)


def _swiglu_kernel(x_ref, wg_ref, wu_ref, wd_ref, o_ref):
    x = x_ref[...].astype(BF16)
    hg = jnp.dot(x, wg_ref[...], preferred_element_type=F32)
    hu = jnp.dot(x, wu_ref[...], preferred_element_type=F32)
    h = (hg * jax.nn.sigmoid(hg) * hu).astype(BF16)
    o_ref[...] = jnp.dot(h, wd_ref[...], preferred_element_type=F32).astype(o_ref.dtype)


def _shared_expert(x, wg, wu, wd):
    m, d = x.shape
    ff = wg.shape[1]
    tm = _pick(m, (512, 384, 256, 128))
    return pl.pallas_call(
        _swiglu_kernel,
        grid=(m // tm,),
        in_specs=[pl.BlockSpec((tm, d), lambda i: (i, 0)),
                  pl.BlockSpec((d, ff), lambda i: (0, 0)),
                  pl.BlockSpec((d, ff), lambda i: (0, 0)),
                  pl.BlockSpec((ff, d), lambda i: (0, 0))],
        out_specs=pl.BlockSpec((tm, d), lambda i: (i, 0)),
        out_shape=jax.ShapeDtypeStruct((m, d), F32),
        compiler_params=_params(("parallel",)),
        name="shared_expert",
    )(x, wg, wu, wd)


def _moe_ffn(f, fp, xa, modsel, rows_per_batch, n_ctx, layer, router_w, router_bias, w_gate, w_up, w_down,
             ws_gate, ws_up, ws_down):
    t, d = f.shape
    tm = EXPERT_TM
    idx, gate, rank, cnt = _route(f, router_w.T, router_bias.astype(F32)[:, None])
    counts = cnt[:, 0]
    padded = (counts + tm - 1) // tm * tm
    pad_end = jnp.cumsum(padded)
    pad_start = pad_end - padded
    n_used = pad_end[-1] // tm
    n_blocks = (t * TOP_K) // tm + N_EXPERTS
    n_rows = n_blocks * tm
    experts = jnp.arange(N_EXPERTS, dtype=jnp.int32)
    start_of = jnp.sum(jnp.where(idx[..., None] == experts, pad_start, 0), axis=-1)
    dest = (start_of + rank).T.reshape(-1).astype(jnp.int32)
    block_index = jnp.minimum(jnp.arange(n_blocks, dtype=jnp.int32), n_used - 1).astype(jnp.int32)
    block_expert = jnp.minimum(jnp.sum(pad_end[None, :] <= (block_index * tm)[:, None], axis=1),
                               N_EXPERTS - 1).astype(jnp.int32)
    tail_rows = jnp.where(padded > counts, pad_end - tm, n_rows + experts * tm).astype(jnp.int32)
    xs = _zero_blocks(tail_rows, n_rows + N_EXPERTS * tm, tm, d // 2)
    xs = _scatter_rows(dest, fp, xs)
    ys = _expert_blocks(block_expert, block_index, n_used.reshape(1).astype(jnp.int32), xs, n_blocks, tm, layer,
                        w_gate, w_up, w_down)
    shared = _shared_expert(f, ws_gate.astype(BF16), ws_up.astype(BF16), ws_down.astype(BF16))
    return _combine(dest, gate.T, shared, ys, xa, modsel, rows_per_batch, n_ctx)


def _rms(u, gain):
    return u * lax.rsqrt(jnp.mean(u * u, axis=-1, keepdims=True) + NORM_EPS) * gain


def _rope_tables(n_ctx, n_lat):
    rows = n_lat // GRID_W
    row = jnp.repeat(jnp.arange(rows, dtype=F32), GRID_W)
    col = jnp.tile(jnp.arange(GRID_W, dtype=F32), rows)
    n_freq = HEAD_DIM // 4
    inv = ROPE_THETA ** (-jnp.arange(n_freq, dtype=F32) / n_freq)
    ang = jnp.concatenate([row[:, None] * inv, col[:, None] * inv], axis=-1)
    cos = jnp.repeat(jnp.cos(ang), 2, axis=-1)
    sin = jnp.repeat(jnp.sin(ang), 2, axis=-1)
    sign = jnp.tile(jnp.array([-1.0, 1.0], F32), HEAD_DIM // 2)
    cos = jnp.concatenate([jnp.ones((n_ctx, HEAD_DIM), F32), cos], axis=0)
    sin = jnp.concatenate([jnp.zeros((n_ctx, HEAD_DIM), F32), sin * sign], axis=0)
    return cos, sin


def _ada_mod(cond, w, bias):
    m = jax.nn.silu(cond)
    rows = m.shape[0]
    pad = (-rows) % 16
    mp = jnp.pad(m, ((0, pad), (0, 0))).astype(BF16)
    out = _matmul(mp, w.astype(BF16), F32)[:rows] + bias
    return out.reshape(rows, N_MOD, -1)


def _attention_layer(xa, modsel, gain, n_ctx, w_in, q_gain, k_gain, lq1, lk1, lq2, lk2, subln, lambda_init):
    b, t, d = xa.shape
    cos, sin = _rope_tables(n_ctx, t - n_ctx)
    head_gains = jnp.stack([q_gain * _Q_SCALE, k_gain]).astype(F32)
    qkv = _fused_proj(xa, gain, modsel, w_in.astype(BF16), n_ctx, BF16,
                      head_gains=head_gains, rope=(cos - 1.0, sin)).reshape(b, t, ATTN_IN)
    lam = (jnp.exp(jnp.sum(lq1 * lk1)) - jnp.exp(jnp.sum(lq2 * lk2)) + lambda_init).reshape(1, 1).astype(F32)
    y = _gqa_attention(qkv, n_ctx)
    return _diff_attention(qkv, y, lam, subln.reshape(1, 2 * HEAD_DIM), 1.0 - lambda_init, n_ctx)


def _ssd_layer(xa, modsel, gain, n_ctx, w_in, conv_w, conv_b, dt_bias, a_log, d_skip, norm_gain):
    b, t, d = xa.shape
    proj = _fused_proj(xa, gain, modsel, w_in.astype(BF16), n_ctx, F32).reshape(b, t, -1)
    xbc = _conv_silu(proj, SSD_INNER, conv_w, conv_b, n_ctx)
    dt = proj[..., SSD_INNER + SSD_CONV_DIM:]
    dt = jax.nn.softplus(dt.reshape(b, t, 2, SSD_GROUPS, SSD_HPG) + dt_bias.reshape(2, SSD_GROUPS, SSD_HPG))
    dt_c = jnp.transpose(dt, (2, 0, 3, 1, 4))
    dt_r = jnp.transpose(dt, (2, 0, 3, 4, 1))
    a_coef = -jnp.exp(a_log)
    a_row = a_coef.reshape(2, SSD_GROUPS, 1, SSD_HPG)
    a_col = a_coef.reshape(2, SSD_GROUPS, SSD_HPG, 1)
    skip = d_skip.reshape(2, SSD_GROUPS, SSD_HPG, 1)
    y = _ssd_scan(xbc, dt_c, dt_r, a_row, a_col, skip, n_ctx, F32)
    return _gate_norm(y, proj, norm_gain, n_ctx)


def kernel(x, c, ctx, c_ctx, mod_w, mod_b, norm_mix, norm_ffn, norm_final, attn_w_in, attn_w_out, attn_q_gain,
           attn_k_gain, diff_lam_q1, diff_lam_k1, diff_lam_q2, diff_lam_k2, diff_subln, ssd_w_in, ssd_conv_w,
           ssd_conv_b, ssd_dt_bias, ssd_a_log, ssd_d, ssd_norm, ssd_w_out, router_w, router_bias, exp_w_gate,
           exp_w_up, exp_w_down, shared_w_gate, shared_w_up, shared_w_down):
    b, s, d = x.shape
    n_ctx = ctx.shape[1]
    t = n_ctx + s
    xa = jnp.concatenate([ctx, x], axis=1)
    cond = jnp.concatenate([c, c_ctx[None]], axis=0)
    out = None
    for i in range(DEPTH):
        last = i == DEPTH - 1
        mod = _ada_mod(cond, mod_w[i], mod_b[i])
        modsel = jnp.stack([jnp.broadcast_to(mod[b], (b, N_MOD, d)), mod[:b]], axis=1)
        j = i // 2
        moe_w = (i, router_w[i], router_bias[i], exp_w_gate, exp_w_up, exp_w_down,
                 shared_w_gate[i], shared_w_up[i], shared_w_down[i])
        if i % 2 == 0:
            lambda_init = 0.8 - 0.6 * math.exp(-0.3 * i)
            y = _attention_layer(xa, modsel, norm_mix[i], n_ctx, attn_w_in[j], attn_q_gain[j], attn_k_gain[j],
                                 diff_lam_q1[j], diff_lam_k1[j], diff_lam_q2[j], diff_lam_k2[j],
                                 diff_subln[j], lambda_init)
            w_out = attn_w_out[j]
        else:
            y = _ssd_layer(xa, modsel, norm_mix[i], n_ctx, ssd_w_in[j], ssd_conv_w[j], ssd_conv_b[j],
                           ssd_dt_bias[j], ssd_a_log[j], ssd_d[j], ssd_norm[j])
            w_out = ssd_w_out[j]
        if last:
            x_new, f, fp = _fused_out_proj(y if y.shape[1] == s else y[:, n_ctx:], w_out.astype(BF16), xa, n_ctx,
                                           modsel, norm_ffn[i], 0)
            out = _moe_ffn(f.reshape(b * s, d), fp.reshape(b * s, d // 2), x_new.reshape(b * s, d), modsel, s, 0,
                           *moe_w).reshape(b, s, d)
        else:
            x_new, f, fp = _fused_out_proj(y, w_out.astype(BF16), xa, 0, modsel, norm_ffn[i], n_ctx)
            xa = _moe_ffn(f.reshape(b * t, d), fp.reshape(b * t, d // 2), x_new.reshape(b * t, d), modsel, t, n_ctx,
                          *moe_w).reshape(b, t, d)
    return _rms(out, norm_final)
```

```python
import functools
import math

import jax
import jax.numpy as jnp
from jax import lax
from jax.experimental import pallas as pl
from jax.experimental.pallas import tpu as pltpu

F32 = jnp.float32
BF16 = jnp.bfloat16

D_MODEL = 2048
DEPTH = 2
GRID_W = 64
NORM_EPS = 1e-6
N_MOD = 6
HEAD_DIM = 128
ROPE_THETA = 10000.0
A_Q_HEADS = 8
A_KV_HEADS = 2
A_GROUP = A_Q_HEADS // A_KV_HEADS
B_HEADS = 4
QA_W = A_Q_HEADS * HEAD_DIM
KA_W = A_KV_HEADS * HEAD_DIM
QB_W = 2 * B_HEADS * HEAD_DIM
ATTN_IN = QA_W + 2 * KA_W + 3 * QB_W
SSD_INNER = 2 * D_MODEL
SSD_HEADDIM = 64
SSD_HEADS = SSD_INNER // SSD_HEADDIM
SSD_GROUPS = 8
SSD_HPG = SSD_HEADS // SSD_GROUPS
SSD_STATE = 128
SSD_CONV = 3
SSD_CHUNK = 128
SSD_GROUP_W = SSD_HPG * SSD_HEADDIM
SSD_CONV_DIM = SSD_INNER + 2 * SSD_GROUPS * SSD_STATE
N_EXPERTS = 64
EXPERT_FF = 512
TOP_K = 8
N_EXPERT_GROUPS = 8
TOPK_GROUPS = 4
ROUTED_SCALE = 2.5
EXPERT_TM = 512

VMEM_LIMIT_BYTES = 56 * 1024 * 1024


def _pick(n, prefs):
    for p in prefs:
        if n % p == 0:
            return p
    raise ValueError(f"no tile in {prefs} divides {n}")


def _params(sem):
    return pltpu.CompilerParams(dimension_semantics=sem, vmem_limit_bytes=VMEM_LIMIT_BYTES)


def _mm_kernel(x_ref, w_ref, o_ref):
    o_ref[...] = jnp.dot(x_ref[...], w_ref[...], preferred_element_type=F32).astype(o_ref.dtype)


def _matmul(x, w, out_dtype):
    m, k = x.shape
    n = w.shape[1]
    tm = _pick(m, (1024, 512, 384, 256, 128, 16, 8))
    tn = _pick(n, (1024, 512, 384, 256, 128))
    return pl.pallas_call(
        _mm_kernel,
        grid=(m // tm, n // tn),
        in_specs=[pl.BlockSpec((tm, k), lambda i, j: (i, 0)),
                  pl.BlockSpec((k, tn), lambda i, j: (0, j))],
        out_specs=pl.BlockSpec((tm, tn), lambda i, j: (i, j)),
        out_shape=jax.ShapeDtypeStruct((m, n), out_dtype),
        compiler_params=_params(("parallel", "arbitrary")),
        name="matmul",
    )(x, w)


def _pack_bf16_pairs(v):
    half = v.shape[1] // 2
    lo = lax.bitcast_convert_type(v[:, :half].astype(BF16).astype(F32), jnp.uint32) >> 16
    hi = lax.bitcast_convert_type(v[:, half:].astype(BF16).astype(F32), jnp.uint32) & jnp.uint32(0xFFFF0000)
    return hi | lo


def _unpack_bf16_pairs(w):
    lo = lax.bitcast_convert_type(w << 16, F32)
    hi = lax.bitcast_convert_type(w & jnp.uint32(0xFFFF0000), F32)
    return lo, hi


def _segment_rows(mod_ref, k, is_ctx):
    return jnp.where(is_ctx, mod_ref[0, 0, k:k + 1, :], mod_ref[0, 1, k:k + 1, :])


def _is_ctx_rows(tm, tiles_per_batch, n_ctx):
    row = (pl.program_id(0) % tiles_per_batch) * tm + lax.broadcasted_iota(jnp.int32, (tm, 1), 0)
    return row < n_ctx


def _norm_mod(x, gain, shift, scale):
    xn = x * lax.rsqrt(jnp.mean(x * x, axis=-1, keepdims=True) + NORM_EPS) * gain
    return xn * (1.0 + scale) + shift


_Q_SCALE = HEAD_DIM ** -0.5
_ATTN_HEADS = ([(0, True, 1.0)] * A_Q_HEADS + [(1, True, 1.0)] * A_KV_HEADS + [(None, False, 1.0)] * A_KV_HEADS
               + [(None, True, _Q_SCALE)] * (2 * B_HEADS) + [(None, True, 1.0)] * (2 * B_HEADS)
               + [(None, False, 1.0)] * (2 * B_HEADS))
_PROJ_HEADS_PER_TILE = 4


def _head_epilogue(u, cfg, gains_ref, cm1, sn, even):
    gain_row, rope, scale = cfg
    if gain_row is not None:
        u = u * lax.rsqrt(jnp.mean(u * u, axis=-1, keepdims=True) + NORM_EPS) * gains_ref[gain_row:gain_row + 1, :]
    if scale != 1.0:
        u = u * scale
    if rope:
        partner = jnp.where(even, pltpu.roll(u, HEAD_DIM - 1, axis=1), pltpu.roll(u, 1, axis=1))
        u = u * (1.0 + cm1) + partner * sn
    return u


def _proj_kernel(x_ref, gain_ref, mod_ref, w_ref, *rest, tiles_per_batch, n_ctx, heads):
    if heads:
        gains_ref, cm1_ref, sn_ref, o_ref, h_s = rest
    else:
        o_ref, h_s = rest
    tm = x_ref.shape[0]
    is_ctx = _is_ctx_rows(tm, tiles_per_batch, n_ctx)

    @pl.when(pl.program_id(1) == 0)
    def _():
        h = _norm_mod(x_ref[...], gain_ref[...], _segment_rows(mod_ref, 0, is_ctx), _segment_rows(mod_ref, 1, is_ctx))
        h_s[...] = h.astype(BF16)

    if not heads:
        o_ref[...] = jnp.dot(h_s[...], w_ref[...], preferred_element_type=F32).astype(o_ref.dtype)
        return
    even = (lax.broadcasted_iota(jnp.int32, (tm, HEAD_DIM), 1) & 1) == 0
    hw = 2 * HEAD_DIM
    for tile in range(len(_ATTN_HEADS) // heads):
        @pl.when(pl.program_id(1) == tile)
        def _(tile=tile):
            cm1 = cm1_ref[...]
            sn = sn_ref[...]
            for half in range(heads // 2):
                acc = jnp.dot(h_s[...], w_ref[:, half * hw:(half + 1) * hw], preferred_element_type=F32)
                for hh in range(2):
                    col = half * hw + hh * HEAD_DIM
                    cfg = _ATTN_HEADS[tile * heads + half * 2 + hh]
                    out = _head_epilogue(acc[:, hh * HEAD_DIM:(hh + 1) * HEAD_DIM], cfg, gains_ref, cm1, sn, even)
                    o_ref[:, col:col + HEAD_DIM] = out.astype(o_ref.dtype)


def _fused_proj(xa, gain, modsel, w, n_ctx, out_dtype, head_gains=None, rope=None):
    b, t, d = xa.shape
    n = w.shape[1]
    tm = _pick(t, (768, 512, 384, 256))
    tpb = t // tm
    heads = 0 if head_gains is None else _PROJ_HEADS_PER_TILE
    tn = heads * HEAD_DIM if heads else _pick(n, (1152, 1024, 768, 512, 384, 256, 128))
    in_specs = [pl.BlockSpec((tm, d), lambda i, j: (i, 0)),
                pl.BlockSpec((1, d), lambda i, j: (0, 0)),
                pl.BlockSpec((1, 2, N_MOD, d), lambda i, j: (i // tpb, 0, 0, 0)),
                pl.BlockSpec((d, tn), lambda i, j: (0, j))]
    args = [xa.reshape(b * t, d), gain.reshape(1, d), modsel, w]
    if heads:
        in_specs += [pl.BlockSpec((2, HEAD_DIM), lambda i, j: (0, 0)),
                     pl.BlockSpec((tm, HEAD_DIM), lambda i, j: (i % tpb, 0)),
                     pl.BlockSpec((tm, HEAD_DIM), lambda i, j: (i % tpb, 0))]
        args += [head_gains, rope[0], rope[1]]
    return pl.pallas_call(
        functools.partial(_proj_kernel, tiles_per_batch=tpb, n_ctx=n_ctx, heads=heads),
        grid=(b * tpb, n // tn),
        in_specs=in_specs,
        out_specs=pl.BlockSpec((tm, tn), lambda i, j: (i, j)),
        out_shape=jax.ShapeDtypeStruct((b * t, n), out_dtype),
        scratch_shapes=[pltpu.VMEM((tm, d), BF16)],
        compiler_params=_params(("parallel", "arbitrary")),
        name="norm_mod_proj",
    )(*args)


def _out_proj_kernel(y_ref, w_ref, xa_ref, mod_ref, gain_ref, xo_ref, f_ref, fp_ref, *, tiles_per_batch, n_ctx):
    tm = y_ref.shape[1]
    is_ctx = _is_ctx_rows(tm, tiles_per_batch, n_ctx)
    acc = jnp.dot(y_ref[0], w_ref[...], preferred_element_type=F32)
    x_new = xa_ref[0] + _segment_rows(mod_ref, 2, is_ctx) * acc
    xo_ref[0] = x_new
    f = _norm_mod(x_new, gain_ref[...], _segment_rows(mod_ref, 3, is_ctx), _segment_rows(mod_ref, 4, is_ctx))
    f_ref[0] = f
    fp_ref[0] = _pack_bf16_pairs(f)


def _fused_out_proj(y, w, xa, xa_row0, modsel, gain, n_ctx):
    b, tq, k = y.shape
    d = w.shape[1]
    tm = _pick(tq, (384, 256)) if n_ctx else _pick(tq, (256, 128))
    assert xa_row0 % tm == 0
    r0 = xa_row0 // tm
    tpb = tq // tm
    tile = pl.BlockSpec((1, tm, d), lambda i: (i // tpb, i % tpb, 0))
    return pl.pallas_call(
        functools.partial(_out_proj_kernel, tiles_per_batch=tpb, n_ctx=n_ctx),
        grid=(b * tpb,),
        in_specs=[pl.BlockSpec((1, tm, k), lambda i: (i // tpb, i % tpb, 0)),
                  pl.BlockSpec((k, d), lambda i: (0, 0)),
                  pl.BlockSpec((1, tm, d), lambda i: (i // tpb, r0 + i % tpb, 0)),
                  pl.BlockSpec((1, 2, N_MOD, d), lambda i: (i // tpb, 0, 0, 0)),
                  pl.BlockSpec((1, d), lambda i: (0, 0))],
        out_specs=[tile, tile, pl.BlockSpec((1, tm, d // 2), lambda i: (i // tpb, i % tpb, 0))],
        out_shape=[jax.ShapeDtypeStruct((b, tq, d), F32), jax.ShapeDtypeStruct((b, tq, d), F32),
                   jax.ShapeDtypeStruct((b, tq, d // 2), jnp.uint32)],
        compiler_params=_params(("parallel",)),
        name="out_proj_residual_norm",
    )(y, w, xa, modsel, gain.reshape(1, d))


def _softmax_rows(s):
    m = jnp.max(s, axis=-1, keepdims=True)
    p = jnp.exp(s - m)
    return p, jnp.sum(p, axis=-1, keepdims=True)


def _gqa_heads(q_ref, k, v, o_ref):
    for g in range(A_GROUP):
        q = q_ref[0, :, g * HEAD_DIM:(g + 1) * HEAD_DIM]
        s = lax.dot_general(q, k, (((1,), (1,)), ((), ())), preferred_element_type=F32)
        p, l = _softmax_rows(s)
        o = jnp.dot(p.astype(BF16), v, preferred_element_type=F32) / l
        o_ref[0, :, g * HEAD_DIM:(g + 1) * HEAD_DIM] = o.astype(o_ref.dtype)


def _gqa_kernel(q_ref, k_ref, v_ref, o_ref, *, n_ctx):
    @pl.when(pl.program_id(2) == 0)
    def _():
        _gqa_heads(q_ref, k_ref[0, :n_ctx], v_ref[0, :n_ctx], o_ref)

    @pl.when(pl.program_id(2) > 0)
    def _():
        _gqa_heads(q_ref, k_ref[0], v_ref[0], o_ref)


def _gqa_attention(qkv, n_ctx):
    b, t, _ = qkv.shape
    tq = n_ctx
    gw = A_GROUP * HEAD_DIM
    k0 = QA_W // HEAD_DIM
    v0 = (QA_W + KA_W) // HEAD_DIM
    return pl.pallas_call(
        functools.partial(_gqa_kernel, n_ctx=n_ctx),
        grid=(b, A_KV_HEADS, t // tq),
        in_specs=[pl.BlockSpec((1, tq, gw), lambda bi, h, i: (bi, i, h)),
                  pl.BlockSpec((1, t, HEAD_DIM), lambda bi, h, i: (bi, 0, k0 + h)),
                  pl.BlockSpec((1, t, HEAD_DIM), lambda bi, h, i: (bi, 0, v0 + h))],
        out_specs=pl.BlockSpec((1, tq, gw), lambda bi, h, i: (bi, i, h)),
        out_shape=jax.ShapeDtypeStruct((b, t, QA_W + QB_W), BF16),
        compiler_params=_params(("parallel", "parallel", "arbitrary")),
        name="gqa_attention",
    )(qkv, qkv, qkv)


def _diff_head(lam, gain_ref, q_ref, k, v, o_ref, out_scale):
    parts = []
    for m in range(2):
        q = q_ref[0, :, m * HEAD_DIM:(m + 1) * HEAD_DIM]
        s = lax.dot_general(q, k[:, m * HEAD_DIM:(m + 1) * HEAD_DIM], (((1,), (1,)), ((), ())),
                            preferred_element_type=F32)
        p, l = _softmax_rows(s)
        parts.append(p / l)
    a = parts[0] - lam * parts[1]
    y = jnp.dot(a.astype(BF16), v, preferred_element_type=F32)
    y = y * lax.rsqrt(jnp.mean(y * y, axis=-1, keepdims=True) + NORM_EPS)
    o_ref[0] = (y * gain_ref[...] * out_scale).astype(o_ref.dtype)


def _diff_kernel(lam_ref, gain_ref, q_ref, k_ref, v_ref, y_in_ref, o_ref, *, out_scale, n_ctx):
    del y_in_ref
    lam = lam_ref[0, 0]

    @pl.when(pl.program_id(2) == 0)
    def _():
        _diff_head(lam, gain_ref, q_ref, k_ref[0, :n_ctx], v_ref[0, :n_ctx], o_ref, out_scale)

    @pl.when(pl.program_id(2) > 0)
    def _():
        _diff_head(lam, gain_ref, q_ref, k_ref[0], v_ref[0], o_ref, out_scale)


def _diff_attention(qkv, y, lam, subln_gain, out_scale, n_ctx):
    b, t, _ = qkv.shape
    tq = n_ctx
    hw = 2 * HEAD_DIM
    q0 = (QA_W + 2 * KA_W) // hw
    k0 = q0 + B_HEADS
    v0 = k0 + B_HEADS
    o0 = QA_W // hw
    return pl.pallas_call(
        functools.partial(_diff_kernel, out_scale=out_scale, n_ctx=n_ctx),
        grid=(b, B_HEADS, t // tq),
        in_specs=[pl.BlockSpec(memory_space=pltpu.SMEM),
                  pl.BlockSpec((1, hw), lambda bi, h, i: (0, 0)),
                  pl.BlockSpec((1, tq, hw), lambda bi, h, i: (bi, i, q0 + h)),
                  pl.BlockSpec((1, t, hw), lambda bi, h, i: (bi, 0, k0 + h)),
                  pl.BlockSpec((1, t, hw), lambda bi, h, i: (bi, 0, v0 + h)),
                  pl.BlockSpec(memory_space=pl.ANY)],
        out_specs=pl.BlockSpec((1, tq, hw), lambda bi, h, i: (bi, i, o0 + h)),
        out_shape=jax.ShapeDtypeStruct(y.shape, y.dtype),
        input_output_aliases={5: 0},
        compiler_params=_params(("parallel", "parallel", "arbitrary")),
        name="diff_attention",
    )(lam, subln_gain, qkv, qkv, qkv, y)


def _split3(a):
    a1 = a.astype(BF16)
    r1 = a - a1.astype(F32)
    a2 = r1.astype(BF16)
    a3 = (r1 - a2.astype(F32)).astype(BF16)
    return a1, a2, a3


def _dot_exact_rhs(a, rhs01):
    out = None
    for part in _split3(a):
        d = jnp.dot(part, rhs01, preferred_element_type=F32)
        out = d if out is None else out + d
    return out


def _dot_exact_lhs(lhs01, a):
    out = None
    for part in _split3(a):
        d = jnp.dot(lhs01, part, preferred_element_type=F32)
        out = d if out is None else out + d
    return out


SSD_GROUPS_PER_STEP = 8


def _ssd_group(d_sign, acr, acc, skip, x_tok, b_tok, c_tok, dt_c, dt_r, state):
    q = SSD_CHUNK
    rows = lax.broadcasted_iota(jnp.int32, (q, q), 0)
    cols = lax.broadcasted_iota(jnp.int32, (q, q), 1)
    signed = (rows - cols) * d_sign
    keep_sl = signed <= 0
    tri_ks01 = jnp.where(keep_sl, 1.0, 0.0).astype(BF16)
    tri_sk01 = jnp.where(signed >= 0, 1.0, 0.0).astype(BF16)

    x = jnp.transpose(x_tok)
    bm = b_tok.astype(BF16)
    cm = c_tok.astype(BF16)
    a_r = dt_r * acc
    a_c = dt_c * acr
    cum_r = _dot_exact_rhs(a_r, tri_ks01)
    cum_c = _dot_exact_lhs(tri_sk01, a_c)
    tot = jnp.sum(a_r, axis=1, keepdims=True)
    to_end_r = jnp.exp(tot - cum_r)
    from_start_r = jnp.exp(cum_r)
    tot_e = jnp.exp(tot)

    g_sl = lax.dot_general(bm, cm, (((1,), (1,)), ((), ())), preferred_element_type=F32)
    y_off = lax.dot_general(state.astype(BF16), cm, (((1,), (1,)), ((), ())), preferred_element_type=F32)

    xw_parts, y_parts = [], []
    for h in range(SSD_HPG):
        sl = slice(h * SSD_HEADDIM, (h + 1) * SSD_HEADDIM)
        xh = x[sl, :]
        xd = xh * dt_r[h:h + 1, :]
        seg = cum_r[h:h + 1, :] - cum_c[:, h:h + 1]
        decay = jnp.where(keep_sl, jnp.exp(jnp.where(keep_sl, seg, 0.0)), 0.0)
        m_h = (g_sl * decay).astype(BF16)
        y_h = jnp.dot(xd.astype(BF16), m_h, preferred_element_type=F32)
        y_parts.append(y_h + y_off[sl, :] * from_start_r[h:h + 1, :] + skip[h:h + 1, :] * xh)
        xw_parts.append((xd * to_end_r[h:h + 1, :]).astype(BF16))
    s_new = jnp.dot(jnp.concatenate(xw_parts, axis=0), bm, preferred_element_type=F32)
    decay_rows = jnp.concatenate([jnp.broadcast_to(tot_e[h:h + 1, :], (SSD_HEADDIM, 1)) for h in range(SSD_HPG)], axis=0)
    return jnp.transpose(jnp.concatenate(y_parts, axis=0)), state * decay_rows + s_new


def _ssd_kernel(acr_ref, acc_ref, skip_ref, x_ref, b_ref, c_ref, dtc_ref, dtr_ref, y_ref, state_ref):
    @pl.when(pl.program_id(3) == 0)
    def _():
        state_ref[...] = jnp.zeros_like(state_ref)

    d_sign = 1 - 2 * pl.program_id(1)
    for g in range(SSD_GROUPS_PER_STEP):
        xs = slice(g * SSD_GROUP_W, (g + 1) * SSD_GROUP_W)
        ns = slice(g * SSD_STATE, (g + 1) * SSD_STATE)
        y, state = _ssd_group(d_sign, acr_ref[0, g], acc_ref[0, g], skip_ref[0, g], x_ref[0, :, xs], b_ref[0, :, ns],
                              c_ref[0, :, ns], dtc_ref[0, 0, g], dtr_ref[0, 0, g], state_ref[g])
        y_ref[0, 0, :, xs] = y.astype(y_ref.dtype)
        state_ref[g] = state


def _ssd_scan(xbc, dt_c, dt_r, a_row, a_col, skip, n_ctx, out_dtype):
    b, t, _ = xbc.shape
    gs = SSD_GROUPS_PER_STEP
    b0 = SSD_INNER // (gs * SSD_STATE)
    c0 = b0 + SSD_GROUPS // gs
    q = SSD_CHUNK
    ncc = n_ctx // q
    nch = t // q

    def chunk(d, s):
        back = jnp.where(s < ncc, ncc - 1 - s, nch - 1 - s + ncc)
        return jnp.where(d == 0, s, back)

    return pl.pallas_call(
        _ssd_kernel,
        grid=(b, 2, SSD_GROUPS // gs, nch),
        in_specs=[pl.BlockSpec((1, gs, 1, SSD_HPG), lambda bi, d, g, s: (d, g, 0, 0)),
                  pl.BlockSpec((1, gs, SSD_HPG, 1), lambda bi, d, g, s: (d, g, 0, 0)),
                  pl.BlockSpec((1, gs, SSD_HPG, 1), lambda bi, d, g, s: (d, g, 0, 0)),
                  pl.BlockSpec((1, q, gs * SSD_GROUP_W), lambda bi, d, g, s: (bi, chunk(d, s), g)),
                  pl.BlockSpec((1, q, gs * SSD_STATE), lambda bi, d, g, s: (bi, chunk(d, s), b0 + g)),
                  pl.BlockSpec((1, q, gs * SSD_STATE), lambda bi, d, g, s: (bi, chunk(d, s), c0 + g)),
                  pl.BlockSpec((1, 1, gs, q, SSD_HPG), lambda bi, d, g, s: (d, bi, g, chunk(d, s), 0)),
                  pl.BlockSpec((1, 1, gs, SSD_HPG, q), lambda bi, d, g, s: (d, bi, g, 0, chunk(d, s)))],
        out_specs=pl.BlockSpec((1, 1, q, gs * SSD_GROUP_W), lambda bi, d, g, s: (d, bi, chunk(d, s), g)),
        out_shape=jax.ShapeDtypeStruct((2, b, t, SSD_INNER), out_dtype),
        scratch_shapes=[pltpu.VMEM((gs, SSD_GROUP_W, SSD_STATE), F32)],
        compiler_params=_params(("parallel", "parallel", "parallel", "arbitrary")),
        name="ssd_scan",
    )(a_row, a_col, skip, xbc, xbc, xbc, dt_c, dt_r)


def _conv_silu_kernel(x_ref, w_ref, b_ref, o_ref, *, n_ctx):
    x = x_ref[0]
    t = x.shape[0]
    row = lax.broadcasted_iota(jnp.int32, (t, 1), 0)
    prev = jnp.where((row == 0) | (row == n_ctx), 0.0, pltpu.roll(x, 1, axis=0))
    nxt = jnp.where((row == n_ctx - 1) | (row == t - 1), 0.0, pltpu.roll(x, t - 1, axis=0))
    u = prev * w_ref[0:1, :] + x * w_ref[1:2, :] + nxt * w_ref[2:3, :] + b_ref[...]
    o_ref[0] = u * jax.nn.sigmoid(u)


def _conv_silu(proj, col0, conv_w, conv_b, n_ctx):
    b, t, _ = proj.shape
    tc = 512
    c0 = col0 // tc
    return pl.pallas_call(
        functools.partial(_conv_silu_kernel, n_ctx=n_ctx),
        grid=(b, SSD_CONV_DIM // tc),
        in_specs=[pl.BlockSpec((1, t, tc), lambda bi, j: (bi, 0, c0 + j)),
                  pl.BlockSpec((SSD_CONV, tc), lambda bi, j: (0, j)),
                  pl.BlockSpec((1, tc), lambda bi, j: (0, j))],
        out_specs=pl.BlockSpec((1, t, tc), lambda bi, j: (bi, 0, j)),
        out_shape=jax.ShapeDtypeStruct((b, t, SSD_CONV_DIM), F32),
        compiler_params=_params(("parallel", "parallel")),
        name="ssd_conv_silu",
    )(proj, conv_w, conv_b.reshape(1, -1))


def _gate_norm_kernel(y_ref, z_ref, gain_ref, o_ref):
    z = z_ref[0]
    g = (y_ref[0, 0] + y_ref[1, 0]) * (z * jax.nn.sigmoid(z))
    o_ref[0] = (g * lax.rsqrt(jnp.mean(g * g, axis=-1, keepdims=True) + NORM_EPS) * gain_ref[...]).astype(o_ref.dtype)


def _gate_norm(y, proj, gain, n_ctx):
    _, b, t, c = y.shape
    tt = 256
    assert n_ctx % tt == 0
    r0 = n_ctx // tt
    s = t - n_ctx
    return pl.pallas_call(
        _gate_norm_kernel,
        grid=(b, s // tt),
        in_specs=[pl.BlockSpec((2, 1, tt, c), lambda bi, i: (0, bi, r0 + i, 0)),
                  pl.BlockSpec((1, tt, c), lambda bi, i: (bi, r0 + i, 0)),
                  pl.BlockSpec((1, c), lambda bi, i: (0, 0))],
        out_specs=pl.BlockSpec((1, tt, c), lambda bi, i: (bi, i, 0)),
        out_shape=jax.ShapeDtypeStruct((b, s, c), BF16),
        compiler_params=_params(("parallel", "parallel")),
        name="ssd_gate_norm",
    )(y, proj, gain.reshape(1, c))


def _route_kernel(f_ref, rwt_ref, bias_ref, idx_ref, gate_ref, rank_ref, cnt_ref, carry_ref):
    tn = f_ref.shape[0]
    ne, ng, pg = N_EXPERTS, N_EXPERT_GROUPS, N_EXPERTS // N_EXPERT_GROUPS
    neg = -jnp.inf

    @pl.when(pl.program_id(0) == 0)
    def _():
        carry_ref[...] = jnp.zeros_like(carry_ref)

    logits = lax.dot_general(rwt_ref[...], f_ref[...], (((1,), (1,)), ((), ())),
                             preferred_element_type=F32, precision=lax.Precision.HIGHEST)
    scores = jax.nn.sigmoid(logits)
    g3 = (scores + bias_ref[...]).reshape(ng, pg, tn)
    io3 = lax.broadcasted_iota(jnp.int32, (ng, pg, tn), 1)
    m1 = jnp.max(g3, axis=1, keepdims=True)
    i1 = jnp.min(jnp.where(g3 == m1, io3, pg), axis=1, keepdims=True)
    m2 = jnp.max(jnp.where(io3 == i1, neg, g3), axis=1, keepdims=True)
    work = (m1 + m2).reshape(ng, tn)
    iog = lax.broadcasted_iota(jnp.int32, (ng, tn), 0)
    ok = jnp.zeros((ng, tn), F32)
    for _ in range(TOPK_GROUPS):
        m = jnp.max(work, axis=0, keepdims=True)
        gi = jnp.min(jnp.where(work == m, iog, ng), axis=0, keepdims=True)
        hit = iog == gi
        ok = jnp.where(hit, 1.0, ok)
        work = jnp.where(hit, neg, work)
    sel = jnp.where(ok.reshape(ng, 1, tn) > 0.0, g3, neg).reshape(ne, tn)
    ioe = lax.broadcasted_iota(jnp.int32, (ne, tn), 0)
    onehot = jnp.zeros((ne, tn), F32)
    idxs, ws = [], []
    for _ in range(TOP_K):
        m = jnp.max(sel, axis=0, keepdims=True)
        ei = jnp.min(jnp.where(sel == m, ioe, ne), axis=0, keepdims=True)
        hit = ioe == ei
        idxs.append(ei)
        ws.append(jnp.sum(jnp.where(hit, scores, 0.0), axis=0, keepdims=True))
        sel = jnp.where(hit, neg, sel)
        onehot = jnp.where(hit, 1.0, onehot)
    w = jnp.concatenate(ws, axis=0)
    gate_ref[...] = w / jnp.sum(w, axis=0, keepdims=True) * ROUTED_SCALE
    idx_ref[...] = jnp.concatenate(idxs, axis=0)
    r = lax.broadcasted_iota(jnp.int32, (tn, tn), 0)
    c = lax.broadcasted_iota(jnp.int32, (tn, tn), 1)
    ahead = jnp.where(r < c, 1.0, 0.0).astype(BF16)
    cum = carry_ref[...] + jnp.dot(onehot.astype(BF16), ahead, preferred_element_type=F32)
    ranks = [jnp.sum(jnp.where(ioe == idxs[k], cum, 0.0), axis=0, keepdims=True) for k in range(TOP_K)]
    rank_ref[...] = jnp.concatenate(ranks, axis=0).astype(jnp.int32)
    total = carry_ref[...] + jnp.sum(onehot, axis=1, keepdims=True)
    carry_ref[...] = total
    cnt_ref[...] = total.astype(jnp.int32)


def _route(f, router_wt, router_bias):
    t, d = f.shape
    tn = _pick(t, (512, 256, 128))
    kt = pl.BlockSpec((TOP_K, tn), lambda i: (0, i))
    return pl.pallas_call(
        _route_kernel,
        grid=(t // tn,),
        in_specs=[pl.BlockSpec((tn, d), lambda i: (i, 0)),
                  pl.BlockSpec((N_EXPERTS, d), lambda i: (0, 0)),
                  pl.BlockSpec((N_EXPERTS, 1), lambda i: (0, 0))],
        out_specs=[kt, kt, kt, pl.BlockSpec((N_EXPERTS, 1), lambda i: (0, 0))],
        out_shape=[jax.ShapeDtypeStruct((TOP_K, t), jnp.int32), jax.ShapeDtypeStruct((TOP_K, t), F32),
                   jax.ShapeDtypeStruct((TOP_K, t), jnp.int32), jax.ShapeDtypeStruct((N_EXPERTS, 1), jnp.int32)],
        scratch_shapes=[pltpu.VMEM((N_EXPERTS, 1), F32)],
        compiler_params=_params(("arbitrary",)),
        name="moe_route",
    )(f, router_wt, router_bias)


def _scatter_kernel(dest_ref, f_ref, xs_in_ref, xs_ref, sem):
    del xs_in_ref
    ts = f_ref.shape[0]

    def row_copy(t, k):
        return pltpu.make_async_copy(f_ref.at[pl.ds(t, 1)], xs_ref.at[pl.ds(dest_ref[t * TOP_K + k], 1)], sem)

    def issue(t, carry):
        for k in range(TOP_K):
            row_copy(t, k).start()
        return carry

    def drain(t, carry):
        for k in range(TOP_K):
            row_copy(t, k).wait()
        return carry

    lax.fori_loop(0, ts, issue, 0)
    lax.fori_loop(0, ts, drain, 0)


def _scatter_rows(dest_flat, f, xs):
    t, d = f.shape
    ts = _pick(t, (512, 256, 128))
    return pl.pallas_call(
        _scatter_kernel,
        grid=(t // ts,),
        in_specs=[pl.BlockSpec((ts * TOP_K,), lambda i: (i,), memory_space=pltpu.SMEM),
                  pl.BlockSpec((ts, d), lambda i: (i, 0)),
                  pl.BlockSpec(memory_space=pl.ANY)],
        out_specs=pl.BlockSpec(memory_space=pl.ANY),
        out_shape=jax.ShapeDtypeStruct(xs.shape, xs.dtype),
        scratch_shapes=[pltpu.SemaphoreType.DMA(())],
        input_output_aliases={2: 0},
        compiler_params=_params(("arbitrary",)),
        name="moe_scatter",
    )(dest_flat, f, xs)


def _zero_blocks_kernel(rows_ref, xs_ref, zero_ref, sem):
    tm = zero_ref.shape[0]
    zero_ref[...] = jnp.zeros_like(zero_ref)

    def block_copy(e):
        return pltpu.make_async_copy(zero_ref, xs_ref.at[pl.ds(pl.multiple_of(rows_ref[e], tm), tm)], sem)

    for e in range(N_EXPERTS):
        block_copy(e).start()
    for e in range(N_EXPERTS):
        block_copy(e).wait()


def _zero_blocks(block_rows, n_rows_alloc, tm, d):
    grid_spec = pltpu.PrefetchScalarGridSpec(
        num_scalar_prefetch=1, grid=(1,), in_specs=[],
        out_specs=pl.BlockSpec(memory_space=pl.ANY),
        scratch_shapes=[pltpu.VMEM((tm, d), jnp.uint32), pltpu.SemaphoreType.DMA(())])
    return pl.pallas_call(
        _zero_blocks_kernel,
        grid_spec=grid_spec,
        out_shape=jax.ShapeDtypeStruct((n_rows_alloc, d), jnp.uint32),
        compiler_params=_params(("arbitrary",)),
        name="moe_zero_tail_blocks",
    )(block_rows)


def _expert_kernel(be_ref, blk_ref, nused_ref, x_ref, wg_ref, wu_ref, wd_ref, o_ref, wg_s, wu_s, wd_s):
    i = pl.program_id(0)

    @pl.when(jnp.logical_or(i == 0, be_ref[i] != be_ref[jnp.maximum(i - 1, 0)]))
    def _():
        wg_s[...] = wg_ref[0, 0].astype(BF16)
        wu_s[...] = wu_ref[0, 0].astype(BF16)
        wd_s[...] = wd_ref[0, 0].astype(BF16)

    @pl.when(i < nused_ref[0])
    def _():
        lo, hi = _unpack_bf16_pairs(x_ref[...])
        x = jnp.concatenate([lo.astype(BF16), hi.astype(BF16)], axis=1)
        hg = jnp.dot(x, wg_s[...], preferred_element_type=F32)
        hu = jnp.dot(x, wu_s[...], preferred_element_type=F32)
        h = (hg * jax.nn.sigmoid(hg) * hu).astype(BF16)
        o_ref[...] = _pack_bf16_pairs(jnp.dot(h, wd_s[...], preferred_element_type=F32))


def _expert_blocks(block_expert, block_index, n_used, xs, n_blocks, tm, layer, w_gate, w_up, w_down):
    d, ff = w_gate.shape[2:]
    grid_spec = pltpu.PrefetchScalarGridSpec(
        num_scalar_prefetch=3,
        grid=(n_blocks,),
        in_specs=[pl.BlockSpec((tm, d // 2), lambda i, be, blk, nu: (blk[i], 0)),
                  pl.BlockSpec((1, 1, d, ff), lambda i, be, blk, nu: (layer, be[i], 0, 0)),
                  pl.BlockSpec((1, 1, d, ff), lambda i, be, blk, nu: (layer, be[i], 0, 0)),
                  pl.BlockSpec((1, 1, ff, d), lambda i, be, blk, nu: (layer, be[i], 0, 0))],
        out_specs=pl.BlockSpec((tm, d // 2), lambda i, be, blk, nu: (blk[i], 0)),
        scratch_shapes=[pltpu.VMEM((d, ff), BF16), pltpu.VMEM((d, ff), BF16), pltpu.VMEM((ff, d), BF16)],
    )
    return pl.pallas_call(
        _expert_kernel,
        grid_spec=grid_spec,
        out_shape=jax.ShapeDtypeStruct((n_blocks * tm, d // 2), jnp.uint32),
        compiler_params=_params(("arbitrary",)),
        name="moe_experts",
    )(block_expert, block_index, n_used, xs, w_gate, w_up, w_down)


def _combine_kernel(dest_ref, dest_next_ref, gate_ref, sh_ref, xa_ref, mod_ref, ys_ref, o_ref, buf, sems, *,
                    tiles_per_batch, n_ctx):
    i = pl.program_id(0)
    n = pl.num_programs(0)
    tn = gate_ref.shape[0]
    slot = i % 2

    def row_copy(idx_ref, s, t, k):
        return pltpu.make_async_copy(ys_ref.at[pl.ds(idx_ref[t * TOP_K + k], 1)], buf.at[s, k, pl.ds(t, 1)],
                                     sems.at[s])

    def issue(idx_ref, s):
        def body(t, carry):
            for k in range(TOP_K):
                row_copy(idx_ref, s, t, k).start()
            return carry
        lax.fori_loop(0, tn, body, 0)

    @pl.when(i == 0)
    def _():
        issue(dest_ref, slot)

    @pl.when(i + 1 < n)
    def _():
        issue(dest_next_ref, 1 - slot)

    def drain(t, carry):
        for k in range(TOP_K):
            row_copy(dest_ref, slot, t, k).wait()
        return carry

    lax.fori_loop(0, tn, drain, 0)
    half = buf.shape[3]
    acc_lo = sh_ref[:, :half]
    acc_hi = sh_ref[:, half:]
    for k in range(TOP_K):
        lo, hi = _unpack_bf16_pairs(buf[slot, k])
        acc_lo = acc_lo + gate_ref[:, k:k + 1] * lo
        acc_hi = acc_hi + gate_ref[:, k:k + 1] * hi
    acc = jnp.concatenate([acc_lo, acc_hi], axis=1)
    is_ctx = _is_ctx_rows(tn, tiles_per_batch, n_ctx)
    o_ref[...] = xa_ref[...] + _segment_rows(mod_ref, 5, is_ctx) * acc


def _combine(dest_flat, gates, shared, ys, xa, modsel, rows_per_batch, n_ctx):
    t, d = shared.shape
    tn = 256
    tpb = rows_per_batch // tn
    n = t // tn
    return pl.pallas_call(
        functools.partial(_combine_kernel, tiles_per_batch=tpb, n_ctx=n_ctx),
        grid=(n,),
        in_specs=[pl.BlockSpec((tn * TOP_K,), lambda i: (i,), memory_space=pltpu.SMEM),
                  pl.BlockSpec((tn * TOP_K,), lambda i: (jnp.minimum(i + 1, n - 1),), memory_space=pltpu.SMEM),
                  pl.BlockSpec((tn, TOP_K), lambda i: (i, 0)),
                  pl.BlockSpec((tn, d), lambda i: (i, 0)),
                  pl.BlockSpec((tn, d), lambda i: (i, 0)),
                  pl.BlockSpec((1, 2, N_MOD, d), lambda i: (i // tpb, 0, 0, 0)),
                  pl.BlockSpec(memory_space=pl.ANY)],
        out_specs=pl.BlockSpec((tn, d), lambda i: (i, 0)),
        out_shape=jax.ShapeDtypeStruct((t, d), F32),
        scratch_shapes=[pltpu.VMEM((2, TOP_K, tn, d // 2), jnp.uint32), pltpu.SemaphoreType.DMA((2,))],
        compiler_params=_params(("arbitrary",)),
        name="moe_combine",
    )(dest_flat, dest_flat, gates, shared, xa, modsel, ys)


def _swiglu_kernel(x_ref, wg_ref, wu_ref, wd_ref, o_ref):
    x = x_ref[...].astype(BF16)
    hg = jnp.dot(x, wg_ref[...], preferred_element_type=F32)
    hu = jnp.dot(x, wu_ref[...], preferred_element_type=F32)
    h = (hg * jax.nn.sigmoid(hg) * hu).astype(BF16)
    o_ref[...] = jnp.dot(h, wd_ref[...], preferred_element_type=F32).astype(o_ref.dtype)


def _shared_expert(x, wg, wu, wd):
    m, d = x.shape
    ff = wg.shape[1]
    tm = _pick(m, (512, 384, 256, 128))
    return pl.pallas_call(
        _swiglu_kernel,
        grid=(m // tm,),
        in_specs=[pl.BlockSpec((tm, d), lambda i: (i, 0)),
                  pl.BlockSpec((d, ff), lambda i: (0, 0)),
                  pl.BlockSpec((d, ff), lambda i: (0, 0)),
                  pl.BlockSpec((ff, d), lambda i: (0, 0))],
        out_specs=pl.BlockSpec((tm, d), lambda i: (i, 0)),
        out_shape=jax.ShapeDtypeStruct((m, d), F32),
        compiler_params=_params(("parallel",)),
        name="shared_expert",
    )(x, wg, wu, wd)


def _moe_ffn(f, fp, xa, modsel, rows_per_batch, n_ctx, layer, router_w, router_bias, w_gate, w_up, w_down,
             ws_gate, ws_up, ws_down):
    t, d = f.shape
    tm = EXPERT_TM
    idx, gate, rank, cnt = _route(f, router_w.T, router_bias.astype(F32)[:, None])
    counts = cnt[:, 0]
    padded = (counts + tm - 1) // tm * tm
    pad_end = jnp.cumsum(padded)
    pad_start = pad_end - padded
    n_used = pad_end[-1] // tm
    n_blocks = (t * TOP_K) // tm + N_EXPERTS
    n_rows = n_blocks * tm
    experts = jnp.arange(N_EXPERTS, dtype=jnp.int32)
    start_of = jnp.sum(jnp.where(idx[..., None] == experts, pad_start, 0), axis=-1)
    dest = (start_of + rank).T.reshape(-1).astype(jnp.int32)
    block_index = jnp.minimum(jnp.arange(n_blocks, dtype=jnp.int32), n_used - 1).astype(jnp.int32)
    block_expert = jnp.minimum(jnp.sum(pad_end[None, :] <= (block_index * tm)[:, None], axis=1),
                               N_EXPERTS - 1).astype(jnp.int32)
    tail_rows = jnp.where(padded > counts, pad_end - tm, n_rows + experts * tm).astype(jnp.int32)
    xs = _zero_blocks(tail_rows, n_rows + N_EXPERTS * tm, tm, d // 2)
    xs = _scatter_rows(dest, fp, xs)
    ys = _expert_blocks(block_expert, block_index, n_used.reshape(1).astype(jnp.int32), xs, n_blocks, tm, layer,
                        w_gate, w_up, w_down)
    shared = _shared_expert(f, ws_gate.astype(BF16), ws_up.astype(BF16), ws_down.astype(BF16))
    return _combine(dest, gate.T, shared, ys, xa, modsel, rows_per_batch, n_ctx)


def _rms(u, gain):
    return u * lax.rsqrt(jnp.mean(u * u, axis=-1, keepdims=True) + NORM_EPS) * gain


def _rope_tables(n_ctx, n_lat):
    rows = n_lat // GRID_W
    row = jnp.repeat(jnp.arange(rows, dtype=F32), GRID_W)
    col = jnp.tile(jnp.arange(GRID_W, dtype=F32), rows)
    n_freq = HEAD_DIM // 4
    inv = ROPE_THETA ** (-jnp.arange(n_freq, dtype=F32) / n_freq)
    ang = jnp.concatenate([row[:, None] * inv, col[:, None] * inv], axis=-1)
    cos = jnp.repeat(jnp.cos(ang), 2, axis=-1)
    sin = jnp.repeat(jnp.sin(ang), 2, axis=-1)
    sign = jnp.tile(jnp.array([-1.0, 1.0], F32), HEAD_DIM // 2)
    cos = jnp.concatenate([jnp.ones((n_ctx, HEAD_DIM), F32), cos], axis=0)
    sin = jnp.concatenate([jnp.zeros((n_ctx, HEAD_DIM), F32), sin * sign], axis=0)
    return cos, sin


def _ada_mod(cond, w, bias):
    m = jax.nn.silu(cond)
    rows = m.shape[0]
    pad = (-rows) % 16
    mp = jnp.pad(m, ((0, pad), (0, 0))).astype(BF16)
    out = _matmul(mp, w.astype(BF16), F32)[:rows] + bias
    return out.reshape(rows, N_MOD, -1)


def _attention_layer(xa, modsel, gain, n_ctx, w_in, q_gain, k_gain, lq1, lk1, lq2, lk2, subln, lambda_init):
    b, t, d = xa.shape
    cos, sin = _rope_tables(n_ctx, t - n_ctx)
    head_gains = jnp.stack([q_gain * _Q_SCALE, k_gain]).astype(F32)
    qkv = _fused_proj(xa, gain, modsel, w_in.astype(BF16), n_ctx, BF16,
                      head_gains=head_gains, rope=(cos - 1.0, sin)).reshape(b, t, ATTN_IN)
    lam = (jnp.exp(jnp.sum(lq1 * lk1)) - jnp.exp(jnp.sum(lq2 * lk2)) + lambda_init).reshape(1, 1).astype(F32)
    y = _gqa_attention(qkv, n_ctx)
    return _diff_attention(qkv, y, lam, subln.reshape(1, 2 * HEAD_DIM), 1.0 - lambda_init, n_ctx)


def _ssd_layer(xa, modsel, gain, n_ctx, w_in, conv_w, conv_b, dt_bias, a_log, d_skip, norm_gain):
    b, t, d = xa.shape
    proj = _fused_proj(xa, gain, modsel, w_in.astype(BF16), n_ctx, F32).reshape(b, t, -1)
    xbc = _conv_silu(proj, SSD_INNER, conv_w, conv_b, n_ctx)
    dt = proj[..., SSD_INNER + SSD_CONV_DIM:]
    dt = jax.nn.softplus(dt.reshape(b, t, 2, SSD_GROUPS, SSD_HPG) + dt_bias.reshape(2, SSD_GROUPS, SSD_HPG))
    dt_c = jnp.transpose(dt, (2, 0, 3, 1, 4))
    dt_r = jnp.transpose(dt, (2, 0, 3, 4, 1))
    a_coef = -jnp.exp(a_log)
    a_row = a_coef.reshape(2, SSD_GROUPS, 1, SSD_HPG)
    a_col = a_coef.reshape(2, SSD_GROUPS, SSD_HPG, 1)
    skip = d_skip.reshape(2, SSD_GROUPS, SSD_HPG, 1)
    y = _ssd_scan(xbc, dt_c, dt_r, a_row, a_col, skip, n_ctx, F32)
    return _gate_norm(y, proj, norm_gain, n_ctx)


def kernel(x, c, ctx, c_ctx, mod_w, mod_b, norm_mix, norm_ffn, norm_final, attn_w_in, attn_w_out, attn_q_gain,
           attn_k_gain, diff_lam_q1, diff_lam_k1, diff_lam_q2, diff_lam_k2, diff_subln, ssd_w_in, ssd_conv_w,
           ssd_conv_b, ssd_dt_bias, ssd_a_log, ssd_d, ssd_norm, ssd_w_out, router_w, router_bias, exp_w_gate,
           exp_w_up, exp_w_down, shared_w_gate, shared_w_up, shared_w_down):
    b, s, d = x.shape
    n_ctx = ctx.shape[1]
    t = n_ctx + s
    xa = jnp.concatenate([ctx, x], axis=1)
    cond = jnp.concatenate([c, c_ctx[None]], axis=0)
    out = None
    for i in range(DEPTH):
        last = i == DEPTH - 1
        mod = _ada_mod(cond, mod_w[i], mod_b[i])
        modsel = jnp.stack([jnp.broadcast_to(mod[b], (b, N_MOD, d)), mod[:b]], axis=1)
        j = i // 2
        moe_w = (i, router_w[i], router_bias[i], exp_w_gate, exp_w_up, exp_w_down,
                 shared_w_gate[i], shared_w_up[i], shared_w_down[i])
        if i % 2 == 0:
            lambda_init = 0.8 - 0.6 * math.exp(-0.3 * i)
            y = _attention_layer(xa, modsel, norm_mix[i], n_ctx, attn_w_in[j], attn_q_gain[j], attn_k_gain[j],
                                 diff_lam_q1[j], diff_lam_k1[j], diff_lam_q2[j], diff_lam_k2[j],
                                 diff_subln[j], lambda_init)
            w_out = attn_w_out[j]
        else:
            y = _ssd_layer(xa, modsel, norm_mix[i], n_ctx, ssd_w_in[j], ssd_conv_w[j], ssd_conv_b[j],
                           ssd_dt_bias[j], ssd_a_log[j], ssd_d[j], ssd_norm[j])
            w_out = ssd_w_out[j]
        if last:
            x_new, f, fp = _fused_out_proj(y if y.shape[1] == s else y[:, n_ctx:], w_out.astype(BF16), xa, n_ctx,
                                           modsel, norm_ffn[i], 0)
            out = _moe_ffn(f.reshape(b * s, d), fp.reshape(b * s, d // 2), x_new.reshape(b * s, d), modsel, s, 0,
                           *moe_w).reshape(b, s, d)
        else:
            x_new, f, fp = _fused_out_proj(y, w_out.astype(BF16), xa, 0, modsel, norm_ffn[i], n_ctx)
            xa = _moe_ffn(f.reshape(b * t, d), fp.reshape(b * t, d // 2), x_new.reshape(b * t, d), modsel, t, n_ctx,
                          *moe_w).reshape(b, t, d)
    return _rms(out, norm_final)
```

```python
import functools
import math

import jax
import jax.numpy as jnp
from jax import lax
from jax.experimental import pallas as pl
from jax.experimental.pallas import tpu as pltpu

F32 = jnp.float32
BF16 = jnp.bfloat16

D_MODEL = 2048
DEPTH = 2
GRID_W = 64
NORM_EPS = 1e-6
N_MOD = 6
HEAD_DIM = 128
ROPE_THETA = 10000.0
A_Q_HEADS = 8
A_KV_HEADS = 2
A_GROUP = A_Q_HEADS // A_KV_HEADS
B_HEADS = 4
QA_W = A_Q_HEADS * HEAD_DIM
KA_W = A_KV_HEADS * HEAD_DIM
QB_W = 2 * B_HEADS * HEAD_DIM
ATTN_IN = QA_W + 2 * KA_W + 3 * QB_W
SSD_INNER = 2 * D_MODEL
SSD_HEADDIM = 64
SSD_HEADS = SSD_INNER // SSD_HEADDIM
SSD_GROUPS = 8
SSD_HPG = SSD_HEADS // SSD_GROUPS
SSD_STATE = 128
SSD_CONV = 3
SSD_CHUNK = 128
SSD_GROUP_W = SSD_HPG * SSD_HEADDIM
SSD_CONV_DIM = SSD_INNER + 2 * SSD_GROUPS * SSD_STATE
N_EXPERTS = 64
EXPERT_FF = 512
TOP_K = 8
N_EXPERT_GROUPS = 8
TOPK_GROUPS = 4
ROUTED_SCALE = 2.5
EXPERT_TM = 512

VMEM_LIMIT_BYTES = 56 * 1024 * 1024


def _pick(n, prefs):
    for p in prefs:
        if n % p == 0:
            return p
    raise ValueError(f"no tile in {prefs} divides {n}")


def _params(sem):
    return pltpu.CompilerParams(dimension_semantics=sem, vmem_limit_bytes=VMEM_LIMIT_BYTES)


def _mm_kernel(x_ref, w_ref, o_ref):
    o_ref[...] = jnp.dot(x_ref[...], w_ref[...], preferred_element_type=F32).astype(o_ref.dtype)


def _matmul(x, w, out_dtype):
    m, k = x.shape
    n = w.shape[1]
    tm = _pick(m, (1024, 512, 384, 256, 128, 16, 8))
    tn = _pick(n, (1024, 512, 384, 256, 128))
    return pl.pallas_call(
        _mm_kernel,
        grid=(m // tm, n // tn),
        in_specs=[pl.BlockSpec((tm, k), lambda i, j: (i, 0)),
                  pl.BlockSpec((k, tn), lambda i, j: (0, j))],
        out_specs=pl.BlockSpec((tm, tn), lambda i, j: (i, j)),
        out_shape=jax.ShapeDtypeStruct((m, n), out_dtype),
        compiler_params=_params(("parallel", "arbitrary")),
        name="matmul",
    )(x, w)


def _pack_bf16_pairs(v):
    half = v.shape[1] // 2
    lo = lax.bitcast_convert_type(v[:, :half].astype(BF16).astype(F32), jnp.uint32) >> 16
    hi = lax.bitcast_convert_type(v[:, half:].astype(BF16).astype(F32), jnp.uint32) & jnp.uint32(0xFFFF0000)
    return hi | lo


def _unpack_bf16_pairs(w):
    lo = lax.bitcast_convert_type(w << 16, F32)
    hi = lax.bitcast_convert_type(w & jnp.uint32(0xFFFF0000), F32)
    return lo, hi


def _segment_rows(mod_ref, k, is_ctx):
    return jnp.where(is_ctx, mod_ref[0, 0, k:k + 1, :], mod_ref[0, 1, k:k + 1, :])


def _is_ctx_rows(tm, tiles_per_batch, n_ctx):
    row = (pl.program_id(0) % tiles_per_batch) * tm + lax.broadcasted_iota(jnp.int32, (tm, 1), 0)
    return row < n_ctx


def _norm_mod(x, gain, shift, scale):
    xn = x * lax.rsqrt(jnp.mean(x * x, axis=-1, keepdims=True) + NORM_EPS) * gain
    return xn * (1.0 + scale) + shift


_Q_SCALE = HEAD_DIM ** -0.5
_ATTN_HEADS = ([(0, True, 1.0)] * A_Q_HEADS + [(1, True, 1.0)] * A_KV_HEADS + [(None, False, 1.0)] * A_KV_HEADS
               + [(None, True, _Q_SCALE)] * (2 * B_HEADS) + [(None, True, 1.0)] * (2 * B_HEADS)
               + [(None, False, 1.0)] * (2 * B_HEADS))
_PROJ_HEADS_PER_TILE = 4


def _head_epilogue(u, cfg, gains_ref, cm1, sn, even):
    gain_row, rope, scale = cfg
    if gain_row is not None:
        u = u * lax.rsqrt(jnp.mean(u * u, axis=-1, keepdims=True) + NORM_EPS) * gains_ref[gain_row:gain_row + 1, :]
    if scale != 1.0:
        u = u * scale
    if rope:
        partner = jnp.where(even, pltpu.roll(u, HEAD_DIM - 1, axis=1), pltpu.roll(u, 1, axis=1))
        u = u * (1.0 + cm1) + partner * sn
    return u


def _proj_kernel(x_ref, gain_ref, mod_ref, w_ref, *rest, tiles_per_batch, n_ctx, heads):
    if heads:
        gains_ref, cm1_ref, sn_ref, o_ref, h_s = rest
    else:
        o_ref, h_s = rest
    tm = x_ref.shape[0]
    is_ctx = _is_ctx_rows(tm, tiles_per_batch, n_ctx)

    @pl.when(pl.program_id(1) == 0)
    def _():
        h = _norm_mod(x_ref[...], gain_ref[...], _segment_rows(mod_ref, 0, is_ctx), _segment_rows(mod_ref, 1, is_ctx))
        h_s[...] = h.astype(BF16)

    if not heads:
        o_ref[...] = jnp.dot(h_s[...], w_ref[...], preferred_element_type=F32).astype(o_ref.dtype)
        return
    even = (lax.broadcasted_iota(jnp.int32, (tm, HEAD_DIM), 1) & 1) == 0
    hw = 2 * HEAD_DIM
    for tile in range(len(_ATTN_HEADS) // heads):
        @pl.when(pl.program_id(1) == tile)
        def _(tile=tile):
            cm1 = cm1_ref[...]
            sn = sn_ref[...]
            for half in range(heads // 2):
                acc = jnp.dot(h_s[...], w_ref[:, half * hw:(half + 1) * hw], preferred_element_type=F32)
                for hh in range(2):
                    col = half * hw + hh * HEAD_DIM
                    cfg = _ATTN_HEADS[tile * heads + half * 2 + hh]
                    out = _head_epilogue(acc[:, hh * HEAD_DIM:(hh + 1) * HEAD_DIM], cfg, gains_ref, cm1, sn, even)
                    o_ref[:, col:col + HEAD_DIM] = out.astype(o_ref.dtype)


def _fused_proj(xa, gain, modsel, w, n_ctx, out_dtype, head_gains=None, rope=None):
    b, t, d = xa.shape
    n = w.shape[1]
    tm = _pick(t, (768, 512, 384, 256))
    tpb = t // tm
    heads = 0 if head_gains is None else _PROJ_HEADS_PER_TILE
    tn = heads * HEAD_DIM if heads else _pick(n, (1152, 1024, 768, 512, 384, 256, 128))
    in_specs = [pl.BlockSpec((tm, d), lambda i, j: (i, 0)),
                pl.BlockSpec((1, d), lambda i, j: (0, 0)),
                pl.BlockSpec((1, 2, N_MOD, d), lambda i, j: (i // tpb, 0, 0, 0)),
                pl.BlockSpec((d, tn), lambda i, j: (0, j))]
    args = [xa.reshape(b * t, d), gain.reshape(1, d), modsel, w]
    if heads:
        in_specs += [pl.BlockSpec((2, HEAD_DIM), lambda i, j: (0, 0)),
                     pl.BlockSpec((tm, HEAD_DIM), lambda i, j: (i % tpb, 0)),
                     pl.BlockSpec((tm, HEAD_DIM), lambda i, j: (i % tpb, 0))]
        args += [head_gains, rope[0], rope[1]]
    return pl.pallas_call(
        functools.partial(_proj_kernel, tiles_per_batch=tpb, n_ctx=n_ctx, heads=heads),
        grid=(b * tpb, n // tn),
        in_specs=in_specs,
        out_specs=pl.BlockSpec((tm, tn), lambda i, j: (i, j)),
        out_shape=jax.ShapeDtypeStruct((b * t, n), out_dtype),
        scratch_shapes=[pltpu.VMEM((tm, d), BF16)],
        compiler_params=_params(("parallel", "arbitrary")),
        name="norm_mod_proj",
    )(*args)


def _out_proj_kernel(y_ref, w_ref, xa_ref, mod_ref, gain_ref, xo_ref, f_ref, fp_ref, *, tiles_per_batch, n_ctx):
    tm = y_ref.shape[1]
    is_ctx = _is_ctx_rows(tm, tiles_per_batch, n_ctx)
    acc = jnp.dot(y_ref[0], w_ref[...], preferred_element_type=F32)
    x_new = xa_ref[0] + _segment_rows(mod_ref, 2, is_ctx) * acc
    xo_ref[0] = x_new
    f = _norm_mod(x_new, gain_ref[...], _segment_rows(mod_ref, 3, is_ctx), _segment_rows(mod_ref, 4, is_ctx))
    f_ref[0] = f
    fp_ref[0] = _pack_bf16_pairs(f)


def _fused_out_proj(y, w, xa, xa_row0, modsel, gain, n_ctx):
    b, tq, k = y.shape
    d = w.shape[1]
    tm = _pick(tq, (384, 256)) if n_ctx else _pick(tq, (256, 128))
    assert xa_row0 % tm == 0
    r0 = xa_row0 // tm
    tpb = tq // tm
    tile = pl.BlockSpec((1, tm, d), lambda i: (i // tpb, i % tpb, 0))
    return pl.pallas_call(
        functools.partial(_out_proj_kernel, tiles_per_batch=tpb, n_ctx=n_ctx),
        grid=(b * tpb,),
        in_specs=[pl.BlockSpec((1, tm, k), lambda i: (i // tpb, i % tpb, 0)),
                  pl.BlockSpec((k, d), lambda i: (0, 0)),
                  pl.BlockSpec((1, tm, d), lambda i: (i // tpb, r0 + i % tpb, 0)),
                  pl.BlockSpec((1, 2, N_MOD, d), lambda i: (i // tpb, 0, 0, 0)),
                  pl.BlockSpec((1, d), lambda i: (0, 0))],
        out_specs=[tile, tile, pl.BlockSpec((1, tm, d // 2), lambda i: (i // tpb, i % tpb, 0))],
        out_shape=[jax.ShapeDtypeStruct((b, tq, d), F32), jax.ShapeDtypeStruct((b, tq, d), F32),
                   jax.ShapeDtypeStruct((b, tq, d // 2), jnp.uint32)],
        compiler_params=_params(("parallel",)),
        name="out_proj_residual_norm",
    )(y, w, xa, modsel, gain.reshape(1, d))


def _softmax_rows(s):
    m = jnp.max(s, axis=-1, keepdims=True)
    p = jnp.exp(s - m)
    return p, jnp.sum(p, axis=-1, keepdims=True)


def _gqa_heads(q_ref, k, v, o_ref):
    for g in range(A_GROUP):
        q = q_ref[0, :, g * HEAD_DIM:(g + 1) * HEAD_DIM]
        s = lax.dot_general(q, k, (((1,), (1,)), ((), ())), preferred_element_type=F32)
        p, l = _softmax_rows(s)
        o = jnp.dot(p.astype(BF16), v, preferred_element_type=F32) / l
        o_ref[0, :, g * HEAD_DIM:(g + 1) * HEAD_DIM] = o.astype(o_ref.dtype)


def _gqa_kernel(q_ref, k_ref, v_ref, o_ref, *, n_ctx):
    @pl.when(pl.program_id(2) == 0)
    def _():
        _gqa_heads(q_ref, k_ref[0, :n_ctx], v_ref[0, :n_ctx], o_ref)

    @pl.when(pl.program_id(2) > 0)
    def _():
        _gqa_heads(q_ref, k_ref[0], v_ref[0], o_ref)


def _gqa_attention(qkv, n_ctx):
    b, t, _ = qkv.shape
    tq = n_ctx
    gw = A_GROUP * HEAD_DIM
    k0 = QA_W // HEAD_DIM
    v0 = (QA_W + KA_W) // HEAD_DIM
    return pl.pallas_call(
        functools.partial(_gqa_kernel, n_ctx=n_ctx),
        grid=(b, A_KV_HEADS, t // tq),
        in_specs=[pl.BlockSpec((1, tq, gw), lambda bi, h, i: (bi, i, h)),
                  pl.BlockSpec((1, t, HEAD_DIM), lambda bi, h, i: (bi, 0, k0 + h)),
                  pl.BlockSpec((1, t, HEAD_DIM), lambda bi, h, i: (bi, 0, v0 + h))],
        out_specs=pl.BlockSpec((1, tq, gw), lambda bi, h, i: (bi, i, h)),
        out_shape=jax.ShapeDtypeStruct((b, t, QA_W + QB_W), BF16),
        compiler_params=_params(("parallel", "parallel", "arbitrary")),
        name="gqa_attention",
    )(qkv, qkv, qkv)


def _diff_head(lam, gain_ref, q_ref, k, v, o_ref, out_scale):
    parts = []
    for m in range(2):
        q = q_ref[0, :, m * HEAD_DIM:(m + 1) * HEAD_DIM]
        s = lax.dot_general(q, k[:, m * HEAD_DIM:(m + 1) * HEAD_DIM], (((1,), (1,)), ((), ())),
                            preferred_element_type=F32)
        p, l = _softmax_rows(s)
        parts.append(p / l)
    a = parts[0] - lam * parts[1]
    y = jnp.dot(a.astype(BF16), v, preferred_element_type=F32)
    y = y * lax.rsqrt(jnp.mean(y * y, axis=-1, keepdims=True) + NORM_EPS)
    o_ref[0] = (y * gain_ref[...] * out_scale).astype(o_ref.dtype)


def _diff_kernel(lam_ref, gain_ref, q_ref, k_ref, v_ref, y_in_ref, o_ref, *, out_scale, n_ctx):
    del y_in_ref
    lam = lam_ref[0, 0]

    @pl.when(pl.program_id(2) == 0)
    def _():
        _diff_head(lam, gain_ref, q_ref, k_ref[0, :n_ctx], v_ref[0, :n_ctx], o_ref, out_scale)

    @pl.when(pl.program_id(2) > 0)
    def _():
        _diff_head(lam, gain_ref, q_ref, k_ref[0], v_ref[0], o_ref, out_scale)


def _diff_attention(qkv, y, lam, subln_gain, out_scale, n_ctx):
    b, t, _ = qkv.shape
    tq = n_ctx
    hw = 2 * HEAD_DIM
    q0 = (QA_W + 2 * KA_W) // hw
    k0 = q0 + B_HEADS
    v0 = k0 + B_HEADS
    o0 = QA_W // hw
    return pl.pallas_call(
        functools.partial(_diff_kernel, out_scale=out_scale, n_ctx=n_ctx),
        grid=(b, B_HEADS, t // tq),
        in_specs=[pl.BlockSpec(memory_space=pltpu.SMEM),
                  pl.BlockSpec((1, hw), lambda bi, h, i: (0, 0)),
                  pl.BlockSpec((1, tq, hw), lambda bi, h, i: (bi, i, q0 + h)),
                  pl.BlockSpec((1, t, hw), lambda bi, h, i: (bi, 0, k0 + h)),
                  pl.BlockSpec((1, t, hw), lambda bi, h, i: (bi, 0, v0 + h)),
                  pl.BlockSpec(memory_space=pl.ANY)],
        out_specs=pl.BlockSpec((1, tq, hw), lambda bi, h, i: (bi, i, o0 + h)),
        out_shape=jax.ShapeDtypeStruct(y.shape, y.dtype),
        input_output_aliases={5: 0},
        compiler_params=_params(("parallel", "parallel", "arbitrary")),
        name="diff_attention",
    )(lam, subln_gain, qkv, qkv, qkv, y)


def _split3(a):
    a1 = a.astype(BF16)
    r1 = a - a1.astype(F32)
    a2 = r1.astype(BF16)
    a3 = (r1 - a2.astype(F32)).astype(BF16)
    return a1, a2, a3


def _dot_exact_rhs(a, rhs01):
    out = None
    for part in _split3(a):
        d = jnp.dot(part, rhs01, preferred_element_type=F32)
        out = d if out is None else out + d
    return out


def _dot_exact_lhs(lhs01, a):
    out = None
    for part in _split3(a):
        d = jnp.dot(lhs01, part, preferred_element_type=F32)
        out = d if out is None else out + d
    return out


SSD_GROUPS_PER_STEP = 8


def _ssd_group(d_sign, acr, acc, skip, x_tok, b_tok, c_tok, dt_c, dt_r, state):
    q = SSD_CHUNK
    rows = lax.broadcasted_iota(jnp.int32, (q, q), 0)
    cols = lax.broadcasted_iota(jnp.int32, (q, q), 1)
    signed = (rows - cols) * d_sign
    keep_sl = signed <= 0
    tri_ks01 = jnp.where(keep_sl, 1.0, 0.0).astype(BF16)
    tri_sk01 = jnp.where(signed >= 0, 1.0, 0.0).astype(BF16)

    x = jnp.transpose(x_tok)
    bm = b_tok.astype(BF16)
    cm = c_tok.astype(BF16)
    a_r = dt_r * acc
    a_c = dt_c * acr
    cum_r = _dot_exact_rhs(a_r, tri_ks01)
    cum_c = _dot_exact_lhs(tri_sk01, a_c)
    tot = jnp.sum(a_r, axis=1, keepdims=True)
    to_end_r = jnp.exp(tot - cum_r)
    from_start_r = jnp.exp(cum_r)
    tot_e = jnp.exp(tot)

    g_sl = lax.dot_general(bm, cm, (((1,), (1,)), ((), ())), preferred_element_type=F32)
    y_off = lax.dot_general(state.astype(BF16), cm, (((1,), (1,)), ((), ())), preferred_element_type=F32)

    xw_parts, y_parts = [], []
    for h in range(SSD_HPG):
        sl = slice(h * SSD_HEADDIM, (h + 1) * SSD_HEADDIM)
        xh = x[sl, :]
        xd = xh * dt_r[h:h + 1, :]
        seg = cum_r[h:h + 1, :] - cum_c[:, h:h + 1]
        decay = jnp.where(keep_sl, jnp.exp(jnp.where(keep_sl, seg, 0.0)), 0.0)
        m_h = (g_sl * decay).astype(BF16)
        y_h = jnp.dot(xd.astype(BF16), m_h, preferred_element_type=F32)
        y_parts.append(y_h + y_off[sl, :] * from_start_r[h:h + 1, :] + skip[h:h + 1, :] * xh)
        xw_parts.append((xd * to_end_r[h:h + 1, :]).astype(BF16))
    s_new = jnp.dot(jnp.concatenate(xw_parts, axis=0), bm, preferred_element_type=F32)
    decay_rows = jnp.concatenate([jnp.broadcast_to(tot_e[h:h + 1, :], (SSD_HEADDIM, 1)) for h in range(SSD_HPG)], axis=0)
    return jnp.transpose(jnp.concatenate(y_parts, axis=0)), state * decay_rows + s_new


def _ssd_kernel(acr_ref, acc_ref, skip_ref, x_ref, b_ref, c_ref, dtc_ref, dtr_ref, y_ref, state_ref):
    @pl.when(pl.program_id(3) == 0)
    def _():
        state_ref[...] = jnp.zeros_like(state_ref)

    d_sign = 1 - 2 * pl.program_id(1)
    for g in range(SSD_GROUPS_PER_STEP):
        xs = slice(g * SSD_GROUP_W, (g + 1) * SSD_GROUP_W)
        ns = slice(g * SSD_STATE, (g + 1) * SSD_STATE)
        y, state = _ssd_group(d_sign, acr_ref[0, g], acc_ref[0, g], skip_ref[0, g], x_ref[0, :, xs], b_ref[0, :, ns],
                              c_ref[0, :, ns], dtc_ref[0, 0, g], dtr_ref[0, 0, g], state_ref[g])
        y_ref[0, 0, :, xs] = y.astype(y_ref.dtype)
        state_ref[g] = state


def _ssd_scan(xbc, dt_c, dt_r, a_row, a_col, skip, n_ctx, out_dtype):
    b, t, _ = xbc.shape
    gs = SSD_GROUPS_PER_STEP
    b0 = SSD_INNER // (gs * SSD_STATE)
    c0 = b0 + SSD_GROUPS // gs
    q = SSD_CHUNK
    ncc = n_ctx // q
    nch = t // q

    def chunk(d, s):
        back = jnp.where(s < ncc, ncc - 1 - s, nch - 1 - s + ncc)
        return jnp.where(d == 0, s, back)

    return pl.pallas_call(
        _ssd_kernel,
        grid=(b, 2, SSD_GROUPS // gs, nch),
        in_specs=[pl.BlockSpec((1, gs, 1, SSD_HPG), lambda bi, d, g, s: (d, g, 0, 0)),
                  pl.BlockSpec((1, gs, SSD_HPG, 1), lambda bi, d, g, s: (d, g, 0, 0)),
                  pl.BlockSpec((1, gs, SSD_HPG, 1), lambda bi, d, g, s: (d, g, 0, 0)),
                  pl.BlockSpec((1, q, gs * SSD_GROUP_W), lambda bi, d, g, s: (bi, chunk(d, s), g)),
                  pl.BlockSpec((1, q, gs * SSD_STATE), lambda bi, d, g, s: (bi, chunk(d, s), b0 + g)),
                  pl.BlockSpec((1, q, gs * SSD_STATE), lambda bi, d, g, s: (bi, chunk(d, s), c0 + g)),
                  pl.BlockSpec((1, 1, gs, q, SSD_HPG), lambda bi, d, g, s: (d, bi, g, chunk(d, s), 0)),
                  pl.BlockSpec((1, 1, gs, SSD_HPG, q), lambda bi, d, g, s: (d, bi, g, 0, chunk(d, s)))],
        out_specs=pl.BlockSpec((1, 1, q, gs * SSD_GROUP_W), lambda bi, d, g, s: (d, bi, chunk(d, s), g)),
        out_shape=jax.ShapeDtypeStruct((2, b, t, SSD_INNER), out_dtype),
        scratch_shapes=[pltpu.VMEM((gs, SSD_GROUP_W, SSD_STATE), F32)],
        compiler_params=_params(("parallel", "parallel", "parallel", "arbitrary")),
        name="ssd_scan",
    )(a_row, a_col, skip, xbc, xbc, xbc, dt_c, dt_r)


def _conv_silu_kernel(x_ref, w_ref, b_ref, o_ref, *, n_ctx):
    x = x_ref[0]
    t = x.shape[0]
    row = lax.broadcasted_iota(jnp.int32, (t, 1), 0)
    prev = jnp.where((row == 0) | (row == n_ctx), 0.0, pltpu.roll(x, 1, axis=0))
    nxt = jnp.where((row == n_ctx - 1) | (row == t - 1), 0.0, pltpu.roll(x, t - 1, axis=0))
    u = prev * w_ref[0:1, :] + x * w_ref[1:2, :] + nxt * w_ref[2:3, :] + b_ref[...]
    o_ref[0] = u * jax.nn.sigmoid(u)


def _conv_silu(proj, col0, conv_w, conv_b, n_ctx):
    b, t, _ = proj.shape
    tc = 512
    c0 = col0 // tc
    return pl.pallas_call(
        functools.partial(_conv_silu_kernel, n_ctx=n_ctx),
        grid=(b, SSD_CONV_DIM // tc),
        in_specs=[pl.BlockSpec((1, t, tc), lambda bi, j: (bi, 0, c0 + j)),
                  pl.BlockSpec((SSD_CONV, tc), lambda bi, j: (0, j)),
                  pl.BlockSpec((1, tc), lambda bi, j: (0, j))],
        out_specs=pl.BlockSpec((1, t, tc), lambda bi, j: (bi, 0, j)),
        out_shape=jax.ShapeDtypeStruct((b, t, SSD_CONV_DIM), F32),
        compiler_params=_params(("parallel", "parallel")),
        name="ssd_conv_silu",
    )(proj, conv_w, conv_b.reshape(1, -1))


def _gate_norm_kernel(y_ref, z_ref, gain_ref, o_ref):
    z = z_ref[0]
    g = (y_ref[0, 0] + y_ref[1, 0]) * (z * jax.nn.sigmoid(z))
    o_ref[0] = (g * lax.rsqrt(jnp.mean(g * g, axis=-1, keepdims=True) + NORM_EPS) * gain_ref[...]).astype(o_ref.dtype)


def _gate_norm(y, proj, gain, n_ctx):
    _, b, t, c = y.shape
    tt = 256
    assert n_ctx % tt == 0
    r0 = n_ctx // tt
    s = t - n_ctx
    return pl.pallas_call(
        _gate_norm_kernel,
        grid=(b, s // tt),
        in_specs=[pl.BlockSpec((2, 1, tt, c), lambda bi, i: (0, bi, r0 + i, 0)),
                  pl.BlockSpec((1, tt, c), lambda bi, i: (bi, r0 + i, 0)),
                  pl.BlockSpec((1, c), lambda bi, i: (0, 0))],
        out_specs=pl.BlockSpec((1, tt, c), lambda bi, i: (bi, i, 0)),
        out_shape=jax.ShapeDtypeStruct((b, s, c), BF16),
        compiler_params=_params(("parallel", "parallel")),
        name="ssd_gate_norm",
    )(y, proj, gain.reshape(1, c))


def _route_kernel(f_ref, rwt_ref, bias_ref, idx_ref, gate_ref, rank_ref, cnt_ref, carry_ref):
    tn = f_ref.shape[0]
    ne, ng, pg = N_EXPERTS, N_EXPERT_GROUPS, N_EXPERTS // N_EXPERT_GROUPS
    neg = -jnp.inf

    @pl.when(pl.program_id(0) == 0)
    def _():
        carry_ref[...] = jnp.zeros_like(carry_ref)

    logits = lax.dot_general(rwt_ref[...], f_ref[...], (((1,), (1,)), ((), ())),
                             preferred_element_type=F32, precision=lax.Precision.HIGHEST)
    scores = jax.nn.sigmoid(logits)
    g3 = (scores + bias_ref[...]).reshape(ng, pg, tn)
    io3 = lax.broadcasted_iota(jnp.int32, (ng, pg, tn), 1)
    m1 = jnp.max(g3, axis=1, keepdims=True)
    i1 = jnp.min(jnp.where(g3 == m1, io3, pg), axis=1, keepdims=True)
    m2 = jnp.max(jnp.where(io3 == i1, neg, g3), axis=1, keepdims=True)
    work = (m1 + m2).reshape(ng, tn)
    iog = lax.broadcasted_iota(jnp.int32, (ng, tn), 0)
    ok = jnp.zeros((ng, tn), F32)
    for _ in range(TOPK_GROUPS):
        m = jnp.max(work, axis=0, keepdims=True)
        gi = jnp.min(jnp.where(work == m, iog, ng), axis=0, keepdims=True)
        hit = iog == gi
        ok = jnp.where(hit, 1.0, ok)
        work = jnp.where(hit, neg, work)
    sel = jnp.where(ok.reshape(ng, 1, tn) > 0.0, g3, neg).reshape(ne, tn)
    ioe = lax.broadcasted_iota(jnp.int32, (ne, tn), 0)
    onehot = jnp.zeros((ne, tn), F32)
    idxs, ws = [], []
    for _ in range(TOP_K):
        m = jnp.max(sel, axis=0, keepdims=True)
        ei = jnp.min(jnp.where(sel == m, ioe, ne), axis=0, keepdims=True)
        hit = ioe == ei
        idxs.append(ei)
        ws.append(jnp.sum(jnp.where(hit, scores, 0.0), axis=0, keepdims=True))
        sel = jnp.where(hit, neg, sel)
        onehot = jnp.where(hit, 1.0, onehot)
    w = jnp.concatenate(ws, axis=0)
    gate_ref[...] = w / jnp.sum(w, axis=0, keepdims=True) * ROUTED_SCALE
    idx_ref[...] = jnp.concatenate(idxs, axis=0)
    r = lax.broadcasted_iota(jnp.int32, (tn, tn), 0)
    c = lax.broadcasted_iota(jnp.int32, (tn, tn), 1)
    ahead = jnp.where(r < c, 1.0, 0.0).astype(BF16)
    cum = carry_ref[...] + jnp.dot(onehot.astype(BF16), ahead, preferred_element_type=F32)
    ranks = [jnp.sum(jnp.where(ioe == idxs[k], cum, 0.0), axis=0, keepdims=True) for k in range(TOP_K)]
    rank_ref[...] = jnp.concatenate(ranks, axis=0).astype(jnp.int32)
    total = carry_ref[...] + jnp.sum(onehot, axis=1, keepdims=True)
    carry_ref[...] = total
    cnt_ref[...] = total.astype(jnp.int32)


def _route(f, router_wt, router_bias):
    t, d = f.shape
    tn = _pick(t, (512, 256, 128))
    kt = pl.BlockSpec((TOP_K, tn), lambda i: (0, i))
    return pl.pallas_call(
        _route_kernel,
        grid=(t // tn,),
        in_specs=[pl.BlockSpec((tn, d), lambda i: (i, 0)),
                  pl.BlockSpec((N_EXPERTS, d), lambda i: (0, 0)),
                  pl.BlockSpec((N_EXPERTS, 1), lambda i: (0, 0))],
        out_specs=[kt, kt, kt, pl.BlockSpec((N_EXPERTS, 1), lambda i: (0, 0))],
        out_shape=[jax.ShapeDtypeStruct((TOP_K, t), jnp.int32), jax.ShapeDtypeStruct((TOP_K, t), F32),
                   jax.ShapeDtypeStruct((TOP_K, t), jnp.int32), jax.ShapeDtypeStruct((N_EXPERTS, 1), jnp.int32)],
        scratch_shapes=[pltpu.VMEM((N_EXPERTS, 1), F32)],
        compiler_params=_params(("arbitrary",)),
        name="moe_route",
    )(f, router_wt, router_bias)


def _scatter_kernel(dest_ref, f_ref, xs_in_ref, xs_ref, sem):
    del xs_in_ref
    ts = f_ref.shape[0]

    def row_copy(t, k):
        return pltpu.make_async_copy(f_ref.at[pl.ds(t, 1)], xs_ref.at[pl.ds(dest_ref[t * TOP_K + k], 1)], sem)

    def issue(t, carry):
        for k in range(TOP_K):
            row_copy(t, k).start(priority=k % 2)
        return carry

    def drain(t, carry):
        for k in range(TOP_K):
            row_copy(t, k).wait()
        return carry

    lax.fori_loop(0, ts, issue, 0)
    lax.fori_loop(0, ts, drain, 0)


def _scatter_rows(dest_flat, f, xs):
    t, d = f.shape
    ts = _pick(t, (512, 256, 128))
    return pl.pallas_call(
        _scatter_kernel,
        grid=(t // ts,),
        in_specs=[pl.BlockSpec((ts * TOP_K,), lambda i: (i,), memory_space=pltpu.SMEM),
                  pl.BlockSpec((ts, d), lambda i: (i, 0)),
                  pl.BlockSpec(memory_space=pl.ANY)],
        out_specs=pl.BlockSpec(memory_space=pl.ANY),
        out_shape=jax.ShapeDtypeStruct(xs.shape, xs.dtype),
        scratch_shapes=[pltpu.SemaphoreType.DMA(())],
        input_output_aliases={2: 0},
        compiler_params=_params(("arbitrary",)),
        name="moe_scatter",
    )(dest_flat, f, xs)


def _zero_blocks_kernel(rows_ref, xs_ref, zero_ref, sem):
    tm = zero_ref.shape[0]
    zero_ref[...] = jnp.zeros_like(zero_ref)

    def block_copy(e):
        return pltpu.make_async_copy(zero_ref, xs_ref.at[pl.ds(pl.multiple_of(rows_ref[e], tm), tm)], sem)

    for e in range(N_EXPERTS):
        block_copy(e).start()
    for e in range(N_EXPERTS):
        block_copy(e).wait()


def _zero_blocks(block_rows, n_rows_alloc, tm, d):
    grid_spec = pltpu.PrefetchScalarGridSpec(
        num_scalar_prefetch=1, grid=(1,), in_specs=[],
        out_specs=pl.BlockSpec(memory_space=pl.ANY),
        scratch_shapes=[pltpu.VMEM((tm, d), jnp.uint32), pltpu.SemaphoreType.DMA(())])
    return pl.pallas_call(
        _zero_blocks_kernel,
        grid_spec=grid_spec,
        out_shape=jax.ShapeDtypeStruct((n_rows_alloc, d), jnp.uint32),
        compiler_params=_params(("arbitrary",)),
        name="moe_zero_tail_blocks",
    )(block_rows)


def _expert_kernel(be_ref, blk_ref, nused_ref, x_ref, wg_ref, wu_ref, wd_ref, o_ref, wg_s, wu_s, wd_s):
    i = pl.program_id(0)

    @pl.when(jnp.logical_or(i == 0, be_ref[i] != be_ref[jnp.maximum(i - 1, 0)]))
    def _():
        wg_s[...] = wg_ref[0, 0].astype(BF16)
        wu_s[...] = wu_ref[0, 0].astype(BF16)
        wd_s[...] = wd_ref[0, 0].astype(BF16)

    @pl.when(i < nused_ref[0])
    def _():
        lo, hi = _unpack_bf16_pairs(x_ref[...])
        x = jnp.concatenate([lo.astype(BF16), hi.astype(BF16)], axis=1)
        hg = jnp.dot(x, wg_s[...], preferred_element_type=F32)
        hu = jnp.dot(x, wu_s[...], preferred_element_type=F32)
        h = (hg * jax.nn.sigmoid(hg) * hu).astype(BF16)
        o_ref[...] = _pack_bf16_pairs(jnp.dot(h, wd_s[...], preferred_element_type=F32))


def _expert_blocks(block_expert, block_index, n_used, xs, n_blocks, tm, layer, w_gate, w_up, w_down):
    d, ff = w_gate.shape[2:]
    grid_spec = pltpu.PrefetchScalarGridSpec(
        num_scalar_prefetch=3,
        grid=(n_blocks,),
        in_specs=[pl.BlockSpec((tm, d // 2), lambda i, be, blk, nu: (blk[i], 0)),
                  pl.BlockSpec((1, 1, d, ff), lambda i, be, blk, nu: (layer, be[i], 0, 0)),
                  pl.BlockSpec((1, 1, d, ff), lambda i, be, blk, nu: (layer, be[i], 0, 0)),
                  pl.BlockSpec((1, 1, ff, d), lambda i, be, blk, nu: (layer, be[i], 0, 0))],
        out_specs=pl.BlockSpec((tm, d // 2), lambda i, be, blk, nu: (blk[i], 0)),
        scratch_shapes=[pltpu.VMEM((d, ff), BF16), pltpu.VMEM((d, ff), BF16), pltpu.VMEM((ff, d), BF16)],
    )
    return pl.pallas_call(
        _expert_kernel,
        grid_spec=grid_spec,
        out_shape=jax.ShapeDtypeStruct((n_blocks * tm, d // 2), jnp.uint32),
        compiler_params=_params(("arbitrary",)),
        name="moe_experts",
    )(block_expert, block_index, n_used, xs, w_gate, w_up, w_down)


def _combine_kernel(dest_ref, gate_ref, sh_ref, xa_ref, mod_ref, ys_ref, o_ref, buf, sem, *, tiles_per_batch, n_ctx):
    tn = gate_ref.shape[0]

    def row_copy(t, k):
        return pltpu.make_async_copy(ys_ref.at[pl.ds(dest_ref[t * TOP_K + k], 1)], buf.at[k, pl.ds(t, 1)], sem)

    def issue(t, carry):
        for k in range(TOP_K):
            row_copy(t, k).start(priority=k % 2)
        return carry

    def drain(t, carry):
        for k in range(TOP_K):
            row_copy(t, k).wait()
        return carry

    lax.fori_loop(0, tn, issue, 0)
    lax.fori_loop(0, tn, drain, 0)
    half = buf.shape[2]
    acc_lo = sh_ref[:, :half]
    acc_hi = sh_ref[:, half:]
    for k in range(TOP_K):
        lo, hi = _unpack_bf16_pairs(buf[k])
        acc_lo = acc_lo + gate_ref[:, k:k + 1] * lo
        acc_hi = acc_hi + gate_ref[:, k:k + 1] * hi
    acc = jnp.concatenate([acc_lo, acc_hi], axis=1)
    is_ctx = _is_ctx_rows(tn, tiles_per_batch, n_ctx)
    o_ref[...] = xa_ref[...] + _segment_rows(mod_ref, 5, is_ctx) * acc


def _combine(dest_flat, gates, shared, ys, xa, modsel, rows_per_batch, n_ctx):
    t, d = shared.shape
    tn = 256
    tpb = rows_per_batch // tn
    return pl.pallas_call(
        functools.partial(_combine_kernel, tiles_per_batch=tpb, n_ctx=n_ctx),
        grid=(t // tn,),
        in_specs=[pl.BlockSpec((tn * TOP_K,), lambda i: (i,), memory_space=pltpu.SMEM),
                  pl.BlockSpec((tn, TOP_K), lambda i: (i, 0)),
                  pl.BlockSpec((tn, d), lambda i: (i, 0)),
                  pl.BlockSpec((tn, d), lambda i: (i, 0)),
                  pl.BlockSpec((1, 2, N_MOD, d), lambda i: (i // tpb, 0, 0, 0)),
                  pl.BlockSpec(memory_space=pl.ANY)],
        out_specs=pl.BlockSpec((tn, d), lambda i: (i, 0)),
        out_shape=jax.ShapeDtypeStruct((t, d), F32),
        scratch_shapes=[pltpu.VMEM((TOP_K, tn, d // 2), jnp.uint32), pltpu.SemaphoreType.DMA(())],
        compiler_params=_params(("arbitrary",)),
        name="moe_combine",
    )(dest_flat, gates, shared, xa, modsel, ys)


def _swiglu_kernel(x_ref, wg_ref, wu_ref, wd_ref, o_ref):
    x = x_ref[...].astype(BF16)
    hg = jnp.dot(x, wg_ref[...], preferred_element_type=F32)
    hu = jnp.dot(x, wu_ref[...], preferred_element_type=F32)
    h = (hg * jax.nn.sigmoid(hg) * hu).astype(BF16)
    o_ref[...] = jnp.dot(h, wd_ref[...], preferred_element_type=F32).astype(o_ref.dtype)


def _shared_expert(x, wg, wu, wd):
    m, d = x.shape
    ff = wg.shape[1]
    tm = _pick(m, (512, 384, 256, 128))
    return pl.pallas_call(
        _swiglu_kernel,
        grid=(m // tm,),
        in_specs=[pl.BlockSpec((tm, d), lambda i: (i, 0)),
                  pl.BlockSpec((d, ff), lambda i: (0, 0)),
                  pl.BlockSpec((d, ff), lambda i: (0, 0)),
                  pl.BlockSpec((ff, d), lambda i: (0, 0))],
        out_specs=pl.BlockSpec((tm, d), lambda i: (i, 0)),
        out_shape=jax.ShapeDtypeStruct((m, d), F32),
        compiler_params=_params(("parallel",)),
        name="shared_expert",
    )(x, wg, wu, wd)


def _moe_ffn(f, fp, xa, modsel, rows_per_batch, n_ctx, layer, router_w, router_bias, w_gate, w_up, w_down,
             ws_gate, ws_up, ws_down):
    t, d = f.shape
    tm = EXPERT_TM
    idx, gate, rank, cnt = _route(f, router_w.T, router_bias.astype(F32)[:, None])
    counts = cnt[:, 0]
    padded = (counts + tm - 1) // tm * tm
    pad_end = jnp.cumsum(padded)
    pad_start = pad_end - padded
    n_used = pad_end[-1] // tm
    n_blocks = (t * TOP_K) // tm + N_EXPERTS
    n_rows = n_blocks * tm
    experts = jnp.arange(N_EXPERTS, dtype=jnp.int32)
    start_of = jnp.sum(jnp.where(idx[..., None] == experts, pad_start, 0), axis=-1)
    dest = (start_of + rank).T.reshape(-1).astype(jnp.int32)
    block_index = jnp.minimum(jnp.arange(n_blocks, dtype=jnp.int32), n_used - 1).astype(jnp.int32)
    block_expert = jnp.minimum(jnp.sum(pad_end[None, :] <= (block_index * tm)[:, None], axis=1),
                               N_EXPERTS - 1).astype(jnp.int32)
    tail_rows = jnp.where(padded > counts, pad_end - tm, n_rows + experts * tm).astype(jnp.int32)
    xs = _zero_blocks(tail_rows, n_rows + N_EXPERTS * tm, tm, d // 2)
    xs = _scatter_rows(dest, fp, xs)
    ys = _expert_blocks(block_expert, block_index, n_used.reshape(1).astype(jnp.int32), xs, n_blocks, tm, layer,
                        w_gate, w_up, w_down)
    shared = _shared_expert(f, ws_gate.astype(BF16), ws_up.astype(BF16), ws_down.astype(BF16))
    return _combine(dest, gate.T, shared, ys, xa, modsel, rows_per_batch, n_ctx)


def _rms(u, gain):
    return u * lax.rsqrt(jnp.mean(u * u, axis=-1, keepdims=True) + NORM_EPS) * gain


def _rope_tables(n_ctx, n_lat):
    rows = n_lat // GRID_W
    row = jnp.repeat(jnp.arange(rows, dtype=F32), GRID_W)
    col = jnp.tile(jnp.arange(GRID_W, dtype=F32), rows)
    n_freq = HEAD_DIM // 4
    inv = ROPE_THETA ** (-jnp.arange(n_freq, dtype=F32) / n_freq)
    ang = jnp.concatenate([row[:, None] * inv, col[:, None] * inv], axis=-1)
    cos = jnp.repeat(jnp.cos(ang), 2, axis=-1)
    sin = jnp.repeat(jnp.sin(ang), 2, axis=-1)
    sign = jnp.tile(jnp.array([-1.0, 1.0], F32), HEAD_DIM // 2)
    cos = jnp.concatenate([jnp.ones((n_ctx, HEAD_DIM), F32), cos], axis=0)
    sin = jnp.concatenate([jnp.zeros((n_ctx, HEAD_DIM), F32), sin * sign], axis=0)
    return cos, sin


def _ada_mod(cond, w, bias):
    m = jax.nn.silu(cond)
    rows = m.shape[0]
    pad = (-rows) % 16
    mp = jnp.pad(m, ((0, pad), (0, 0))).astype(BF16)
    out = _matmul(mp, w.astype(BF16), F32)[:rows] + bias
    return out.reshape(rows, N_MOD, -1)


def _attention_layer(xa, modsel, gain, n_ctx, w_in, q_gain, k_gain, lq1, lk1, lq2, lk2, subln, lambda_init):
    b, t, d = xa.shape
    cos, sin = _rope_tables(n_ctx, t - n_ctx)
    head_gains = jnp.stack([q_gain * _Q_SCALE, k_gain]).astype(F32)
    qkv = _fused_proj(xa, gain, modsel, w_in.astype(BF16), n_ctx, BF16,
                      head_gains=head_gains, rope=(cos - 1.0, sin)).reshape(b, t, ATTN_IN)
    lam = (jnp.exp(jnp.sum(lq1 * lk1)) - jnp.exp(jnp.sum(lq2 * lk2)) + lambda_init).reshape(1, 1).astype(F32)
    y = _gqa_attention(qkv, n_ctx)
    return _diff_attention(qkv, y, lam, subln.reshape(1, 2 * HEAD_DIM), 1.0 - lambda_init, n_ctx)


def _ssd_layer(xa, modsel, gain, n_ctx, w_in, conv_w, conv_b, dt_bias, a_log, d_skip, norm_gain):
    b, t, d = xa.shape
    proj = _fused_proj(xa, gain, modsel, w_in.astype(BF16), n_ctx, F32).reshape(b, t, -1)
    xbc = _conv_silu(proj, SSD_INNER, conv_w, conv_b, n_ctx)
    dt = proj[..., SSD_INNER + SSD_CONV_DIM:]
    dt = jax.nn.softplus(dt.reshape(b, t, 2, SSD_GROUPS, SSD_HPG) + dt_bias.reshape(2, SSD_GROUPS, SSD_HPG))
    dt_c = jnp.transpose(dt, (2, 0, 3, 1, 4))
    dt_r = jnp.transpose(dt, (2, 0, 3, 4, 1))
    a_coef = -jnp.exp(a_log)
    a_row = a_coef.reshape(2, SSD_GROUPS, 1, SSD_HPG)
    a_col = a_coef.reshape(2, SSD_GROUPS, SSD_HPG, 1)
    skip = d_skip.reshape(2, SSD_GROUPS, SSD_HPG, 1)
    y = _ssd_scan(xbc, dt_c, dt_r, a_row, a_col, skip, n_ctx, F32)
    return _gate_norm(y, proj, norm_gain, n_ctx)


def kernel(x, c, ctx, c_ctx, mod_w, mod_b, norm_mix, norm_ffn, norm_final, attn_w_in, attn_w_out, attn_q_gain,
           attn_k_gain, diff_lam_q1, diff_lam_k1, diff_lam_q2, diff_lam_k2, diff_subln, ssd_w_in, ssd_conv_w,
           ssd_conv_b, ssd_dt_bias, ssd_a_log, ssd_d, ssd_norm, ssd_w_out, router_w, router_bias, exp_w_gate,
           exp_w_up, exp_w_down, shared_w_gate, shared_w_up, shared_w_down):
    b, s, d = x.shape
    n_ctx = ctx.shape[1]
    t = n_ctx + s
    xa = jnp.concatenate([ctx, x], axis=1)
    cond = jnp.concatenate([c, c_ctx[None]], axis=0)
    out = None
    for i in range(DEPTH):
        last = i == DEPTH - 1
        mod = _ada_mod(cond, mod_w[i], mod_b[i])
        modsel = jnp.stack([jnp.broadcast_to(mod[b], (b, N_MOD, d)), mod[:b]], axis=1)
        j = i // 2
        moe_w = (i, router_w[i], router_bias[i], exp_w_gate, exp_w_up, exp_w_down,
                 shared_w_gate[i], shared_w_up[i], shared_w_down[i])
        if i % 2 == 0:
            lambda_init = 0.8 - 0.6 * math.exp(-0.3 * i)
            y = _attention_layer(xa, modsel, norm_mix[i], n_ctx, attn_w_in[j], attn_q_gain[j], attn_k_gain[j],
                                 diff_lam_q1[j], diff_lam_k1[j], diff_lam_q2[j], diff_lam_k2[j],
                                 diff_subln[j], lambda_init)
            w_out = attn_w_out[j]
        else:
            y = _ssd_layer(xa, modsel, norm_mix[i], n_ctx, ssd_w_in[j], ssd_conv_w[j], ssd_conv_b[j],
                           ssd_dt_bias[j], ssd_a_log[j], ssd_d[j], ssd_norm[j])
            w_out = ssd_w_out[j]
        if last:
            x_new, f, fp = _fused_out_proj(y if y.shape[1] == s else y[:, n_ctx:], w_out.astype(BF16), xa, n_ctx,
                                           modsel, norm_ffn[i], 0)
            out = _moe_ffn(f.reshape(b * s, d), fp.reshape(b * s, d // 2), x_new.reshape(b * s, d), modsel, s, 0,
                           *moe_w).reshape(b, s, d)
        else:
            x_new, f, fp = _fused_out_proj(y, w_out.astype(BF16), xa, 0, modsel, norm_ffn[i], n_ctx)
            xa = _moe_ffn(f.reshape(b * t, d), fp.reshape(b * t, d // 2), x_new.reshape(b * t, d), modsel, t, n_ctx,
                          *moe_w).reshape(b, t, d)
    return _rms(out, norm_final)
```

```python
import functools
import math

import jax
import jax.numpy as jnp
from jax import lax
from jax.experimental import pallas as pl
from jax.experimental.pallas import tpu as pltpu

F32 = jnp.float32
BF16 = jnp.bfloat16

D_MODEL = 2048
DEPTH = 2
GRID_W = 64
NORM_EPS = 1e-6
N_MOD = 6
HEAD_DIM = 128
ROPE_THETA = 10000.0
A_Q_HEADS = 8
A_KV_HEADS = 2
A_GROUP = A_Q_HEADS // A_KV_HEADS
B_HEADS = 4
QA_W = A_Q_HEADS * HEAD_DIM
KA_W = A_KV_HEADS * HEAD_DIM
QB_W = 2 * B_HEADS * HEAD_DIM
ATTN_IN = QA_W + 2 * KA_W + 3 * QB_W
SSD_INNER = 2 * D_MODEL
SSD_HEADDIM = 64
SSD_HEADS = SSD_INNER // SSD_HEADDIM
SSD_GROUPS = 8
SSD_HPG = SSD_HEADS // SSD_GROUPS
SSD_STATE = 128
SSD_CONV = 3
SSD_CHUNK = 128
SSD_GROUP_W = SSD_HPG * SSD_HEADDIM
SSD_CONV_DIM = SSD_INNER + 2 * SSD_GROUPS * SSD_STATE
N_EXPERTS = 64
EXPERT_FF = 512
TOP_K = 8
N_EXPERT_GROUPS = 8
TOPK_GROUPS = 4
ROUTED_SCALE = 2.5
EXPERT_TM = 512

VMEM_LIMIT_BYTES = 56 * 1024 * 1024


def _pick(n, prefs):
    for p in prefs:
        if n % p == 0:
            return p
    raise ValueError(f"no tile in {prefs} divides {n}")


def _params(sem):
    return pltpu.CompilerParams(dimension_semantics=sem, vmem_limit_bytes=VMEM_LIMIT_BYTES)


def _mm_kernel(x_ref, w_ref, o_ref):
    o_ref[...] = jnp.dot(x_ref[...], w_ref[...], preferred_element_type=F32).astype(o_ref.dtype)


def _matmul(x, w, out_dtype):
    m, k = x.shape
    n = w.shape[1]
    tm = _pick(m, (1024, 512, 384, 256, 128, 16, 8))
    tn = _pick(n, (1024, 512, 384, 256, 128))
    return pl.pallas_call(
        _mm_kernel,
        grid=(m // tm, n // tn),
        in_specs=[pl.BlockSpec((tm, k), lambda i, j: (i, 0)),
                  pl.BlockSpec((k, tn), lambda i, j: (0, j))],
        out_specs=pl.BlockSpec((tm, tn), lambda i, j: (i, j)),
        out_shape=jax.ShapeDtypeStruct((m, n), out_dtype),
        compiler_params=_params(("parallel", "arbitrary")),
        name="matmul",
    )(x, w)


def _pack_bf16_pairs(v):
    half = v.shape[1] // 2
    lo = lax.bitcast_convert_type(v[:, :half].astype(BF16).astype(F32), jnp.uint32) >> 16
    hi = lax.bitcast_convert_type(v[:, half:].astype(BF16).astype(F32), jnp.uint32) & jnp.uint32(0xFFFF0000)
    return hi | lo


def _unpack_bf16_pairs(w):
    lo = lax.bitcast_convert_type(w << 16, F32)
    hi = lax.bitcast_convert_type(w & jnp.uint32(0xFFFF0000), F32)
    return lo, hi


def _segment_rows(mod_ref, k, is_ctx):
    return jnp.where(is_ctx, mod_ref[0, 0, k:k + 1, :], mod_ref[0, 1, k:k + 1, :])


def _is_ctx_rows(tm, tiles_per_batch, n_ctx):
    row = (pl.program_id(0) % tiles_per_batch) * tm + lax.broadcasted_iota(jnp.int32, (tm, 1), 0)
    return row < n_ctx


def _norm_mod(x, gain, shift, scale):
    xn = x * lax.rsqrt(jnp.mean(x * x, axis=-1, keepdims=True) + NORM_EPS) * gain
    return xn * (1.0 + scale) + shift


_Q_SCALE = HEAD_DIM ** -0.5
_ATTN_HEADS = ([(0, True, 1.0)] * A_Q_HEADS + [(1, True, 1.0)] * A_KV_HEADS + [(None, False, 1.0)] * A_KV_HEADS
               + [(None, True, _Q_SCALE)] * (2 * B_HEADS) + [(None, True, 1.0)] * (2 * B_HEADS)
               + [(None, False, 1.0)] * (2 * B_HEADS))
_PROJ_HEADS_PER_TILE = 4


def _head_epilogue(u, cfg, gains_ref, cm1, sn, even):
    gain_row, rope, scale = cfg
    if gain_row is not None:
        u = u * lax.rsqrt(jnp.mean(u * u, axis=-1, keepdims=True) + NORM_EPS) * gains_ref[gain_row:gain_row + 1, :]
    if scale != 1.0:
        u = u * scale
    if rope:
        partner = jnp.where(even, pltpu.roll(u, HEAD_DIM - 1, axis=1), pltpu.roll(u, 1, axis=1))
        u = u * (1.0 + cm1) + partner * sn
    return u


def _proj_kernel(x_ref, gain_ref, mod_ref, w_ref, *rest, tiles_per_batch, n_ctx, heads):
    if heads:
        gains_ref, cm1_ref, sn_ref, o_ref, h_s = rest
    else:
        o_ref, h_s = rest
    tm = x_ref.shape[0]
    is_ctx = _is_ctx_rows(tm, tiles_per_batch, n_ctx)

    @pl.when(pl.program_id(1) == 0)
    def _():
        h = _norm_mod(x_ref[...], gain_ref[...], _segment_rows(mod_ref, 0, is_ctx), _segment_rows(mod_ref, 1, is_ctx))
        h_s[...] = h.astype(BF16)

    if not heads:
        o_ref[...] = jnp.dot(h_s[...], w_ref[...], preferred_element_type=F32).astype(o_ref.dtype)
        return
    even = (lax.broadcasted_iota(jnp.int32, (tm, HEAD_DIM), 1) & 1) == 0
    hw = 2 * HEAD_DIM
    for tile in range(len(_ATTN_HEADS) // heads):
        @pl.when(pl.program_id(1) == tile)
        def _(tile=tile):
            cm1 = cm1_ref[...]
            sn = sn_ref[...]
            for half in range(heads // 2):
                acc = jnp.dot(h_s[...], w_ref[:, half * hw:(half + 1) * hw], preferred_element_type=F32)
                for hh in range(2):
                    col = half * hw + hh * HEAD_DIM
                    cfg = _ATTN_HEADS[tile * heads + half * 2 + hh]
                    out = _head_epilogue(acc[:, hh * HEAD_DIM:(hh + 1) * HEAD_DIM], cfg, gains_ref, cm1, sn, even)
                    o_ref[:, col:col + HEAD_DIM] = out.astype(o_ref.dtype)


def _fused_proj(xa, gain, modsel, w, n_ctx, out_dtype, head_gains=None, rope=None):
    b, t, d = xa.shape
    n = w.shape[1]
    tm = _pick(t, (768, 512, 384, 256))
    tpb = t // tm
    heads = 0 if head_gains is None else _PROJ_HEADS_PER_TILE
    tn = heads * HEAD_DIM if heads else _pick(n, (1152, 1024, 768, 512, 384, 256, 128))
    in_specs = [pl.BlockSpec((tm, d), lambda i, j: (i, 0)),
                pl.BlockSpec((1, d), lambda i, j: (0, 0)),
                pl.BlockSpec((1, 2, N_MOD, d), lambda i, j: (i // tpb, 0, 0, 0)),
                pl.BlockSpec((d, tn), lambda i, j: (0, j))]
    args = [xa.reshape(b * t, d), gain.reshape(1, d), modsel, w]
    if heads:
        in_specs += [pl.BlockSpec((2, HEAD_DIM), lambda i, j: (0, 0)),
                     pl.BlockSpec((tm, HEAD_DIM), lambda i, j: (i % tpb, 0)),
                     pl.BlockSpec((tm, HEAD_DIM), lambda i, j: (i % tpb, 0))]
        args += [head_gains, rope[0], rope[1]]
    return pl.pallas_call(
        functools.partial(_proj_kernel, tiles_per_batch=tpb, n_ctx=n_ctx, heads=heads),
        grid=(b * tpb, n // tn),
        in_specs=in_specs,
        out_specs=pl.BlockSpec((tm, tn), lambda i, j: (i, j)),
        out_shape=jax.ShapeDtypeStruct((b * t, n), out_dtype),
        scratch_shapes=[pltpu.VMEM((tm, d), BF16)],
        compiler_params=_params(("parallel", "arbitrary")),
        name="norm_mod_proj",
    )(*args)


def _out_proj_kernel(y_ref, w_ref, xa_ref, mod_ref, gain_ref, xo_ref, f_ref, fp_ref, *, tiles_per_batch, n_ctx):
    tm = y_ref.shape[1]
    is_ctx = _is_ctx_rows(tm, tiles_per_batch, n_ctx)
    acc = jnp.dot(y_ref[0], w_ref[...], preferred_element_type=F32)
    x_new = xa_ref[0] + _segment_rows(mod_ref, 2, is_ctx) * acc
    xo_ref[0] = x_new
    f = _norm_mod(x_new, gain_ref[...], _segment_rows(mod_ref, 3, is_ctx), _segment_rows(mod_ref, 4, is_ctx))
    f_ref[0] = f
    fp_ref[0] = _pack_bf16_pairs(f)


def _fused_out_proj(y, w, xa, xa_row0, modsel, gain, n_ctx):
    b, tq, k = y.shape
    d = w.shape[1]
    tm = _pick(tq, (384, 256)) if n_ctx else _pick(tq, (256, 128))
    assert xa_row0 % tm == 0
    r0 = xa_row0 // tm
    tpb = tq // tm
    tile = pl.BlockSpec((1, tm, d), lambda i: (i // tpb, i % tpb, 0))
    return pl.pallas_call(
        functools.partial(_out_proj_kernel, tiles_per_batch=tpb, n_ctx=n_ctx),
        grid=(b * tpb,),
        in_specs=[pl.BlockSpec((1, tm, k), lambda i: (i // tpb, i % tpb, 0)),
                  pl.BlockSpec((k, d), lambda i: (0, 0)),
                  pl.BlockSpec((1, tm, d), lambda i: (i // tpb, r0 + i % tpb, 0)),
                  pl.BlockSpec((1, 2, N_MOD, d), lambda i: (i // tpb, 0, 0, 0)),
                  pl.BlockSpec((1, d), lambda i: (0, 0))],
        out_specs=[tile, tile, pl.BlockSpec((1, tm, d // 2), lambda i: (i // tpb, i % tpb, 0))],
        out_shape=[jax.ShapeDtypeStruct((b, tq, d), F32), jax.ShapeDtypeStruct((b, tq, d), F32),
                   jax.ShapeDtypeStruct((b, tq, d // 2), jnp.uint32)],
        compiler_params=_params(("parallel",)),
        name="out_proj_residual_norm",
    )(y, w, xa, modsel, gain.reshape(1, d))


def _softmax_rows(s):
    m = jnp.max(s, axis=-1, keepdims=True)
    p = jnp.exp(s - m)
    return p, jnp.sum(p, axis=-1, keepdims=True)


def _gqa_heads(q_ref, k, v, o_ref):
    for g in range(A_GROUP):
        q = q_ref[0, :, g * HEAD_DIM:(g + 1) * HEAD_DIM]
        s = lax.dot_general(q, k, (((1,), (1,)), ((), ())), preferred_element_type=F32)
        p, l = _softmax_rows(s)
        o = jnp.dot(p.astype(BF16), v, preferred_element_type=F32) / l
        o_ref[0, :, g * HEAD_DIM:(g + 1) * HEAD_DIM] = o.astype(o_ref.dtype)


def _gqa_kernel(q_ref, k_ref, v_ref, o_ref, *, n_ctx):
    @pl.when(pl.program_id(2) == 0)
    def _():
        _gqa_heads(q_ref, k_ref[0, :n_ctx], v_ref[0, :n_ctx], o_ref)

    @pl.when(pl.program_id(2) > 0)
    def _():
        _gqa_heads(q_ref, k_ref[0], v_ref[0], o_ref)


def _gqa_attention(qkv, n_ctx):
    b, t, _ = qkv.shape
    tq = n_ctx
    gw = A_GROUP * HEAD_DIM
    k0 = QA_W // HEAD_DIM
    v0 = (QA_W + KA_W) // HEAD_DIM
    return pl.pallas_call(
        functools.partial(_gqa_kernel, n_ctx=n_ctx),
        grid=(b, A_KV_HEADS, t // tq),
        in_specs=[pl.BlockSpec((1, tq, gw), lambda bi, h, i: (bi, i, h)),
                  pl.BlockSpec((1, t, HEAD_DIM), lambda bi, h, i: (bi, 0, k0 + h)),
                  pl.BlockSpec((1, t, HEAD_DIM), lambda bi, h, i: (bi, 0, v0 + h))],
        out_specs=pl.BlockSpec((1, tq, gw), lambda bi, h, i: (bi, i, h)),
        out_shape=jax.ShapeDtypeStruct((b, t, QA_W + QB_W), BF16),
        compiler_params=_params(("parallel", "parallel", "arbitrary")),
        name="gqa_attention",
    )(qkv, qkv, qkv)


def _diff_head(lam, gain_ref, q_ref, k, v, o_ref, out_scale):
    parts = []
    for m in range(2):
        q = q_ref[0, :, m * HEAD_DIM:(m + 1) * HEAD_DIM]
        s = lax.dot_general(q, k[:, m * HEAD_DIM:(m + 1) * HEAD_DIM], (((1,), (1,)), ((), ())),
                            preferred_element_type=F32)
        p, l = _softmax_rows(s)
        parts.append(p / l)
    a = parts[0] - lam * parts[1]
    y = jnp.dot(a.astype(BF16), v, preferred_element_type=F32)
    y = y * lax.rsqrt(jnp.mean(y * y, axis=-1, keepdims=True) + NORM_EPS)
    o_ref[0] = (y * gain_ref[...] * out_scale).astype(o_ref.dtype)


def _diff_kernel(lam_ref, gain_ref, q_ref, k_ref, v_ref, y_in_ref, o_ref, *, out_scale, n_ctx):
    del y_in_ref
    lam = lam_ref[0, 0]

    @pl.when(pl.program_id(2) == 0)
    def _():
        _diff_head(lam, gain_ref, q_ref, k_ref[0, :n_ctx], v_ref[0, :n_ctx], o_ref, out_scale)

    @pl.when(pl.program_id(2) > 0)
    def _():
        _diff_head(lam, gain_ref, q_ref, k_ref[0], v_ref[0], o_ref, out_scale)


def _diff_attention(qkv, y, lam, subln_gain, out_scale, n_ctx):
    b, t, _ = qkv.shape
    tq = n_ctx
    hw = 2 * HEAD_DIM
    q0 = (QA_W + 2 * KA_W) // hw
    k0 = q0 + B_HEADS
    v0 = k0 + B_HEADS
    o0 = QA_W // hw
    return pl.pallas_call(
        functools.partial(_diff_kernel, out_scale=out_scale, n_ctx=n_ctx),
        grid=(b, B_HEADS, t // tq),
        in_specs=[pl.BlockSpec(memory_space=pltpu.SMEM),
                  pl.BlockSpec((1, hw), lambda bi, h, i: (0, 0)),
                  pl.BlockSpec((1, tq, hw), lambda bi, h, i: (bi, i, q0 + h)),
                  pl.BlockSpec((1, t, hw), lambda bi, h, i: (bi, 0, k0 + h)),
                  pl.BlockSpec((1, t, hw), lambda bi, h, i: (bi, 0, v0 + h)),
                  pl.BlockSpec(memory_space=pl.ANY)],
        out_specs=pl.BlockSpec((1, tq, hw), lambda bi, h, i: (bi, i, o0 + h)),
        out_shape=jax.ShapeDtypeStruct(y.shape, y.dtype),
        input_output_aliases={5: 0},
        compiler_params=_params(("parallel", "parallel", "arbitrary")),
        name="diff_attention",
    )(lam, subln_gain, qkv, qkv, qkv, y)


def _split3(a):
    a1 = a.astype(BF16)
    r1 = a - a1.astype(F32)
    a2 = r1.astype(BF16)
    a3 = (r1 - a2.astype(F32)).astype(BF16)
    return a1, a2, a3


def _dot_exact_rhs(a, rhs01):
    out = None
    for part in _split3(a):
        d = jnp.dot(part, rhs01, preferred_element_type=F32)
        out = d if out is None else out + d
    return out


def _dot_exact_lhs(lhs01, a):
    out = None
    for part in _split3(a):
        d = jnp.dot(lhs01, part, preferred_element_type=F32)
        out = d if out is None else out + d
    return out


SSD_GROUPS_PER_STEP = 8


def _ssd_group(d_sign, acr, acc, skip, x_tok, b_tok, c_tok, dt_c, dt_r, state):
    q = SSD_CHUNK
    rows = lax.broadcasted_iota(jnp.int32, (q, q), 0)
    cols = lax.broadcasted_iota(jnp.int32, (q, q), 1)
    signed = (rows - cols) * d_sign
    keep_sl = signed <= 0
    tri_ks01 = jnp.where(keep_sl, 1.0, 0.0).astype(BF16)
    tri_sk01 = jnp.where(signed >= 0, 1.0, 0.0).astype(BF16)

    x = jnp.transpose(x_tok)
    bm = b_tok.astype(BF16)
    cm = c_tok.astype(BF16)
    a_r = dt_r * acc
    a_c = dt_c * acr
    cum_r = _dot_exact_rhs(a_r, tri_ks01)
    cum_c = _dot_exact_lhs(tri_sk01, a_c)
    tot = jnp.sum(a_r, axis=1, keepdims=True)
    to_end_r = jnp.exp(tot - cum_r)
    from_start_r = jnp.exp(cum_r)
    tot_e = jnp.exp(tot)

    g_sl = lax.dot_general(bm, cm, (((1,), (1,)), ((), ())), preferred_element_type=F32)
    y_off = lax.dot_general(state.astype(BF16), cm, (((1,), (1,)), ((), ())), preferred_element_type=F32)

    xw_parts, y_parts = [], []
    for h in range(SSD_HPG):
        sl = slice(h * SSD_HEADDIM, (h + 1) * SSD_HEADDIM)
        xh = x[sl, :]
        xd = xh * dt_r[h:h + 1, :]
        seg = cum_r[h:h + 1, :] - cum_c[:, h:h + 1]
        decay = jnp.exp(jnp.where(keep_sl, seg, -1e30))
        m_h = (g_sl * decay).astype(BF16)
        y_h = jnp.dot(xd.astype(BF16), m_h, preferred_element_type=F32)
        y_parts.append(y_h + y_off[sl, :] * from_start_r[h:h + 1, :] + skip[h:h + 1, :] * xh)
        xw_parts.append((xd * to_end_r[h:h + 1, :]).astype(BF16))
    s_new = jnp.dot(jnp.concatenate(xw_parts, axis=0), bm, preferred_element_type=F32)
    decay_rows = jnp.concatenate([jnp.broadcast_to(tot_e[h:h + 1, :], (SSD_HEADDIM, 1)) for h in range(SSD_HPG)], axis=0)
    return jnp.transpose(jnp.concatenate(y_parts, axis=0)), state * decay_rows + s_new


def _ssd_kernel(acr_ref, acc_ref, skip_ref, x_ref, b_ref, c_ref, dtc_ref, dtr_ref, y_ref, state_ref):
    @pl.when(pl.program_id(3) == 0)
    def _():
        state_ref[...] = jnp.zeros_like(state_ref)

    d_sign = 1 - 2 * pl.program_id(1)
    for g in range(SSD_GROUPS_PER_STEP):
        xs = slice(g * SSD_GROUP_W, (g + 1) * SSD_GROUP_W)
        ns = slice(g * SSD_STATE, (g + 1) * SSD_STATE)
        y, state = _ssd_group(d_sign, acr_ref[0, g], acc_ref[0, g], skip_ref[0, g], x_ref[0, :, xs], b_ref[0, :, ns],
                              c_ref[0, :, ns], dtc_ref[0, 0, g], dtr_ref[0, 0, g], state_ref[g])
        y_ref[0, 0, :, xs] = y.astype(y_ref.dtype)
        state_ref[g] = state


def _ssd_scan(xbc, dt_c, dt_r, a_row, a_col, skip, n_ctx, out_dtype):
    b, t, _ = xbc.shape
    gs = SSD_GROUPS_PER_STEP
    b0 = SSD_INNER // (gs * SSD_STATE)
    c0 = b0 + SSD_GROUPS // gs
    q = SSD_CHUNK
    ncc = n_ctx // q
    nch = t // q

    def chunk(d, s):
        back = jnp.where(s < ncc, ncc - 1 - s, nch - 1 - s + ncc)
        return jnp.where(d == 0, s, back)

    return pl.pallas_call(
        _ssd_kernel,
        grid=(b, 2, SSD_GROUPS // gs, nch),
        in_specs=[pl.BlockSpec((1, gs, 1, SSD_HPG), lambda bi, d, g, s: (d, g, 0, 0)),
                  pl.BlockSpec((1, gs, SSD_HPG, 1), lambda bi, d, g, s: (d, g, 0, 0)),
                  pl.BlockSpec((1, gs, SSD_HPG, 1), lambda bi, d, g, s: (d, g, 0, 0)),
                  pl.BlockSpec((1, q, gs * SSD_GROUP_W), lambda bi, d, g, s: (bi, chunk(d, s), g)),
                  pl.BlockSpec((1, q, gs * SSD_STATE), lambda bi, d, g, s: (bi, chunk(d, s), b0 + g)),
                  pl.BlockSpec((1, q, gs * SSD_STATE), lambda bi, d, g, s: (bi, chunk(d, s), c0 + g)),
                  pl.BlockSpec((1, 1, gs, q, SSD_HPG), lambda bi, d, g, s: (d, bi, g, chunk(d, s), 0)),
                  pl.BlockSpec((1, 1, gs, SSD_HPG, q), lambda bi, d, g, s: (d, bi, g, 0, chunk(d, s)))],
        out_specs=pl.BlockSpec((1, 1, q, gs * SSD_GROUP_W), lambda bi, d, g, s: (d, bi, chunk(d, s), g)),
        out_shape=jax.ShapeDtypeStruct((2, b, t, SSD_INNER), out_dtype),
        scratch_shapes=[pltpu.VMEM((gs, SSD_GROUP_W, SSD_STATE), F32)],
        compiler_params=_params(("parallel", "parallel", "parallel", "arbitrary")),
        name="ssd_scan",
    )(a_row, a_col, skip, xbc, xbc, xbc, dt_c, dt_r)


def _conv_silu_kernel(x_ref, w_ref, b_ref, o_ref, *, n_ctx):
    x = x_ref[0]
    t = x.shape[0]
    row = lax.broadcasted_iota(jnp.int32, (t, 1), 0)
    prev = jnp.where((row == 0) | (row == n_ctx), 0.0, pltpu.roll(x, 1, axis=0))
    nxt = jnp.where((row == n_ctx - 1) | (row == t - 1), 0.0, pltpu.roll(x, t - 1, axis=0))
    u = prev * w_ref[0:1, :] + x * w_ref[1:2, :] + nxt * w_ref[2:3, :] + b_ref[...]
    o_ref[0] = u * jax.nn.sigmoid(u)


def _conv_silu(proj, col0, conv_w, conv_b, n_ctx):
    b, t, _ = proj.shape
    tc = 512
    c0 = col0 // tc
    return pl.pallas_call(
        functools.partial(_conv_silu_kernel, n_ctx=n_ctx),
        grid=(b, SSD_CONV_DIM // tc),
        in_specs=[pl.BlockSpec((1, t, tc), lambda bi, j: (bi, 0, c0 + j)),
                  pl.BlockSpec((SSD_CONV, tc), lambda bi, j: (0, j)),
                  pl.BlockSpec((1, tc), lambda bi, j: (0, j))],
        out_specs=pl.BlockSpec((1, t, tc), lambda bi, j: (bi, 0, j)),
        out_shape=jax.ShapeDtypeStruct((b, t, SSD_CONV_DIM), F32),
        compiler_params=_params(("parallel", "parallel")),
        name="ssd_conv_silu",
    )(proj, conv_w, conv_b.reshape(1, -1))


def _gate_norm_kernel(y_ref, z_ref, gain_ref, o_ref):
    z = z_ref[0]
    g = (y_ref[0, 0] + y_ref[1, 0]) * (z * jax.nn.sigmoid(z))
    o_ref[0] = (g * lax.rsqrt(jnp.mean(g * g, axis=-1, keepdims=True) + NORM_EPS) * gain_ref[...]).astype(o_ref.dtype)


def _gate_norm(y, proj, gain, n_ctx):
    _, b, t, c = y.shape
    tt = 256
    assert n_ctx % tt == 0
    r0 = n_ctx // tt
    s = t - n_ctx
    return pl.pallas_call(
        _gate_norm_kernel,
        grid=(b, s // tt),
        in_specs=[pl.BlockSpec((2, 1, tt, c), lambda bi, i: (0, bi, r0 + i, 0)),
                  pl.BlockSpec((1, tt, c), lambda bi, i: (bi, r0 + i, 0)),
                  pl.BlockSpec((1, c), lambda bi, i: (0, 0))],
        out_specs=pl.BlockSpec((1, tt, c), lambda bi, i: (bi, i, 0)),
        out_shape=jax.ShapeDtypeStruct((b, s, c), BF16),
        compiler_params=_params(("parallel", "parallel")),
        name="ssd_gate_norm",
    )(y, proj, gain.reshape(1, c))


def _route_kernel(f_ref, rwt_ref, bias_ref, idx_ref, gate_ref, rank_ref, cnt_ref, carry_ref):
    tn = f_ref.shape[0]
    ne, ng, pg = N_EXPERTS, N_EXPERT_GROUPS, N_EXPERTS // N_EXPERT_GROUPS
    neg = -jnp.inf

    @pl.when(pl.program_id(0) == 0)
    def _():
        carry_ref[...] = jnp.zeros_like(carry_ref)

    dn = (((1,), (1,)), ((), ()))
    w = rwt_ref[...]
    f = f_ref[...]
    w1 = w.astype(BF16)
    w2 = (w - w1.astype(F32)).astype(BF16)
    f1 = f.astype(BF16)
    f2 = (f - f1.astype(F32)).astype(BF16)
    logits = (lax.dot_general(w1, f1, dn, preferred_element_type=F32)
              + lax.dot_general(w1, f2, dn, preferred_element_type=F32)
              + lax.dot_general(w2, f1, dn, preferred_element_type=F32))
    scores = jax.nn.sigmoid(logits)
    g3 = (scores + bias_ref[...]).reshape(ng, pg, tn)
    io3 = lax.broadcasted_iota(jnp.int32, (ng, pg, tn), 1)
    m1 = jnp.max(g3, axis=1, keepdims=True)
    i1 = jnp.min(jnp.where(g3 == m1, io3, pg), axis=1, keepdims=True)
    m2 = jnp.max(jnp.where(io3 == i1, neg, g3), axis=1, keepdims=True)
    work = (m1 + m2).reshape(ng, tn)
    iog = lax.broadcasted_iota(jnp.int32, (ng, tn), 0)
    ok = jnp.zeros((ng, tn), F32)
    for _ in range(TOPK_GROUPS):
        m = jnp.max(work, axis=0, keepdims=True)
        gi = jnp.min(jnp.where(work == m, iog, ng), axis=0, keepdims=True)
        hit = iog == gi
        ok = jnp.where(hit, 1.0, ok)
        work = jnp.where(hit, neg, work)
    sel = jnp.where(ok.reshape(ng, 1, tn) > 0.0, g3, neg).reshape(ne, tn)
    ioe = lax.broadcasted_iota(jnp.int32, (ne, tn), 0)
    onehot = jnp.zeros((ne, tn), F32)
    idxs, ws = [], []
    for _ in range(TOP_K):
        m = jnp.max(sel, axis=0, keepdims=True)
        ei = jnp.min(jnp.where(sel == m, ioe, ne), axis=0, keepdims=True)
        hit = ioe == ei
        idxs.append(ei)
        ws.append(jnp.sum(jnp.where(hit, scores, 0.0), axis=0, keepdims=True))
        sel = jnp.where(hit, neg, sel)
        onehot = jnp.where(hit, 1.0, onehot)
    w = jnp.concatenate(ws, axis=0)
    gate_ref[...] = w / jnp.sum(w, axis=0, keepdims=True) * ROUTED_SCALE
    idx_ref[...] = jnp.concatenate(idxs, axis=0)
    r = lax.broadcasted_iota(jnp.int32, (tn, tn), 0)
    c = lax.broadcasted_iota(jnp.int32, (tn, tn), 1)
    ahead = jnp.where(r < c, 1.0, 0.0).astype(BF16)
    cum = carry_ref[...] + jnp.dot(onehot.astype(BF16), ahead, preferred_element_type=F32)
    ranks = [jnp.sum(jnp.where(ioe == idxs[k], cum, 0.0), axis=0, keepdims=True) for k in range(TOP_K)]
    rank_ref[...] = jnp.concatenate(ranks, axis=0).astype(jnp.int32)
    total = carry_ref[...] + jnp.sum(onehot, axis=1, keepdims=True)
    carry_ref[...] = total
    cnt_ref[...] = total.astype(jnp.int32)


def _route(f, router_wt, router_bias):
    t, d = f.shape
    tn = _pick(t, (512, 256, 128))
    kt = pl.BlockSpec((TOP_K, tn), lambda i: (0, i))
    return pl.pallas_call(
        _route_kernel,
        grid=(t // tn,),
        in_specs=[pl.BlockSpec((tn, d), lambda i: (i, 0)),
                  pl.BlockSpec((N_EXPERTS, d), lambda i: (0, 0)),
                  pl.BlockSpec((N_EXPERTS, 1), lambda i: (0, 0))],
        out_specs=[kt, kt, kt, pl.BlockSpec((N_EXPERTS, 1), lambda i: (0, 0))],
        out_shape=[jax.ShapeDtypeStruct((TOP_K, t), jnp.int32), jax.ShapeDtypeStruct((TOP_K, t), F32),
                   jax.ShapeDtypeStruct((TOP_K, t), jnp.int32), jax.ShapeDtypeStruct((N_EXPERTS, 1), jnp.int32)],
        scratch_shapes=[pltpu.VMEM((N_EXPERTS, 1), F32)],
        compiler_params=_params(("arbitrary",)),
        name="moe_route",
    )(f, router_wt, router_bias)


def _scatter_kernel(dest_ref, f_ref, xs_in_ref, xs_ref, sem):
    del xs_in_ref
    ts = f_ref.shape[0]

    def row_copy(t, k):
        return pltpu.make_async_copy(f_ref.at[pl.ds(t, 1)], xs_ref.at[pl.ds(dest_ref[t * TOP_K + k], 1)], sem)

    def issue(t, carry):
        for k in range(TOP_K):
            row_copy(t, k).start(priority=k % 2)
        return carry

    def drain(t, carry):
        for k in range(TOP_K):
            row_copy(t, k).wait()
        return carry

    lax.fori_loop(0, ts, issue, 0)
    lax.fori_loop(0, ts, drain, 0)


def _scatter_rows(dest_flat, f, xs):
    t, d = f.shape
    ts = _pick(t, (512, 256, 128))
    return pl.pallas_call(
        _scatter_kernel,
        grid=(t // ts,),
        in_specs=[pl.BlockSpec((ts * TOP_K,), lambda i: (i,), memory_space=pltpu.SMEM),
                  pl.BlockSpec((ts, d), lambda i: (i, 0)),
                  pl.BlockSpec(memory_space=pl.ANY)],
        out_specs=pl.BlockSpec(memory_space=pl.ANY),
        out_shape=jax.ShapeDtypeStruct(xs.shape, xs.dtype),
        scratch_shapes=[pltpu.SemaphoreType.DMA(())],
        input_output_aliases={2: 0},
        compiler_params=_params(("arbitrary",)),
        name="moe_scatter",
    )(dest_flat, f, xs)


def _zero_blocks_kernel(rows_ref, xs_ref, zero_ref, sem):
    tm = zero_ref.shape[0]
    zero_ref[...] = jnp.zeros_like(zero_ref)

    def block_copy(e):
        return pltpu.make_async_copy(zero_ref, xs_ref.at[pl.ds(pl.multiple_of(rows_ref[e], tm), tm)], sem)

    for e in range(N_EXPERTS):
        block_copy(e).start()
    for e in range(N_EXPERTS):
        block_copy(e).wait()


def _zero_blocks(block_rows, n_rows_alloc, tm, d):
    grid_spec = pltpu.PrefetchScalarGridSpec(
        num_scalar_prefetch=1, grid=(1,), in_specs=[],
        out_specs=pl.BlockSpec(memory_space=pl.ANY),
        scratch_shapes=[pltpu.VMEM((tm, d), jnp.uint32), pltpu.SemaphoreType.DMA(())])
    return pl.pallas_call(
        _zero_blocks_kernel,
        grid_spec=grid_spec,
        out_shape=jax.ShapeDtypeStruct((n_rows_alloc, d), jnp.uint32),
        compiler_params=_params(("arbitrary",)),
        name="moe_zero_tail_blocks",
    )(block_rows)


def _expert_kernel(be_ref, blk_ref, nused_ref, x_ref, wg_ref, wu_ref, wd_ref, o_ref, wg_s, wu_s, wd_s):
    i = pl.program_id(0)

    @pl.when(jnp.logical_or(i == 0, be_ref[i] != be_ref[jnp.maximum(i - 1, 0)]))
    def _():
        wg_s[...] = wg_ref[0, 0].astype(BF16)
        wu_s[...] = wu_ref[0, 0].astype(BF16)
        wd_s[...] = wd_ref[0, 0].astype(BF16)

    @pl.when(i < nused_ref[0])
    def _():
        lo, hi = _unpack_bf16_pairs(x_ref[...])
        x = jnp.concatenate([lo.astype(BF16), hi.astype(BF16)], axis=1)
        hg = jnp.dot(x, wg_s[...], preferred_element_type=F32)
        hu = jnp.dot(x, wu_s[...], preferred_element_type=F32)
        h = (hg * jax.nn.sigmoid(hg) * hu).astype(BF16)
        o_ref[...] = _pack_bf16_pairs(jnp.dot(h, wd_s[...], preferred_element_type=F32))


def _expert_blocks(block_expert, block_index, n_used, xs, n_blocks, tm, layer, w_gate, w_up, w_down):
    d, ff = w_gate.shape[2:]
    grid_spec = pltpu.PrefetchScalarGridSpec(
        num_scalar_prefetch=3,
        grid=(n_blocks,),
        in_specs=[pl.BlockSpec((tm, d // 2), lambda i, be, blk, nu: (blk[i], 0)),
                  pl.BlockSpec((1, 1, d, ff), lambda i, be, blk, nu: (layer, be[i], 0, 0)),
                  pl.BlockSpec((1, 1, d, ff), lambda i, be, blk, nu: (layer, be[i], 0, 0)),
                  pl.BlockSpec((1, 1, ff, d), lambda i, be, blk, nu: (layer, be[i], 0, 0))],
        out_specs=pl.BlockSpec((tm, d // 2), lambda i, be, blk, nu: (blk[i], 0)),
        scratch_shapes=[pltpu.VMEM((d, ff), BF16), pltpu.VMEM((d, ff), BF16), pltpu.VMEM((ff, d), BF16)],
    )
    return pl.pallas_call(
        _expert_kernel,
        grid_spec=grid_spec,
        out_shape=jax.ShapeDtypeStruct((n_blocks * tm, d // 2), jnp.uint32),
        compiler_params=_params(("arbitrary",)),
        name="moe_experts",
    )(block_expert, block_index, n_used, xs, w_gate, w_up, w_down)


def _combine_kernel(dest_ref, gate_ref, sh_ref, xa_ref, mod_ref, ys_ref, o_ref, buf, sem, *, tiles_per_batch, n_ctx):
    tn = gate_ref.shape[0]

    def row_copy(t, k):
        return pltpu.make_async_copy(ys_ref.at[pl.ds(dest_ref[t * TOP_K + k], 1)], buf.at[k, pl.ds(t, 1)], sem)

    def issue(t, carry):
        for k in range(TOP_K):
            row_copy(t, k).start(priority=k % 2)
        return carry

    def drain(t, carry):
        for k in range(TOP_K):
            row_copy(t, k).wait()
        return carry

    lax.fori_loop(0, tn, issue, 0)
    lax.fori_loop(0, tn, drain, 0)
    half = buf.shape[2]
    acc_lo = sh_ref[:, :half]
    acc_hi = sh_ref[:, half:]
    for k in range(TOP_K):
        lo, hi = _unpack_bf16_pairs(buf[k])
        acc_lo = acc_lo + gate_ref[:, k:k + 1] * lo
        acc_hi = acc_hi + gate_ref[:, k:k + 1] * hi
    acc = jnp.concatenate([acc_lo, acc_hi], axis=1)
    is_ctx = _is_ctx_rows(tn, tiles_per_batch, n_ctx)
    o_ref[...] = xa_ref[...] + _segment_rows(mod_ref, 5, is_ctx) * acc


def _combine(dest_flat, gates, shared, ys, xa, modsel, rows_per_batch, n_ctx):
    t, d = shared.shape
    tn = 256
    tpb = rows_per_batch // tn
    return pl.pallas_call(
        functools.partial(_combine_kernel, tiles_per_batch=tpb, n_ctx=n_ctx),
        grid=(t // tn,),
        in_specs=[pl.BlockSpec((tn * TOP_K,), lambda i: (i,), memory_space=pltpu.SMEM),
                  pl.BlockSpec((tn, TOP_K), lambda i: (i, 0)),
                  pl.BlockSpec((tn, d), lambda i: (i, 0)),
                  pl.BlockSpec((tn, d), lambda i: (i, 0)),
                  pl.BlockSpec((1, 2, N_MOD, d), lambda i: (i // tpb, 0, 0, 0)),
                  pl.BlockSpec(memory_space=pl.ANY)],
        out_specs=pl.BlockSpec((tn, d), lambda i: (i, 0)),
        out_shape=jax.ShapeDtypeStruct((t, d), F32),
        scratch_shapes=[pltpu.VMEM((TOP_K, tn, d // 2), jnp.uint32), pltpu.SemaphoreType.DMA(())],
        compiler_params=_params(("arbitrary",)),
        name="moe_combine",
    )(dest_flat, gates, shared, xa, modsel, ys)


def _swiglu_kernel(x_ref, wg_ref, wu_ref, wd_ref, o_ref):
    x = x_ref[...].astype(BF16)
    hg = jnp.dot(x, wg_ref[...], preferred_element_type=F32)
    hu = jnp.dot(x, wu_ref[...], preferred_element_type=F32)
    h = (hg * jax.nn.sigmoid(hg) * hu).astype(BF16)
    o_ref[...] = jnp.dot(h, wd_ref[...], preferred_element_type=F32).astype(o_ref.dtype)


def _shared_expert(x, wg, wu, wd):
    m, d = x.shape
    ff = wg.shape[1]
    tm = _pick(m, (512, 384, 256, 128))
    return pl.pallas_call(
        _swiglu_kernel,
        grid=(m // tm,),
        in_specs=[pl.BlockSpec((tm, d), lambda i: (i, 0)),
                  pl.BlockSpec((d, ff), lambda i: (0, 0)),
                  pl.BlockSpec((d, ff), lambda i: (0, 0)),
                  pl.BlockSpec((ff, d), lambda i: (0, 0))],
        out_specs=pl.BlockSpec((tm, d), lambda i: (i, 0)),
        out_shape=jax.ShapeDtypeStruct((m, d), F32),
        compiler_params=_params(("parallel",)),
        name="shared_expert",
    )(x, wg, wu, wd)


def _moe_ffn(f, fp, xa, modsel, rows_per_batch, n_ctx, layer, router_w, router_bias, w_gate, w_up, w_down,
             ws_gate, ws_up, ws_down):
    t, d = f.shape
    tm = EXPERT_TM
    idx, gate, rank, cnt = _route(f, router_w.T, router_bias.astype(F32)[:, None])
    counts = cnt[:, 0]
    padded = (counts + tm - 1) // tm * tm
    pad_end = jnp.cumsum(padded)
    pad_start = pad_end - padded
    n_used = pad_end[-1] // tm
    n_blocks = (t * TOP_K) // tm + N_EXPERTS
    n_rows = n_blocks * tm
    experts = jnp.arange(N_EXPERTS, dtype=jnp.int32)
    start_of = jnp.sum(jnp.where(idx[..., None] == experts, pad_start, 0), axis=-1)
    dest = (start_of + rank).T.reshape(-1).astype(jnp.int32)
    block_index = jnp.minimum(jnp.arange(n_blocks, dtype=jnp.int32), n_used - 1).astype(jnp.int32)
    block_expert = jnp.minimum(jnp.sum(pad_end[None, :] <= (block_index * tm)[:, None], axis=1),
                               N_EXPERTS - 1).astype(jnp.int32)
    tail_rows = jnp.where(padded > counts, pad_end - tm, n_rows + experts * tm).astype(jnp.int32)
    xs = _zero_blocks(tail_rows, n_rows + N_EXPERTS * tm, tm, d // 2)
    xs = _scatter_rows(dest, fp, xs)
    ys = _expert_blocks(block_expert, block_index, n_used.reshape(1).astype(jnp.int32), xs, n_blocks, tm, layer,
                        w_gate, w_up, w_down)
    shared = _shared_expert(f, ws_gate.astype(BF16), ws_up.astype(BF16), ws_down.astype(BF16))
    return _combine(dest, gate.T, shared, ys, xa, modsel, rows_per_batch, n_ctx)


def _rms(u, gain):
    return u * lax.rsqrt(jnp.mean(u * u, axis=-1, keepdims=True) + NORM_EPS) * gain


def _rope_tables(n_ctx, n_lat):
    rows = n_lat // GRID_W
    row = jnp.repeat(jnp.arange(rows, dtype=F32), GRID_W)
    col = jnp.tile(jnp.arange(GRID_W, dtype=F32), rows)
    n_freq = HEAD_DIM // 4
    inv = ROPE_THETA ** (-jnp.arange(n_freq, dtype=F32) / n_freq)
    ang = jnp.concatenate([row[:, None] * inv, col[:, None] * inv], axis=-1)
    cos = jnp.repeat(jnp.cos(ang), 2, axis=-1)
    sin = jnp.repeat(jnp.sin(ang), 2, axis=-1)
    sign = jnp.tile(jnp.array([-1.0, 1.0], F32), HEAD_DIM // 2)
    cos = jnp.concatenate([jnp.ones((n_ctx, HEAD_DIM), F32), cos], axis=0)
    sin = jnp.concatenate([jnp.zeros((n_ctx, HEAD_DIM), F32), sin * sign], axis=0)
    return cos, sin


def _ada_mod(cond, w, bias):
    m = jax.nn.silu(cond)
    rows = m.shape[0]
    pad = (-rows) % 16
    mp = jnp.pad(m, ((0, pad), (0, 0))).astype(BF16)
    out = _matmul(mp, w.astype(BF16), F32)[:rows] + bias
    return out.reshape(rows, N_MOD, -1)


def _attention_layer(xa, modsel, gain, n_ctx, w_in, q_gain, k_gain, lq1, lk1, lq2, lk2, subln, lambda_init):
    b, t, d = xa.shape
    cos, sin = _rope_tables(n_ctx, t - n_ctx)
    head_gains = jnp.stack([q_gain * _Q_SCALE, k_gain]).astype(F32)
    qkv = _fused_proj(xa, gain, modsel, w_in.astype(BF16), n_ctx, BF16,
                      head_gains=head_gains, rope=(cos - 1.0, sin)).reshape(b, t, ATTN_IN)
    lam = (jnp.exp(jnp.sum(lq1 * lk1)) - jnp.exp(jnp.sum(lq2 * lk2)) + lambda_init).reshape(1, 1).astype(F32)
    y = _gqa_attention(qkv, n_ctx)
    return _diff_attention(qkv, y, lam, subln.reshape(1, 2 * HEAD_DIM), 1.0 - lambda_init, n_ctx)


def _ssd_layer(xa, modsel, gain, n_ctx, w_in, conv_w, conv_b, dt_bias, a_log, d_skip, norm_gain):
    b, t, d = xa.shape
    proj = _fused_proj(xa, gain, modsel, w_in.astype(BF16), n_ctx, F32).reshape(b, t, -1)
    xbc = _conv_silu(proj, SSD_INNER, conv_w, conv_b, n_ctx)
    dt = proj[..., SSD_INNER + SSD_CONV_DIM:]
    dt = jax.nn.softplus(dt.reshape(b, t, 2, SSD_GROUPS, SSD_HPG) + dt_bias.reshape(2, SSD_GROUPS, SSD_HPG))
    dt_c = jnp.transpose(dt, (2, 0, 3, 1, 4))
    dt_r = jnp.transpose(dt, (2, 0, 3, 4, 1))
    a_coef = -jnp.exp(a_log)
    a_row = a_coef.reshape(2, SSD_GROUPS, 1, SSD_HPG)
    a_col = a_coef.reshape(2, SSD_GROUPS, SSD_HPG, 1)
    skip = d_skip.reshape(2, SSD_GROUPS, SSD_HPG, 1)
    y = _ssd_scan(xbc, dt_c, dt_r, a_row, a_col, skip, n_ctx, F32)
    return _gate_norm(y, proj, norm_gain, n_ctx)


def kernel(x, c, ctx, c_ctx, mod_w, mod_b, norm_mix, norm_ffn, norm_final, attn_w_in, attn_w_out, attn_q_gain,
           attn_k_gain, diff_lam_q1, diff_lam_k1, diff_lam_q2, diff_lam_k2, diff_subln, ssd_w_in, ssd_conv_w,
           ssd_conv_b, ssd_dt_bias, ssd_a_log, ssd_d, ssd_norm, ssd_w_out, router_w, router_bias, exp_w_gate,
           exp_w_up, exp_w_down, shared_w_gate, shared_w_up, shared_w_down):
    b, s, d = x.shape
    n_ctx = ctx.shape[1]
    t = n_ctx + s
    xa = jnp.concatenate([ctx, x], axis=1)
    cond = jnp.concatenate([c, c_ctx[None]], axis=0)
    out = None
    for i in range(DEPTH):
        last = i == DEPTH - 1
        mod = _ada_mod(cond, mod_w[i], mod_b[i])
        modsel = jnp.stack([jnp.broadcast_to(mod[b], (b, N_MOD, d)), mod[:b]], axis=1)
        j = i // 2
        moe_w = (i, router_w[i], router_bias[i], exp_w_gate, exp_w_up, exp_w_down,
                 shared_w_gate[i], shared_w_up[i], shared_w_down[i])
        if i % 2 == 0:
            lambda_init = 0.8 - 0.6 * math.exp(-0.3 * i)
            y = _attention_layer(xa, modsel, norm_mix[i], n_ctx, attn_w_in[j], attn_q_gain[j], attn_k_gain[j],
                                 diff_lam_q1[j], diff_lam_k1[j], diff_lam_q2[j], diff_lam_k2[j],
                                 diff_subln[j], lambda_init)
            w_out = attn_w_out[j]
        else:
            y = _ssd_layer(xa, modsel, norm_mix[i], n_ctx, ssd_w_in[j], ssd_conv_w[j], ssd_conv_b[j],
                           ssd_dt_bias[j], ssd_a_log[j], ssd_d[j], ssd_norm[j])
            w_out = ssd_w_out[j]
        if last:
            x_new, f, fp = _fused_out_proj(y if y.shape[1] == s else y[:, n_ctx:], w_out.astype(BF16), xa, n_ctx,
                                           modsel, norm_ffn[i], 0)
            out = _moe_ffn(f.reshape(b * s, d), fp.reshape(b * s, d // 2), x_new.reshape(b * s, d), modsel, s, 0,
                           *moe_w).reshape(b, s, d)
        else:
            x_new, f, fp = _fused_out_proj(y, w_out.astype(BF16), xa, 0, modsel, norm_ffn[i], n_ctx)
            xa = _moe_ffn(f.reshape(b * t, d), fp.reshape(b * t, d // 2), x_new.reshape(b * t, d), modsel, t, n_ctx,
                          *moe_w).reshape(b, t, d)
    return _rms(out, norm_final)
```

```python
import functools
import math

import jax
import jax.numpy as jnp
from jax import lax
from jax.experimental import pallas as pl
from jax.experimental.pallas import tpu as pltpu

F32 = jnp.float32
BF16 = jnp.bfloat16

D_MODEL = 2048
DEPTH = 2
GRID_W = 64
NORM_EPS = 1e-6
N_MOD = 6
HEAD_DIM = 128
ROPE_THETA = 10000.0
A_Q_HEADS = 8
A_KV_HEADS = 2
A_GROUP = A_Q_HEADS // A_KV_HEADS
B_HEADS = 4
QA_W = A_Q_HEADS * HEAD_DIM
KA_W = A_KV_HEADS * HEAD_DIM
QB_W = 2 * B_HEADS * HEAD_DIM
ATTN_IN = QA_W + 2 * KA_W + 3 * QB_W
SSD_INNER = 2 * D_MODEL
SSD_HEADDIM = 64
SSD_HEADS = SSD_INNER // SSD_HEADDIM
SSD_GROUPS = 8
SSD_HPG = SSD_HEADS // SSD_GROUPS
SSD_STATE = 128
SSD_CONV = 3
SSD_CHUNK = 128
SSD_GROUP_W = SSD_HPG * SSD_HEADDIM
SSD_CONV_DIM = SSD_INNER + 2 * SSD_GROUPS * SSD_STATE
N_EXPERTS = 64
EXPERT_FF = 512
TOP_K = 8
N_EXPERT_GROUPS = 8
TOPK_GROUPS = 4
ROUTED_SCALE = 2.5
EXPERT_TM = 512

VMEM_LIMIT_BYTES = 56 * 1024 * 1024


def _pick(n, prefs):
    for p in prefs:
        if n % p == 0:
            return p
    raise ValueError(f"no tile in {prefs} divides {n}")


def _params(sem):
    return pltpu.CompilerParams(dimension_semantics=sem, vmem_limit_bytes=VMEM_LIMIT_BYTES)


def _mm_kernel(x_ref, w_ref, o_ref):
    o_ref[...] = jnp.dot(x_ref[...], w_ref[...], preferred_element_type=F32).astype(o_ref.dtype)


def _matmul(x, w, out_dtype):
    m, k = x.shape
    n = w.shape[1]
    tm = _pick(m, (1024, 512, 384, 256, 128, 16, 8))
    tn = _pick(n, (1024, 512, 384, 256, 128))
    return pl.pallas_call(
        _mm_kernel,
        grid=(m // tm, n // tn),
        in_specs=[pl.BlockSpec((tm, k), lambda i, j: (i, 0)),
                  pl.BlockSpec((k, tn), lambda i, j: (0, j))],
        out_specs=pl.BlockSpec((tm, tn), lambda i, j: (i, j)),
        out_shape=jax.ShapeDtypeStruct((m, n), out_dtype),
        compiler_params=_params(("parallel", "arbitrary")),
        name="matmul",
    )(x, w)


def _pack_bf16_pairs(v):
    half = v.shape[1] // 2
    lo = lax.bitcast_convert_type(v[:, :half].astype(BF16).astype(F32), jnp.uint32) >> 16
    hi = lax.bitcast_convert_type(v[:, half:].astype(BF16).astype(F32), jnp.uint32) & jnp.uint32(0xFFFF0000)
    return hi | lo


def _unpack_bf16_pairs(w):
    lo = lax.bitcast_convert_type(w << 16, F32)
    hi = lax.bitcast_convert_type(w & jnp.uint32(0xFFFF0000), F32)
    return lo, hi


def _segment_rows(mod_ref, k, is_ctx):
    return jnp.where(is_ctx, mod_ref[0, 0, k:k + 1, :], mod_ref[0, 1, k:k + 1, :])


def _is_ctx_rows(tm, tiles_per_batch, n_ctx):
    row = (pl.program_id(0) % tiles_per_batch) * tm + lax.broadcasted_iota(jnp.int32, (tm, 1), 0)
    return row < n_ctx


def _norm_mod(x, gain, shift, scale):
    xn = x * lax.rsqrt(jnp.mean(x * x, axis=-1, keepdims=True) + NORM_EPS) * gain
    return xn * (1.0 + scale) + shift


_Q_SCALE = HEAD_DIM ** -0.5
_ATTN_HEADS = ([(0, True, 1.0)] * A_Q_HEADS + [(1, True, 1.0)] * A_KV_HEADS + [(None, False, 1.0)] * A_KV_HEADS
               + [(None, True, _Q_SCALE)] * (2 * B_HEADS) + [(None, True, 1.0)] * (2 * B_HEADS)
               + [(None, False, 1.0)] * (2 * B_HEADS))
_PROJ_HEADS_PER_TILE = 4


def _head_epilogue(u, cfg, gains_ref, cm1, sn, even):
    gain_row, rope, scale = cfg
    if gain_row is not None:
        u = u * lax.rsqrt(jnp.mean(u * u, axis=-1, keepdims=True) + NORM_EPS) * gains_ref[gain_row:gain_row + 1, :]
    if scale != 1.0:
        u = u * scale
    if rope:
        partner = jnp.where(even, pltpu.roll(u, HEAD_DIM - 1, axis=1), pltpu.roll(u, 1, axis=1))
        u = u * (1.0 + cm1) + partner * sn
    return u


def _proj_kernel(x_ref, gain_ref, mod_ref, w_ref, *rest, tiles_per_batch, n_ctx, heads):
    if heads:
        gains_ref, cm1_ref, sn_ref, o_ref, h_s = rest
    else:
        o_ref, h_s = rest
    tm = x_ref.shape[0]
    is_ctx = _is_ctx_rows(tm, tiles_per_batch, n_ctx)

    @pl.when(pl.program_id(1) == 0)
    def _():
        h = _norm_mod(x_ref[...], gain_ref[...], _segment_rows(mod_ref, 0, is_ctx), _segment_rows(mod_ref, 1, is_ctx))
        h_s[...] = h.astype(BF16)

    if not heads:
        o_ref[...] = jnp.dot(h_s[...], w_ref[...], preferred_element_type=F32).astype(o_ref.dtype)
        return
    even = (lax.broadcasted_iota(jnp.int32, (tm, HEAD_DIM), 1) & 1) == 0
    hw = 2 * HEAD_DIM
    for tile in range(len(_ATTN_HEADS) // heads):
        @pl.when(pl.program_id(1) == tile)
        def _(tile=tile):
            cm1 = cm1_ref[...]
            sn = sn_ref[...]
            for half in range(heads // 2):
                acc = jnp.dot(h_s[...], w_ref[:, half * hw:(half + 1) * hw], preferred_element_type=F32)
                for hh in range(2):
                    col = half * hw + hh * HEAD_DIM
                    cfg = _ATTN_HEADS[tile * heads + half * 2 + hh]
                    out = _head_epilogue(acc[:, hh * HEAD_DIM:(hh + 1) * HEAD_DIM], cfg, gains_ref, cm1, sn, even)
                    o_ref[:, col:col + HEAD_DIM] = out.astype(o_ref.dtype)


def _fused_proj(xa, gain, modsel, w, n_ctx, out_dtype, head_gains=None, rope=None):
    b, t, d = xa.shape
    n = w.shape[1]
    tm = _pick(t, (768, 512, 384, 256))
    tpb = t // tm
    heads = 0 if head_gains is None else _PROJ_HEADS_PER_TILE
    tn = heads * HEAD_DIM if heads else _pick(n, (1152, 1024, 768, 512, 384, 256, 128))
    in_specs = [pl.BlockSpec((tm, d), lambda i, j: (i, 0)),
                pl.BlockSpec((1, d), lambda i, j: (0, 0)),
                pl.BlockSpec((1, 2, N_MOD, d), lambda i, j: (i // tpb, 0, 0, 0)),
                pl.BlockSpec((d, tn), lambda i, j: (0, j))]
    args = [xa.reshape(b * t, d), gain.reshape(1, d), modsel, w]
    if heads:
        in_specs += [pl.BlockSpec((2, HEAD_DIM), lambda i, j: (0, 0)),
                     pl.BlockSpec((tm, HEAD_DIM), lambda i, j: (i % tpb, 0)),
                     pl.BlockSpec((tm, HEAD_DIM), lambda i, j: (i % tpb, 0))]
        args += [head_gains, rope[0], rope[1]]
    return pl.pallas_call(
        functools.partial(_proj_kernel, tiles_per_batch=tpb, n_ctx=n_ctx, heads=heads),
        grid=(b * tpb, n // tn),
        in_specs=in_specs,
        out_specs=pl.BlockSpec((tm, tn), lambda i, j: (i, j)),
        out_shape=jax.ShapeDtypeStruct((b * t, n), out_dtype),
        scratch_shapes=[pltpu.VMEM((tm, d), BF16)],
        compiler_params=_params(("parallel", "arbitrary")),
        name="norm_mod_proj",
    )(*args)


def _out_proj_kernel(y_ref, w_ref, xa_ref, mod_ref, gain_ref, xo_ref, fp_ref, *, tiles_per_batch, n_ctx):
    tm = y_ref.shape[1]
    is_ctx = _is_ctx_rows(tm, tiles_per_batch, n_ctx)
    acc = jnp.dot(y_ref[0], w_ref[...], preferred_element_type=F32)
    x_new = xa_ref[0] + _segment_rows(mod_ref, 2, is_ctx) * acc
    xo_ref[0] = x_new
    f = _norm_mod(x_new, gain_ref[...], _segment_rows(mod_ref, 3, is_ctx), _segment_rows(mod_ref, 4, is_ctx))
    fp_ref[0] = _pack_bf16_pairs(f)


def _fused_out_proj(y, w, xa, xa_row0, modsel, gain, n_ctx):
    b, tq, k = y.shape
    d = w.shape[1]
    tm = _pick(tq, (384, 256)) if n_ctx else _pick(tq, (256, 128))
    assert xa_row0 % tm == 0
    r0 = xa_row0 // tm
    tpb = tq // tm
    tile = pl.BlockSpec((1, tm, d), lambda i: (i // tpb, i % tpb, 0))
    return pl.pallas_call(
        functools.partial(_out_proj_kernel, tiles_per_batch=tpb, n_ctx=n_ctx),
        grid=(b * tpb,),
        in_specs=[pl.BlockSpec((1, tm, k), lambda i: (i // tpb, i % tpb, 0)),
                  pl.BlockSpec((k, d), lambda i: (0, 0)),
                  pl.BlockSpec((1, tm, d), lambda i: (i // tpb, r0 + i % tpb, 0)),
                  pl.BlockSpec((1, 2, N_MOD, d), lambda i: (i // tpb, 0, 0, 0)),
                  pl.BlockSpec((1, d), lambda i: (0, 0))],
        out_specs=[tile, pl.BlockSpec((1, tm, d // 2), lambda i: (i // tpb, i % tpb, 0))],
        out_shape=[jax.ShapeDtypeStruct((b, tq, d), F32), jax.ShapeDtypeStruct((b, tq, d // 2), jnp.uint32)],
        compiler_params=_params(("parallel",)),
        name="out_proj_residual_norm",
    )(y, w, xa, modsel, gain.reshape(1, d))


def _softmax_rows(s):
    m = jnp.max(s, axis=-1, keepdims=True)
    p = jnp.exp(s - m)
    return p, jnp.sum(p, axis=-1, keepdims=True)


def _gqa_heads(q_ref, k, v, o_ref):
    for g in range(A_GROUP):
        q = q_ref[0, :, g * HEAD_DIM:(g + 1) * HEAD_DIM]
        s = lax.dot_general(q, k, (((1,), (1,)), ((), ())), preferred_element_type=F32)
        p, l = _softmax_rows(s)
        o = jnp.dot(p.astype(BF16), v, preferred_element_type=F32) / l
        o_ref[0, :, g * HEAD_DIM:(g + 1) * HEAD_DIM] = o.astype(o_ref.dtype)


def _gqa_kernel(q_ref, k_ref, v_ref, o_ref, *, n_ctx):
    @pl.when(pl.program_id(2) == 0)
    def _():
        _gqa_heads(q_ref, k_ref[0, :n_ctx], v_ref[0, :n_ctx], o_ref)

    @pl.when(pl.program_id(2) > 0)
    def _():
        _gqa_heads(q_ref, k_ref[0], v_ref[0], o_ref)


def _gqa_attention(qkv, n_ctx):
    b, t, _ = qkv.shape
    tq = n_ctx
    gw = A_GROUP * HEAD_DIM
    k0 = QA_W // HEAD_DIM
    v0 = (QA_W + KA_W) // HEAD_DIM
    return pl.pallas_call(
        functools.partial(_gqa_kernel, n_ctx=n_ctx),
        grid=(b, A_KV_HEADS, t // tq),
        in_specs=[pl.BlockSpec((1, tq, gw), lambda bi, h, i: (bi, i, h)),
                  pl.BlockSpec((1, t, HEAD_DIM), lambda bi, h, i: (bi, 0, k0 + h)),
                  pl.BlockSpec((1, t, HEAD_DIM), lambda bi, h, i: (bi, 0, v0 + h))],
        out_specs=pl.BlockSpec((1, tq, gw), lambda bi, h, i: (bi, i, h)),
        out_shape=jax.ShapeDtypeStruct((b, t, QA_W + QB_W), BF16),
        compiler_params=_params(("parallel", "parallel", "arbitrary")),
        name="gqa_attention",
    )(qkv, qkv, qkv)


def _diff_head(lam, gain_ref, q_ref, k, v, o_ref, out_scale):
    parts = []
    for m in range(2):
        q = q_ref[0, :, m * HEAD_DIM:(m + 1) * HEAD_DIM]
        s = lax.dot_general(q, k[:, m * HEAD_DIM:(m + 1) * HEAD_DIM], (((1,), (1,)), ((), ())),
                            preferred_element_type=F32)
        p, l = _softmax_rows(s)
        parts.append(p / l)
    a = parts[0] - lam * parts[1]
    y = jnp.dot(a.astype(BF16), v, preferred_element_type=F32)
    y = y * lax.rsqrt(jnp.mean(y * y, axis=-1, keepdims=True) + NORM_EPS)
    o_ref[0] = (y * gain_ref[...] * out_scale).astype(o_ref.dtype)


def _diff_kernel(lam_ref, gain_ref, q_ref, k_ref, v_ref, y_in_ref, o_ref, *, out_scale, n_ctx):
    del y_in_ref
    lam = lam_ref[0, 0]

    @pl.when(pl.program_id(2) == 0)
    def _():
        _diff_head(lam, gain_ref, q_ref, k_ref[0, :n_ctx], v_ref[0, :n_ctx], o_ref, out_scale)

    @pl.when(pl.program_id(2) > 0)
    def _():
        _diff_head(lam, gain_ref, q_ref, k_ref[0], v_ref[0], o_ref, out_scale)


def _diff_attention(qkv, y, lam, subln_gain, out_scale, n_ctx):
    b, t, _ = qkv.shape
    tq = n_ctx
    hw = 2 * HEAD_DIM
    q0 = (QA_W + 2 * KA_W) // hw
    k0 = q0 + B_HEADS
    v0 = k0 + B_HEADS
    o0 = QA_W // hw
    return pl.pallas_call(
        functools.partial(_diff_kernel, out_scale=out_scale, n_ctx=n_ctx),
        grid=(b, B_HEADS, t // tq),
        in_specs=[pl.BlockSpec(memory_space=pltpu.SMEM),
                  pl.BlockSpec((1, hw), lambda bi, h, i: (0, 0)),
                  pl.BlockSpec((1, tq, hw), lambda bi, h, i: (bi, i, q0 + h)),
                  pl.BlockSpec((1, t, hw), lambda bi, h, i: (bi, 0, k0 + h)),
                  pl.BlockSpec((1, t, hw), lambda bi, h, i: (bi, 0, v0 + h)),
                  pl.BlockSpec(memory_space=pl.ANY)],
        out_specs=pl.BlockSpec((1, tq, hw), lambda bi, h, i: (bi, i, o0 + h)),
        out_shape=jax.ShapeDtypeStruct(y.shape, y.dtype),
        input_output_aliases={5: 0},
        compiler_params=_params(("parallel", "parallel", "arbitrary")),
        name="diff_attention",
    )(lam, subln_gain, qkv, qkv, qkv, y)


def _split3(a):
    a1 = a.astype(BF16)
    r1 = a - a1.astype(F32)
    a2 = r1.astype(BF16)
    a3 = (r1 - a2.astype(F32)).astype(BF16)
    return a1, a2, a3


def _dot_exact_rhs(a, rhs01):
    out = None
    for part in _split3(a):
        d = jnp.dot(part, rhs01, preferred_element_type=F32)
        out = d if out is None else out + d
    return out


def _dot_exact_lhs(lhs01, a):
    out = None
    for part in _split3(a):
        d = jnp.dot(lhs01, part, preferred_element_type=F32)
        out = d if out is None else out + d
    return out


SSD_GROUPS_PER_STEP = 8


def _ssd_group(d_sign, acr, acc, skip, x_tok, b_tok, c_tok, dt_c, dt_r, state):
    q = SSD_CHUNK
    rows = lax.broadcasted_iota(jnp.int32, (q, q), 0)
    cols = lax.broadcasted_iota(jnp.int32, (q, q), 1)
    signed = (rows - cols) * d_sign
    keep_sl = signed <= 0
    tri_ks01 = jnp.where(keep_sl, 1.0, 0.0).astype(BF16)
    tri_sk01 = jnp.where(signed >= 0, 1.0, 0.0).astype(BF16)

    x = jnp.transpose(x_tok)
    bm = b_tok.astype(BF16)
    cm = c_tok.astype(BF16)
    a_r = dt_r * acc
    a_c = dt_c * acr
    cum_r = _dot_exact_rhs(a_r, tri_ks01)
    cum_c = _dot_exact_lhs(tri_sk01, a_c)
    tot = jnp.sum(a_r, axis=1, keepdims=True)
    to_end_r = jnp.exp(tot - cum_r)
    from_start_r = jnp.exp(cum_r)
    tot_e = jnp.exp(tot)

    g_sl = lax.dot_general(bm, cm, (((1,), (1,)), ((), ())), preferred_element_type=F32)
    y_off = lax.dot_general(state.astype(BF16), cm, (((1,), (1,)), ((), ())), preferred_element_type=F32)

    xw_parts, y_parts = [], []
    for h in range(SSD_HPG):
        sl = slice(h * SSD_HEADDIM, (h + 1) * SSD_HEADDIM)
        xh = x[sl, :]
        xd = xh * dt_r[h:h + 1, :]
        seg = cum_r[h:h + 1, :] - cum_c[:, h:h + 1]
        decay = jnp.exp(jnp.where(keep_sl, seg, -1e30))
        m_h = (g_sl * decay).astype(BF16)
        y_h = jnp.dot(xd.astype(BF16), m_h, preferred_element_type=F32)
        y_parts.append(y_h + y_off[sl, :] * from_start_r[h:h + 1, :] + skip[h:h + 1, :] * xh)
        xw_parts.append((xd * to_end_r[h:h + 1, :]).astype(BF16))
    s_new = jnp.dot(jnp.concatenate(xw_parts, axis=0), bm, preferred_element_type=F32)
    decay_rows = jnp.concatenate([jnp.broadcast_to(tot_e[h:h + 1, :], (SSD_HEADDIM, 1)) for h in range(SSD_HPG)], axis=0)
    return jnp.transpose(jnp.concatenate(y_parts, axis=0)), state * decay_rows + s_new


def _ssd_kernel(acr_ref, acc_ref, skip_ref, x_ref, b_ref, c_ref, dtc_ref, dtr_ref, y_ref, state_ref):
    @pl.when(pl.program_id(3) == 0)
    def _():
        state_ref[...] = jnp.zeros_like(state_ref)

    d_sign = 1 - 2 * pl.program_id(1)
    for g in range(SSD_GROUPS_PER_STEP):
        xs = slice(g * SSD_GROUP_W, (g + 1) * SSD_GROUP_W)
        ns = slice(g * SSD_STATE, (g + 1) * SSD_STATE)
        y, state = _ssd_group(d_sign, acr_ref[0, g], acc_ref[0, g], skip_ref[0, g], x_ref[0, :, xs], b_ref[0, :, ns],
                              c_ref[0, :, ns], dtc_ref[0, 0, g], dtr_ref[0, 0, g], state_ref[g])
        y_ref[0, 0, :, xs] = y.astype(y_ref.dtype)
        state_ref[g] = state


def _ssd_scan(xbc, dt_c, dt_r, a_row, a_col, skip, n_ctx, out_dtype):
    b, t, _ = xbc.shape
    gs = SSD_GROUPS_PER_STEP
    b0 = SSD_INNER // (gs * SSD_STATE)
    c0 = b0 + SSD_GROUPS // gs
    q = SSD_CHUNK
    ncc = n_ctx // q
    nch = t // q

    def chunk(d, s):
        back = jnp.where(s < ncc, ncc - 1 - s, nch - 1 - s + ncc)
        return jnp.where(d == 0, s, back)

    return pl.pallas_call(
        _ssd_kernel,
        grid=(b, 2, SSD_GROUPS // gs, nch),
        in_specs=[pl.BlockSpec((1, gs, 1, SSD_HPG), lambda bi, d, g, s: (d, g, 0, 0)),
                  pl.BlockSpec((1, gs, SSD_HPG, 1), lambda bi, d, g, s: (d, g, 0, 0)),
                  pl.BlockSpec((1, gs, SSD_HPG, 1), lambda bi, d, g, s: (d, g, 0, 0)),
                  pl.BlockSpec((1, q, gs * SSD_GROUP_W), lambda bi, d, g, s: (bi, chunk(d, s), g)),
                  pl.BlockSpec((1, q, gs * SSD_STATE), lambda bi, d, g, s: (bi, chunk(d, s), b0 + g)),
                  pl.BlockSpec((1, q, gs * SSD_STATE), lambda bi, d, g, s: (bi, chunk(d, s), c0 + g)),
                  pl.BlockSpec((1, 1, gs, q, SSD_HPG), lambda bi, d, g, s: (d, bi, g, chunk(d, s), 0)),
                  pl.BlockSpec((1, 1, gs, SSD_HPG, q), lambda bi, d, g, s: (d, bi, g, 0, chunk(d, s)))],
        out_specs=pl.BlockSpec((1, 1, q, gs * SSD_GROUP_W), lambda bi, d, g, s: (d, bi, chunk(d, s), g)),
        out_shape=jax.ShapeDtypeStruct((2, b, t, SSD_INNER), out_dtype),
        scratch_shapes=[pltpu.VMEM((gs, SSD_GROUP_W, SSD_STATE), F32)],
        compiler_params=_params(("parallel", "parallel", "parallel", "arbitrary")),
        name="ssd_scan",
    )(a_row, a_col, skip, xbc, xbc, xbc, dt_c, dt_r)


def _conv_silu_kernel(x_ref, w_ref, b_ref, o_ref, *, n_ctx):
    x = x_ref[0]
    t = x.shape[0]
    row = lax.broadcasted_iota(jnp.int32, (t, 1), 0)
    prev = jnp.where((row == 0) | (row == n_ctx), 0.0, pltpu.roll(x, 1, axis=0))
    nxt = jnp.where((row == n_ctx - 1) | (row == t - 1), 0.0, pltpu.roll(x, t - 1, axis=0))
    u = prev * w_ref[0:1, :] + x * w_ref[1:2, :] + nxt * w_ref[2:3, :] + b_ref[...]
    o_ref[0] = u * jax.nn.sigmoid(u)


def _conv_silu(proj, col0, conv_w, conv_b, n_ctx):
    b, t, _ = proj.shape
    tc = 512
    c0 = col0 // tc
    return pl.pallas_call(
        functools.partial(_conv_silu_kernel, n_ctx=n_ctx),
        grid=(b, SSD_CONV_DIM // tc),
        in_specs=[pl.BlockSpec((1, t, tc), lambda bi, j: (bi, 0, c0 + j)),
                  pl.BlockSpec((SSD_CONV, tc), lambda bi, j: (0, j)),
                  pl.BlockSpec((1, tc), lambda bi, j: (0, j))],
        out_specs=pl.BlockSpec((1, t, tc), lambda bi, j: (bi, 0, j)),
        out_shape=jax.ShapeDtypeStruct((b, t, SSD_CONV_DIM), F32),
        compiler_params=_params(("parallel", "parallel")),
        name="ssd_conv_silu",
    )(proj, conv_w, conv_b.reshape(1, -1))


def _gate_norm_kernel(y_ref, z_ref, gain_ref, o_ref):
    z = z_ref[0]
    g = (y_ref[0, 0] + y_ref[1, 0]) * (z * jax.nn.sigmoid(z))
    o_ref[0] = (g * lax.rsqrt(jnp.mean(g * g, axis=-1, keepdims=True) + NORM_EPS) * gain_ref[...]).astype(o_ref.dtype)


def _gate_norm(y, proj, gain, n_ctx):
    _, b, t, c = y.shape
    tt = 256
    assert n_ctx % tt == 0
    r0 = n_ctx // tt
    s = t - n_ctx
    return pl.pallas_call(
        _gate_norm_kernel,
        grid=(b, s // tt),
        in_specs=[pl.BlockSpec((2, 1, tt, c), lambda bi, i: (0, bi, r0 + i, 0)),
                  pl.BlockSpec((1, tt, c), lambda bi, i: (bi, r0 + i, 0)),
                  pl.BlockSpec((1, c), lambda bi, i: (0, 0))],
        out_specs=pl.BlockSpec((1, tt, c), lambda bi, i: (bi, i, 0)),
        out_shape=jax.ShapeDtypeStruct((b, s, c), BF16),
        compiler_params=_params(("parallel", "parallel")),
        name="ssd_gate_norm",
    )(y, proj, gain.reshape(1, c))


def _route_kernel(f_ref, rwt_ref, bias_ref, idx_ref, gate_ref, rank_ref, cnt_ref, carry_ref):
    tn = f_ref.shape[0]
    ne, ng, pg = N_EXPERTS, N_EXPERT_GROUPS, N_EXPERTS // N_EXPERT_GROUPS
    neg = -jnp.inf

    @pl.when(pl.program_id(0) == 0)
    def _():
        carry_ref[...] = jnp.zeros_like(carry_ref)

    dn = (((1,), (1,)), ((), ()))
    lo, hi = _unpack_bf16_pairs(f_ref[...])
    f = jnp.concatenate([lo.astype(BF16), hi.astype(BF16)], axis=1)
    w = rwt_ref[...]
    w1 = w.astype(BF16)
    w2 = (w - w1.astype(F32)).astype(BF16)
    logits = (lax.dot_general(w1, f, dn, preferred_element_type=F32)
              + lax.dot_general(w2, f, dn, preferred_element_type=F32))
    scores = jax.nn.sigmoid(logits)
    g3 = (scores + bias_ref[...]).reshape(ng, pg, tn)
    io3 = lax.broadcasted_iota(jnp.int32, (ng, pg, tn), 1)
    m1 = jnp.max(g3, axis=1, keepdims=True)
    i1 = jnp.min(jnp.where(g3 == m1, io3, pg), axis=1, keepdims=True)
    m2 = jnp.max(jnp.where(io3 == i1, neg, g3), axis=1, keepdims=True)
    work = (m1 + m2).reshape(ng, tn)
    iog = lax.broadcasted_iota(jnp.int32, (ng, tn), 0)
    ok = jnp.zeros((ng, tn), F32)
    for _ in range(TOPK_GROUPS):
        m = jnp.max(work, axis=0, keepdims=True)
        gi = jnp.min(jnp.where(work == m, iog, ng), axis=0, keepdims=True)
        hit = iog == gi
        ok = jnp.where(hit, 1.0, ok)
        work = jnp.where(hit, neg, work)
    sel = jnp.where(ok.reshape(ng, 1, tn) > 0.0, g3, neg).reshape(ne, tn)
    ioe = lax.broadcasted_iota(jnp.int32, (ne, tn), 0)
    onehot = jnp.zeros((ne, tn), F32)
    idxs, ws = [], []
    for _ in range(TOP_K):
        m = jnp.max(sel, axis=0, keepdims=True)
        ei = jnp.min(jnp.where(sel == m, ioe, ne), axis=0, keepdims=True)
        hit = ioe == ei
        idxs.append(ei)
        ws.append(jnp.sum(jnp.where(hit, scores, 0.0), axis=0, keepdims=True))
        sel = jnp.where(hit, neg, sel)
        onehot = jnp.where(hit, 1.0, onehot)
    w = jnp.concatenate(ws, axis=0)
    gate_ref[...] = w / jnp.sum(w, axis=0, keepdims=True) * ROUTED_SCALE
    idx_ref[...] = jnp.concatenate(idxs, axis=0)
    r = lax.broadcasted_iota(jnp.int32, (tn, tn), 0)
    c = lax.broadcasted_iota(jnp.int32, (tn, tn), 1)
    ahead = jnp.where(r < c, 1.0, 0.0).astype(BF16)
    cum = carry_ref[...] + jnp.dot(onehot.astype(BF16), ahead, preferred_element_type=F32)
    ranks = [jnp.sum(jnp.where(ioe == idxs[k], cum, 0.0), axis=0, keepdims=True) for k in range(TOP_K)]
    rank_ref[...] = jnp.concatenate(ranks, axis=0).astype(jnp.int32)
    total = carry_ref[...] + jnp.sum(onehot, axis=1, keepdims=True)
    carry_ref[...] = total
    cnt_ref[...] = total.astype(jnp.int32)


def _route(fp, router_wt, router_bias):
    t = fp.shape[0]
    d = router_wt.shape[1]
    tn = _pick(t, (512, 256, 128))
    kt = pl.BlockSpec((TOP_K, tn), lambda i: (0, i))
    return pl.pallas_call(
        _route_kernel,
        grid=(t // tn,),
        in_specs=[pl.BlockSpec((tn, d // 2), lambda i: (i, 0)),
                  pl.BlockSpec((N_EXPERTS, d), lambda i: (0, 0)),
                  pl.BlockSpec((N_EXPERTS, 1), lambda i: (0, 0))],
        out_specs=[kt, kt, kt, pl.BlockSpec((N_EXPERTS, 1), lambda i: (0, 0))],
        out_shape=[jax.ShapeDtypeStruct((TOP_K, t), jnp.int32), jax.ShapeDtypeStruct((TOP_K, t), F32),
                   jax.ShapeDtypeStruct((TOP_K, t), jnp.int32), jax.ShapeDtypeStruct((N_EXPERTS, 1), jnp.int32)],
        scratch_shapes=[pltpu.VMEM((N_EXPERTS, 1), F32)],
        compiler_params=_params(("arbitrary",)),
        name="moe_route",
    )(fp, router_wt, router_bias)


def _scatter_kernel(dest_ref, f_ref, xs_in_ref, xs_ref, sem):
    del xs_in_ref
    ts = f_ref.shape[0]

    def row_copy(t, k):
        return pltpu.make_async_copy(f_ref.at[pl.ds(t, 1)], xs_ref.at[pl.ds(dest_ref[t * TOP_K + k], 1)], sem)

    def issue(t, carry):
        for k in range(TOP_K):
            row_copy(t, k).start(priority=k % 2)
        return carry

    def drain(t, carry):
        for k in range(TOP_K):
            row_copy(t, k).wait()
        return carry

    lax.fori_loop(0, ts, issue, 0)
    lax.fori_loop(0, ts, drain, 0)


def _scatter_rows(dest_flat, f, xs):
    t, d = f.shape
    ts = _pick(t, (512, 256, 128))
    return pl.pallas_call(
        _scatter_kernel,
        grid=(t // ts,),
        in_specs=[pl.BlockSpec((ts * TOP_K,), lambda i: (i,), memory_space=pltpu.SMEM),
                  pl.BlockSpec((ts, d), lambda i: (i, 0)),
                  pl.BlockSpec(memory_space=pl.ANY)],
        out_specs=pl.BlockSpec(memory_space=pl.ANY),
        out_shape=jax.ShapeDtypeStruct(xs.shape, xs.dtype),
        scratch_shapes=[pltpu.SemaphoreType.DMA(())],
        input_output_aliases={2: 0},
        compiler_params=_params(("arbitrary",)),
        name="moe_scatter",
    )(dest_flat, f, xs)


def _zero_blocks_kernel(rows_ref, xs_ref, zero_ref, sem):
    tm = zero_ref.shape[0]
    zero_ref[...] = jnp.zeros_like(zero_ref)

    def block_copy(e):
        return pltpu.make_async_copy(zero_ref, xs_ref.at[pl.ds(pl.multiple_of(rows_ref[e], tm), tm)], sem)

    for e in range(N_EXPERTS):
        block_copy(e).start()
    for e in range(N_EXPERTS):
        block_copy(e).wait()


def _zero_blocks(block_rows, n_rows_alloc, tm, d):
    grid_spec = pltpu.PrefetchScalarGridSpec(
        num_scalar_prefetch=1, grid=(1,), in_specs=[],
        out_specs=pl.BlockSpec(memory_space=pl.ANY),
        scratch_shapes=[pltpu.VMEM((tm, d), jnp.uint32), pltpu.SemaphoreType.DMA(())])
    return pl.pallas_call(
        _zero_blocks_kernel,
        grid_spec=grid_spec,
        out_shape=jax.ShapeDtypeStruct((n_rows_alloc, d), jnp.uint32),
        compiler_params=_params(("arbitrary",)),
        name="moe_zero_tail_blocks",
    )(block_rows)


def _expert_kernel(be_ref, blk_ref, nused_ref, x_ref, wg_ref, wu_ref, wd_ref, o_ref, wg_s, wu_s, wd_s):
    i = pl.program_id(0)

    @pl.when(jnp.logical_or(i == 0, be_ref[i] != be_ref[jnp.maximum(i - 1, 0)]))
    def _():
        wg_s[...] = wg_ref[0, 0].astype(BF16)
        wu_s[...] = wu_ref[0, 0].astype(BF16)
        wd_s[...] = wd_ref[0, 0].astype(BF16)

    @pl.when(i < nused_ref[0])
    def _():
        lo, hi = _unpack_bf16_pairs(x_ref[...])
        x = jnp.concatenate([lo.astype(BF16), hi.astype(BF16)], axis=1)
        hg = jnp.dot(x, wg_s[...], preferred_element_type=F32)
        hu = jnp.dot(x, wu_s[...], preferred_element_type=F32)
        h = (hg * jax.nn.sigmoid(hg) * hu).astype(BF16)
        o_ref[...] = _pack_bf16_pairs(jnp.dot(h, wd_s[...], preferred_element_type=F32))


def _expert_blocks(block_expert, block_index, n_used, xs, n_blocks, tm, layer, w_gate, w_up, w_down):
    d, ff = w_gate.shape[2:]
    grid_spec = pltpu.PrefetchScalarGridSpec(
        num_scalar_prefetch=3,
        grid=(n_blocks,),
        in_specs=[pl.BlockSpec((tm, d // 2), lambda i, be, blk, nu: (blk[i], 0)),
                  pl.BlockSpec((1, 1, d, ff), lambda i, be, blk, nu: (layer, be[i], 0, 0)),
                  pl.BlockSpec((1, 1, d, ff), lambda i, be, blk, nu: (layer, be[i], 0, 0)),
                  pl.BlockSpec((1, 1, ff, d), lambda i, be, blk, nu: (layer, be[i], 0, 0))],
        out_specs=pl.BlockSpec((tm, d // 2), lambda i, be, blk, nu: (blk[i], 0)),
        scratch_shapes=[pltpu.VMEM((d, ff), BF16), pltpu.VMEM((d, ff), BF16), pltpu.VMEM((ff, d), BF16)],
    )
    return pl.pallas_call(
        _expert_kernel,
        grid_spec=grid_spec,
        out_shape=jax.ShapeDtypeStruct((n_blocks * tm, d // 2), jnp.uint32),
        compiler_params=_params(("arbitrary",)),
        name="moe_experts",
    )(block_expert, block_index, n_used, xs, w_gate, w_up, w_down)


def _combine_kernel(dest_ref, gate_ref, sh_ref, xa_ref, mod_ref, ys_ref, o_ref, buf, sem, *, tiles_per_batch, n_ctx):
    tn = gate_ref.shape[0]

    def row_copy(t, k):
        return pltpu.make_async_copy(ys_ref.at[pl.ds(dest_ref[t * TOP_K + k], 1)], buf.at[k, pl.ds(t, 1)], sem)

    def issue(t, carry):
        for k in range(TOP_K):
            row_copy(t, k).start(priority=k % 2)
        return carry

    def drain(t, carry):
        for k in range(TOP_K):
            row_copy(t, k).wait()
        return carry

    lax.fori_loop(0, tn, issue, 0)
    lax.fori_loop(0, tn, drain, 0)
    half = buf.shape[2]
    acc_lo = sh_ref[:, :half]
    acc_hi = sh_ref[:, half:]
    for k in range(TOP_K):
        lo, hi = _unpack_bf16_pairs(buf[k])
        acc_lo = acc_lo + gate_ref[:, k:k + 1] * lo
        acc_hi = acc_hi + gate_ref[:, k:k + 1] * hi
    acc = jnp.concatenate([acc_lo, acc_hi], axis=1)
    is_ctx = _is_ctx_rows(tn, tiles_per_batch, n_ctx)
    o_ref[...] = xa_ref[...] + _segment_rows(mod_ref, 5, is_ctx) * acc


def _combine(dest_flat, gates, shared, ys, xa, modsel, rows_per_batch, n_ctx):
    t, d = shared.shape
    tn = 256
    tpb = rows_per_batch // tn
    return pl.pallas_call(
        functools.partial(_combine_kernel, tiles_per_batch=tpb, n_ctx=n_ctx),
        grid=(t // tn,),
        in_specs=[pl.BlockSpec((tn * TOP_K,), lambda i: (i,), memory_space=pltpu.SMEM),
                  pl.BlockSpec((tn, TOP_K), lambda i: (i, 0)),
                  pl.BlockSpec((tn, d), lambda i: (i, 0)),
                  pl.BlockSpec((tn, d), lambda i: (i, 0)),
                  pl.BlockSpec((1, 2, N_MOD, d), lambda i: (i // tpb, 0, 0, 0)),
                  pl.BlockSpec(memory_space=pl.ANY)],
        out_specs=pl.BlockSpec((tn, d), lambda i: (i, 0)),
        out_shape=jax.ShapeDtypeStruct((t, d), F32),
        scratch_shapes=[pltpu.VMEM((TOP_K, tn, d // 2), jnp.uint32), pltpu.SemaphoreType.DMA(())],
        compiler_params=_params(("arbitrary",)),
        name="moe_combine",
    )(dest_flat, gates, shared, xa, modsel, ys)


def _swiglu_kernel(x_ref, wg_ref, wu_ref, wd_ref, o_ref):
    lo, hi = _unpack_bf16_pairs(x_ref[...])
    x = jnp.concatenate([lo.astype(BF16), hi.astype(BF16)], axis=1)
    hg = jnp.dot(x, wg_ref[...], preferred_element_type=F32)
    hu = jnp.dot(x, wu_ref[...], preferred_element_type=F32)
    h = (hg * jax.nn.sigmoid(hg) * hu).astype(BF16)
    o_ref[...] = jnp.dot(h, wd_ref[...], preferred_element_type=F32).astype(o_ref.dtype)


def _shared_expert(x, wg, wu, wd):
    m = x.shape[0]
    d, ff = wg.shape
    tm = _pick(m, (512, 384, 256, 128))
    return pl.pallas_call(
        _swiglu_kernel,
        grid=(m // tm,),
        in_specs=[pl.BlockSpec((tm, d // 2), lambda i: (i, 0)),
                  pl.BlockSpec((d, ff), lambda i: (0, 0)),
                  pl.BlockSpec((d, ff), lambda i: (0, 0)),
                  pl.BlockSpec((ff, d), lambda i: (0, 0))],
        out_specs=pl.BlockSpec((tm, d), lambda i: (i, 0)),
        out_shape=jax.ShapeDtypeStruct((m, d), F32),
        compiler_params=_params(("parallel",)),
        name="shared_expert",
    )(x, wg, wu, wd)


def _moe_ffn(fp, xa, modsel, rows_per_batch, n_ctx, layer, router_w, router_bias, w_gate, w_up, w_down,
             ws_gate, ws_up, ws_down):
    t, d = xa.shape
    tm = EXPERT_TM
    idx, gate, rank, cnt = _route(fp, router_w.T, router_bias.astype(F32)[:, None])
    counts = cnt[:, 0]
    padded = (counts + tm - 1) // tm * tm
    pad_end = jnp.cumsum(padded)
    pad_start = pad_end - padded
    n_used = pad_end[-1] // tm
    n_blocks = (t * TOP_K) // tm + N_EXPERTS
    n_rows = n_blocks * tm
    experts = jnp.arange(N_EXPERTS, dtype=jnp.int32)
    start_of = jnp.sum(jnp.where(idx[..., None] == experts, pad_start, 0), axis=-1)
    dest = (start_of + rank).T.reshape(-1).astype(jnp.int32)
    block_index = jnp.minimum(jnp.arange(n_blocks, dtype=jnp.int32), n_used - 1).astype(jnp.int32)
    block_expert = jnp.minimum(jnp.sum(pad_end[None, :] <= (block_index * tm)[:, None], axis=1),
                               N_EXPERTS - 1).astype(jnp.int32)
    tail_rows = jnp.where(padded > counts, pad_end - tm, n_rows + experts * tm).astype(jnp.int32)
    xs = _zero_blocks(tail_rows, n_rows + N_EXPERTS * tm, tm, d // 2)
    xs = _scatter_rows(dest, fp, xs)
    ys = _expert_blocks(block_expert, block_index, n_used.reshape(1).astype(jnp.int32), xs, n_blocks, tm, layer,
                        w_gate, w_up, w_down)
    shared = _shared_expert(fp, ws_gate.astype(BF16), ws_up.astype(BF16), ws_down.astype(BF16))
    return _combine(dest, gate.T, shared, ys, xa, modsel, rows_per_batch, n_ctx)


def _rms(u, gain):
    return u * lax.rsqrt(jnp.mean(u * u, axis=-1, keepdims=True) + NORM_EPS) * gain


def _rope_tables(n_ctx, n_lat):
    rows = n_lat // GRID_W
    row = jnp.repeat(jnp.arange(rows, dtype=F32), GRID_W)
    col = jnp.tile(jnp.arange(GRID_W, dtype=F32), rows)
    n_freq = HEAD_DIM // 4
    inv = ROPE_THETA ** (-jnp.arange(n_freq, dtype=F32) / n_freq)
    ang = jnp.concatenate([row[:, None] * inv, col[:, None] * inv], axis=-1)
    cos = jnp.repeat(jnp.cos(ang), 2, axis=-1)
    sin = jnp.repeat(jnp.sin(ang), 2, axis=-1)
    sign = jnp.tile(jnp.array([-1.0, 1.0], F32), HEAD_DIM // 2)
    cos = jnp.concatenate([jnp.ones((n_ctx, HEAD_DIM), F32), cos], axis=0)
    sin = jnp.concatenate([jnp.zeros((n_ctx, HEAD_DIM), F32), sin * sign], axis=0)
    return cos, sin


def _ada_mod(cond, w, bias):
    m = jax.nn.silu(cond)
    rows = m.shape[0]
    pad = (-rows) % 16
    mp = jnp.pad(m, ((0, pad), (0, 0))).astype(BF16)
    out = _matmul(mp, w.astype(BF16), F32)[:rows] + bias
    return out.reshape(rows, N_MOD, -1)


def _attention_layer(xa, modsel, gain, n_ctx, w_in, q_gain, k_gain, lq1, lk1, lq2, lk2, subln, lambda_init):
    b, t, d = xa.shape
    cos, sin = _rope_tables(n_ctx, t - n_ctx)
    head_gains = jnp.stack([q_gain * _Q_SCALE, k_gain]).astype(F32)
    qkv = _fused_proj(xa, gain, modsel, w_in.astype(BF16), n_ctx, BF16,
                      head_gains=head_gains, rope=(cos - 1.0, sin)).reshape(b, t, ATTN_IN)
    lam = (jnp.exp(jnp.sum(lq1 * lk1)) - jnp.exp(jnp.sum(lq2 * lk2)) + lambda_init).reshape(1, 1).astype(F32)
    y = _gqa_attention(qkv, n_ctx)
    return _diff_attention(qkv, y, lam, subln.reshape(1, 2 * HEAD_DIM), 1.0 - lambda_init, n_ctx)


def _ssd_layer(xa, modsel, gain, n_ctx, w_in, conv_w, conv_b, dt_bias, a_log, d_skip, norm_gain):
    b, t, d = xa.shape
    proj = _fused_proj(xa, gain, modsel, w_in.astype(BF16), n_ctx, F32).reshape(b, t, -1)
    xbc = _conv_silu(proj, SSD_INNER, conv_w, conv_b, n_ctx)
    dt = proj[..., SSD_INNER + SSD_CONV_DIM:]
    dt = jax.nn.softplus(dt.reshape(b, t, 2, SSD_GROUPS, SSD_HPG) + dt_bias.reshape(2, SSD_GROUPS, SSD_HPG))
    dt_c = jnp.transpose(dt, (2, 0, 3, 1, 4))
    dt_r = jnp.transpose(dt, (2, 0, 3, 4, 1))
    a_coef = -jnp.exp(a_log)
    a_row = a_coef.reshape(2, SSD_GROUPS, 1, SSD_HPG)
    a_col = a_coef.reshape(2, SSD_GROUPS, SSD_HPG, 1)
    skip = d_skip.reshape(2, SSD_GROUPS, SSD_HPG, 1)
    y = _ssd_scan(xbc, dt_c, dt_r, a_row, a_col, skip, n_ctx, F32)
    return _gate_norm(y, proj, norm_gain, n_ctx)


def kernel(x, c, ctx, c_ctx, mod_w, mod_b, norm_mix, norm_ffn, norm_final, attn_w_in, attn_w_out, attn_q_gain,
           attn_k_gain, diff_lam_q1, diff_lam_k1, diff_lam_q2, diff_lam_k2, diff_subln, ssd_w_in, ssd_conv_w,
           ssd_conv_b, ssd_dt_bias, ssd_a_log, ssd_d, ssd_norm, ssd_w_out, router_w, router_bias, exp_w_gate,
           exp_w_up, exp_w_down, shared_w_gate, shared_w_up, shared_w_down):
    b, s, d = x.shape
    n_ctx = ctx.shape[1]
    t = n_ctx + s
    xa = jnp.concatenate([ctx, x], axis=1)
    cond = jnp.concatenate([c, c_ctx[None]], axis=0)
    out = None
    for i in range(DEPTH):
        last = i == DEPTH - 1
        mod = _ada_mod(cond, mod_w[i], mod_b[i])
        modsel = jnp.stack([jnp.broadcast_to(mod[b], (b, N_MOD, d)), mod[:b]], axis=1)
        j = i // 2
        moe_w = (i, router_w[i], router_bias[i], exp_w_gate, exp_w_up, exp_w_down,
                 shared_w_gate[i], shared_w_up[i], shared_w_down[i])
        if i % 2 == 0:
            lambda_init = 0.8 - 0.6 * math.exp(-0.3 * i)
            y = _attention_layer(xa, modsel, norm_mix[i], n_ctx, attn_w_in[j], attn_q_gain[j], attn_k_gain[j],
                                 diff_lam_q1[j], diff_lam_k1[j], diff_lam_q2[j], diff_lam_k2[j],
                                 diff_subln[j], lambda_init)
            w_out = attn_w_out[j]
        else:
            y = _ssd_layer(xa, modsel, norm_mix[i], n_ctx, ssd_w_in[j], ssd_conv_w[j], ssd_conv_b[j],
                           ssd_dt_bias[j], ssd_a_log[j], ssd_d[j], ssd_norm[j])
            w_out = ssd_w_out[j]
        if last:
            x_new, fp = _fused_out_proj(y if y.shape[1] == s else y[:, n_ctx:], w_out.astype(BF16), xa, n_ctx,
                                        modsel, norm_ffn[i], 0)
            out = _moe_ffn(fp.reshape(b * s, d // 2), x_new.reshape(b * s, d), modsel, s, 0, *moe_w).reshape(b, s, d)
        else:
            x_new, fp = _fused_out_proj(y, w_out.astype(BF16), xa, 0, modsel, norm_ffn[i], n_ctx)
            xa = _moe_ffn(fp.reshape(b * t, d // 2), x_new.reshape(b * t, d), modsel, t, n_ctx,
                          *moe_w).reshape(b, t, d)
    return _rms(out, norm_final)
```

```python
import functools
import math

import jax
import jax.numpy as jnp
from jax import lax
from jax.experimental import pallas as pl
from jax.experimental.pallas import tpu as pltpu

F32 = jnp.float32
BF16 = jnp.bfloat16

D_MODEL = 2048
DEPTH = 2
GRID_W = 64
NORM_EPS = 1e-6
N_MOD = 6
HEAD_DIM = 128
ROPE_THETA = 10000.0
A_Q_HEADS = 8
A_KV_HEADS = 2
A_GROUP = A_Q_HEADS // A_KV_HEADS
B_HEADS = 4
QA_W = A_Q_HEADS * HEAD_DIM
KA_W = A_KV_HEADS * HEAD_DIM
QB_W = 2 * B_HEADS * HEAD_DIM
ATTN_IN = QA_W + 2 * KA_W + 3 * QB_W
SSD_INNER = 2 * D_MODEL
SSD_HEADDIM = 64
SSD_HEADS = SSD_INNER // SSD_HEADDIM
SSD_GROUPS = 8
SSD_HPG = SSD_HEADS // SSD_GROUPS
SSD_STATE = 128
SSD_CONV = 3
SSD_CHUNK = 128
SSD_GROUP_W = SSD_HPG * SSD_HEADDIM
SSD_CONV_DIM = SSD_INNER + 2 * SSD_GROUPS * SSD_STATE
N_EXPERTS = 64
EXPERT_FF = 512
TOP_K = 8
N_EXPERT_GROUPS = 8
TOPK_GROUPS = 4
ROUTED_SCALE = 2.5
EXPERT_TM = 512

VMEM_LIMIT_BYTES = 56 * 1024 * 1024


def _pick(n, prefs):
    for p in prefs:
        if n % p == 0:
            return p
    raise ValueError(f"no tile in {prefs} divides {n}")


def _params(sem):
    return pltpu.CompilerParams(dimension_semantics=sem, vmem_limit_bytes=VMEM_LIMIT_BYTES)


def _mm_kernel(x_ref, w_ref, o_ref):
    o_ref[...] = jnp.dot(x_ref[...], w_ref[...], preferred_element_type=F32).astype(o_ref.dtype)


def _matmul(x, w, out_dtype):
    m, k = x.shape
    n = w.shape[1]
    tm = _pick(m, (1024, 512, 384, 256, 128, 16, 8))
    tn = _pick(n, (1024, 512, 384, 256, 128))
    return pl.pallas_call(
        _mm_kernel,
        grid=(m // tm, n // tn),
        in_specs=[pl.BlockSpec((tm, k), lambda i, j: (i, 0)),
                  pl.BlockSpec((k, tn), lambda i, j: (0, j))],
        out_specs=pl.BlockSpec((tm, tn), lambda i, j: (i, j)),
        out_shape=jax.ShapeDtypeStruct((m, n), out_dtype),
        compiler_params=_params(("parallel", "arbitrary")),
        name="matmul",
    )(x, w)


def _pack_bf16_pairs(v):
    half = v.shape[1] // 2
    lo = lax.bitcast_convert_type(v[:, :half].astype(BF16).astype(F32), jnp.uint32) >> 16
    hi = lax.bitcast_convert_type(v[:, half:].astype(BF16).astype(F32), jnp.uint32) & jnp.uint32(0xFFFF0000)
    return hi | lo


def _unpack_bf16_pairs(w):
    lo = lax.bitcast_convert_type(w << 16, F32)
    hi = lax.bitcast_convert_type(w & jnp.uint32(0xFFFF0000), F32)
    return lo, hi


def _segment_rows(mod_ref, k, is_ctx):
    return jnp.where(is_ctx, mod_ref[0, 0, k:k + 1, :], mod_ref[0, 1, k:k + 1, :])


def _is_ctx_rows(tm, tiles_per_batch, n_ctx):
    row = (pl.program_id(0) % tiles_per_batch) * tm + lax.broadcasted_iota(jnp.int32, (tm, 1), 0)
    return row < n_ctx


def _norm_mod(x, gain, shift, scale):
    xn = x * lax.rsqrt(jnp.mean(x * x, axis=-1, keepdims=True) + NORM_EPS) * gain
    return xn * (1.0 + scale) + shift


_Q_SCALE = HEAD_DIM ** -0.5
_ATTN_HEADS = ([(0, True, 1.0)] * A_Q_HEADS + [(1, True, 1.0)] * A_KV_HEADS + [(None, False, 1.0)] * A_KV_HEADS
               + [(None, True, _Q_SCALE)] * (2 * B_HEADS) + [(None, True, 1.0)] * (2 * B_HEADS)
               + [(None, False, 1.0)] * (2 * B_HEADS))
_PROJ_HEADS_PER_TILE = 4


def _head_epilogue(u, cfg, gains_ref, cm1, sn, even):
    gain_row, rope, scale = cfg
    if gain_row is not None:
        u = u * lax.rsqrt(jnp.mean(u * u, axis=-1, keepdims=True) + NORM_EPS) * gains_ref[gain_row:gain_row + 1, :]
    if scale != 1.0:
        u = u * scale
    if rope:
        partner = jnp.where(even, pltpu.roll(u, HEAD_DIM - 1, axis=1), pltpu.roll(u, 1, axis=1))
        u = u * (1.0 + cm1) + partner * sn
    return u


def _proj_kernel(x_ref, gain_ref, mod_ref, w_ref, *rest, tiles_per_batch, n_ctx, heads):
    if heads:
        gains_ref, cm1_ref, sn_ref, o_ref, h_s = rest
    else:
        o_ref, h_s = rest
    tm = x_ref.shape[0]
    is_ctx = _is_ctx_rows(tm, tiles_per_batch, n_ctx)

    @pl.when(pl.program_id(1) == 0)
    def _():
        h = _norm_mod(x_ref[...], gain_ref[...], _segment_rows(mod_ref, 0, is_ctx), _segment_rows(mod_ref, 1, is_ctx))
        h_s[...] = h.astype(BF16)

    if not heads:
        o_ref[...] = jnp.dot(h_s[...], w_ref[...], preferred_element_type=F32).astype(o_ref.dtype)
        return
    even = (lax.broadcasted_iota(jnp.int32, (tm, HEAD_DIM), 1) & 1) == 0
    hw = 2 * HEAD_DIM
    for tile in range(len(_ATTN_HEADS) // heads):
        @pl.when(pl.program_id(1) == tile)
        def _(tile=tile):
            cm1 = cm1_ref[...]
            sn = sn_ref[...]
            for half in range(heads // 2):
                acc = jnp.dot(h_s[...], w_ref[:, half * hw:(half + 1) * hw], preferred_element_type=F32)
                for hh in range(2):
                    col = half * hw + hh * HEAD_DIM
                    cfg = _ATTN_HEADS[tile * heads + half * 2 + hh]
                    out = _head_epilogue(acc[:, hh * HEAD_DIM:(hh + 1) * HEAD_DIM], cfg, gains_ref, cm1, sn, even)
                    o_ref[:, col:col + HEAD_DIM] = out.astype(o_ref.dtype)


def _fused_proj(xa, gain, modsel, w, n_ctx, out_dtype, head_gains=None, rope=None):
    b, t, d = xa.shape
    n = w.shape[1]
    tm = _pick(t, (768, 512, 384, 256))
    tpb = t // tm
    heads = 0 if head_gains is None else _PROJ_HEADS_PER_TILE
    tn = heads * HEAD_DIM if heads else _pick(n, (1152, 1024, 768, 512, 384, 256, 128))
    in_specs = [pl.BlockSpec((tm, d), lambda i, j: (i, 0)),
                pl.BlockSpec((1, d), lambda i, j: (0, 0)),
                pl.BlockSpec((1, 2, N_MOD, d), lambda i, j: (i // tpb, 0, 0, 0)),
                pl.BlockSpec((d, tn), lambda i, j: (0, j))]
    args = [xa.reshape(b * t, d), gain.reshape(1, d), modsel, w]
    if heads:
        in_specs += [pl.BlockSpec((2, HEAD_DIM), lambda i, j: (0, 0)),
                     pl.BlockSpec((tm, HEAD_DIM), lambda i, j: (i % tpb, 0)),
                     pl.BlockSpec((tm, HEAD_DIM), lambda i, j: (i % tpb, 0))]
        args += [head_gains, rope[0], rope[1]]
    return pl.pallas_call(
        functools.partial(_proj_kernel, tiles_per_batch=tpb, n_ctx=n_ctx, heads=heads),
        grid=(b * tpb, n // tn),
        in_specs=in_specs,
        out_specs=pl.BlockSpec((tm, tn), lambda i, j: (i, j)),
        out_shape=jax.ShapeDtypeStruct((b * t, n), out_dtype),
        scratch_shapes=[pltpu.VMEM((tm, d), BF16)],
        compiler_params=_params(("parallel", "arbitrary")),
        name="norm_mod_proj",
    )(*args)


def _out_proj_kernel(y_ref, w_ref, xa_ref, mod_ref, gain_ref, xo_ref, fp_ref, *, tiles_per_batch, n_ctx):
    tm = y_ref.shape[1]
    is_ctx = _is_ctx_rows(tm, tiles_per_batch, n_ctx)
    acc = jnp.dot(y_ref[0], w_ref[...], preferred_element_type=F32)
    x_new = xa_ref[0] + _segment_rows(mod_ref, 2, is_ctx) * acc
    xo_ref[0] = x_new
    f = _norm_mod(x_new, gain_ref[...], _segment_rows(mod_ref, 3, is_ctx), _segment_rows(mod_ref, 4, is_ctx))
    fp_ref[0] = _pack_bf16_pairs(f)


def _fused_out_proj(y, w, xa, xa_row0, modsel, gain, n_ctx):
    b, tq, k = y.shape
    d = w.shape[1]
    tm = _pick(tq, (384, 256)) if n_ctx else _pick(tq, (256, 128))
    assert xa_row0 % tm == 0
    r0 = xa_row0 // tm
    tpb = tq // tm
    tile = pl.BlockSpec((1, tm, d), lambda i: (i // tpb, i % tpb, 0))
    return pl.pallas_call(
        functools.partial(_out_proj_kernel, tiles_per_batch=tpb, n_ctx=n_ctx),
        grid=(b * tpb,),
        in_specs=[pl.BlockSpec((1, tm, k), lambda i: (i // tpb, i % tpb, 0)),
                  pl.BlockSpec((k, d), lambda i: (0, 0)),
                  pl.BlockSpec((1, tm, d), lambda i: (i // tpb, r0 + i % tpb, 0)),
                  pl.BlockSpec((1, 2, N_MOD, d), lambda i: (i // tpb, 0, 0, 0)),
                  pl.BlockSpec((1, d), lambda i: (0, 0))],
        out_specs=[tile, pl.BlockSpec((1, tm, d // 2), lambda i: (i // tpb, i % tpb, 0))],
        out_shape=[jax.ShapeDtypeStruct((b, tq, d), F32), jax.ShapeDtypeStruct((b, tq, d // 2), jnp.uint32)],
        compiler_params=_params(("parallel",)),
        name="out_proj_residual_norm",
    )(y, w, xa, modsel, gain.reshape(1, d))


def _softmax_rows(s):
    m = jnp.max(s, axis=-1, keepdims=True)
    p = jnp.exp(s - m)
    return p, jnp.sum(p, axis=-1, keepdims=True)


def _gqa_heads(q_ref, k, v, o_ref):
    for g in range(A_GROUP):
        q = q_ref[0, :, g * HEAD_DIM:(g + 1) * HEAD_DIM]
        s = lax.dot_general(q, k, (((1,), (1,)), ((), ())), preferred_element_type=F32)
        p, l = _softmax_rows(s)
        o = jnp.dot(p.astype(BF16), v, preferred_element_type=F32) / l
        o_ref[0, :, g * HEAD_DIM:(g + 1) * HEAD_DIM] = o.astype(o_ref.dtype)


def _gqa_kernel(q_ref, k_ref, v_ref, o_ref, *, n_ctx):
    @pl.when(pl.program_id(2) == 0)
    def _():
        _gqa_heads(q_ref, k_ref[0, :n_ctx], v_ref[0, :n_ctx], o_ref)

    @pl.when(pl.program_id(2) > 0)
    def _():
        _gqa_heads(q_ref, k_ref[0], v_ref[0], o_ref)


def _gqa_attention(qkv, n_ctx):
    b, t, _ = qkv.shape
    tq = n_ctx
    gw = A_GROUP * HEAD_DIM
    k0 = QA_W // HEAD_DIM
    v0 = (QA_W + KA_W) // HEAD_DIM
    return pl.pallas_call(
        functools.partial(_gqa_kernel, n_ctx=n_ctx),
        grid=(b, A_KV_HEADS, t // tq),
        in_specs=[pl.BlockSpec((1, tq, gw), lambda bi, h, i: (bi, i, h)),
                  pl.BlockSpec((1, t, HEAD_DIM), lambda bi, h, i: (bi, 0, k0 + h)),
                  pl.BlockSpec((1, t, HEAD_DIM), lambda bi, h, i: (bi, 0, v0 + h))],
        out_specs=pl.BlockSpec((1, tq, gw), lambda bi, h, i: (bi, i, h)),
        out_shape=jax.ShapeDtypeStruct((b, t, QA_W + QB_W), BF16),
        compiler_params=_params(("parallel", "parallel", "arbitrary")),
        name="gqa_attention",
    )(qkv, qkv, qkv)


def _diff_head(lam, gain_ref, q_ref, k, v, o_ref, out_scale):
    parts = []
    for m in range(2):
        q = q_ref[0, :, m * HEAD_DIM:(m + 1) * HEAD_DIM]
        s = lax.dot_general(q, k[:, m * HEAD_DIM:(m + 1) * HEAD_DIM], (((1,), (1,)), ((), ())),
                            preferred_element_type=F32)
        p, l = _softmax_rows(s)
        parts.append(p / l)
    a = parts[0] - lam * parts[1]
    y = jnp.dot(a.astype(BF16), v, preferred_element_type=F32)
    y = y * lax.rsqrt(jnp.mean(y * y, axis=-1, keepdims=True) + NORM_EPS)
    o_ref[0] = (y * gain_ref[...] * out_scale).astype(o_ref.dtype)


def _diff_kernel(lam_ref, gain_ref, q_ref, k_ref, v_ref, y_in_ref, o_ref, *, out_scale, n_ctx):
    del y_in_ref
    lam = lam_ref[0, 0]

    @pl.when(pl.program_id(2) == 0)
    def _():
        _diff_head(lam, gain_ref, q_ref, k_ref[0, :n_ctx], v_ref[0, :n_ctx], o_ref, out_scale)

    @pl.when(pl.program_id(2) > 0)
    def _():
        _diff_head(lam, gain_ref, q_ref, k_ref[0], v_ref[0], o_ref, out_scale)


def _diff_attention(qkv, y, lam, subln_gain, out_scale, n_ctx):
    b, t, _ = qkv.shape
    tq = n_ctx
    hw = 2 * HEAD_DIM
    q0 = (QA_W + 2 * KA_W) // hw
    k0 = q0 + B_HEADS
    v0 = k0 + B_HEADS
    o0 = QA_W // hw
    return pl.pallas_call(
        functools.partial(_diff_kernel, out_scale=out_scale, n_ctx=n_ctx),
        grid=(b, B_HEADS, t // tq),
        in_specs=[pl.BlockSpec(memory_space=pltpu.SMEM),
                  pl.BlockSpec((1, hw), lambda bi, h, i: (0, 0)),
                  pl.BlockSpec((1, tq, hw), lambda bi, h, i: (bi, i, q0 + h)),
                  pl.BlockSpec((1, t, hw), lambda bi, h, i: (bi, 0, k0 + h)),
                  pl.BlockSpec((1, t, hw), lambda bi, h, i: (bi, 0, v0 + h)),
                  pl.BlockSpec(memory_space=pl.ANY)],
        out_specs=pl.BlockSpec((1, tq, hw), lambda bi, h, i: (bi, i, o0 + h)),
        out_shape=jax.ShapeDtypeStruct(y.shape, y.dtype),
        input_output_aliases={5: 0},
        compiler_params=_params(("parallel", "parallel", "arbitrary")),
        name="diff_attention",
    )(lam, subln_gain, qkv, qkv, qkv, y)


def _split3(a):
    a1 = a.astype(BF16)
    r1 = a - a1.astype(F32)
    a2 = r1.astype(BF16)
    a3 = (r1 - a2.astype(F32)).astype(BF16)
    return a1, a2, a3


def _dot_exact_rhs(a, rhs01):
    out = None
    for part in _split3(a):
        d = jnp.dot(part, rhs01, preferred_element_type=F32)
        out = d if out is None else out + d
    return out


def _dot_exact_lhs(lhs01, a):
    out = None
    for part in _split3(a):
        d = jnp.dot(lhs01, part, preferred_element_type=F32)
        out = d if out is None else out + d
    return out


SSD_GROUPS_PER_STEP = 8


def _softplus(v):
    return jnp.maximum(v, 0.0) + jnp.log1p(jnp.exp(-jnp.abs(v)))


def _ssd_group(d_sign, acr, acc, skip, x_tok, b_tok, c_tok, dt_c, dt_r, state):
    q = SSD_CHUNK
    rows = lax.broadcasted_iota(jnp.int32, (q, q), 0)
    cols = lax.broadcasted_iota(jnp.int32, (q, q), 1)
    signed = (rows - cols) * d_sign
    keep_sl = signed <= 0
    tri_ks01 = jnp.where(keep_sl, 1.0, 0.0).astype(BF16)
    tri_sk01 = jnp.where(signed >= 0, 1.0, 0.0).astype(BF16)

    x = jnp.transpose(x_tok)
    bm = b_tok.astype(BF16)
    cm = c_tok.astype(BF16)
    a_r = dt_r * acc
    a_c = dt_c * acr
    cum_r = _dot_exact_rhs(a_r, tri_ks01)
    cum_c = _dot_exact_lhs(tri_sk01, a_c)
    tot = jnp.sum(a_r, axis=1, keepdims=True)
    to_end_r = jnp.exp(tot - cum_r)
    from_start_r = jnp.exp(cum_r)
    tot_e = jnp.exp(tot)

    g_sl = lax.dot_general(bm, cm, (((1,), (1,)), ((), ())), preferred_element_type=F32)
    y_off = lax.dot_general(state.astype(BF16), cm, (((1,), (1,)), ((), ())), preferred_element_type=F32)

    xw_parts, y_parts = [], []
    for h in range(SSD_HPG):
        sl = slice(h * SSD_HEADDIM, (h + 1) * SSD_HEADDIM)
        xh = x[sl, :]
        xd = xh * dt_r[h:h + 1, :]
        seg = cum_r[h:h + 1, :] - cum_c[:, h:h + 1]
        decay = jnp.exp(jnp.where(keep_sl, seg, -1e30))
        m_h = (g_sl * decay).astype(BF16)
        y_h = jnp.dot(xd.astype(BF16), m_h, preferred_element_type=F32)
        y_parts.append(y_h + y_off[sl, :] * from_start_r[h:h + 1, :] + skip[h:h + 1, :] * xh)
        xw_parts.append((xd * to_end_r[h:h + 1, :]).astype(BF16))
    s_new = jnp.dot(jnp.concatenate(xw_parts, axis=0), bm, preferred_element_type=F32)
    decay_rows = jnp.concatenate([jnp.broadcast_to(tot_e[h:h + 1, :], (SSD_HEADDIM, 1)) for h in range(SSD_HPG)], axis=0)
    return jnp.transpose(jnp.concatenate(y_parts, axis=0)), state * decay_rows + s_new


def _ssd_kernel(acr_ref, acc_ref, skip_ref, dbr_ref, dbc_ref, x_ref, b_ref, c_ref, dtc_ref, dtr_ref, y_ref, state_ref):
    @pl.when(pl.program_id(3) == 0)
    def _():
        state_ref[...] = jnp.zeros_like(state_ref)

    d_sign = 1 - 2 * pl.program_id(1)
    for g in range(SSD_GROUPS_PER_STEP):
        xs = slice(g * SSD_GROUP_W, (g + 1) * SSD_GROUP_W)
        ns = slice(g * SSD_STATE, (g + 1) * SSD_STATE)
        y, state = _ssd_group(d_sign, acr_ref[0, g], acc_ref[0, g], skip_ref[0, g], x_ref[0, :, xs], b_ref[0, :, ns],
                              c_ref[0, :, ns], _softplus(dtc_ref[0, 0, g] + dbr_ref[0, g]),
                              _softplus(dtr_ref[0, 0, g] + dbc_ref[0, g]), state_ref[g])
        y_ref[0, 0, :, xs] = y.astype(y_ref.dtype)
        state_ref[g] = state


def _ssd_scan(xbc, dt_c, dt_r, a_row, a_col, skip, bias_row, bias_col, n_ctx, out_dtype):
    b, t, _ = xbc.shape
    gs = SSD_GROUPS_PER_STEP
    b0 = SSD_INNER // (gs * SSD_STATE)
    c0 = b0 + SSD_GROUPS // gs
    q = SSD_CHUNK
    ncc = n_ctx // q
    nch = t // q

    def chunk(d, s):
        back = jnp.where(s < ncc, ncc - 1 - s, nch - 1 - s + ncc)
        return jnp.where(d == 0, s, back)

    return pl.pallas_call(
        _ssd_kernel,
        grid=(b, 2, SSD_GROUPS // gs, nch),
        in_specs=[pl.BlockSpec((1, gs, 1, SSD_HPG), lambda bi, d, g, s: (d, g, 0, 0)),
                  pl.BlockSpec((1, gs, SSD_HPG, 1), lambda bi, d, g, s: (d, g, 0, 0)),
                  pl.BlockSpec((1, gs, SSD_HPG, 1), lambda bi, d, g, s: (d, g, 0, 0)),
                  pl.BlockSpec((1, gs, 1, SSD_HPG), lambda bi, d, g, s: (d, g, 0, 0)),
                  pl.BlockSpec((1, gs, SSD_HPG, 1), lambda bi, d, g, s: (d, g, 0, 0)),
                  pl.BlockSpec((1, q, gs * SSD_GROUP_W), lambda bi, d, g, s: (bi, chunk(d, s), g)),
                  pl.BlockSpec((1, q, gs * SSD_STATE), lambda bi, d, g, s: (bi, chunk(d, s), b0 + g)),
                  pl.BlockSpec((1, q, gs * SSD_STATE), lambda bi, d, g, s: (bi, chunk(d, s), c0 + g)),
                  pl.BlockSpec((1, 1, gs, q, SSD_HPG), lambda bi, d, g, s: (d, bi, g, chunk(d, s), 0)),
                  pl.BlockSpec((1, 1, gs, SSD_HPG, q), lambda bi, d, g, s: (d, bi, g, 0, chunk(d, s)))],
        out_specs=pl.BlockSpec((1, 1, q, gs * SSD_GROUP_W), lambda bi, d, g, s: (d, bi, chunk(d, s), g)),
        out_shape=jax.ShapeDtypeStruct((2, b, t, SSD_INNER), out_dtype),
        scratch_shapes=[pltpu.VMEM((gs, SSD_GROUP_W, SSD_STATE), F32)],
        compiler_params=_params(("parallel", "parallel", "parallel", "arbitrary")),
        name="ssd_scan",
    )(a_row, a_col, skip, bias_row, bias_col, xbc, xbc, xbc, dt_c, dt_r)


def _conv_silu_kernel(x_ref, w_ref, b_ref, o_ref, *, n_ctx):
    x = x_ref[0]
    t = x.shape[0]
    row = lax.broadcasted_iota(jnp.int32, (t, 1), 0)
    prev = jnp.where((row == 0) | (row == n_ctx), 0.0, pltpu.roll(x, 1, axis=0))
    nxt = jnp.where((row == n_ctx - 1) | (row == t - 1), 0.0, pltpu.roll(x, t - 1, axis=0))
    u = prev * w_ref[0:1, :] + x * w_ref[1:2, :] + nxt * w_ref[2:3, :] + b_ref[...]
    o_ref[0] = u * jax.nn.sigmoid(u)


def _conv_silu(proj, col0, conv_w, conv_b, n_ctx):
    b, t, _ = proj.shape
    tc = 512
    c0 = col0 // tc
    return pl.pallas_call(
        functools.partial(_conv_silu_kernel, n_ctx=n_ctx),
        grid=(b, SSD_CONV_DIM // tc),
        in_specs=[pl.BlockSpec((1, t, tc), lambda bi, j: (bi, 0, c0 + j)),
                  pl.BlockSpec((SSD_CONV, tc), lambda bi, j: (0, j)),
                  pl.BlockSpec((1, tc), lambda bi, j: (0, j))],
        out_specs=pl.BlockSpec((1, t, tc), lambda bi, j: (bi, 0, j)),
        out_shape=jax.ShapeDtypeStruct((b, t, SSD_CONV_DIM), F32),
        compiler_params=_params(("parallel", "parallel")),
        name="ssd_conv_silu",
    )(proj, conv_w, conv_b.reshape(1, -1))


def _gate_norm_kernel(y_ref, z_ref, gain_ref, o_ref):
    z = z_ref[0]
    g = (y_ref[0, 0] + y_ref[1, 0]) * (z * jax.nn.sigmoid(z))
    o_ref[0] = (g * lax.rsqrt(jnp.mean(g * g, axis=-1, keepdims=True) + NORM_EPS) * gain_ref[...]).astype(o_ref.dtype)


def _gate_norm(y, proj, gain, n_ctx):
    _, b, t, c = y.shape
    tt = 256
    assert n_ctx % tt == 0
    r0 = n_ctx // tt
    s = t - n_ctx
    return pl.pallas_call(
        _gate_norm_kernel,
        grid=(b, s // tt),
        in_specs=[pl.BlockSpec((2, 1, tt, c), lambda bi, i: (0, bi, r0 + i, 0)),
                  pl.BlockSpec((1, tt, c), lambda bi, i: (bi, r0 + i, 0)),
                  pl.BlockSpec((1, c), lambda bi, i: (0, 0))],
        out_specs=pl.BlockSpec((1, tt, c), lambda bi, i: (bi, i, 0)),
        out_shape=jax.ShapeDtypeStruct((b, s, c), BF16),
        compiler_params=_params(("parallel", "parallel")),
        name="ssd_gate_norm",
    )(y, proj, gain.reshape(1, c))


def _route_kernel(f_ref, rwt_ref, bias_ref, idx_ref, gate_ref, rank_ref, cnt_ref, carry_ref):
    tn = f_ref.shape[0]
    ne, ng, pg = N_EXPERTS, N_EXPERT_GROUPS, N_EXPERTS // N_EXPERT_GROUPS
    neg = -jnp.inf

    @pl.when(pl.program_id(0) == 0)
    def _():
        carry_ref[...] = jnp.zeros_like(carry_ref)

    dn = (((1,), (1,)), ((), ()))
    lo, hi = _unpack_bf16_pairs(f_ref[...])
    f = jnp.concatenate([lo.astype(BF16), hi.astype(BF16)], axis=1)
    w = rwt_ref[...]
    w1 = w.astype(BF16)
    w2 = (w - w1.astype(F32)).astype(BF16)
    logits = (lax.dot_general(w1, f, dn, preferred_element_type=F32)
              + lax.dot_general(w2, f, dn, preferred_element_type=F32))
    scores = jax.nn.sigmoid(logits)
    g3 = (scores + bias_ref[...]).reshape(ng, pg, tn)
    io3 = lax.broadcasted_iota(jnp.int32, (ng, pg, tn), 1)
    m1 = jnp.max(g3, axis=1, keepdims=True)
    i1 = jnp.min(jnp.where(g3 == m1, io3, pg), axis=1, keepdims=True)
    m2 = jnp.max(jnp.where(io3 == i1, neg, g3), axis=1, keepdims=True)
    work = (m1 + m2).reshape(ng, tn)
    iog = lax.broadcasted_iota(jnp.int32, (ng, tn), 0)
    ok = jnp.zeros((ng, tn), F32)
    for _ in range(TOPK_GROUPS):
        m = jnp.max(work, axis=0, keepdims=True)
        gi = jnp.min(jnp.where(work == m, iog, ng), axis=0, keepdims=True)
        hit = iog == gi
        ok = jnp.where(hit, 1.0, ok)
        work = jnp.where(hit, neg, work)
    sel = jnp.where(ok.reshape(ng, 1, tn) > 0.0, g3, neg).reshape(ne, tn)
    ioe = lax.broadcasted_iota(jnp.int32, (ne, tn), 0)
    onehot = jnp.zeros((ne, tn), F32)
    idxs, ws = [], []
    for _ in range(TOP_K):
        m = jnp.max(sel, axis=0, keepdims=True)
        ei = jnp.min(jnp.where(sel == m, ioe, ne), axis=0, keepdims=True)
        hit = ioe == ei
        idxs.append(ei)
        ws.append(jnp.sum(jnp.where(hit, scores, 0.0), axis=0, keepdims=True))
        sel = jnp.where(hit, neg, sel)
        onehot = jnp.where(hit, 1.0, onehot)
    w = jnp.concatenate(ws, axis=0)
    gate_ref[...] = w / jnp.sum(w, axis=0, keepdims=True) * ROUTED_SCALE
    idx_ref[...] = jnp.concatenate(idxs, axis=0)
    r = lax.broadcasted_iota(jnp.int32, (tn, tn), 0)
    c = lax.broadcasted_iota(jnp.int32, (tn, tn), 1)
    ahead = jnp.where(r < c, 1.0, 0.0).astype(BF16)
    cum = carry_ref[...] + jnp.dot(onehot.astype(BF16), ahead, preferred_element_type=F32)
    ranks = [jnp.sum(jnp.where(ioe == idxs[k], cum, 0.0), axis=0, keepdims=True) for k in range(TOP_K)]
    rank_ref[...] = jnp.concatenate(ranks, axis=0).astype(jnp.int32)
    total = carry_ref[...] + jnp.sum(onehot, axis=1, keepdims=True)
    carry_ref[...] = total
    cnt_ref[...] = total.astype(jnp.int32)


def _route(fp, router_wt, router_bias):
    t = fp.shape[0]
    d = router_wt.shape[1]
    tn = _pick(t, (512, 256, 128))
    kt = pl.BlockSpec((TOP_K, tn), lambda i: (0, i))
    return pl.pallas_call(
        _route_kernel,
        grid=(t // tn,),
        in_specs=[pl.BlockSpec((tn, d // 2), lambda i: (i, 0)),
                  pl.BlockSpec((N_EXPERTS, d), lambda i: (0, 0)),
                  pl.BlockSpec((N_EXPERTS, 1), lambda i: (0, 0))],
        out_specs=[kt, kt, kt, pl.BlockSpec((N_EXPERTS, 1), lambda i: (0, 0))],
        out_shape=[jax.ShapeDtypeStruct((TOP_K, t), jnp.int32), jax.ShapeDtypeStruct((TOP_K, t), F32),
                   jax.ShapeDtypeStruct((TOP_K, t), jnp.int32), jax.ShapeDtypeStruct((N_EXPERTS, 1), jnp.int32)],
        scratch_shapes=[pltpu.VMEM((N_EXPERTS, 1), F32)],
        compiler_params=_params(("arbitrary",)),
        name="moe_route",
    )(fp, router_wt, router_bias)


def _scatter_kernel(dest_ref, f_ref, xs_in_ref, xs_ref, sem):
    del xs_in_ref
    ts = f_ref.shape[0]

    def row_copy(t, k):
        return pltpu.make_async_copy(f_ref.at[pl.ds(t, 1)], xs_ref.at[pl.ds(dest_ref[t * TOP_K + k], 1)], sem)

    def issue(t, carry):
        for k in range(TOP_K):
            row_copy(t, k).start(priority=k % 2)
        return carry

    def drain(t, carry):
        for k in range(TOP_K):
            row_copy(t, k).wait()
        return carry

    lax.fori_loop(0, ts, issue, 0)
    lax.fori_loop(0, ts, drain, 0)


def _scatter_rows(dest_flat, f, xs):
    t, d = f.shape
    ts = _pick(t, (512, 256, 128))
    return pl.pallas_call(
        _scatter_kernel,
        grid=(t // ts,),
        in_specs=[pl.BlockSpec((ts * TOP_K,), lambda i: (i,), memory_space=pltpu.SMEM),
                  pl.BlockSpec((ts, d), lambda i: (i, 0)),
                  pl.BlockSpec(memory_space=pl.ANY)],
        out_specs=pl.BlockSpec(memory_space=pl.ANY),
        out_shape=jax.ShapeDtypeStruct(xs.shape, xs.dtype),
        scratch_shapes=[pltpu.SemaphoreType.DMA(())],
        input_output_aliases={2: 0},
        compiler_params=_params(("arbitrary",)),
        name="moe_scatter",
    )(dest_flat, f, xs)


def _zero_blocks_kernel(rows_ref, xs_ref, zero_ref, sem):
    tm = zero_ref.shape[0]
    zero_ref[...] = jnp.zeros_like(zero_ref)

    def block_copy(e):
        return pltpu.make_async_copy(zero_ref, xs_ref.at[pl.ds(pl.multiple_of(rows_ref[e], tm), tm)], sem)

    for e in range(N_EXPERTS):
        block_copy(e).start()
    for e in range(N_EXPERTS):
        block_copy(e).wait()


def _zero_blocks(block_rows, n_rows_alloc, tm, d):
    grid_spec = pltpu.PrefetchScalarGridSpec(
        num_scalar_prefetch=1, grid=(1,), in_specs=[],
        out_specs=pl.BlockSpec(memory_space=pl.ANY),
        scratch_shapes=[pltpu.VMEM((tm, d), jnp.uint32), pltpu.SemaphoreType.DMA(())])
    return pl.pallas_call(
        _zero_blocks_kernel,
        grid_spec=grid_spec,
        out_shape=jax.ShapeDtypeStruct((n_rows_alloc, d), jnp.uint32),
        compiler_params=_params(("arbitrary",)),
        name="moe_zero_tail_blocks",
    )(block_rows)


def _expert_kernel(be_ref, blk_ref, nused_ref, x_ref, wg_ref, wu_ref, wd_ref, o_ref, wg_s, wu_s, wd_s):
    i = pl.program_id(0)

    @pl.when(jnp.logical_or(i == 0, be_ref[i] != be_ref[jnp.maximum(i - 1, 0)]))
    def _():
        wg_s[...] = wg_ref[0, 0].astype(BF16)
        wu_s[...] = wu_ref[0, 0].astype(BF16)
        wd_s[...] = wd_ref[0, 0].astype(BF16)

    @pl.when(i < nused_ref[0])
    def _():
        lo, hi = _unpack_bf16_pairs(x_ref[...])
        x = jnp.concatenate([lo.astype(BF16), hi.astype(BF16)], axis=1)
        hg = jnp.dot(x, wg_s[...], preferred_element_type=F32)
        hu = jnp.dot(x, wu_s[...], preferred_element_type=F32)
        h = (hg * jax.nn.sigmoid(hg) * hu).astype(BF16)
        o_ref[...] = _pack_bf16_pairs(jnp.dot(h, wd_s[...], preferred_element_type=F32))


def _expert_blocks(block_expert, block_index, n_used, xs, n_blocks, tm, layer, w_gate, w_up, w_down):
    d, ff = w_gate.shape[2:]
    grid_spec = pltpu.PrefetchScalarGridSpec(
        num_scalar_prefetch=3,
        grid=(n_blocks,),
        in_specs=[pl.BlockSpec((tm, d // 2), lambda i, be, blk, nu: (blk[i], 0)),
                  pl.BlockSpec((1, 1, d, ff), lambda i, be, blk, nu: (layer, be[i], 0, 0)),
                  pl.BlockSpec((1, 1, d, ff), lambda i, be, blk, nu: (layer, be[i], 0, 0)),
                  pl.BlockSpec((1, 1, ff, d), lambda i, be, blk, nu: (layer, be[i], 0, 0))],
        out_specs=pl.BlockSpec((tm, d // 2), lambda i, be, blk, nu: (blk[i], 0)),
        scratch_shapes=[pltpu.VMEM((d, ff), BF16), pltpu.VMEM((d, ff), BF16), pltpu.VMEM((ff, d), BF16)],
    )
    return pl.pallas_call(
        _expert_kernel,
        grid_spec=grid_spec,
        out_shape=jax.ShapeDtypeStruct((n_blocks * tm, d // 2), jnp.uint32),
        compiler_params=_params(("arbitrary",)),
        name="moe_experts",
    )(block_expert, block_index, n_used, xs, w_gate, w_up, w_down)


def _combine_kernel(dest_ref, gate_ref, sh_ref, xa_ref, mod_ref, fin_ref, ys_ref, o_ref, buf, sem, *,
                    tiles_per_batch, n_ctx, final_norm):
    tn = gate_ref.shape[0]

    def row_copy(t, k):
        return pltpu.make_async_copy(ys_ref.at[pl.ds(dest_ref[t * TOP_K + k], 1)], buf.at[k, pl.ds(t, 1)], sem)

    def issue(t, carry):
        for k in range(TOP_K):
            row_copy(t, k).start(priority=k % 2)
        return carry

    def drain(t, carry):
        for k in range(TOP_K):
            row_copy(t, k).wait()
        return carry

    lax.fori_loop(0, tn, issue, 0)
    lax.fori_loop(0, tn, drain, 0)
    half = buf.shape[2]
    acc_lo = sh_ref[:, :half]
    acc_hi = sh_ref[:, half:]
    for k in range(TOP_K):
        lo, hi = _unpack_bf16_pairs(buf[k])
        acc_lo = acc_lo + gate_ref[:, k:k + 1] * lo
        acc_hi = acc_hi + gate_ref[:, k:k + 1] * hi
    acc = jnp.concatenate([acc_lo, acc_hi], axis=1)
    is_ctx = _is_ctx_rows(tn, tiles_per_batch, n_ctx)
    out = xa_ref[...] + _segment_rows(mod_ref, 5, is_ctx) * acc
    if final_norm:
        out = out * lax.rsqrt(jnp.mean(out * out, axis=-1, keepdims=True) + NORM_EPS) * fin_ref[...]
    o_ref[...] = out


def _combine(dest_flat, gates, shared, ys, xa, modsel, rows_per_batch, n_ctx, final_gain):
    t, d = shared.shape
    tn = 256
    tpb = rows_per_batch // tn
    return pl.pallas_call(
        functools.partial(_combine_kernel, tiles_per_batch=tpb, n_ctx=n_ctx, final_norm=final_gain is not None),
        grid=(t // tn,),
        in_specs=[pl.BlockSpec((tn * TOP_K,), lambda i: (i,), memory_space=pltpu.SMEM),
                  pl.BlockSpec((tn, TOP_K), lambda i: (i, 0)),
                  pl.BlockSpec((tn, d), lambda i: (i, 0)),
                  pl.BlockSpec((tn, d), lambda i: (i, 0)),
                  pl.BlockSpec((1, 2, N_MOD, d), lambda i: (i // tpb, 0, 0, 0)),
                  pl.BlockSpec((1, d), lambda i: (0, 0)),
                  pl.BlockSpec(memory_space=pl.ANY)],
        out_specs=pl.BlockSpec((tn, d), lambda i: (i, 0)),
        out_shape=jax.ShapeDtypeStruct((t, d), F32),
        scratch_shapes=[pltpu.VMEM((TOP_K, tn, d // 2), jnp.uint32), pltpu.SemaphoreType.DMA(())],
        compiler_params=_params(("arbitrary",)),
        name="moe_combine",
    )(dest_flat, gates, shared, xa, modsel, (jnp.ones((d,), F32) if final_gain is None else final_gain).reshape(1, d), ys)


def _swiglu_kernel(x_ref, wg_ref, wu_ref, wd_ref, o_ref):
    lo, hi = _unpack_bf16_pairs(x_ref[...])
    x = jnp.concatenate([lo.astype(BF16), hi.astype(BF16)], axis=1)
    hg = jnp.dot(x, wg_ref[...], preferred_element_type=F32)
    hu = jnp.dot(x, wu_ref[...], preferred_element_type=F32)
    h = (hg * jax.nn.sigmoid(hg) * hu).astype(BF16)
    o_ref[...] = jnp.dot(h, wd_ref[...], preferred_element_type=F32).astype(o_ref.dtype)


def _shared_expert(x, wg, wu, wd):
    m = x.shape[0]
    d, ff = wg.shape
    tm = _pick(m, (512, 384, 256, 128))
    return pl.pallas_call(
        _swiglu_kernel,
        grid=(m // tm,),
        in_specs=[pl.BlockSpec((tm, d // 2), lambda i: (i, 0)),
                  pl.BlockSpec((d, ff), lambda i: (0, 0)),
                  pl.BlockSpec((d, ff), lambda i: (0, 0)),
                  pl.BlockSpec((ff, d), lambda i: (0, 0))],
        out_specs=pl.BlockSpec((tm, d), lambda i: (i, 0)),
        out_shape=jax.ShapeDtypeStruct((m, d), F32),
        compiler_params=_params(("parallel",)),
        name="shared_expert",
    )(x, wg, wu, wd)


def _moe_ffn(fp, xa, modsel, rows_per_batch, n_ctx, final_gain, layer, router_w, router_bias, w_gate, w_up, w_down,
             ws_gate, ws_up, ws_down):
    t, d = xa.shape
    tm = EXPERT_TM
    idx, gate, rank, cnt = _route(fp, router_w.T, router_bias.astype(F32)[:, None])
    counts = cnt[:, 0]
    padded = (counts + tm - 1) // tm * tm
    pad_end = jnp.cumsum(padded)
    pad_start = pad_end - padded
    n_used = pad_end[-1] // tm
    n_blocks = (t * TOP_K) // tm + N_EXPERTS
    n_rows = n_blocks * tm
    experts = jnp.arange(N_EXPERTS, dtype=jnp.int32)
    start_of = jnp.sum(jnp.where(idx[..., None] == experts, pad_start, 0), axis=-1)
    dest = (start_of + rank).T.reshape(-1).astype(jnp.int32)
    block_index = jnp.minimum(jnp.arange(n_blocks, dtype=jnp.int32), n_used - 1).astype(jnp.int32)
    block_expert = jnp.minimum(jnp.sum(pad_end[None, :] <= (block_index * tm)[:, None], axis=1),
                               N_EXPERTS - 1).astype(jnp.int32)
    tail_rows = jnp.where(padded > counts, pad_end - tm, n_rows + experts * tm).astype(jnp.int32)
    xs = _zero_blocks(tail_rows, n_rows + N_EXPERTS * tm, tm, d // 2)
    xs = _scatter_rows(dest, fp, xs)
    ys = _expert_blocks(block_expert, block_index, n_used.reshape(1).astype(jnp.int32), xs, n_blocks, tm, layer,
                        w_gate, w_up, w_down)
    shared = _shared_expert(fp, ws_gate.astype(BF16), ws_up.astype(BF16), ws_down.astype(BF16))
    return _combine(dest, gate.T, shared, ys, xa, modsel, rows_per_batch, n_ctx, final_gain)


def _rms(u, gain):
    return u * lax.rsqrt(jnp.mean(u * u, axis=-1, keepdims=True) + NORM_EPS) * gain


def _rope_tables(n_ctx, n_lat):
    rows = n_lat // GRID_W
    row = jnp.repeat(jnp.arange(rows, dtype=F32), GRID_W)
    col = jnp.tile(jnp.arange(GRID_W, dtype=F32), rows)
    n_freq = HEAD_DIM // 4
    inv = ROPE_THETA ** (-jnp.arange(n_freq, dtype=F32) / n_freq)
    ang = jnp.concatenate([row[:, None] * inv, col[:, None] * inv], axis=-1)
    cos = jnp.repeat(jnp.cos(ang), 2, axis=-1)
    sin = jnp.repeat(jnp.sin(ang), 2, axis=-1)
    sign = jnp.tile(jnp.array([-1.0, 1.0], F32), HEAD_DIM // 2)
    cos = jnp.concatenate([jnp.ones((n_ctx, HEAD_DIM), F32), cos], axis=0)
    sin = jnp.concatenate([jnp.zeros((n_ctx, HEAD_DIM), F32), sin * sign], axis=0)
    return cos, sin


def _ada_mod(cond, w, bias):
    m = jax.nn.silu(cond)
    rows = m.shape[0]
    pad = (-rows) % 16
    mp = jnp.pad(m, ((0, pad), (0, 0))).astype(BF16)
    out = _matmul(mp, w.astype(BF16), F32)[:rows] + bias
    return out.reshape(rows, N_MOD, -1)


def _attention_layer(xa, modsel, gain, n_ctx, w_in, q_gain, k_gain, lq1, lk1, lq2, lk2, subln, lambda_init):
    b, t, d = xa.shape
    cos, sin = _rope_tables(n_ctx, t - n_ctx)
    head_gains = jnp.stack([q_gain * _Q_SCALE, k_gain]).astype(F32)
    qkv = _fused_proj(xa, gain, modsel, w_in.astype(BF16), n_ctx, BF16,
                      head_gains=head_gains, rope=(cos - 1.0, sin)).reshape(b, t, ATTN_IN)
    lam = (jnp.exp(jnp.sum(lq1 * lk1)) - jnp.exp(jnp.sum(lq2 * lk2)) + lambda_init).reshape(1, 1).astype(F32)
    y = _gqa_attention(qkv, n_ctx)
    return _diff_attention(qkv, y, lam, subln.reshape(1, 2 * HEAD_DIM), 1.0 - lambda_init, n_ctx)


def _ssd_layer(xa, modsel, gain, n_ctx, w_in, conv_w, conv_b, dt_bias, a_log, d_skip, norm_gain):
    b, t, d = xa.shape
    proj = _fused_proj(xa, gain, modsel, w_in.astype(BF16), n_ctx, F32).reshape(b, t, -1)
    xbc = _conv_silu(proj, SSD_INNER, conv_w, conv_b, n_ctx)
    dt = proj[..., SSD_INNER + SSD_CONV_DIM:].reshape(b, t, 2, SSD_GROUPS, SSD_HPG)
    dt_c = jnp.transpose(dt, (2, 0, 3, 1, 4))
    dt_r = jnp.transpose(dt, (2, 0, 3, 4, 1))
    bias_row = dt_bias.reshape(2, SSD_GROUPS, 1, SSD_HPG)
    bias_col = dt_bias.reshape(2, SSD_GROUPS, SSD_HPG, 1)
    a_coef = -jnp.exp(a_log)
    a_row = a_coef.reshape(2, SSD_GROUPS, 1, SSD_HPG)
    a_col = a_coef.reshape(2, SSD_GROUPS, SSD_HPG, 1)
    skip = d_skip.reshape(2, SSD_GROUPS, SSD_HPG, 1)
    y = _ssd_scan(xbc, dt_c, dt_r, a_row, a_col, skip, bias_row, bias_col, n_ctx, F32)
    return _gate_norm(y, proj, norm_gain, n_ctx)


def kernel(x, c, ctx, c_ctx, mod_w, mod_b, norm_mix, norm_ffn, norm_final, attn_w_in, attn_w_out, attn_q_gain,
           attn_k_gain, diff_lam_q1, diff_lam_k1, diff_lam_q2, diff_lam_k2, diff_subln, ssd_w_in, ssd_conv_w,
           ssd_conv_b, ssd_dt_bias, ssd_a_log, ssd_d, ssd_norm, ssd_w_out, router_w, router_bias, exp_w_gate,
           exp_w_up, exp_w_down, shared_w_gate, shared_w_up, shared_w_down):
    b, s, d = x.shape
    n_ctx = ctx.shape[1]
    t = n_ctx + s
    xa = jnp.concatenate([ctx, x], axis=1)
    cond = jnp.concatenate([c, c_ctx[None]], axis=0)
    out = None
    for i in range(DEPTH):
        last = i == DEPTH - 1
        mod = _ada_mod(cond, mod_w[i], mod_b[i])
        modsel = jnp.stack([jnp.broadcast_to(mod[b], (b, N_MOD, d)), mod[:b]], axis=1)
        j = i // 2
        moe_w = (i, router_w[i], router_bias[i], exp_w_gate, exp_w_up, exp_w_down,
                 shared_w_gate[i], shared_w_up[i], shared_w_down[i])
        if i % 2 == 0:
            lambda_init = 0.8 - 0.6 * math.exp(-0.3 * i)
            y = _attention_layer(xa, modsel, norm_mix[i], n_ctx, attn_w_in[j], attn_q_gain[j], attn_k_gain[j],
                                 diff_lam_q1[j], diff_lam_k1[j], diff_lam_q2[j], diff_lam_k2[j],
                                 diff_subln[j], lambda_init)
            w_out = attn_w_out[j]
        else:
            y = _ssd_layer(xa, modsel, norm_mix[i], n_ctx, ssd_w_in[j], ssd_conv_w[j], ssd_conv_b[j],
                           ssd_dt_bias[j], ssd_a_log[j], ssd_d[j], ssd_norm[j])
            w_out = ssd_w_out[j]
        if last:
            x_new, fp = _fused_out_proj(y if y.shape[1] == s else y[:, n_ctx:], w_out.astype(BF16), xa, n_ctx,
                                        modsel, norm_ffn[i], 0)
            out = _moe_ffn(fp.reshape(b * s, d // 2), x_new.reshape(b * s, d), modsel, s, 0, norm_final,
                           *moe_w).reshape(b, s, d)
        else:
            x_new, fp = _fused_out_proj(y, w_out.astype(BF16), xa, 0, modsel, norm_ffn[i], n_ctx)
            xa = _moe_ffn(fp.reshape(b * t, d // 2), x_new.reshape(b * t, d), modsel, t, n_ctx, None,
                          *moe_w).reshape(b, t, d)
    return out
```

```python
import functools
import math

import jax
import jax.numpy as jnp
from jax import lax
from jax.experimental import pallas as pl
from jax.experimental.pallas import tpu as pltpu

F32 = jnp.float32
BF16 = jnp.bfloat16

D_MODEL = 2048
DEPTH = 2
GRID_W = 64
NORM_EPS = 1e-6
N_MOD = 6
HEAD_DIM = 128
ROPE_THETA = 10000.0
A_Q_HEADS = 8
A_KV_HEADS = 2
A_GROUP = A_Q_HEADS // A_KV_HEADS
B_HEADS = 4
QA_W = A_Q_HEADS * HEAD_DIM
KA_W = A_KV_HEADS * HEAD_DIM
QB_W = 2 * B_HEADS * HEAD_DIM
ATTN_IN = QA_W + 2 * KA_W + 3 * QB_W
SSD_INNER = 2 * D_MODEL
SSD_HEADDIM = 64
SSD_HEADS = SSD_INNER // SSD_HEADDIM
SSD_GROUPS = 8
SSD_HPG = SSD_HEADS // SSD_GROUPS
SSD_STATE = 128
SSD_CONV = 3
SSD_CHUNK = 128
SSD_GROUP_W = SSD_HPG * SSD_HEADDIM
SSD_CONV_DIM = SSD_INNER + 2 * SSD_GROUPS * SSD_STATE
N_EXPERTS = 64
EXPERT_FF = 512
TOP_K = 8
N_EXPERT_GROUPS = 8
TOPK_GROUPS = 4
ROUTED_SCALE = 2.5
EXPERT_TM = 512

VMEM_LIMIT_BYTES = 56 * 1024 * 1024


def _pick(n, prefs):
    for p in prefs:
        if n % p == 0:
            return p
    raise ValueError(f"no tile in {prefs} divides {n}")


def _params(sem):
    return pltpu.CompilerParams(dimension_semantics=sem, vmem_limit_bytes=VMEM_LIMIT_BYTES)


def _mm_kernel(x_ref, w_ref, o_ref):
    o_ref[...] = jnp.dot(x_ref[...], w_ref[...], preferred_element_type=F32).astype(o_ref.dtype)


def _matmul(x, w, out_dtype):
    m, k = x.shape
    n = w.shape[1]
    tm = _pick(m, (1024, 512, 384, 256, 128, 16, 8))
    tn = _pick(n, (1024, 512, 384, 256, 128))
    return pl.pallas_call(
        _mm_kernel,
        grid=(m // tm, n // tn),
        in_specs=[pl.BlockSpec((tm, k), lambda i, j: (i, 0)),
                  pl.BlockSpec((k, tn), lambda i, j: (0, j))],
        out_specs=pl.BlockSpec((tm, tn), lambda i, j: (i, j)),
        out_shape=jax.ShapeDtypeStruct((m, n), out_dtype),
        compiler_params=_params(("parallel", "arbitrary")),
        name="matmul",
    )(x, w)


def _pack_bf16_pairs(v):
    half = v.shape[1] // 2
    lo = lax.bitcast_convert_type(v[:, :half].astype(BF16).astype(F32), jnp.uint32) >> 16
    hi = lax.bitcast_convert_type(v[:, half:].astype(BF16).astype(F32), jnp.uint32) & jnp.uint32(0xFFFF0000)
    return hi | lo


def _unpack_bf16_pairs(w):
    lo = lax.bitcast_convert_type(w << 16, F32)
    hi = lax.bitcast_convert_type(w & jnp.uint32(0xFFFF0000), F32)
    return lo, hi


def _segment_rows(mod_ref, k, is_ctx):
    return jnp.where(is_ctx, mod_ref[0, 0, k:k + 1, :], mod_ref[0, 1, k:k + 1, :])


def _is_ctx_rows(tm, tiles_per_batch, n_ctx):
    row = (pl.program_id(0) % tiles_per_batch) * tm + lax.broadcasted_iota(jnp.int32, (tm, 1), 0)
    return row < n_ctx


def _norm_mod(x, gain, shift, scale):
    xn = x * lax.rsqrt(jnp.mean(x * x, axis=-1, keepdims=True) + NORM_EPS) * gain
    return xn * (1.0 + scale) + shift


_Q_SCALE = HEAD_DIM ** -0.5
_ATTN_HEADS = ([(0, True, 1.0)] * A_Q_HEADS + [(1, True, 1.0)] * A_KV_HEADS + [(None, False, 1.0)] * A_KV_HEADS
               + [(None, True, _Q_SCALE)] * (2 * B_HEADS) + [(None, True, 1.0)] * (2 * B_HEADS)
               + [(None, False, 1.0)] * (2 * B_HEADS))
_PROJ_HEADS_PER_TILE = 4


def _head_epilogue(u, cfg, gains_ref, cm1, sn, even):
    gain_row, rope, scale = cfg
    if gain_row is not None:
        u = u * lax.rsqrt(jnp.mean(u * u, axis=-1, keepdims=True) + NORM_EPS) * gains_ref[gain_row:gain_row + 1, :]
    if scale != 1.0:
        u = u * scale
    if rope:
        partner = jnp.where(even, pltpu.roll(u, HEAD_DIM - 1, axis=1), pltpu.roll(u, 1, axis=1))
        u = u * (1.0 + cm1) + partner * sn
    return u


def _proj_kernel(x_ref, gain_ref, mod_ref, w_ref, *rest, tiles_per_batch, n_ctx, heads):
    if heads:
        gains_ref, cm1_ref, sn_ref, o_ref, h_s = rest
    else:
        o_ref, h_s = rest
    tm = x_ref.shape[0]
    is_ctx = _is_ctx_rows(tm, tiles_per_batch, n_ctx)

    @pl.when(pl.program_id(1) == 0)
    def _():
        h = _norm_mod(x_ref[...], gain_ref[...], _segment_rows(mod_ref, 0, is_ctx), _segment_rows(mod_ref, 1, is_ctx))
        h_s[...] = h.astype(BF16)

    if not heads:
        o_ref[...] = jnp.dot(h_s[...], w_ref[...], preferred_element_type=F32).astype(o_ref.dtype)
        return
    even = (lax.broadcasted_iota(jnp.int32, (tm, HEAD_DIM), 1) & 1) == 0
    hw = 2 * HEAD_DIM
    for tile in range(len(_ATTN_HEADS) // heads):
        @pl.when(pl.program_id(1) == tile)
        def _(tile=tile):
            cm1 = cm1_ref[...]
            sn = sn_ref[...]
            for half in range(heads // 2):
                acc = jnp.dot(h_s[...], w_ref[:, half * hw:(half + 1) * hw], preferred_element_type=F32)
                for hh in range(2):
                    col = half * hw + hh * HEAD_DIM
                    cfg = _ATTN_HEADS[tile * heads + half * 2 + hh]
                    out = _head_epilogue(acc[:, hh * HEAD_DIM:(hh + 1) * HEAD_DIM], cfg, gains_ref, cm1, sn, even)
                    o_ref[:, col:col + HEAD_DIM] = out.astype(o_ref.dtype)


def _fused_proj(xa, gain, modsel, w, n_ctx, out_dtype, head_gains=None, rope=None):
    b, t, d = xa.shape
    n = w.shape[1]
    tm = _pick(t, (768, 512, 384, 256))
    tpb = t // tm
    heads = 0 if head_gains is None else _PROJ_HEADS_PER_TILE
    tn = heads * HEAD_DIM if heads else _pick(n, (1152, 1024, 768, 512, 384, 256, 128))
    in_specs = [pl.BlockSpec((tm, d), lambda i, j: (i, 0)),
                pl.BlockSpec((1, d), lambda i, j: (0, 0)),
                pl.BlockSpec((1, 2, N_MOD, d), lambda i, j: (i // tpb, 0, 0, 0)),
                pl.BlockSpec((d, tn), lambda i, j: (0, j))]
    args = [xa.reshape(b * t, d), gain.reshape(1, d), modsel, w]
    if heads:
        in_specs += [pl.BlockSpec((2, HEAD_DIM), lambda i, j: (0, 0)),
                     pl.BlockSpec((tm, HEAD_DIM), lambda i, j: (i % tpb, 0)),
                     pl.BlockSpec((tm, HEAD_DIM), lambda i, j: (i % tpb, 0))]
        args += [head_gains, rope[0], rope[1]]
    return pl.pallas_call(
        functools.partial(_proj_kernel, tiles_per_batch=tpb, n_ctx=n_ctx, heads=heads),
        grid=(b * tpb, n // tn),
        in_specs=in_specs,
        out_specs=pl.BlockSpec((tm, tn), lambda i, j: (i, j)),
        out_shape=jax.ShapeDtypeStruct((b * t, n), out_dtype),
        scratch_shapes=[pltpu.VMEM((tm, d), BF16)],
        compiler_params=_params(("parallel", "arbitrary")),
        name="norm_mod_proj",
    )(*args)


def _out_proj_kernel(y_ref, w_ref, xa_ref, mod_ref, gain_ref, xo_ref, fp_ref, *, tiles_per_batch, n_ctx):
    tm = y_ref.shape[1]
    is_ctx = _is_ctx_rows(tm, tiles_per_batch, n_ctx)
    acc = jnp.dot(y_ref[0], w_ref[...], preferred_element_type=F32)
    x_new = xa_ref[0] + _segment_rows(mod_ref, 2, is_ctx) * acc
    xo_ref[0] = x_new
    f = _norm_mod(x_new, gain_ref[...], _segment_rows(mod_ref, 3, is_ctx), _segment_rows(mod_ref, 4, is_ctx))
    fp_ref[0] = _pack_bf16_pairs(f)


def _fused_out_proj(y, w, xa, xa_row0, modsel, gain, n_ctx):
    b, tq, k = y.shape
    d = w.shape[1]
    tm = _pick(tq, (384, 256)) if n_ctx else _pick(tq, (256, 128))
    assert xa_row0 % tm == 0
    r0 = xa_row0 // tm
    tpb = tq // tm
    tile = pl.BlockSpec((1, tm, d), lambda i: (i // tpb, i % tpb, 0))
    return pl.pallas_call(
        functools.partial(_out_proj_kernel, tiles_per_batch=tpb, n_ctx=n_ctx),
        grid=(b * tpb,),
        in_specs=[pl.BlockSpec((1, tm, k), lambda i: (i // tpb, i % tpb, 0)),
                  pl.BlockSpec((k, d), lambda i: (0, 0)),
                  pl.BlockSpec((1, tm, d), lambda i: (i // tpb, r0 + i % tpb, 0)),
                  pl.BlockSpec((1, 2, N_MOD, d), lambda i: (i // tpb, 0, 0, 0)),
                  pl.BlockSpec((1, d), lambda i: (0, 0))],
        out_specs=[tile, pl.BlockSpec((1, tm, d // 2), lambda i: (i // tpb, i % tpb, 0))],
        out_shape=[jax.ShapeDtypeStruct((b, tq, d), F32), jax.ShapeDtypeStruct((b, tq, d // 2), jnp.uint32)],
        compiler_params=_params(("parallel",)),
        name="out_proj_residual_norm",
    )(y, w, xa, modsel, gain.reshape(1, d))


def _softmax_rows(s):
    m = jnp.max(s, axis=-1, keepdims=True)
    p = jnp.exp(s - m)
    return p, jnp.sum(p, axis=-1, keepdims=True)


def _gqa_heads(q_ref, k, v, o_ref):
    for g in range(A_GROUP):
        q = q_ref[0, :, g * HEAD_DIM:(g + 1) * HEAD_DIM]
        s = lax.dot_general(q, k, (((1,), (1,)), ((), ())), preferred_element_type=F32)
        p, l = _softmax_rows(s)
        o = jnp.dot(p.astype(BF16), v, preferred_element_type=F32) / l
        o_ref[0, :, g * HEAD_DIM:(g + 1) * HEAD_DIM] = o.astype(o_ref.dtype)


def _gqa_kernel(q_ref, k_ref, v_ref, o_ref, *, n_ctx):
    @pl.when(pl.program_id(2) == 0)
    def _():
        _gqa_heads(q_ref, k_ref[0, :n_ctx], v_ref[0, :n_ctx], o_ref)

    @pl.when(pl.program_id(2) > 0)
    def _():
        _gqa_heads(q_ref, k_ref[0], v_ref[0], o_ref)


def _gqa_attention(qkv, n_ctx):
    b, t, _ = qkv.shape
    tq = n_ctx
    gw = A_GROUP * HEAD_DIM
    k0 = QA_W // HEAD_DIM
    v0 = (QA_W + KA_W) // HEAD_DIM
    return pl.pallas_call(
        functools.partial(_gqa_kernel, n_ctx=n_ctx),
        grid=(b, A_KV_HEADS, t // tq),
        in_specs=[pl.BlockSpec((1, tq, gw), lambda bi, h, i: (bi, i, h)),
                  pl.BlockSpec((1, t, HEAD_DIM), lambda bi, h, i: (bi, 0, k0 + h)),
                  pl.BlockSpec((1, t, HEAD_DIM), lambda bi, h, i: (bi, 0, v0 + h))],
        out_specs=pl.BlockSpec((1, tq, gw), lambda bi, h, i: (bi, i, h)),
        out_shape=jax.ShapeDtypeStruct((b, t, QA_W + QB_W), BF16),
        compiler_params=_params(("parallel", "parallel", "arbitrary")),
        name="gqa_attention",
    )(qkv, qkv, qkv)


def _diff_head(lam, gain_ref, q_ref, k, v, o_ref, out_scale):
    parts = []
    for m in range(2):
        q = q_ref[0, :, m * HEAD_DIM:(m + 1) * HEAD_DIM]
        s = lax.dot_general(q, k[:, m * HEAD_DIM:(m + 1) * HEAD_DIM], (((1,), (1,)), ((), ())),
                            preferred_element_type=F32)
        p, l = _softmax_rows(s)
        parts.append(p * (1.0 / l))
    a = parts[0] - lam * parts[1]
    y = jnp.dot(a.astype(BF16), v, preferred_element_type=F32)
    y = y * lax.rsqrt(jnp.mean(y * y, axis=-1, keepdims=True) + NORM_EPS)
    o_ref[0] = (y * gain_ref[...] * out_scale).astype(o_ref.dtype)


def _diff_kernel(lam_ref, gain_ref, q_ref, k_ref, v_ref, y_in_ref, o_ref, *, out_scale, n_ctx):
    del y_in_ref
    lam = lam_ref[0, 0]

    @pl.when(pl.program_id(2) == 0)
    def _():
        _diff_head(lam, gain_ref, q_ref, k_ref[0, :n_ctx], v_ref[0, :n_ctx], o_ref, out_scale)

    @pl.when(pl.program_id(2) > 0)
    def _():
        _diff_head(lam, gain_ref, q_ref, k_ref[0], v_ref[0], o_ref, out_scale)


def _diff_attention(qkv, y, lam, subln_gain, out_scale, n_ctx):
    b, t, _ = qkv.shape
    tq = n_ctx
    hw = 2 * HEAD_DIM
    q0 = (QA_W + 2 * KA_W) // hw
    k0 = q0 + B_HEADS
    v0 = k0 + B_HEADS
    o0 = QA_W // hw
    return pl.pallas_call(
        functools.partial(_diff_kernel, out_scale=out_scale, n_ctx=n_ctx),
        grid=(b, B_HEADS, t // tq),
        in_specs=[pl.BlockSpec(memory_space=pltpu.SMEM),
                  pl.BlockSpec((1, hw), lambda bi, h, i: (0, 0)),
                  pl.BlockSpec((1, tq, hw), lambda bi, h, i: (bi, i, q0 + h)),
                  pl.BlockSpec((1, t, hw), lambda bi, h, i: (bi, 0, k0 + h)),
                  pl.BlockSpec((1, t, hw), lambda bi, h, i: (bi, 0, v0 + h)),
                  pl.BlockSpec(memory_space=pl.ANY)],
        out_specs=pl.BlockSpec((1, tq, hw), lambda bi, h, i: (bi, i, o0 + h)),
        out_shape=jax.ShapeDtypeStruct(y.shape, y.dtype),
        input_output_aliases={5: 0},
        compiler_params=_params(("parallel", "parallel", "arbitrary")),
        name="diff_attention",
    )(lam, subln_gain, qkv, qkv, qkv, y)


def _split3(a):
    a1 = a.astype(BF16)
    r1 = a - a1.astype(F32)
    a2 = r1.astype(BF16)
    a3 = (r1 - a2.astype(F32)).astype(BF16)
    return a1, a2, a3


def _dot_exact_rhs(a, rhs01):
    out = None
    for part in _split3(a):
        d = jnp.dot(part, rhs01, preferred_element_type=F32)
        out = d if out is None else out + d
    return out


def _dot_exact_lhs(lhs01, a):
    out = None
    for part in _split3(a):
        d = jnp.dot(lhs01, part, preferred_element_type=F32)
        out = d if out is None else out + d
    return out


SSD_GROUPS_PER_STEP = 8


def _softplus(v):
    return jnp.maximum(v, 0.0) + jnp.log1p(jnp.exp(-jnp.abs(v)))


def _ssd_group(d_sign, acr, acc, skip, x_tok, b_tok, c_tok, dt_c, dt_r, state):
    q = SSD_CHUNK
    rows = lax.broadcasted_iota(jnp.int32, (q, q), 0)
    cols = lax.broadcasted_iota(jnp.int32, (q, q), 1)
    signed = (rows - cols) * d_sign
    keep_sl = signed <= 0
    tri_ks01 = jnp.where(keep_sl, 1.0, 0.0).astype(BF16)
    tri_sk01 = jnp.where(signed >= 0, 1.0, 0.0).astype(BF16)

    x = jnp.transpose(x_tok)
    bm = b_tok.astype(BF16)
    cm = c_tok.astype(BF16)
    a_r = dt_r * acc
    a_c = dt_c * acr
    cum_r = _dot_exact_rhs(a_r, tri_ks01)
    cum_c = _dot_exact_lhs(tri_sk01, a_c)
    tot = jnp.sum(a_r, axis=1, keepdims=True)
    to_end_r = jnp.exp(tot - cum_r)
    from_start_r = jnp.exp(cum_r)
    tot_e = jnp.exp(tot)

    g_sl = lax.dot_general(bm, cm, (((1,), (1,)), ((), ())), preferred_element_type=F32)
    y_off = lax.dot_general(state.astype(BF16), cm, (((1,), (1,)), ((), ())), preferred_element_type=F32)

    xw_parts, y_parts = [], []
    for h in range(SSD_HPG):
        sl = slice(h * SSD_HEADDIM, (h + 1) * SSD_HEADDIM)
        xh = x[sl, :]
        xd = xh * dt_r[h:h + 1, :]
        seg = cum_r[h:h + 1, :] - cum_c[:, h:h + 1]
        decay = jnp.exp(jnp.where(keep_sl, seg, -1e30))
        m_h = (g_sl * decay).astype(BF16)
        y_h = jnp.dot(xd.astype(BF16), m_h, preferred_element_type=F32)
        y_parts.append(y_h + y_off[sl, :] * from_start_r[h:h + 1, :] + skip[h:h + 1, :] * xh)
        xw_parts.append((xd * to_end_r[h:h + 1, :]).astype(BF16))
    s_new = jnp.dot(jnp.concatenate(xw_parts, axis=0), bm, preferred_element_type=F32)
    decay_rows = jnp.concatenate([jnp.broadcast_to(tot_e[h:h + 1, :], (SSD_HEADDIM, 1)) for h in range(SSD_HPG)], axis=0)
    return jnp.transpose(jnp.concatenate(y_parts, axis=0)), state * decay_rows + s_new


def _ssd_kernel(acr_ref, acc_ref, skip_ref, dbr_ref, dbc_ref, x_ref, b_ref, c_ref, dtc_ref, dtr_ref, y_ref, state_ref):
    @pl.when(pl.program_id(3) == 0)
    def _():
        state_ref[...] = jnp.zeros_like(state_ref)

    d_sign = 1 - 2 * pl.program_id(1)
    for g in range(SSD_GROUPS_PER_STEP):
        xs = slice(g * SSD_GROUP_W, (g + 1) * SSD_GROUP_W)
        ns = slice(g * SSD_STATE, (g + 1) * SSD_STATE)
        y, state = _ssd_group(d_sign, acr_ref[0, g], acc_ref[0, g], skip_ref[0, g], x_ref[0, :, xs], b_ref[0, :, ns],
                              c_ref[0, :, ns], _softplus(dtc_ref[0, 0, g] + dbr_ref[0, g]),
                              _softplus(dtr_ref[0, 0, g] + dbc_ref[0, g]), state_ref[g])
        y_ref[0, 0, :, xs] = y.astype(y_ref.dtype)
        state_ref[g] = state


def _ssd_scan(xbc, dt_c, dt_r, a_row, a_col, skip, bias_row, bias_col, n_ctx, out_dtype):
    b, t, _ = xbc.shape
    gs = SSD_GROUPS_PER_STEP
    b0 = SSD_INNER // (gs * SSD_STATE)
    c0 = b0 + SSD_GROUPS // gs
    q = SSD_CHUNK
    ncc = n_ctx // q
    nch = t // q

    def chunk(d, s):
        back = jnp.where(s < ncc, ncc - 1 - s, nch - 1 - s + ncc)
        return jnp.where(d == 0, s, back)

    return pl.pallas_call(
        _ssd_kernel,
        grid=(b, 2, SSD_GROUPS // gs, nch),
        in_specs=[pl.BlockSpec((1, gs, 1, SSD_HPG), lambda bi, d, g, s: (d, g, 0, 0)),
                  pl.BlockSpec((1, gs, SSD_HPG, 1), lambda bi, d, g, s: (d, g, 0, 0)),
                  pl.BlockSpec((1, gs, SSD_HPG, 1), lambda bi, d, g, s: (d, g, 0, 0)),
                  pl.BlockSpec((1, gs, 1, SSD_HPG), lambda bi, d, g, s: (d, g, 0, 0)),
                  pl.BlockSpec((1, gs, SSD_HPG, 1), lambda bi, d, g, s: (d, g, 0, 0)),
                  pl.BlockSpec((1, q, gs * SSD_GROUP_W), lambda bi, d, g, s: (bi, chunk(d, s), g)),
                  pl.BlockSpec((1, q, gs * SSD_STATE), lambda bi, d, g, s: (bi, chunk(d, s), b0 + g)),
                  pl.BlockSpec((1, q, gs * SSD_STATE), lambda bi, d, g, s: (bi, chunk(d, s), c0 + g)),
                  pl.BlockSpec((1, 1, gs, q, SSD_HPG), lambda bi, d, g, s: (d, bi, g, chunk(d, s), 0)),
                  pl.BlockSpec((1, 1, gs, SSD_HPG, q), lambda bi, d, g, s: (d, bi, g, 0, chunk(d, s)))],
        out_specs=pl.BlockSpec((1, 1, q, gs * SSD_GROUP_W), lambda bi, d, g, s: (d, bi, chunk(d, s), g)),
        out_shape=jax.ShapeDtypeStruct((2, b, t, SSD_INNER), out_dtype),
        scratch_shapes=[pltpu.VMEM((gs, SSD_GROUP_W, SSD_STATE), F32)],
        compiler_params=_params(("parallel", "parallel", "parallel", "arbitrary")),
        name="ssd_scan",
    )(a_row, a_col, skip, bias_row, bias_col, xbc, xbc, xbc, dt_c, dt_r)


def _conv_silu_kernel(x_ref, w_ref, b_ref, o_ref, *, n_ctx):
    x = x_ref[0]
    t = x.shape[0]
    row = lax.broadcasted_iota(jnp.int32, (t, 1), 0)
    prev = jnp.where((row == 0) | (row == n_ctx), 0.0, pltpu.roll(x, 1, axis=0))
    nxt = jnp.where((row == n_ctx - 1) | (row == t - 1), 0.0, pltpu.roll(x, t - 1, axis=0))
    u = prev * w_ref[0:1, :] + x * w_ref[1:2, :] + nxt * w_ref[2:3, :] + b_ref[...]
    o_ref[0] = u * jax.nn.sigmoid(u)


def _conv_silu(proj, col0, conv_w, conv_b, n_ctx):
    b, t, _ = proj.shape
    tc = 1024
    c0 = col0 // tc
    return pl.pallas_call(
        functools.partial(_conv_silu_kernel, n_ctx=n_ctx),
        grid=(b, SSD_CONV_DIM // tc),
        in_specs=[pl.BlockSpec((1, t, tc), lambda bi, j: (bi, 0, c0 + j)),
                  pl.BlockSpec((SSD_CONV, tc), lambda bi, j: (0, j)),
                  pl.BlockSpec((1, tc), lambda bi, j: (0, j))],
        out_specs=pl.BlockSpec((1, t, tc), lambda bi, j: (bi, 0, j)),
        out_shape=jax.ShapeDtypeStruct((b, t, SSD_CONV_DIM), F32),
        compiler_params=_params(("parallel", "parallel")),
        name="ssd_conv_silu",
    )(proj, conv_w, conv_b.reshape(1, -1))


def _gate_norm_kernel(y_ref, z_ref, gain_ref, o_ref):
    z = z_ref[0]
    g = (y_ref[0, 0] + y_ref[1, 0]) * (z * jax.nn.sigmoid(z))
    o_ref[0] = (g * lax.rsqrt(jnp.mean(g * g, axis=-1, keepdims=True) + NORM_EPS) * gain_ref[...]).astype(o_ref.dtype)


def _gate_norm(y, proj, gain, n_ctx):
    _, b, t, c = y.shape
    tt = 256
    assert n_ctx % tt == 0
    r0 = n_ctx // tt
    s = t - n_ctx
    return pl.pallas_call(
        _gate_norm_kernel,
        grid=(b, s // tt),
        in_specs=[pl.BlockSpec((2, 1, tt, c), lambda bi, i: (0, bi, r0 + i, 0)),
                  pl.BlockSpec((1, tt, c), lambda bi, i: (bi, r0 + i, 0)),
                  pl.BlockSpec((1, c), lambda bi, i: (0, 0))],
        out_specs=pl.BlockSpec((1, tt, c), lambda bi, i: (bi, i, 0)),
        out_shape=jax.ShapeDtypeStruct((b, s, c), BF16),
        compiler_params=_params(("parallel", "parallel")),
        name="ssd_gate_norm",
    )(y, proj, gain.reshape(1, c))


def _route_kernel(f_ref, rwt_ref, bias_ref, idx_ref, gate_ref, rank_ref, cnt_ref, carry_ref):
    tn = f_ref.shape[0]
    ne, ng, pg = N_EXPERTS, N_EXPERT_GROUPS, N_EXPERTS // N_EXPERT_GROUPS
    neg = -jnp.inf

    @pl.when(pl.program_id(0) == 0)
    def _():
        carry_ref[...] = jnp.zeros_like(carry_ref)

    dn = (((1,), (1,)), ((), ()))
    lo, hi = _unpack_bf16_pairs(f_ref[...])
    f = jnp.concatenate([lo.astype(BF16), hi.astype(BF16)], axis=1)
    w = rwt_ref[...]
    w1 = w.astype(BF16)
    w2 = (w - w1.astype(F32)).astype(BF16)
    logits = (lax.dot_general(w1, f, dn, preferred_element_type=F32)
              + lax.dot_general(w2, f, dn, preferred_element_type=F32))
    scores = jax.nn.sigmoid(logits)
    g3 = (scores + bias_ref[...]).reshape(ng, pg, tn)
    io3 = lax.broadcasted_iota(jnp.int32, (ng, pg, tn), 1)
    m1 = jnp.max(g3, axis=1, keepdims=True)
    i1 = jnp.min(jnp.where(g3 == m1, io3, pg), axis=1, keepdims=True)
    m2 = jnp.max(jnp.where(io3 == i1, neg, g3), axis=1, keepdims=True)
    work = (m1 + m2).reshape(ng, tn)
    iog = lax.broadcasted_iota(jnp.int32, (ng, tn), 0)
    ok = jnp.zeros((ng, tn), F32)
    for _ in range(TOPK_GROUPS):
        m = jnp.max(work, axis=0, keepdims=True)
        gi = jnp.min(jnp.where(work == m, iog, ng), axis=0, keepdims=True)
        hit = iog == gi
        ok = jnp.where(hit, 1.0, ok)
        work = jnp.where(hit, neg, work)
    sel = jnp.where(ok.reshape(ng, 1, tn) > 0.0, g3, neg).reshape(ne, tn)
    ioe = lax.broadcasted_iota(jnp.int32, (ne, tn), 0)
    onehot = jnp.zeros((ne, tn), F32)
    idxs, ws = [], []
    for _ in range(TOP_K):
        m = jnp.max(sel, axis=0, keepdims=True)
        ei = jnp.min(jnp.where(sel == m, ioe, ne), axis=0, keepdims=True)
        hit = ioe == ei
        idxs.append(ei)
        ws.append(jnp.sum(jnp.where(hit, scores, 0.0), axis=0, keepdims=True))
        sel = jnp.where(hit, neg, sel)
        onehot = jnp.where(hit, 1.0, onehot)
    w = jnp.concatenate(ws, axis=0)
    gate_ref[...] = w / jnp.sum(w, axis=0, keepdims=True) * ROUTED_SCALE
    idx_ref[...] = jnp.concatenate(idxs, axis=0)
    r = lax.broadcasted_iota(jnp.int32, (tn, tn), 0)
    c = lax.broadcasted_iota(jnp.int32, (tn, tn), 1)
    ahead = jnp.where(r < c, 1.0, 0.0).astype(BF16)
    cum = carry_ref[...] + jnp.dot(onehot.astype(BF16), ahead, preferred_element_type=F32)
    ranks = [jnp.sum(jnp.where(ioe == idxs[k], cum, 0.0), axis=0, keepdims=True) for k in range(TOP_K)]
    rank_ref[...] = jnp.concatenate(ranks, axis=0).astype(jnp.int32)
    total = carry_ref[...] + jnp.sum(onehot, axis=1, keepdims=True)
    carry_ref[...] = total
    cnt_ref[...] = total.astype(jnp.int32)


def _route(fp, router_wt, router_bias):
    t = fp.shape[0]
    d = router_wt.shape[1]
    tn = _pick(t, (512, 256, 128))
    kt = pl.BlockSpec((TOP_K, tn), lambda i: (0, i))
    return pl.pallas_call(
        _route_kernel,
        grid=(t // tn,),
        in_specs=[pl.BlockSpec((tn, d // 2), lambda i: (i, 0)),
                  pl.BlockSpec((N_EXPERTS, d), lambda i: (0, 0)),
                  pl.BlockSpec((N_EXPERTS, 1), lambda i: (0, 0))],
        out_specs=[kt, kt, kt, pl.BlockSpec((N_EXPERTS, 1), lambda i: (0, 0))],
        out_shape=[jax.ShapeDtypeStruct((TOP_K, t), jnp.int32), jax.ShapeDtypeStruct((TOP_K, t), F32),
                   jax.ShapeDtypeStruct((TOP_K, t), jnp.int32), jax.ShapeDtypeStruct((N_EXPERTS, 1), jnp.int32)],
        scratch_shapes=[pltpu.VMEM((N_EXPERTS, 1), F32)],
        compiler_params=_params(("arbitrary",)),
        name="moe_route",
    )(fp, router_wt, router_bias)


def _scatter_kernel(dest_ref, f_ref, xs_in_ref, xs_ref, sem):
    del xs_in_ref
    ts = f_ref.shape[0]

    def row_copy(t, k):
        return pltpu.make_async_copy(f_ref.at[pl.ds(t, 1)], xs_ref.at[pl.ds(dest_ref[t * TOP_K + k], 1)], sem)

    def issue(t, carry):
        for k in range(TOP_K):
            row_copy(t, k).start(priority=k % 2)
        return carry

    def drain(t, carry):
        for k in range(TOP_K):
            row_copy(t, k).wait()
        return carry

    lax.fori_loop(0, ts, issue, 0)
    lax.fori_loop(0, ts, drain, 0)


def _scatter_rows(dest_flat, f, xs):
    t, d = f.shape
    ts = _pick(t, (512, 256, 128))
    return pl.pallas_call(
        _scatter_kernel,
        grid=(t // ts,),
        in_specs=[pl.BlockSpec((ts * TOP_K,), lambda i: (i,), memory_space=pltpu.SMEM),
                  pl.BlockSpec((ts, d), lambda i: (i, 0)),
                  pl.BlockSpec(memory_space=pl.ANY)],
        out_specs=pl.BlockSpec(memory_space=pl.ANY),
        out_shape=jax.ShapeDtypeStruct(xs.shape, xs.dtype),
        scratch_shapes=[pltpu.SemaphoreType.DMA(())],
        input_output_aliases={2: 0},
        compiler_params=_params(("arbitrary",)),
        name="moe_scatter",
    )(dest_flat, f, xs)


def _zero_blocks_kernel(rows_ref, xs_ref, zero_ref, sem):
    tm = zero_ref.shape[0]
    zero_ref[...] = jnp.zeros_like(zero_ref)

    def block_copy(e):
        return pltpu.make_async_copy(zero_ref, xs_ref.at[pl.ds(pl.multiple_of(rows_ref[e], tm), tm)], sem)

    for e in range(N_EXPERTS):
        block_copy(e).start()
    for e in range(N_EXPERTS):
        block_copy(e).wait()


def _zero_blocks(block_rows, n_rows_alloc, tm, d):
    grid_spec = pltpu.PrefetchScalarGridSpec(
        num_scalar_prefetch=1, grid=(1,), in_specs=[],
        out_specs=pl.BlockSpec(memory_space=pl.ANY),
        scratch_shapes=[pltpu.VMEM((tm, d), jnp.uint32), pltpu.SemaphoreType.DMA(())])
    return pl.pallas_call(
        _zero_blocks_kernel,
        grid_spec=grid_spec,
        out_shape=jax.ShapeDtypeStruct((n_rows_alloc, d), jnp.uint32),
        compiler_params=_params(("arbitrary",)),
        name="moe_zero_tail_blocks",
    )(block_rows)


def _expert_kernel(be_ref, blk_ref, nused_ref, x_ref, wg_ref, wu_ref, wd_ref, o_ref, wg_s, wu_s, wd_s):
    i = pl.program_id(0)

    @pl.when(jnp.logical_or(i == 0, be_ref[i] != be_ref[jnp.maximum(i - 1, 0)]))
    def _():
        wg_s[...] = wg_ref[0, 0].astype(BF16)
        wu_s[...] = wu_ref[0, 0].astype(BF16)
        wd_s[...] = wd_ref[0, 0].astype(BF16)

    @pl.when(i < nused_ref[0])
    def _():
        lo, hi = _unpack_bf16_pairs(x_ref[...])
        x = jnp.concatenate([lo.astype(BF16), hi.astype(BF16)], axis=1)
        hg = jnp.dot(x, wg_s[...], preferred_element_type=F32)
        hu = jnp.dot(x, wu_s[...], preferred_element_type=F32)
        h = (hg * jax.nn.sigmoid(hg) * hu).astype(BF16)
        o_ref[...] = _pack_bf16_pairs(jnp.dot(h, wd_s[...], preferred_element_type=F32))


def _expert_blocks(block_expert, block_index, n_used, xs, n_blocks, tm, layer, w_gate, w_up, w_down):
    d, ff = w_gate.shape[2:]
    grid_spec = pltpu.PrefetchScalarGridSpec(
        num_scalar_prefetch=3,
        grid=(n_blocks,),
        in_specs=[pl.BlockSpec((tm, d // 2), lambda i, be, blk, nu: (blk[i], 0)),
                  pl.BlockSpec((1, 1, d, ff), lambda i, be, blk, nu: (layer, be[i], 0, 0)),
                  pl.BlockSpec((1, 1, d, ff), lambda i, be, blk, nu: (layer, be[i], 0, 0)),
                  pl.BlockSpec((1, 1, ff, d), lambda i, be, blk, nu: (layer, be[i], 0, 0))],
        out_specs=pl.BlockSpec((tm, d // 2), lambda i, be, blk, nu: (blk[i], 0)),
        scratch_shapes=[pltpu.VMEM((d, ff), BF16), pltpu.VMEM((d, ff), BF16), pltpu.VMEM((ff, d), BF16)],
    )
    return pl.pallas_call(
        _expert_kernel,
        grid_spec=grid_spec,
        out_shape=jax.ShapeDtypeStruct((n_blocks * tm, d // 2), jnp.uint32),
        compiler_params=_params(("arbitrary",)),
        name="moe_experts",
    )(block_expert, block_index, n_used, xs, w_gate, w_up, w_down)


def _combine_kernel(dest_ref, gate_ref, sh_ref, xa_ref, mod_ref, fin_ref, ys_ref, o_ref, buf, sem, *,
                    tiles_per_batch, n_ctx, final_norm):
    tn = gate_ref.shape[0]

    def row_copy(t, k):
        return pltpu.make_async_copy(ys_ref.at[pl.ds(dest_ref[t * TOP_K + k], 1)], buf.at[k, pl.ds(t, 1)], sem)

    def issue(t, carry):
        for k in range(TOP_K):
            row_copy(t, k).start(priority=k % 2)
        return carry

    def drain(t, carry):
        for k in range(TOP_K):
            row_copy(t, k).wait()
        return carry

    lax.fori_loop(0, tn, issue, 0)
    lax.fori_loop(0, tn, drain, 0)
    half = buf.shape[2]
    acc_lo = sh_ref[:, :half]
    acc_hi = sh_ref[:, half:]
    for k in range(TOP_K):
        lo, hi = _unpack_bf16_pairs(buf[k])
        acc_lo = acc_lo + gate_ref[:, k:k + 1] * lo
        acc_hi = acc_hi + gate_ref[:, k:k + 1] * hi
    acc = jnp.concatenate([acc_lo, acc_hi], axis=1)
    is_ctx = _is_ctx_rows(tn, tiles_per_batch, n_ctx)
    out = xa_ref[...] + _segment_rows(mod_ref, 5, is_ctx) * acc
    if final_norm:
        out = out * lax.rsqrt(jnp.mean(out * out, axis=-1, keepdims=True) + NORM_EPS) * fin_ref[...]
    o_ref[...] = out


def _combine(dest_flat, gates, shared, ys, xa, modsel, rows_per_batch, n_ctx, final_gain):
    t, d = shared.shape
    tn = 256
    tpb = rows_per_batch // tn
    return pl.pallas_call(
        functools.partial(_combine_kernel, tiles_per_batch=tpb, n_ctx=n_ctx, final_norm=final_gain is not None),
        grid=(t // tn,),
        in_specs=[pl.BlockSpec((tn * TOP_K,), lambda i: (i,), memory_space=pltpu.SMEM),
                  pl.BlockSpec((tn, TOP_K), lambda i: (i, 0)),
                  pl.BlockSpec((tn, d), lambda i: (i, 0)),
                  pl.BlockSpec((tn, d), lambda i: (i, 0)),
                  pl.BlockSpec((1, 2, N_MOD, d), lambda i: (i // tpb, 0, 0, 0)),
                  pl.BlockSpec((1, d), lambda i: (0, 0)),
                  pl.BlockSpec(memory_space=pl.ANY)],
        out_specs=pl.BlockSpec((tn, d), lambda i: (i, 0)),
        out_shape=jax.ShapeDtypeStruct((t, d), F32),
        scratch_shapes=[pltpu.VMEM((TOP_K, tn, d // 2), jnp.uint32), pltpu.SemaphoreType.DMA(())],
        compiler_params=_params(("arbitrary",)),
        name="moe_combine",
    )(dest_flat, gates, shared, xa, modsel, (jnp.ones((d,), F32) if final_gain is None else final_gain).reshape(1, d), ys)


def _swiglu_kernel(x_ref, wg_ref, wu_ref, wd_ref, o_ref):
    lo, hi = _unpack_bf16_pairs(x_ref[...])
    x = jnp.concatenate([lo.astype(BF16), hi.astype(BF16)], axis=1)
    hg = jnp.dot(x, wg_ref[...], preferred_element_type=F32)
    hu = jnp.dot(x, wu_ref[...], preferred_element_type=F32)
    h = (hg * jax.nn.sigmoid(hg) * hu).astype(BF16)
    o_ref[...] = jnp.dot(h, wd_ref[...], preferred_element_type=F32).astype(o_ref.dtype)


def _shared_expert(x, wg, wu, wd):
    m = x.shape[0]
    d, ff = wg.shape
    tm = _pick(m, (512, 384, 256, 128))
    return pl.pallas_call(
        _swiglu_kernel,
        grid=(m // tm,),
        in_specs=[pl.BlockSpec((tm, d // 2), lambda i: (i, 0)),
                  pl.BlockSpec((d, ff), lambda i: (0, 0)),
                  pl.BlockSpec((d, ff), lambda i: (0, 0)),
                  pl.BlockSpec((ff, d), lambda i: (0, 0))],
        out_specs=pl.BlockSpec((tm, d), lambda i: (i, 0)),
        out_shape=jax.ShapeDtypeStruct((m, d), F32),
        compiler_params=_params(("parallel",)),
        name="shared_expert",
    )(x, wg, wu, wd)


def _moe_ffn(fp, xa, modsel, rows_per_batch, n_ctx, final_gain, layer, router_w, router_bias, w_gate, w_up, w_down,
             ws_gate, ws_up, ws_down):
    t, d = xa.shape
    tm = EXPERT_TM
    idx, gate, rank, cnt = _route(fp, router_w.T, router_bias.astype(F32)[:, None])
    counts = cnt[:, 0]
    padded = (counts + tm - 1) // tm * tm
    pad_end = jnp.cumsum(padded)
    pad_start = pad_end - padded
    n_used = pad_end[-1] // tm
    n_blocks = (t * TOP_K) // tm + N_EXPERTS
    n_rows = n_blocks * tm
    experts = jnp.arange(N_EXPERTS, dtype=jnp.int32)
    start_of = jnp.sum(jnp.where(idx[..., None] == experts, pad_start, 0), axis=-1)
    dest = (start_of + rank).T.reshape(-1).astype(jnp.int32)
    block_index = jnp.minimum(jnp.arange(n_blocks, dtype=jnp.int32), n_used - 1).astype(jnp.int32)
    block_expert = jnp.minimum(jnp.sum(pad_end[None, :] <= (block_index * tm)[:, None], axis=1),
                               N_EXPERTS - 1).astype(jnp.int32)
    tail_rows = jnp.where(padded > counts, pad_end - tm, n_rows + experts * tm).astype(jnp.int32)
    xs = _zero_blocks(tail_rows, n_rows + N_EXPERTS * tm, tm, d // 2)
    xs = _scatter_rows(dest, fp, xs)
    ys = _expert_blocks(block_expert, block_index, n_used.reshape(1).astype(jnp.int32), xs, n_blocks, tm, layer,
                        w_gate, w_up, w_down)
    shared = _shared_expert(fp, ws_gate.astype(BF16), ws_up.astype(BF16), ws_down.astype(BF16))
    return _combine(dest, gate.T, shared, ys, xa, modsel, rows_per_batch, n_ctx, final_gain)


def _rms(u, gain):
    return u * lax.rsqrt(jnp.mean(u * u, axis=-1, keepdims=True) + NORM_EPS) * gain


def _rope_tables(n_ctx, n_lat):
    rows = n_lat // GRID_W
    row = jnp.repeat(jnp.arange(rows, dtype=F32), GRID_W)
    col = jnp.tile(jnp.arange(GRID_W, dtype=F32), rows)
    n_freq = HEAD_DIM // 4
    inv = ROPE_THETA ** (-jnp.arange(n_freq, dtype=F32) / n_freq)
    ang = jnp.concatenate([row[:, None] * inv, col[:, None] * inv], axis=-1)
    cos = jnp.repeat(jnp.cos(ang), 2, axis=-1)
    sin = jnp.repeat(jnp.sin(ang), 2, axis=-1)
    sign = jnp.tile(jnp.array([-1.0, 1.0], F32), HEAD_DIM // 2)
    cos = jnp.concatenate([jnp.ones((n_ctx, HEAD_DIM), F32), cos], axis=0)
    sin = jnp.concatenate([jnp.zeros((n_ctx, HEAD_DIM), F32), sin * sign], axis=0)
    return cos, sin


def _ada_mod(cond, w, bias):
    m = jax.nn.silu(cond)
    rows = m.shape[0]
    pad = (-rows) % 16
    mp = jnp.pad(m, ((0, pad), (0, 0))).astype(BF16)
    out = _matmul(mp, w.astype(BF16), F32)[:rows] + bias
    return out.reshape(rows, N_MOD, -1)


def _attention_layer(xa, modsel, gain, n_ctx, w_in, q_gain, k_gain, lq1, lk1, lq2, lk2, subln, lambda_init):
    b, t, d = xa.shape
    cos, sin = _rope_tables(n_ctx, t - n_ctx)
    head_gains = jnp.stack([q_gain * _Q_SCALE, k_gain]).astype(F32)
    qkv = _fused_proj(xa, gain, modsel, w_in.astype(BF16), n_ctx, BF16,
                      head_gains=head_gains, rope=(cos - 1.0, sin)).reshape(b, t, ATTN_IN)
    lam = (jnp.exp(jnp.sum(lq1 * lk1)) - jnp.exp(jnp.sum(lq2 * lk2)) + lambda_init).reshape(1, 1).astype(F32)
    y = _gqa_attention(qkv, n_ctx)
    return _diff_attention(qkv, y, lam, subln.reshape(1, 2 * HEAD_DIM), 1.0 - lambda_init, n_ctx)


def _ssd_layer(xa, modsel, gain, n_ctx, w_in, conv_w, conv_b, dt_bias, a_log, d_skip, norm_gain):
    b, t, d = xa.shape
    proj = _fused_proj(xa, gain, modsel, w_in.astype(BF16), n_ctx, F32).reshape(b, t, -1)
    xbc = _conv_silu(proj, SSD_INNER, conv_w, conv_b, n_ctx)
    dt = proj[..., SSD_INNER + SSD_CONV_DIM:].reshape(b, t, 2, SSD_GROUPS, SSD_HPG)
    dt_c = jnp.transpose(dt, (2, 0, 3, 1, 4))
    dt_r = jnp.transpose(dt, (2, 0, 3, 4, 1))
    bias_row = dt_bias.reshape(2, SSD_GROUPS, 1, SSD_HPG)
    bias_col = dt_bias.reshape(2, SSD_GROUPS, SSD_HPG, 1)
    a_coef = -jnp.exp(a_log)
    a_row = a_coef.reshape(2, SSD_GROUPS, 1, SSD_HPG)
    a_col = a_coef.reshape(2, SSD_GROUPS, SSD_HPG, 1)
    skip = d_skip.reshape(2, SSD_GROUPS, SSD_HPG, 1)
    y = _ssd_scan(xbc, dt_c, dt_r, a_row, a_col, skip, bias_row, bias_col, n_ctx, F32)
    return _gate_norm(y, proj, norm_gain, n_ctx)


def kernel(x, c, ctx, c_ctx, mod_w, mod_b, norm_mix, norm_ffn, norm_final, attn_w_in, attn_w_out, attn_q_gain,
           attn_k_gain, diff_lam_q1, diff_lam_k1, diff_lam_q2, diff_lam_k2, diff_subln, ssd_w_in, ssd_conv_w,
           ssd_conv_b, ssd_dt_bias, ssd_a_log, ssd_d, ssd_norm, ssd_w_out, router_w, router_bias, exp_w_gate,
           exp_w_up, exp_w_down, shared_w_gate, shared_w_up, shared_w_down):
    b, s, d = x.shape
    n_ctx = ctx.shape[1]
    t = n_ctx + s
    xa = jnp.concatenate([ctx, x], axis=1)
    cond = jnp.concatenate([c, c_ctx[None]], axis=0)
    out = None
    for i in range(DEPTH):
        last = i == DEPTH - 1
        mod = _ada_mod(cond, mod_w[i], mod_b[i])
        modsel = jnp.stack([jnp.broadcast_to(mod[b], (b, N_MOD, d)), mod[:b]], axis=1)
        j = i // 2
        moe_w = (i, router_w[i], router_bias[i], exp_w_gate, exp_w_up, exp_w_down,
                 shared_w_gate[i], shared_w_up[i], shared_w_down[i])
        if i % 2 == 0:
            lambda_init = 0.8 - 0.6 * math.exp(-0.3 * i)
            y = _attention_layer(xa, modsel, norm_mix[i], n_ctx, attn_w_in[j], attn_q_gain[j], attn_k_gain[j],
                                 diff_lam_q1[j], diff_lam_k1[j], diff_lam_q2[j], diff_lam_k2[j],
                                 diff_subln[j], lambda_init)
            w_out = attn_w_out[j]
        else:
            y = _ssd_layer(xa, modsel, norm_mix[i], n_ctx, ssd_w_in[j], ssd_conv_w[j], ssd_conv_b[j],
                           ssd_dt_bias[j], ssd_a_log[j], ssd_d[j], ssd_norm[j])
            w_out = ssd_w_out[j]
        if last:
            x_new, fp = _fused_out_proj(y if y.shape[1] == s else y[:, n_ctx:], w_out.astype(BF16), xa, n_ctx,
                                        modsel, norm_ffn[i], 0)
            out = _moe_ffn(fp.reshape(b * s, d // 2), x_new.reshape(b * s, d), modsel, s, 0, norm_final,
                           *moe_w).reshape(b, s, d)
        else:
            x_new, fp = _fused_out_proj(y, w_out.astype(BF16), xa, 0, modsel, norm_ffn[i], n_ctx)
            xa = _moe_ffn(fp.reshape(b * t, d // 2), x_new.reshape(b * t, d), modsel, t, n_ctx, None,
                          *moe_w).reshape(b, t, d)
    return out
```
